```python
import jax
import jax.numpy as jnp
from jax import lax
import numpy as np

D_MODEL = 1024
BATCH = 2
SEQ = 16384
DEPTH = 2

GRID_W = 64
CTX_LEN = 256
D_FF = 2816
FFN_RES_WEIGHT = 0.5
N_MOD = 9
EPS = 1e-6
ROPE_THETA = 10000.0
Q_BLOCK = 128
F_FLOOR = 1e-20

A_WIDTH = 256
A_GROUP = 16
A_GROUPS = A_WIDTH // A_GROUP
A_STATE = 64
DT_MIN = 1e-3
DT_MAX = 1e-1

B_HEADS = 4
B_NOPE = 64
B_ROPE = 32
B_V = 64
B_Q_LORA = 192
B_KV_LORA = 128

C_HEADS = 4
C_DK = 64
C_DV = 64
C_CHUNK = 64

D_HEADS = 4
D_KV_HEADS = 2
D_HEAD = 64

N_BRANCH = 4
BRANCH_W = 256

IN_SPLITS = (A_WIDTH,
             B_Q_LORA, B_KV_LORA, B_ROPE,
             C_HEADS * C_DK, C_HEADS * C_DV, C_HEADS * C_DK, C_HEADS * C_DK, C_HEADS * C_DV,
             D_HEADS * D_HEAD, D_KV_HEADS * D_HEAD, D_KV_HEADS * D_HEAD,
             N_BRANCH * D_MODEL)
N_IN = sum(IN_SPLITS)

kernel_name = 'hybrid_s5_mla_hgrn2_gqa_block'


def rms_norm(x, g):
    xf = x.astype(jnp.float32)
    y = xf * lax.rsqrt(jnp.mean(xf * xf, axis=-1, keepdims=True) + EPS)
    return (y * g.astype(jnp.float32)).astype(x.dtype)


def modulate(x, shift, scale):
    return x * (1.0 + scale) + shift


def swiglu(h, w1, w3, w2):
    return (jax.nn.silu(h @ w1) * (h @ w3)) @ w2


def ffn_sublayer(x, mod, j, g_pre, g_post, w1, w3, w2):
    h = modulate(rms_norm(x, g_pre), mod[:, :, 3 * j], mod[:, :, 3 * j + 1])
    return x + FFN_RES_WEIGHT * mod[:, :, 3 * j + 2] * rms_norm(swiglu(h, w1, w3, w2), g_post)


def axial_rope_tables(n_rows, rot_dim):
    rows = jnp.repeat(jnp.arange(n_rows, dtype=jnp.float32), GRID_W)
    cols = jnp.tile(jnp.arange(GRID_W, dtype=jnp.float32), n_rows)
    half = rot_dim // 2
    inv = ROPE_THETA ** (-jnp.arange(0, half, 2, dtype=jnp.float32) / half)
    ang_r = rows[:, None] * inv
    ang_c = cols[:, None] * inv
    ang = jnp.concatenate([ang_r, ang_r, ang_c, ang_c], axis=-1)
    return jnp.cos(ang), jnp.sin(ang)


def apply_axial_rope(x, cos, sin):
    q = x.shape[-1] // 4
    xr = x.reshape(x.shape[:-1] + (2, 2, q))
    rot = jnp.stack([-xr[..., 1, :], xr[..., 0, :]], axis=-2).reshape(x.shape)
    return x * cos[None, :, None, :].astype(x.dtype) + rot * sin[None, :, None, :].astype(x.dtype)


def attend_block(q, k, v, scale):
    s = jnp.einsum('bqhgd,bkhd->bhgqk', q, k, preferred_element_type=jnp.float32) * scale
    p = jax.nn.softmax(s, axis=-1).astype(v.dtype)
    return jnp.einsum('bhgqk,bkhd->bqhgd', p, v)


def blocked_attention(q, k, v, scale):
    b, lq, hq, dk = q.shape
    hkv = k.shape[2]
    nblk = lq // Q_BLOCK
    qb = q.reshape(b, nblk, Q_BLOCK, hkv, hq // hkv, dk).swapaxes(0, 1)
    ob = lax.map(lambda qi: attend_block(qi, k, v, scale), qb)
    return ob.swapaxes(0, 1).reshape(b, lq, hq, v.shape[-1])


def split_projection(z):
    offsets = np.cumsum(IN_SPLITS)[:-1].tolist()
    return jnp.split(z, offsets, axis=-1)


def s5_discretize(lam_re, lam_im, log_dt, b_re, b_im):
    lam_re = jnp.minimum(lam_re.astype(jnp.float32), -1e-4)
    lam_im = lam_im.astype(jnp.float32)
    dt = jnp.exp(log_dt.astype(jnp.float32))[:, None]
    mag = jnp.exp(lam_re * dt)
    a_re = mag * jnp.cos(lam_im * dt)
    a_im = mag * jnp.sin(lam_im * dt)
    den = lam_re * lam_re + lam_im * lam_im
    num_re = a_re - 1.0
    f_re = (num_re * lam_re + a_im * lam_im) / den
    f_im = (a_im * lam_re - num_re * lam_im) / den
    b_re = b_re.astype(jnp.float32)
    b_im = b_im.astype(jnp.float32)
    bb_re = f_re[..., None] * b_re - f_im[..., None] * b_im
    bb_im = f_re[..., None] * b_im + f_im[..., None] * b_re
    return a_re, a_im, bb_re, bb_im


def complex_combine(e1, e2):
    a1r, a1i, b1r, b1i = e1
    a2r, a2i, b2r, b2i = e2
    return (a2r * a1r - a2i * a1i, a2r * a1i + a2i * a1r,
            a2r * b1r - a2i * b1i + b2r, a2r * b1i + a2i * b1r + b2i)


def complex_scan(a_re, a_im, bu_re, bu_im, h0, reverse):
    if h0 is not None:
        h_re, h_im = h0
        first = -1 if reverse else 0
        bu_re = bu_re.at[:, first].add(a_re * h_re - a_im * h_im)
        bu_im = bu_im.at[:, first].add(a_re * h_im + a_im * h_re)
    a_re = jnp.broadcast_to(a_re, bu_re.shape)
    a_im = jnp.broadcast_to(a_im, bu_im.shape)
    _, _, x_re, x_im = lax.associative_scan(complex_combine, (a_re, a_im, bu_re, bu_im),
                                            reverse=reverse, axis=1)
    return x_re, x_im


def s5_branch(u_lat, u_ctx, lam_re, lam_im, log_dt, b_re, b_im, c_re, c_im, d_skip, w_glu, with_ctx_out):
    dt_ = u_lat.dtype
    ug_lat = u_lat.reshape(u_lat.shape[0], u_lat.shape[1], A_GROUPS, A_GROUP)
    ug_ctx = u_ctx.reshape(u_ctx.shape[0], u_ctx.shape[1], A_GROUPS, A_GROUP)
    y_lat = d_skip * u_lat
    y_ctx = d_skip * u_ctx if with_ctx_out else None
    for dr, rev in enumerate((False, True)):
        a_re, a_im, bb_re, bb_im = (t.astype(dt_) for t in s5_discretize(
            lam_re[dr], lam_im[dr], log_dt[dr], b_re[dr], b_im[dr]))
        x_ctx = complex_scan(a_re, a_im,
                             jnp.einsum('gpn,blgn->blgp', bb_re, ug_ctx),
                             jnp.einsum('gpn,blgn->blgp', bb_im, ug_ctx), None, rev)
        end = 0 if rev else -1
        x_lat = complex_scan(a_re, a_im,
                             jnp.einsum('gpn,blgn->blgp', bb_re, ug_lat),
                             jnp.einsum('gpn,blgn->blgp', bb_im, ug_lat),
                             (x_ctx[0][:, end], x_ctx[1][:, end]), rev)
        y_lat = y_lat + (jnp.einsum('gnp,blgp->blgn', c_re[dr], x_lat[0])
                         - jnp.einsum('gnp,blgp->blgn', c_im[dr], x_lat[1])).reshape(u_lat.shape)
        if with_ctx_out:
            y_ctx = y_ctx + (jnp.einsum('gnp,blgp->blgn', c_re[dr], x_ctx[0])
                             - jnp.einsum('gnp,blgp->blgn', c_im[dr], x_ctx[1])).reshape(u_ctx.shape)

    def glu(y):
        g = jax.nn.gelu(y)
        return g * jax.nn.sigmoid(g @ w_glu)
    return glu(y_lat), (glu(y_ctx) if with_ctx_out else None)


def mla_keys(ckv, kr, kv_norm, w_ukv, rope):
    b, l, _ = ckv.shape
    kv = (rms_norm(ckv, kv_norm) @ w_ukv).reshape(b, l, B_HEADS, B_NOPE + B_V)
    k_nope, v = kv[..., :B_NOPE], kv[..., B_NOPE:]
    k_rope = kr[:, :, None, :]
    if rope is not None:
        k_rope = apply_axial_rope(k_rope, *rope)
    k = jnp.concatenate([k_nope, jnp.broadcast_to(k_rope, (b, l, B_HEADS, B_ROPE))], axis=-1)
    return k, v


def mla_queries(cq, q_norm, w_uq, rope):
    b, l, _ = cq.shape
    q = (rms_norm(cq, q_norm) @ w_uq).reshape(b, l, B_HEADS, B_NOPE + B_ROPE)
    if rope is None:
        return q
    return jnp.concatenate([q[..., :B_NOPE], apply_axial_rope(q[..., B_NOPE:], *rope)], axis=-1)


def mla_branch(lat, ctx, q_norm, w_uq, kv_norm, w_ukv, rope, with_ctx_out):
    scale = (B_NOPE + B_ROPE) ** -0.5
    k_ctx, v_ctx = mla_keys(ctx[1], ctx[2], kv_norm, w_ukv, None)
    k_lat, v_lat = mla_keys(lat[1], lat[2], kv_norm, w_ukv, rope)
    q_lat = mla_queries(lat[0], q_norm, w_uq, rope)
    o_lat = blocked_attention(q_lat, jnp.concatenate([k_ctx, k_lat], axis=1),
                              jnp.concatenate([v_ctx, v_lat], axis=1), scale)
    y_lat = o_lat.reshape(o_lat.shape[0], o_lat.shape[1], B_HEADS * B_V)
    if not with_ctx_out:
        return y_lat, None
    o_ctx = blocked_attention(mla_queries(ctx[0], q_norm, w_uq, None), k_ctx, v_ctx, scale)
    return y_lat, o_ctx.reshape(o_ctx.shape[0], o_ctx.shape[1], B_HEADS * B_V)


def hgrn_gates(z, lb):
    z = z.astype(jnp.float32)
    f = lb + (1.0 - lb) * jax.nn.sigmoid(z)
    log_f = jnp.log(jnp.maximum(f, F_FLOOR))
    k = (1.0 - lb) * jax.nn.sigmoid(-z)
    return log_f, k


def hgrn_chunk_scan(q, k, v, log_f, s0):
    b, l, h, _ = q.shape
    n = l // C_CHUNK
    tri = jnp.tril(jnp.ones((C_CHUNK, C_CHUNK), dtype=bool))[None, :, :, None, None]

    def chunks(t):
        return t.reshape(b, n, C_CHUNK, h, t.shape[-1]).swapaxes(0, 1)

    def step(s, inp):
        qc, kc, vc, fc = inp
        cum = jnp.cumsum(fc, axis=1)
        o_inter = jnp.einsum('bchk,bhkv->bchv', qc * jnp.exp(cum), s)
        diff = cum[:, :, None] - cum[:, None, :]
        decay = jnp.where(tri, jnp.exp(jnp.minimum(diff, 0.0)), 0.0)
        att = jnp.einsum('bthk,bshk,btshk->bths', qc, kc, decay)
        o_intra = jnp.einsum('bths,bshv->bthv', att, vc)
        last = cum[:, -1]
        s_new = jnp.exp(last)[..., None] * s + jnp.einsum(
            'bshk,bshv->bhkv', kc * jnp.exp(last[:, None] - cum), vc)
        return s_new, o_inter + o_intra

    s_fin, o = lax.scan(step, s0, (chunks(q), chunks(k), chunks(v), chunks(log_f)))
    return o.swapaxes(0, 1).reshape(b, l, h, v.shape[-1]), s_fin


def hgrn_branch(lat, ctx, lb, o_norm, with_ctx_out):
    def heads(t):
        return t.reshape(t.shape[0], t.shape[1], C_HEADS, -1).astype(jnp.float32)

    def flip(t):
        return jnp.flip(t, axis=1)

    b = ctx[0].shape[0]
    o_lat = 0.0
    o_ctx = 0.0
    for dr in range(2):
        lb_d = lb[dr].reshape(C_HEADS, C_DK)
        streams = []
        for s in (ctx, lat):
            log_f, k = hgrn_gates(heads(s[2 + dr]), lb_d)
            t = (heads(s[0]), k, heads(s[1]), log_f)
            streams.append(tuple(flip(u) for u in t) if dr == 1 else t)
        s0 = jnp.zeros((b, C_HEADS, C_DK, C_DV), jnp.float32)
        oc, s_ctx = hgrn_chunk_scan(*streams[0], s0)
        ol, _ = hgrn_chunk_scan(*streams[1], s_ctx)
        if dr == 1:
            oc, ol = flip(oc), flip(ol)
        o_lat = o_lat + ol
        o_ctx = o_ctx + oc

    def readout(o, g):
        o = rms_norm(o, o_norm).reshape(o.shape[0], o.shape[1], C_HEADS * C_DV)
        return o.astype(g.dtype) * jax.nn.silu(g)
    return readout(o_lat, lat[4]), (readout(o_ctx, ctx[4]) if with_ctx_out else None)


def gqa_keys(kd, vd, k_norm, rope):
    b, l, _ = kd.shape
    k = rms_norm(kd.reshape(b, l, D_KV_HEADS, D_HEAD), k_norm)
    v = vd.reshape(b, l, D_KV_HEADS, D_HEAD)
    return (k if rope is None else apply_axial_rope(k, *rope)), v


def gqa_queries(qd, q_norm, rope):
    b, l, _ = qd.shape
    q = rms_norm(qd.reshape(b, l, D_HEADS, D_HEAD), q_norm)
    return q if rope is None else apply_axial_rope(q, *rope)


def gqa_branch(lat, ctx, q_norm, k_norm, rope, with_ctx_out):
    scale = D_HEAD ** -0.5
    k_ctx, v_ctx = gqa_keys(ctx[1], ctx[2], k_norm, None)
    k_lat, v_lat = gqa_keys(lat[1], lat[2], k_norm, rope)
    o_lat = blocked_attention(gqa_queries(lat[0], q_norm, rope),
                              jnp.concatenate([k_ctx, k_lat], axis=1),
                              jnp.concatenate([v_ctx, v_lat], axis=1), scale)
    y_lat = o_lat.reshape(o_lat.shape[0], o_lat.shape[1], D_HEADS * D_HEAD)
    if not with_ctx_out:
        return y_lat, None
    o_ctx = blocked_attention(gqa_queries(ctx[0], q_norm, None), k_ctx, v_ctx, scale)
    return y_lat, o_ctx.reshape(o_ctx.shape[0], o_ctx.shape[1], D_HEADS * D_HEAD)


def merge_branches(branches, gate_raw, w_branch, w_out):
    b, l, _ = gate_raw.shape
    gates = jax.nn.sigmoid(gate_raw).reshape(b, l, N_BRANCH, D_MODEL)
    merged = gates[:, :, 0] * (branches[0] @ w_branch[0])
    for i in range(1, N_BRANCH):
        merged = merged + gates[:, :, i] * (branches[i] @ w_branch[i])
    return merged @ w_out


def token_mixing(h_lat, h_ctx, w_in, s5_p, mla_p, hgrn_p, gqa_p, w_branch, w_out, rope_b, rope_d, with_ctx_out):
    zl = split_projection(h_lat @ w_in)
    zc = split_projection(h_ctx @ w_in)
    ya_l, ya_c = s5_branch(zl[0], zc[0], *s5_p, with_ctx_out)
    yb_l, yb_c = mla_branch(zl[1:4], zc[1:4], *mla_p, rope_b, with_ctx_out)
    yc_l, yc_c = hgrn_branch(zl[4:9], zc[4:9], *hgrn_p, with_ctx_out)
    yd_l, yd_c = gqa_branch(zl[9:12], zc[9:12], *gqa_p, rope_d, with_ctx_out)
    y_lat = merge_branches((ya_l, yb_l, yc_l, yd_l), zl[12], w_branch, w_out)
    if not with_ctx_out:
        return y_lat, None
    y_ctx = merge_branches((ya_c, yb_c, yc_c, yd_c), zc[12], w_branch, w_out)
    return y_lat, y_ctx


def setup_inputs(seed: int = 0) -> dict:
    key = jax.random.key(seed)
    ks = iter(jax.random.split(key, 32))
    f32 = jnp.float32

    def nrm(shape, std):
        return std * jax.random.normal(next(ks), shape, f32)

    def gain(shape):
        return 1.0 + nrm(shape, 0.02)

    L = DEPTH
    return {
        'x': nrm((BATCH, SEQ, D_MODEL), 1.0),
        'c': nrm((BATCH, D_MODEL), 1.0),
        'ctx': nrm((BATCH, CTX_LEN, D_MODEL), 1.0),
        'c_ctx': nrm((D_MODEL,), 1.0),
        'w_ada': nrm((L, D_MODEL, N_MOD * D_MODEL), 0.5 * D_MODEL ** -0.5),
        'b_ada': nrm((L, N_MOD * D_MODEL), 0.02),
        'norm_pre': gain((L, 3, D_MODEL)),
        'norm_post': gain((L, 3, D_MODEL)),
        'ffn_w1': nrm((L, 2, D_MODEL, D_FF), D_MODEL ** -0.5),
        'ffn_w3': nrm((L, 2, D_MODEL, D_FF), D_MODEL ** -0.5),
        'ffn_w2': nrm((L, 2, D_FF, D_MODEL), D_FF ** -0.5),
        'w_in': nrm((L, D_MODEL, N_IN), D_MODEL ** -0.5),
        's5_lambda_re': -0.5 + nrm((L, 2, A_GROUPS, A_STATE), 0.01),
        's5_lambda_im': jnp.pi * jnp.arange(A_STATE, dtype=f32) + nrm((L, 2, A_GROUPS, A_STATE), 0.01),
        's5_log_dt': jax.random.uniform(next(ks), (L, 2, A_GROUPS), f32,
                                        minval=float(np.log(DT_MIN)), maxval=float(np.log(DT_MAX))),
        's5_b_re': nrm((L, 2, A_GROUPS, A_STATE, A_GROUP), A_GROUP ** -0.5),
        's5_b_im': nrm((L, 2, A_GROUPS, A_STATE, A_GROUP), A_GROUP ** -0.5),
        's5_c_re': nrm((L, 2, A_GROUPS, A_GROUP, A_STATE), A_STATE ** -0.5),
        's5_c_im': nrm((L, 2, A_GROUPS, A_GROUP, A_STATE), A_STATE ** -0.5),
        's5_d': nrm((L, A_WIDTH), 1.0),
        's5_w_glu': nrm((L, A_WIDTH, A_WIDTH), A_WIDTH ** -0.5),
        'mla_q_norm': gain((L, B_Q_LORA)),
        'mla_w_uq': nrm((L, B_Q_LORA, B_HEADS * (B_NOPE + B_ROPE)), B_Q_LORA ** -0.5),
        'mla_kv_norm': gain((L, B_KV_LORA)),
        'mla_w_ukv': nrm((L, B_KV_LORA, B_HEADS * (B_NOPE + B_V)), B_KV_LORA ** -0.5),
        'hgrn_lb_raw': nrm((2, L, C_HEADS * C_DK), 0.5),
        'hgrn_o_norm': gain((L, C_DV)),
        'gqa_q_norm': gain((L, D_HEAD)),
        'gqa_k_norm': gain((L, D_HEAD)),
        'w_branch': nrm((L, N_BRANCH, BRANCH_W, D_MODEL), BRANCH_W ** -0.5),
        'w_out': nrm((L, D_MODEL, D_MODEL), D_MODEL ** -0.5),
    }


def reference(x, c, ctx, c_ctx, w_ada, b_ada, norm_pre, norm_post, ffn_w1, ffn_w3, ffn_w2, w_in,
              s5_lambda_re, s5_lambda_im, s5_log_dt, s5_b_re, s5_b_im, s5_c_re, s5_c_im, s5_d, s5_w_glu,
              mla_q_norm, mla_w_uq, mla_kv_norm, mla_w_ukv, hgrn_lb_raw, hgrn_o_norm,
              gqa_q_norm, gqa_k_norm, w_branch, w_out):
    b, seq_len, _ = x.shape
    n_rows = seq_len // GRID_W
    rope_b = axial_rope_tables(n_rows, B_ROPE)
    rope_d = axial_rope_tables(n_rows, D_HEAD)
    lb_step = jax.nn.softmax(hgrn_lb_raw.astype(jnp.float32), axis=1)
    lb_all = jnp.clip(jnp.cumsum(lb_step, axis=1) - lb_step[:, :1], 0.0, 1.0)

    x_lat, x_ctx = x, ctx
    for layer in range(DEPTH):
        last = layer == DEPTH - 1
        mod_lat = (jax.nn.silu(c) @ w_ada[layer] + b_ada[layer]).reshape(b, 1, N_MOD, D_MODEL)
        mod_ctx = (jax.nn.silu(c_ctx) @ w_ada[layer] + b_ada[layer]).reshape(1, 1, N_MOD, D_MODEL)
        ffn_a = (norm_pre[layer, 0], norm_post[layer, 0], ffn_w1[layer, 0], ffn_w3[layer, 0], ffn_w2[layer, 0])
        ffn_b = (norm_pre[layer, 2], norm_post[layer, 2], ffn_w1[layer, 1], ffn_w3[layer, 1], ffn_w2[layer, 1])

        x_lat = ffn_sublayer(x_lat, mod_lat, 0, *ffn_a)
        x_ctx = ffn_sublayer(x_ctx, mod_ctx, 0, *ffn_a)

        h_lat = modulate(rms_norm(x_lat, norm_pre[layer, 1]), mod_lat[:, :, 3], mod_lat[:, :, 4])
        h_ctx = modulate(rms_norm(x_ctx, norm_pre[layer, 1]), mod_ctx[:, :, 3], mod_ctx[:, :, 4])
        s5_p = (s5_lambda_re[layer], s5_lambda_im[layer], s5_log_dt[layer], s5_b_re[layer], s5_b_im[layer],
                s5_c_re[layer], s5_c_im[layer], s5_d[layer], s5_w_glu[layer])
        mla_p = (mla_q_norm[layer], mla_w_uq[layer], mla_kv_norm[layer], mla_w_ukv[layer])
        hgrn_p = (lb_all[:, layer], hgrn_o_norm[layer])
        gqa_p = (gqa_q_norm[layer], gqa_k_norm[layer])
        y_lat, y_ctx = token_mixing(h_lat, h_ctx, w_in[layer], s5_p, mla_p, hgrn_p, gqa_p,
                                    w_branch[layer], w_out[layer], rope_b, rope_d, not last)
        x_lat = x_lat + mod_lat[:, :, 5] * rms_norm(y_lat, norm_post[layer, 1])

        x_lat = ffn_sublayer(x_lat, mod_lat, 2, *ffn_b)
        if not last:
            x_ctx = x_ctx + mod_ctx[:, :, 5] * rms_norm(y_ctx, norm_post[layer, 1])
            x_ctx = ffn_sublayer(x_ctx, mod_ctx, 2, *ffn_b)
    return x_lat
```

```python
import functools

import jax
import jax.numpy as jnp
from jax import lax
from jax.experimental import pallas as pl
from jax.experimental.pallas import tpu as pltpu

GRID_W = 64
FFN_RES_WEIGHT = 0.5
N_MOD = 9
EPS = 1e-6
ROPE_THETA = 10000.0
F_FLOOR = 1e-20

A_WIDTH = 256
A_GROUP = 16
A_GROUPS = A_WIDTH // A_GROUP
A_STATE = 64

B_HEADS = 4
B_NOPE = 64
B_ROPE = 32
B_V = 64
B_Q_LORA = 192
B_KV_LORA = 128

C_HEADS = 4
C_DK = 64
C_DV = 64

D_HEADS = 4
D_KV_HEADS = 2
D_HEAD = 64

N_BRANCH = 4
BRANCH_W = 256

LANES = 128
VMEM_LIMIT_BYTES = 56 * 1024 * 1024

S5_R = 4
S5_LANES = S5_R * A_WIDTH
S5_NSTATE = A_GROUPS * A_STATE
HG_T = 128
HG_LEVELS = 7
HG_W = C_HEADS * C_DK

SLOT_S5 = 0
SLOT_CQ = 256
SLOT_CKV = 512
SLOT_KR = 640
SLOT_HQ = 768
SLOT_HV = 1024
SLOT_HF = 1280
SLOT_HB = 1536
SLOT_HG = 1792
SLOT_GQ = 2048
SLOT_GK = 2560
SLOT_GV = 2816
N_PROJ = 3072

BF16 = jnp.bfloat16
F32 = jnp.float32


def _params(sem):
    return pltpu.CompilerParams(dimension_semantics=sem, vmem_limit_bytes=VMEM_LIMIT_BYTES)


def _pick(n, candidates):
    for c in candidates:
        if n % c == 0:
            return c
    raise ValueError(f"no tile for {n} in {candidates}")


def _dot(a, b):
    return jnp.dot(a, b, preferred_element_type=F32)


def _dot_nt(a, b):
    return lax.dot_general(a, b, (((1,), (1,)), ((), ())), preferred_element_type=F32)


def _rms(x, g, n=None):
    n = x.shape[-1] if n is None else n
    ms = jnp.sum(x * x, axis=-1, keepdims=True) * (1.0 / n)
    return x * lax.rsqrt(ms + EPS) * g


def _mod_kernel(c_ref, w_ref, b_ref, o_ref):
    c = c_ref[...]
    a = (c * jax.nn.sigmoid(c)).astype(BF16)
    o_ref[0] = _dot(a, w_ref[0].astype(BF16)) + b_ref[0]


def _modulation(cvec, w_ada, b_ada):
    nl, d, nm = w_ada.shape
    rows = cvec.shape[0]
    tn = _pick(nm, (1152, 1024, 512, 256, 128))
    return pl.pallas_call(
        _mod_kernel,
        grid=(nl, nm // tn),
        in_specs=[
            pl.BlockSpec((rows, d), lambda l, n: (0, 0)),
            pl.BlockSpec((1, d, tn), lambda l, n: (l, 0, n)),
            pl.BlockSpec((1, 1, tn), lambda l, n: (l, 0, n)),
        ],
        out_specs=pl.BlockSpec((1, rows, tn), lambda l, n: (l, 0, n)),
        out_shape=jax.ShapeDtypeStruct((nl, rows, nm), F32),
        compiler_params=_params(("parallel", "parallel")),
        name="adaln_mod",
    )(cvec, w_ada, b_ada.reshape(nl, 1, nm))


def _ffn_kernel(x_ref, mod_ref, gpre_ref, gpost_ref, w1_ref, w3_ref, w2_ref, o_ref, h_scr, acc_scr, *, j, nf):
    f = pl.program_id(2)

    @pl.when(f == 0)
    def _():
        x = x_ref[0]
        shift = mod_ref[0, pl.ds(3 * j, 1), :]
        scale = mod_ref[0, pl.ds(3 * j + 1, 1), :]
        h = _rms(x, gpre_ref[...]) * (1.0 + scale) + shift
        h_scr[...] = h.astype(BF16)
        acc_scr[...] = jnp.zeros_like(acc_scr)

    h = h_scr[...]
    a = _dot(h, w1_ref[...])
    b = _dot(h, w3_ref[...])
    t = (a * jax.nn.sigmoid(a) * b).astype(BF16)
    acc_scr[...] += _dot(t, w2_ref[...])

    @pl.when(f == nf - 1)
    def _():
        gate = mod_ref[0, pl.ds(3 * j + 2, 1), :]
        y = _rms(acc_scr[...], gpost_ref[...])
        o_ref[0] = x_ref[0] + FFN_RES_WEIGHT * gate * y


def _ffn(x, mod, j, g_pre, g_post, w1, w3, w2):
    nb, s, d = x.shape
    dff = w1.shape[1]
    tm = _pick(s, (1024, 512, 256))
    tf = _pick(dff, (256, 128))
    nf = dff // tf
    return pl.pallas_call(
        functools.partial(_ffn_kernel, j=j, nf=nf),
        grid=(nb, s // tm, nf),
        in_specs=[
            pl.BlockSpec((1, tm, d), lambda b, i, f: (b, i, 0)),
            pl.BlockSpec((1, N_MOD, d), lambda b, i, f: (b, 0, 0)),
            pl.BlockSpec((1, d), lambda b, i, f: (0, 0)),
            pl.BlockSpec((1, d), lambda b, i, f: (0, 0)),
            pl.BlockSpec((d, tf), lambda b, i, f: (0, f)),
            pl.BlockSpec((d, tf), lambda b, i, f: (0, f)),
            pl.BlockSpec((tf, d), lambda b, i, f: (f, 0)),
        ],
        out_specs=pl.BlockSpec((1, tm, d), lambda b, i, f: (b, i, 0)),
        out_shape=jax.ShapeDtypeStruct(x.shape, F32),
        scratch_shapes=[pltpu.VMEM((tm, d), BF16), pltpu.VMEM((tm, d), F32)],
        compiler_params=_params(("parallel", "parallel", "arbitrary")),
        name="ffn_sublayer",
    )(x, mod, g_pre.reshape(1, d), g_post.reshape(1, d), w1, w3, w2)


def _rope(x, cos, sin_a, sin_b, quarter):
    w = x.shape[-1]
    return x * cos + pltpu.roll(x, w - quarter, 1) * sin_a + pltpu.roll(x, quarter, 1) * sin_b


def _inproj_kernel(x_ref, mod_ref, gpre_ref, w_ref, rb_ref, rd_ref, qn_ref, wuq_ref, kvn_ref, wuk_ref, wuv_ref,
                   lb_ref, gqn_ref, gkn_ref,
                   u_ref, u2_ref, bq_ref, bk_ref, bv_ref, hq_ref, hv_ref, hk_ref, hl_ref, hg_ref,
                   dq_ref, dk_ref, dv_ref, h_scr, u_scr, *, tm):
    x = x_ref[0]
    shift = mod_ref[0, pl.ds(3, 1), :]
    scale = mod_ref[0, pl.ds(4, 1), :]
    h_scr[...] = (_rms(x, gpre_ref[...]) * (1.0 + scale) + shift).astype(BF16)

    def proj(lo, width):
        return _dot(h_scr[...], w_ref[:, lo:lo + width])

    u = proj(SLOT_S5, A_WIDTH)
    u_ref[0] = u.astype(BF16)
    for hf in range(A_WIDTH // LANES):
        u_scr[hf] = u[:, hf * LANES:(hf + 1) * LANES]
    for r in range(S5_R):
        for hf in range(A_WIDTH // LANES):
            lo = r * A_WIDTH + hf * LANES
            u2_ref[0, :, lo:lo + LANES] = u_scr[hf, pl.ds(r, tm // S5_R, stride=S5_R), :].astype(BF16)

    cos_b, sa_b, sb_b = rb_ref[0], rb_ref[1], rb_ref[2]
    cq = _rms(proj(SLOT_CQ, 256), qn_ref[...], n=B_Q_LORA).astype(BF16)
    q = _dot(cq, wuq_ref[...])
    b_scale = (B_NOPE + B_ROPE) ** -0.5
    for hd in range(B_HEADS):
        qh = _rope(q[:, hd * LANES:(hd + 1) * LANES], cos_b, sa_b, sb_b, B_ROPE // 4)
        bq_ref[0, hd] = (qh * b_scale).astype(BF16)
    ckv = _rms(proj(SLOT_CKV, B_KV_LORA), kvn_ref[...]).astype(BF16)
    kn = _dot(ckv, wuk_ref[...])
    kr = _rope(proj(SLOT_KR, LANES), cos_b, sa_b, sb_b, B_ROPE // 4)
    for hd in range(B_HEADS):
        bk_ref[0, hd] = (kn[:, hd * LANES:(hd + 1) * LANES] + kr).astype(BF16)
        bv_ref[0, hd] = _dot(ckv, wuv_ref[hd]).astype(BF16)

    hq_ref[0] = proj(SLOT_HQ, HG_W).astype(BF16)
    hv_ref[0] = proj(SLOT_HV, HG_W).astype(BF16)
    hg_ref[0] = proj(SLOT_HG, HG_W).astype(BF16)
    for dr, slot in enumerate((SLOT_HF, SLOT_HB)):
        z = proj(slot, HG_W)
        lb = lb_ref[pl.ds(dr, 1), :]
        f = lb + (1.0 - lb) * jax.nn.sigmoid(z)
        hl_ref[dr, 0] = jnp.log(jnp.maximum(f, F_FLOOR))
        hk_ref[dr, 0] = ((1.0 - lb) * jax.nn.sigmoid(-z)).astype(BF16)

    cos_d, sa_d, sb_d = rd_ref[0], rd_ref[1], rd_ref[2]
    d_scale = D_HEAD ** -0.5
    for hd in range(D_HEADS):
        qh = _rms(proj(SLOT_GQ + hd * LANES, LANES), gqn_ref[...], n=D_HEAD)
        qh = _rope(qh, cos_d, sa_d, sb_d, D_HEAD // 4) * d_scale
        dq_ref[0, hd] = qh[:, :D_HEAD].astype(BF16)
    for hd in range(D_KV_HEADS):
        kh = _rms(proj(SLOT_GK + hd * LANES, LANES), gkn_ref[...], n=D_HEAD)
        kh = _rope(kh, cos_d, sa_d, sb_d, D_HEAD // 4)
        dk_ref[0, hd] = kh[:, :D_HEAD].astype(BF16)
        dv_ref[0, hd] = proj(SLOT_GV + hd * LANES, LANES)[:, :D_HEAD].astype(BF16)


def _inproj(x, mod, g_pre, wp, rope_b, rope_d, qn, wuq, kvn, wuk, wuv, lb, gqn, gkn):
    nb, s, d = x.shape
    tm = _pick(s, (512, 256))
    const2 = lambda b, i: (0, 0)
    const3 = lambda b, i: (0, 0, 0)
    tok = lambda w: pl.BlockSpec((1, tm, w), lambda b, i: (b, i, 0))
    headed = lambda nh, w: pl.BlockSpec((1, nh, tm, w), lambda b, i: (b, 0, i, 0))
    dirtok = lambda w: pl.BlockSpec((2, 1, tm, w), lambda b, i: (0, b, i, 0))
    sd = jax.ShapeDtypeStruct
    outs = [
        (sd((nb, s, A_WIDTH), BF16), tok(A_WIDTH)),
        (sd((nb, s // S5_R, S5_LANES), BF16), pl.BlockSpec((1, tm // S5_R, S5_LANES), lambda b, i: (b, i, 0))),
        (sd((nb, B_HEADS, s, LANES), BF16), headed(B_HEADS, LANES)),
        (sd((nb, B_HEADS, s, LANES), BF16), headed(B_HEADS, LANES)),
        (sd((nb, B_HEADS, s, B_V), BF16), headed(B_HEADS, B_V)),
        (sd((nb, s, HG_W), BF16), tok(HG_W)),
        (sd((nb, s, HG_W), BF16), tok(HG_W)),
        (sd((2, nb, s, HG_W), BF16), dirtok(HG_W)),
        (sd((2, nb, s, HG_W), F32), dirtok(HG_W)),
        (sd((nb, s, HG_W), BF16), tok(HG_W)),
        (sd((nb, D_HEADS, s, D_HEAD), BF16), headed(D_HEADS, D_HEAD)),
        (sd((nb, D_KV_HEADS, s, D_HEAD), BF16), headed(D_KV_HEADS, D_HEAD)),
        (sd((nb, D_KV_HEADS, s, D_HEAD), BF16), headed(D_KV_HEADS, D_HEAD)),
    ]
    return pl.pallas_call(
        functools.partial(_inproj_kernel, tm=tm),
        grid=(nb, s // tm),
        in_specs=[
            pl.BlockSpec((1, tm, d), lambda b, i: (b, i, 0)),
            pl.BlockSpec((1, N_MOD, d), lambda b, i: (b, 0, 0)),
            pl.BlockSpec((1, d), const2),
            pl.BlockSpec((d, N_PROJ), const2),
            pl.BlockSpec((3, tm, LANES), lambda b, i: (0, i, 0)),
            pl.BlockSpec((3, tm, LANES), lambda b, i: (0, i, 0)),
            pl.BlockSpec((1, 256), const2),
            pl.BlockSpec((256, B_HEADS * LANES), const2),
            pl.BlockSpec((1, B_KV_LORA), const2),
            pl.BlockSpec((B_KV_LORA, B_HEADS * LANES), const2),
            pl.BlockSpec((B_HEADS, B_KV_LORA, B_V), const3),
            pl.BlockSpec((2, HG_W), const2),
            pl.BlockSpec((1, LANES), const2),
            pl.BlockSpec((1, LANES), const2),
        ],
        out_specs=[o[1] for o in outs],
        out_shape=[o[0] for o in outs],
        scratch_shapes=[pltpu.VMEM((tm, d), BF16), pltpu.VMEM((A_WIDTH // LANES, tm, LANES), F32)],
        compiler_params=_params(("parallel", "parallel")),
        name="mixer_inproj",
    )(x, mod, g_pre.reshape(1, d), wp, rope_b, rope_d, qn, wuq, kvn, wuk, wuv, lb, gqn, gkn)


def _s5_kernel(u2_ref, x0_ref, tb_ref, wst_ref, wout_ref, pq_ref, y_ref, xf_ref, carry, *, rows, reverse, nlev):
    i = pl.program_id(1)

    @pl.when(i == 0)
    def _():
        carry[...] = x0_ref[0]

    u2 = u2_ref[0]
    sloc = _dot(u2, wst_ref[...])
    ridx = lax.broadcasted_iota(jnp.int32, (rows, 1), 0)

    def cmul(xv, lev):
        p = pq_ref[pl.ds(2 * lev, 1), :]
        q = pq_ref[pl.ds(2 * lev + 1, 1), :]
        return xv * p + pltpu.roll(xv, S5_NSTATE, 1) * q

    if reverse:
        e = jnp.where(ridx == rows - 1, carry[...], pltpu.roll(sloc, rows - 1, 0))
    else:
        e = jnp.where(ridx == 0, carry[...], pltpu.roll(sloc, 1, 0))
    xin = e
    for lev in range(nlev):
        dist = 1 << lev
        if reverse:
            sh = jnp.where(ridx < rows - dist, pltpu.roll(xin, rows - dist, 0), 0.0)
        else:
            sh = jnp.where(ridx >= dist, pltpu.roll(xin, dist, 0), 0.0)
        xin = xin + cmul(sh, lev)

    last = 0 if reverse else rows - 1
    nxt = cmul(xin[last:last + 1, :], 0) + sloc[last:last + 1, :]
    carry[...] = nxt
    xf_ref[0] = nxt

    y2 = _dot(u2, tb_ref[...]) + _dot(xin.astype(BF16), wout_ref[...])
    for r in range(S5_R):
        for hf in range(A_WIDTH // LANES):
            lo = r * A_WIDTH + hf * LANES
            y_ref[0, hf, pl.ds(r, rows, stride=S5_R), :] = y2[:, lo:lo + LANES]


def _s5_scan(u2, x0, wts, reverse):
    tb, wst, wout, pq = wts
    nb, n2, _ = u2.shape
    rows = _pick(n2, (512, 256, 128, 64))
    nt = n2 // rows
    nlev = max(1, (rows - 1).bit_length())
    order = (lambda b, i: (b, nt - 1 - i, 0)) if reverse else (lambda b, i: (b, i, 0))
    order_out = (lambda b, i: (b, 0, nt - 1 - i, 0)) if reverse else (lambda b, i: (b, 0, i, 0))
    const2 = lambda b, i: (0, 0)
    y, xf = pl.pallas_call(
        functools.partial(_s5_kernel, rows=rows, reverse=reverse, nlev=nlev),
        grid=(nb, nt),
        in_specs=[
            pl.BlockSpec((1, rows, S5_LANES), order),
            pl.BlockSpec((1, 1, 2 * S5_NSTATE), lambda b, i: (b, 0, 0)),
            pl.BlockSpec((S5_LANES, S5_LANES), const2),
            pl.BlockSpec((S5_LANES, 2 * S5_NSTATE), const2),
            pl.BlockSpec((2 * S5_NSTATE, S5_LANES), const2),
            pl.BlockSpec(pq.shape, const2),
        ],
        out_specs=[
            pl.BlockSpec((1, A_WIDTH // LANES, rows * S5_R, LANES), order_out),
            pl.BlockSpec((1, 1, 2 * S5_NSTATE), lambda b, i: (b, 0, 0)),
        ],
        out_shape=[
            jax.ShapeDtypeStruct((nb, A_WIDTH // LANES, n2 * S5_R, LANES), F32),
            jax.ShapeDtypeStruct((nb, 1, 2 * S5_NSTATE), F32),
        ],
        scratch_shapes=[pltpu.VMEM((1, 2 * S5_NSTATE), F32)],
        compiler_params=_params(("parallel", "arbitrary")),
        name="s5_scan_rev" if reverse else "s5_scan_fwd",
    )(u2, x0, tb, wst, wout, pq)
    return y, xf


def _s5_weights(lam_re, lam_im, log_dt, b_re, b_im, c_re, c_im, reverse, nlev):
    g, n, r = A_GROUPS, A_STATE, S5_R
    lam_re = jnp.minimum(lam_re.astype(F32), -1e-4)
    lam_im = lam_im.astype(F32)
    dt = jnp.exp(log_dt.astype(F32))[:, None]
    mag = jnp.exp(lam_re * dt)
    a_re = mag * jnp.cos(lam_im * dt)
    a_im = mag * jnp.sin(lam_im * dt)
    den = lam_re * lam_re + lam_im * lam_im
    num_re = a_re - 1.0
    f_re = (num_re * lam_re + a_im * lam_im) / den
    f_im = (a_im * lam_re - num_re * lam_im) / den
    bb_re = f_re[..., None] * b_re - f_im[..., None] * b_im
    bb_im = f_re[..., None] * b_im + f_im[..., None] * b_re

    def cm(xr, xi, yr, yi):
        return xr * yr - xi * yi, xr * yi + xi * yr

    pr, pi = [jnp.ones_like(a_re)], [jnp.zeros_like(a_im)]
    for _ in range(r):
        nr, ni = cm(pr[-1], pi[-1], a_re, a_im)
        pr.append(nr)
        pi.append(ni)
    pr, pi = jnp.stack(pr), jnp.stack(pi)

    def lag(tau):
        xr, xi = cm(pr[tau][..., None], pi[tau][..., None], bb_re, bb_im)
        return jnp.einsum('gon,gnc->goc', c_re, xr) - jnp.einsum('gon,gnc->goc', c_im, xi)

    eye_g = jnp.eye(g, dtype=F32)
    zero_blk = jnp.zeros((g, A_GROUP, g, A_GROUP), F32)

    def group_diag(m):
        return jnp.einsum('gio,gh->giho', m, eye_g).reshape(g * A_GROUP, g * A_GROUP)

    blocks = []
    for r_in in range(r):
        row = []
        for r_out in range(r):
            tau = (r_in - r_out) if reverse else (r_out - r_in)
            if tau < 0:
                row.append(zero_blk.reshape(g * A_GROUP, g * A_GROUP))
            else:
                row.append(group_diag(jnp.swapaxes(lag(tau), 1, 2)))
        blocks.append(jnp.concatenate(row, axis=1))
    tb = jnp.concatenate(blocks, axis=0)

    wst_rows = []
    for r_in in range(r):
        steps = r_in if reverse else (r - 1 - r_in)
        xr, xi = cm(pr[steps][..., None], pi[steps][..., None], bb_re, bb_im)
        wr = jnp.einsum('gnc,gh->gchn', xr, eye_g).reshape(g * A_GROUP, g * n)
        wi = jnp.einsum('gnc,gh->gchn', xi, eye_g).reshape(g * A_GROUP, g * n)
        wst_rows.append(jnp.concatenate([wr, wi], axis=1))
    wst = jnp.concatenate(wst_rows, axis=0)

    wout_cols = []
    for r_out in range(r):
        steps = (r - r_out) if reverse else (r_out + 1)
        yr, yi = cm(c_re, c_im, pr[steps][:, None, :], pi[steps][:, None, :])
        wr = jnp.einsum('gon,gh->gnho', yr, eye_g).reshape(g * n, g * A_GROUP)
        wi = jnp.einsum('gon,gh->gnho', -yi, eye_g).reshape(g * n, g * A_GROUP)
        wout_cols.append(jnp.concatenate([wr, wi], axis=0))
    wout = jnp.concatenate(wout_cols, axis=1)

    lr, li = pr[r].reshape(1, g * n), pi[r].reshape(1, g * n)
    rows = []
    for _ in range(nlev):
        rows.append(jnp.concatenate([lr, lr], axis=1))
        rows.append(jnp.concatenate([-li, li], axis=1))
        lr, li = cm(lr, li, lr, li)
    pq = jnp.concatenate(rows, axis=0)
    return tb.astype(BF16), wst.astype(BF16), wout.astype(BF16), pq


def _split3(x):
    x1 = x.astype(BF16)
    r1 = x - x1.astype(F32)
    x2 = r1.astype(BF16)
    x3 = (r1 - x2.astype(F32)).astype(BF16)
    return x1, x2, x3


def _hgrn_kernel(q_ref, k_ref, v_ref, lf_ref, s0_ref, tri_ref, sel_ref, sgn_ref, msk_ref, hm_ref, bd_ref,
                 o_ref, sf_ref, st_scr, *, nlev):
    c = pl.program_id(2)

    @pl.when(c == 0)
    def _():
        st_scr[...] = s0_ref[0, 0]

    q = q_ref[0].astype(F32)
    k = k_ref[0, 0].astype(F32)
    vb = v_ref[0]
    lf = lf_ref[0, 0]
    tri = tri_ref[0]

    l1, l2, l3 = _split3(lf)
    cum = _dot(tri, l1) + _dot(tri, l2) + _dot(tri, l3)
    tot = jnp.sum(lf, axis=0, keepdims=True)

    mids = _dot(sel_ref[0], cum.astype(BF16))
    t = q.shape[0]
    kb = k.astype(BF16)
    qb = q.astype(BF16)
    heads = [hm_ref[pl.ds(hd, 1), :] > 0.5 for hd in range(C_HEADS)]

    att = [jnp.where(msk_ref[0, nlev] > 0.5, _dot_nt(qb, jnp.where(heads[hd], kb, jnp.zeros_like(kb))), 0.0)
           for hd in range(C_HEADS)]
    for lev in range(nlev):
        e = jnp.exp(sgn_ref[0, lev] * (cum - mids[lev * t:(lev + 1) * t, :]))
        qs = (q * e).astype(BF16)
        ks = (k * e).astype(BF16)
        m = msk_ref[0, lev] > 0.5
        for hd in range(C_HEADS):
            s = _dot_nt(qs, jnp.where(heads[hd], ks, jnp.zeros_like(ks)))
            att[hd] = att[hd] + jnp.where(m, s, 0.0)

    st = st_scr[...]
    o = _dot_nt((q * jnp.exp(cum)).astype(BF16), st.astype(BF16))
    for hd in range(C_HEADS):
        o = o + _dot(att[hd].astype(BF16), jnp.where(heads[hd], vb, jnp.zeros_like(vb)))
    o_ref[0, 0] = o

    kend = (k * jnp.exp(tot - cum)).astype(BF16)
    vt = vb.astype(F32).T.astype(BF16)
    new = st * jnp.exp(tot) + jnp.where(bd_ref[...] > 0.5, _dot(vt, kend), 0.0)
    st_scr[...] = new
    sf_ref[0, 0] = new


def _hgrn_consts():
    t = HG_T
    ti = jnp.arange(t)[:, None]
    si = jnp.arange(t)[None, :]
    tri, sel, sgn, msk = [], [], [], []
    for reverse in (False, True):
        tri.append((si >= ti) if reverse else (si <= ti))
        sels, sgns, msks = [], [], []
        for lev in range(HG_LEVELS):
            h = 1 << lev
            blk_t, blk_s = ti // (2 * h), si // (2 * h)
            hi_t, hi_s = (ti % (2 * h)) >= h, (si % (2 * h)) >= h
            if reverse:
                mid = blk_t * 2 * h + h
                q_role_t, k_role_s = ~hi_t, hi_s
            else:
                mid = blk_t * 2 * h + h - 1
                q_role_t, k_role_s = hi_t, ~hi_s
            sels.append(si == mid)
            sgns.append(jnp.where(q_role_t, 1.0, -1.0))
            msks.append((blk_t == blk_s) & q_role_t & k_role_s)
        msks.append(ti == si)
        sel.append(jnp.concatenate(sels, axis=0))
        sgn.append(jnp.stack(sgns))
        msk.append(jnp.stack(msks))
    lane_head = jnp.arange(HG_W) // C_DK
    hm = (lane_head[None, :] == jnp.arange(C_HEADS)[:, None]).astype(F32)
    bd = (lane_head[:, None] == lane_head[None, :]).astype(F32)
    return (jnp.stack(tri).astype(BF16), jnp.stack(sel).astype(BF16), jnp.stack(sgn).astype(F32),
            jnp.stack(msk).astype(F32), hm, bd)


def _hgrn_scan(q, k, v, lf, s0, consts):
    tri, sel, sgn, msk, hm, bd = consts
    nb, s, w = q.shape
    t = HG_T
    nc = s // t
    chunk = lambda d, b, c: jnp.where(d == 0, c, nc - 1 - c)
    const2 = lambda d, b, c: (0, 0)
    o, sf = pl.pallas_call(
        functools.partial(_hgrn_kernel, nlev=HG_LEVELS),
        grid=(2, nb, nc),
        in_specs=[
            pl.BlockSpec((1, t, w), lambda d, b, c: (b, chunk(d, b, c), 0)),
            pl.BlockSpec((1, 1, t, w), lambda d, b, c: (d, b, chunk(d, b, c), 0)),
            pl.BlockSpec((1, t, w), lambda d, b, c: (b, chunk(d, b, c), 0)),
            pl.BlockSpec((1, 1, t, w), lambda d, b, c: (d, b, chunk(d, b, c), 0)),
            pl.BlockSpec((1, 1, w, w), lambda d, b, c: (d, b, 0, 0)),
            pl.BlockSpec((1, t, t), lambda d, b, c: (d, 0, 0)),
            pl.BlockSpec((1, HG_LEVELS * t, t), lambda d, b, c: (d, 0, 0)),
            pl.BlockSpec((1, HG_LEVELS, t, 1), lambda d, b, c: (d, 0, 0, 0)),
            pl.BlockSpec((1, HG_LEVELS + 1, t, t), lambda d, b, c: (d, 0, 0, 0)),
            pl.BlockSpec((C_HEADS, w), const2),
            pl.BlockSpec((w, w), const2),
        ],
        out_specs=[
            pl.BlockSpec((1, 1, t, w), lambda d, b, c: (d, b, chunk(d, b, c), 0)),
            pl.BlockSpec((1, 1, w, w), lambda d, b, c: (d, b, 0, 0)),
        ],
        out_shape=[
            jax.ShapeDtypeStruct((2, nb, s, w), F32),
            jax.ShapeDtypeStruct((2, nb, w, w), F32),
        ],
        scratch_shapes=[pltpu.VMEM((w, w), F32)],
        compiler_params=_params(("parallel", "parallel", "arbitrary")),
        name="hgrn_scan",
    )(q, k, v, lf, s0, tri, sel, sgn.reshape(2, HG_LEVELS, t, 1), msk, hm, bd)
    return o, sf


def _attn_kernel(*refs, nseg, tks, g, tq):
    q_ref = refs[0]
    kv_refs = refs[1:1 + 2 * nseg]
    o_ref = refs[1 + 2 * nseg]
    dk = q_ref.shape[-1]
    dv = o_ref.shape[-1]
    q = q_ref[0].reshape(g * tq, dk)

    def step(k, v, carry):
        m, l, acc = carry
        s = _dot_nt(q, k)
        m_new = jnp.maximum(m, jnp.max(s, axis=-1, keepdims=True))
        alpha = jnp.exp(m - m_new)
        p = jnp.exp(s - m_new)
        l = alpha * l + jnp.sum(p, axis=-1, keepdims=True)
        acc = alpha * acc + _dot(p.astype(BF16), v)
        return m_new, l, acc

    carry = (jnp.full((g * tq, 1), -jnp.inf, F32), jnp.zeros((g * tq, 1), F32), jnp.zeros((g * tq, dv), F32))
    for seg in range(nseg):
        k_ref, v_ref = kv_refs[2 * seg], kv_refs[2 * seg + 1]
        tk = tks[seg]
        nk = k_ref.shape[2] // tk
        if nk == 1:
            carry = step(k_ref[0, 0], v_ref[0, 0], carry)
        else:
            def body(j, c, k_ref=k_ref, v_ref=v_ref, tk=tk):
                off = pl.multiple_of(j * tk, tk)
                return step(k_ref[0, 0, pl.ds(off, tk), :], v_ref[0, 0, pl.ds(off, tk), :], c)
            carry = lax.fori_loop(0, nk, body, carry)
    m, l, acc = carry
    o_ref[0] = (acc / l).reshape(g, tq, dv).astype(o_ref.dtype)


def _attention(q, kvs):
    nb, hq, sq, dk = q.shape
    hkv = kvs[0][0].shape[1]
    dv = kvs[0][1].shape[-1]
    g = hq // hkv
    tq = _pick(sq, (512, 256, 128))
    tks = tuple(_pick(k.shape[2], (512, 256, 128)) for k, _ in kvs)
    in_specs = [pl.BlockSpec((1, g, tq, dk), lambda b, h, i: (b, h, i, 0))]
    args = [q]
    for k, v in kvs:
        sk = k.shape[2]
        in_specs.append(pl.BlockSpec((1, 1, sk, dk), lambda b, h, i: (b, h, 0, 0)))
        in_specs.append(pl.BlockSpec((1, 1, sk, dv), lambda b, h, i: (b, h, 0, 0)))
        args += [k, v]
    return pl.pallas_call(
        functools.partial(_attn_kernel, nseg=len(kvs), tks=tks, g=g, tq=tq),
        grid=(nb, hkv, sq // tq),
        in_specs=in_specs,
        out_specs=pl.BlockSpec((1, g, tq, dv), lambda b, h, i: (b, h, i, 0)),
        out_shape=jax.ShapeDtypeStruct((nb, hq, sq, dv), BF16),
        compiler_params=_params(("parallel", "parallel", "arbitrary")),
        name="attention",
    )(*args)


def _merge_kernel(x_ref, mod_ref, gpre_ref, gpost_ref, wg_ref, wb_ref, wo_ref,
                  u_ref, yf_ref, yr_ref, sd_ref, wglu_ref, bo_ref, ho_ref, hg_ref, hn_ref, avg_ref, do_ref,
                  o_ref):
    x = x_ref[0]
    shift = mod_ref[0, pl.ds(3, 1), :]
    scale = mod_ref[0, pl.ds(4, 1), :]
    gate = mod_ref[0, pl.ds(5, 1), :]
    h = (_rms(x, gpre_ref[...]) * (1.0 + scale) + shift).astype(BF16)
    d = x.shape[-1]

    def branch_gate(i):
        return jax.nn.sigmoid(_dot(h, wg_ref[:, i * d:(i + 1) * d]))

    ysum = yf_ref[0] + yr_ref[0]
    y = sd_ref[...] * u_ref[0].astype(F32) + jnp.concatenate([ysum[hf] for hf in range(A_WIDTH // LANES)], axis=-1)
    ge = jax.nn.gelu(y)
    ya = ge * jax.nn.sigmoid(_dot(ge.astype(BF16), wglu_ref[...]))
    merged = branch_gate(0) * _dot(ya.astype(BF16), wb_ref[0])

    yb = _dot(bo_ref[0, 0], wb_ref[1, 0:B_V, :])
    for hd in range(1, B_HEADS):
        yb = yb + _dot(bo_ref[0, hd], wb_ref[1, hd * B_V:(hd + 1) * B_V, :])
    merged = merged + branch_gate(1) * yb

    o2 = ho_ref[0, 0] + ho_ref[1, 0]
    ms = _dot((o2 * o2).astype(BF16), avg_ref[...])
    gz = hg_ref[0].astype(F32)
    yc = o2 * lax.rsqrt(ms + EPS) * hn_ref[...] * (gz * jax.nn.sigmoid(gz))
    merged = merged + branch_gate(2) * _dot(yc.astype(BF16), wb_ref[2])

    yd = _dot(do_ref[0, 0], wb_ref[3, 0:D_HEAD, :])
    for hd in range(1, D_HEADS):
        yd = yd + _dot(do_ref[0, hd], wb_ref[3, hd * D_HEAD:(hd + 1) * D_HEAD, :])
    merged = merged + branch_gate(3) * yd

    yo = _dot(merged.astype(BF16), wo_ref[...])
    o_ref[0] = x + gate * _rms(yo, gpost_ref[...])


def _merge(x, mod, g_pre, g_post, wg, wb, wo, u, yf, yr, s5d, wglu, bo, ho, hg, hn, avg, do):
    nb, s, d = x.shape
    tm = _pick(s, (512, 256))
    const2 = lambda b, i: (0, 0)
    const3 = lambda b, i: (0, 0, 0)
    tok = lambda w: pl.BlockSpec((1, tm, w), lambda b, i: (b, i, 0))
    halves = pl.BlockSpec((1, A_WIDTH // LANES, tm, LANES), lambda b, i: (b, 0, i, 0))
    return pl.pallas_call(
        _merge_kernel,
        grid=(nb, s // tm),
        in_specs=[
            tok(d),
            pl.BlockSpec((1, N_MOD, d), lambda b, i: (b, 0, 0)),
            pl.BlockSpec((1, d), const2),
            pl.BlockSpec((1, d), const2),
            pl.BlockSpec((d, N_BRANCH * d), const2),
            pl.BlockSpec((N_BRANCH, BRANCH_W, d), const3),
            pl.BlockSpec((d, d), const2),
            tok(A_WIDTH), halves, halves,
            pl.BlockSpec((1, A_WIDTH), const2),
            pl.BlockSpec((A_WIDTH, A_WIDTH), const2),
            pl.BlockSpec((1, B_HEADS, tm, B_V), lambda b, i: (b, 0, i, 0)),
            pl.BlockSpec((2, 1, tm, HG_W), lambda b, i: (0, b, i, 0)),
            tok(HG_W),
            pl.BlockSpec((1, HG_W), const2),
            pl.BlockSpec((HG_W, HG_W), const2),
            pl.BlockSpec((1, D_HEADS, tm, D_HEAD), lambda b, i: (b, 0, i, 0)),
        ],
        out_specs=tok(d),
        out_shape=jax.ShapeDtypeStruct(x.shape, F32),
        compiler_params=_params(("parallel", "parallel")),
        name="merge_out",
    )(x, mod, g_pre.reshape(1, d), g_post.reshape(1, d), wg, wb, wo, u, yf, yr, s5d, wglu, bo, ho, hg, hn, avg, do)


def _pad_cols(w, width):
    return jnp.pad(w, ((0, 0), (0, width - w.shape[1])))


def _proj_weight(w_in_mix):
    offs = [0]
    for wdt in (A_WIDTH, B_Q_LORA, B_KV_LORA, B_ROPE, HG_W, HG_W, HG_W, HG_W, HG_W,
                D_HEADS * D_HEAD, D_KV_HEADS * D_HEAD, D_KV_HEADS * D_HEAD):
        offs.append(offs[-1] + wdt)
    p = [w_in_mix[:, offs[i]:offs[i + 1]] for i in range(12)]
    d = w_in_mix.shape[0]
    z = lambda n: jnp.zeros((d, n), w_in_mix.dtype)
    cols = [p[0], _pad_cols(p[1], 256), p[2],
            jnp.concatenate([z(B_NOPE), p[3], z(LANES - B_NOPE - B_ROPE)], axis=1),
            p[4], p[5], p[6], p[7], p[8]]
    for i, nh in ((9, D_HEADS), (10, D_KV_HEADS), (11, D_KV_HEADS)):
        for hd in range(nh):
            cols.append(_pad_cols(p[i][:, hd * D_HEAD:(hd + 1) * D_HEAD], LANES))
    return jnp.concatenate(cols, axis=1).astype(BF16)


def _rope_tables(n_tok, rot_dim, lane_off, identity):
    cos = jnp.ones((n_tok, LANES), F32)
    sin_a = jnp.zeros((n_tok, LANES), F32)
    sin_b = jnp.zeros((n_tok, LANES), F32)
    if not identity:
        n_rows = n_tok // GRID_W
        rows = jnp.repeat(jnp.arange(n_rows, dtype=F32), GRID_W)
        cols = jnp.tile(jnp.arange(GRID_W, dtype=F32), n_rows)
        half = rot_dim // 2
        inv = ROPE_THETA ** (-jnp.arange(0, half, 2, dtype=F32) / half)
        ang_r = rows[:, None] * inv
        ang_c = cols[:, None] * inv
        ang = jnp.concatenate([ang_r, ang_r, ang_c, ang_c], axis=-1)
        c, s = jnp.cos(ang), jnp.sin(ang)
        quarter = rot_dim // 4
        first = (jnp.arange(rot_dim) % (2 * quarter)) < quarter
        cos = cos.at[:, lane_off:lane_off + rot_dim].set(c)
        sin_a = sin_a.at[:, lane_off:lane_off + rot_dim].set(jnp.where(first, -s, 0.0))
        sin_b = sin_b.at[:, lane_off:lane_off + rot_dim].set(jnp.where(first, 0.0, s))
    return jnp.stack([cos, sin_a, sin_b])


def kernel(x, c, ctx, c_ctx, w_ada, b_ada, norm_pre, norm_post, ffn_w1, ffn_w3, ffn_w2, w_in,
           s5_lambda_re, s5_lambda_im, s5_log_dt, s5_b_re, s5_b_im, s5_c_re, s5_c_im, s5_d, s5_w_glu,
           mla_q_norm, mla_w_uq, mla_kv_norm, mla_w_ukv, hgrn_lb_raw, hgrn_o_norm,
           gqa_q_norm, gqa_k_norm, w_branch, w_out):
    nb, seq, d = x.shape
    n_ctx = ctx.shape[1]
    depth = w_ada.shape[0]
    n_mix = N_PROJ

    rows = max(8, -(-(nb + 1) // 8) * 8)
    cvec = jnp.zeros((rows, d), F32).at[:nb].set(c).at[nb].set(c_ctx)
    mod_all = _modulation(cvec, w_ada, b_ada).reshape(depth, rows, N_MOD, d)

    lb_step = jax.nn.softmax(hgrn_lb_raw.astype(F32), axis=1)
    lb_all = jnp.clip(jnp.cumsum(lb_step, axis=1) - lb_step[:, :1], 0.0, 1.0)

    rope_b_lat = _rope_tables(seq, B_ROPE, B_NOPE, False)
    rope_d_lat = _rope_tables(seq, D_HEAD, 0, False)
    rope_b_ctx = _rope_tables(n_ctx, B_ROPE, B_NOPE, True)
    rope_d_ctx = _rope_tables(n_ctx, D_HEAD, 0, True)
    hg_consts = _hgrn_consts()
    lane_head = jnp.arange(HG_W) // C_DV
    avg = (lane_head[:, None] == lane_head[None, :]).astype(BF16) * (1.0 / C_DV)
    n_mixcols = w_in.shape[-1] - N_BRANCH * d

    x_lat, x_ctx = x, ctx
    for layer in range(depth):
        last = layer == depth - 1
        mod_lat = mod_all[layer, :nb]
        mod_ctx = jnp.broadcast_to(mod_all[layer, nb:nb + 1], (nb, N_MOD, d))
        bf = lambda w: w.astype(BF16)
        ffn_a = (norm_pre[layer, 0], norm_post[layer, 0], bf(ffn_w1[layer, 0]), bf(ffn_w3[layer, 0]), bf(ffn_w2[layer, 0]))
        ffn_b = (norm_pre[layer, 2], norm_post[layer, 2], bf(ffn_w1[layer, 1]), bf(ffn_w3[layer, 1]), bf(ffn_w2[layer, 1]))

        x_lat = _ffn(x_lat, mod_lat, 0, *ffn_a)
        x_ctx = _ffn(x_ctx, mod_ctx, 0, *ffn_a)

        wp = _proj_weight(w_in[layer, :, :n_mixcols])
        wg = bf(w_in[layer, :, n_mixcols:])
        qn = _pad_cols(mla_q_norm[layer].reshape(1, B_Q_LORA), 256)
        wuq = mla_w_uq[layer].reshape(B_Q_LORA, B_HEADS, B_NOPE + B_ROPE)
        wuq = jnp.pad(wuq, ((0, 256 - B_Q_LORA), (0, 0), (0, LANES - B_NOPE - B_ROPE))).reshape(256, B_HEADS * LANES)
        wukv = mla_w_ukv[layer].reshape(B_KV_LORA, B_HEADS, B_NOPE + B_V)
        wuk = jnp.pad(wukv[:, :, :B_NOPE], ((0, 0), (0, 0), (0, LANES - B_NOPE))).reshape(B_KV_LORA, B_HEADS * LANES)
        wuv = jnp.transpose(wukv[:, :, B_NOPE:], (1, 0, 2))
        kvn = mla_kv_norm[layer].reshape(1, B_KV_LORA)
        gqn = _pad_cols(gqa_q_norm[layer].reshape(1, D_HEAD), LANES)
        gkn = _pad_cols(gqa_k_norm[layer].reshape(1, D_HEAD), LANES)
        lb = lb_all[:, layer]
        proj_args = (norm_pre[layer, 1], wp)
        mla_args = (qn, bf(wuq), kvn, bf(wuk), bf(wuv), lb, gqn, gkn)

        pl_ = _inproj(x_lat, mod_lat, *proj_args, rope_b_lat, rope_d_lat, *mla_args)
        pc_ = _inproj(x_ctx, mod_ctx, *proj_args, rope_b_ctx, rope_d_ctx, *mla_args)
        (u_l, u2_l, bq_l, bk_l, bv_l, hq_l, hv_l, hk_l, hl_l, hg_l, dq_l, dk_l, dv_l) = pl_
        (u_c, u2_c, bq_c, bk_c, bv_c, hq_c, hv_c, hk_c, hl_c, hg_c, dq_c, dk_c, dv_c) = pc_

        ys_l, ys_c = [], []
        for dr, reverse in enumerate((False, True)):
            nlev = 9
            wts = _s5_weights(s5_lambda_re[layer, dr], s5_lambda_im[layer, dr], s5_log_dt[layer, dr],
                              s5_b_re[layer, dr], s5_b_im[layer, dr], s5_c_re[layer, dr], s5_c_im[layer, dr],
                              reverse, nlev)
            x0 = jnp.zeros((nb, 1, 2 * S5_NSTATE), F32)
            y_c, x_end = _s5_scan(u2_c, x0, wts, reverse)
            y_l, _ = _s5_scan(u2_l, x_end, wts, reverse)
            ys_l.append(y_l)
            ys_c.append(y_c)

        s0 = jnp.zeros((2, nb, HG_W, HG_W), F32)
        ho_c, s_ctx = _hgrn_scan(hq_c, hk_c, hv_c, hl_c, s0, hg_consts)
        ho_l, _ = _hgrn_scan(hq_l, hk_l, hv_l, hl_l, s_ctx, hg_consts)

        bo_l = _attention(bq_l, [(bk_c, bv_c), (bk_l, bv_l)])
        do_l = _attention(dq_l, [(dk_c, dv_c), (dk_l, dv_l)])

        merge_w = (norm_pre[layer, 1], norm_post[layer, 1], wg, bf(w_branch[layer]), bf(w_out[layer]))
        s5_ro = (s5_d[layer].reshape(1, A_WIDTH), bf(s5_w_glu[layer]))
        hn = jnp.tile(hgrn_o_norm[layer], C_HEADS).reshape(1, HG_W)
        x_lat_new = _merge(x_lat, mod_lat, *merge_w, u_l, ys_l[0], ys_l[1], *s5_ro, bo_l, ho_l, hg_l, hn, avg, do_l)
        if not last:
            bo_c = _attention(bq_c, [(bk_c, bv_c)])
            do_c = _attention(dq_c, [(dk_c, dv_c)])
            x_ctx = _merge(x_ctx, mod_ctx, *merge_w, u_c, ys_c[0], ys_c[1], *s5_ro, bo_c, ho_c, hg_c, hn, avg, do_c)
            x_ctx = _ffn(x_ctx, mod_ctx, 2, *ffn_b)
        x_lat = _ffn(x_lat_new, mod_lat, 2, *ffn_b)
    return x_lat
```

```python
import functools

import jax
import jax.numpy as jnp
from jax import lax
from jax.experimental import pallas as pl
from jax.experimental.pallas import tpu as pltpu

GRID_W = 64
FFN_RES_WEIGHT = 0.5
N_MOD = 9
EPS = 1e-6
ROPE_THETA = 10000.0
F_FLOOR = 1e-20

A_WIDTH = 256
A_GROUP = 16
A_GROUPS = A_WIDTH // A_GROUP
A_STATE = 64

B_HEADS = 4
B_NOPE = 64
B_ROPE = 32
B_V = 64
B_Q_LORA = 192
B_KV_LORA = 128

C_HEADS = 4
C_DK = 64
C_DV = 64

D_HEADS = 4
D_KV_HEADS = 2
D_HEAD = 64

N_BRANCH = 4
BRANCH_W = 256

LANES = 128
VMEM_LIMIT_BYTES = 56 * 1024 * 1024

S5_R = 4
S5_LANES = S5_R * A_WIDTH
S5_NSTATE = A_GROUPS * A_STATE
HG_T = 128
HG_LEVELS = 7
HG_W = C_HEADS * C_DK
ATTN_ONES = 16
ATTN_UNROLL = 8

SLOT_S5 = 0
SLOT_CQ = 256
SLOT_CKV = 512
SLOT_KR = 640
SLOT_HQ = 768
SLOT_HV = 1024
SLOT_HF = 1280
SLOT_HB = 1536
SLOT_HG = 1792
SLOT_GQ = 2048
SLOT_GK = 2560
SLOT_GV = 2816
N_PROJ = 3072

BF16 = jnp.bfloat16
F32 = jnp.float32
LOG2E = 1.4426950408889634


def _params(sem):
    return pltpu.CompilerParams(dimension_semantics=sem, vmem_limit_bytes=VMEM_LIMIT_BYTES)


def _pick(n, candidates):
    for c in candidates:
        if n % c == 0:
            return c
    raise ValueError(f"no tile for {n} in {candidates}")


def _dot(a, b):
    return jnp.dot(a, b, preferred_element_type=F32)


def _dot_nt(a, b):
    return lax.dot_general(a, b, (((1,), (1,)), ((), ())), preferred_element_type=F32)


def _rms(x, g, n=None):
    n = x.shape[-1] if n is None else n
    ms = jnp.sum(x * x, axis=-1, keepdims=True) * (1.0 / n)
    return x * lax.rsqrt(ms + EPS) * g


def _mod_kernel(c_ref, w_ref, b_ref, o_ref):
    c = c_ref[...]
    a = (c * jax.nn.sigmoid(c)).astype(BF16)
    o_ref[0] = _dot(a, w_ref[0].astype(BF16)) + b_ref[0]


def _modulation(cvec, w_ada, b_ada):
    nl, d, nm = w_ada.shape
    rows = cvec.shape[0]
    tn = _pick(nm, (1152, 1024, 512, 256, 128))
    return pl.pallas_call(
        _mod_kernel,
        grid=(nl, nm // tn),
        in_specs=[
            pl.BlockSpec((rows, d), lambda l, n: (0, 0)),
            pl.BlockSpec((1, d, tn), lambda l, n: (l, 0, n)),
            pl.BlockSpec((1, 1, tn), lambda l, n: (l, 0, n)),
        ],
        out_specs=pl.BlockSpec((1, rows, tn), lambda l, n: (l, 0, n)),
        out_shape=jax.ShapeDtypeStruct((nl, rows, nm), F32),
        compiler_params=_params(("parallel", "parallel")),
        name="adaln_mod",
    )(cvec, w_ada, b_ada.reshape(nl, 1, nm))


def _ffn_kernel(x_ref, mod_ref, gpre_ref, gpost_ref, w1_ref, w3_ref, w2_ref, o_ref, h_scr, acc_scr, *, j, nf):
    f = pl.program_id(2)

    @pl.when(f == 0)
    def _():
        x = x_ref[0]
        shift = mod_ref[0, pl.ds(3 * j, 1), :]
        scale = mod_ref[0, pl.ds(3 * j + 1, 1), :]
        h = _rms(x, gpre_ref[...]) * (1.0 + scale) + shift
        h_scr[...] = h.astype(BF16)
        acc_scr[...] = jnp.zeros_like(acc_scr)

    h = h_scr[...]
    a = _dot(h, w1_ref[...])
    b = _dot(h, w3_ref[...])
    t = (a * jax.nn.sigmoid(a) * b).astype(BF16)
    acc_scr[...] += _dot(t, w2_ref[...])

    @pl.when(f == nf - 1)
    def _():
        gate = mod_ref[0, pl.ds(3 * j + 2, 1), :]
        y = _rms(acc_scr[...], gpost_ref[...])
        o_ref[0] = x_ref[0] + FFN_RES_WEIGHT * gate * y


def _ffn(x, mod, j, g_pre, g_post, w1, w3, w2):
    nb, s, d = x.shape
    dff = w1.shape[1]
    tm = _pick(s, (1024, 512, 256))
    tf = _pick(dff, (256, 128))
    nf = dff // tf
    return pl.pallas_call(
        functools.partial(_ffn_kernel, j=j, nf=nf),
        grid=(nb, s // tm, nf),
        in_specs=[
            pl.BlockSpec((1, tm, d), lambda b, i, f: (b, i, 0)),
            pl.BlockSpec((1, N_MOD, d), lambda b, i, f: (b, 0, 0)),
            pl.BlockSpec((1, d), lambda b, i, f: (0, 0)),
            pl.BlockSpec((1, d), lambda b, i, f: (0, 0)),
            pl.BlockSpec((d, tf), lambda b, i, f: (0, f)),
            pl.BlockSpec((d, tf), lambda b, i, f: (0, f)),
            pl.BlockSpec((tf, d), lambda b, i, f: (f, 0)),
        ],
        out_specs=pl.BlockSpec((1, tm, d), lambda b, i, f: (b, i, 0)),
        out_shape=jax.ShapeDtypeStruct(x.shape, F32),
        scratch_shapes=[pltpu.VMEM((tm, d), BF16), pltpu.VMEM((tm, d), F32)],
        compiler_params=_params(("parallel", "parallel", "arbitrary")),
        name="ffn_sublayer",
    )(x, mod, g_pre.reshape(1, d), g_post.reshape(1, d), w1, w3, w2)


def _rope(x, cos, sin_a, sin_b, quarter):
    w = x.shape[-1]
    return x * cos + pltpu.roll(x, w - quarter, 1) * sin_a + pltpu.roll(x, quarter, 1) * sin_b


def _inproj_kernel(x_ref, mod_ref, gpre_ref, w_ref, rb_ref, rd_ref, qn_ref, wuq_ref, kvn_ref, wuk_ref, wuv_ref,
                   lb_ref, gqn_ref, gkn_ref,
                   u_ref, u2_ref, bq_ref, bk_ref, bv_ref, hq_ref, hv_ref, hk_ref, hl_ref, hg_ref,
                   dq_ref, dk_ref, dv_ref, h_scr, u_scr, *, tm):
    x = x_ref[0]
    shift = mod_ref[0, pl.ds(3, 1), :]
    scale = mod_ref[0, pl.ds(4, 1), :]
    h_scr[...] = (_rms(x, gpre_ref[...]) * (1.0 + scale) + shift).astype(BF16)

    def proj(lo, width):
        return _dot(h_scr[...], w_ref[:, lo:lo + width])

    u = proj(SLOT_S5, A_WIDTH)
    u_ref[0] = u.astype(BF16)
    for hf in range(A_WIDTH // LANES):
        u_scr[hf] = u[:, hf * LANES:(hf + 1) * LANES]
    for r in range(S5_R):
        for hf in range(A_WIDTH // LANES):
            lo = r * A_WIDTH + hf * LANES
            u2_ref[0, :, lo:lo + LANES] = u_scr[hf, pl.ds(r, tm // S5_R, stride=S5_R), :].astype(BF16)

    cos_b, sa_b, sb_b = rb_ref[0], rb_ref[1], rb_ref[2]
    cq = _rms(proj(SLOT_CQ, 256), qn_ref[...], n=B_Q_LORA).astype(BF16)
    q = _dot(cq, wuq_ref[...])
    b_scale = (B_NOPE + B_ROPE) ** -0.5 * LOG2E
    for hd in range(B_HEADS):
        qh = _rope(q[:, hd * LANES:(hd + 1) * LANES], cos_b, sa_b, sb_b, B_ROPE // 4)
        bq_ref[0, hd] = (qh * b_scale).astype(BF16)
    ckv = _rms(proj(SLOT_CKV, B_KV_LORA), kvn_ref[...]).astype(BF16)
    kn = _dot(ckv, wuk_ref[...])
    kr = _rope(proj(SLOT_KR, LANES), cos_b, sa_b, sb_b, B_ROPE // 4)
    for hd in range(B_HEADS):
        bk_ref[0, hd] = (kn[:, hd * LANES:(hd + 1) * LANES] + kr).astype(BF16)
        bv_ref[0, hd] = _dot(ckv, wuv_ref[hd]).T[:B_V].astype(BF16)

    hq_ref[0] = proj(SLOT_HQ, HG_W).astype(BF16)
    hv_ref[0] = proj(SLOT_HV, HG_W).astype(BF16)
    hg_ref[0] = proj(SLOT_HG, HG_W).astype(BF16)
    for dr, slot in enumerate((SLOT_HF, SLOT_HB)):
        z = proj(slot, HG_W)
        lb = lb_ref[pl.ds(dr, 1), :]
        f = lb + (1.0 - lb) * jax.nn.sigmoid(z)
        hl_ref[dr, 0] = jnp.log(jnp.maximum(f, F_FLOOR))
        hk_ref[dr, 0] = ((1.0 - lb) * jax.nn.sigmoid(-z)).astype(BF16)

    cos_d, sa_d, sb_d = rd_ref[0], rd_ref[1], rd_ref[2]
    d_scale = D_HEAD ** -0.5 * LOG2E
    for hd in range(D_HEADS):
        qh = _rms(proj(SLOT_GQ + hd * LANES, LANES), gqn_ref[...], n=D_HEAD)
        qh = _rope(qh, cos_d, sa_d, sb_d, D_HEAD // 4) * d_scale
        dq_ref[0, hd] = qh[:, :D_HEAD].astype(BF16)
    for hd in range(D_KV_HEADS):
        kh = _rms(proj(SLOT_GK + hd * LANES, LANES), gkn_ref[...], n=D_HEAD)
        kh = _rope(kh, cos_d, sa_d, sb_d, D_HEAD // 4)
        dk_ref[0, hd] = kh[:, :D_HEAD].astype(BF16)
        dv_ref[0, hd] = proj(SLOT_GV + hd * LANES, LANES).T[:D_HEAD].astype(BF16)


def _inproj(x, mod, g_pre, wp, rope_b, rope_d, qn, wuq, kvn, wuk, wuv, lb, gqn, gkn):
    nb, s, d = x.shape
    tm = _pick(s, (512, 256))
    const2 = lambda b, i: (0, 0)
    const3 = lambda b, i: (0, 0, 0)
    tok = lambda w: pl.BlockSpec((1, tm, w), lambda b, i: (b, i, 0))
    headed = lambda nh, w: pl.BlockSpec((1, nh, tm, w), lambda b, i: (b, 0, i, 0))
    headed_t = lambda nh, w: pl.BlockSpec((1, nh, w, tm), lambda b, i: (b, 0, 0, i))
    dirtok = lambda w: pl.BlockSpec((2, 1, tm, w), lambda b, i: (0, b, i, 0))
    sd = jax.ShapeDtypeStruct
    outs = [
        (sd((nb, s, A_WIDTH), BF16), tok(A_WIDTH)),
        (sd((nb, s // S5_R, S5_LANES), BF16), pl.BlockSpec((1, tm // S5_R, S5_LANES), lambda b, i: (b, i, 0))),
        (sd((nb, B_HEADS, s, LANES), BF16), headed(B_HEADS, LANES)),
        (sd((nb, B_HEADS, s, LANES), BF16), headed(B_HEADS, LANES)),
        (sd((nb, B_HEADS, B_V, s), BF16), headed_t(B_HEADS, B_V)),
        (sd((nb, s, HG_W), BF16), tok(HG_W)),
        (sd((nb, s, HG_W), BF16), tok(HG_W)),
        (sd((2, nb, s, HG_W), BF16), dirtok(HG_W)),
        (sd((2, nb, s, HG_W), F32), dirtok(HG_W)),
        (sd((nb, s, HG_W), BF16), tok(HG_W)),
        (sd((nb, D_HEADS, s, D_HEAD), BF16), headed(D_HEADS, D_HEAD)),
        (sd((nb, D_KV_HEADS, s, D_HEAD), BF16), headed(D_KV_HEADS, D_HEAD)),
        (sd((nb, D_KV_HEADS, D_HEAD, s), BF16), headed_t(D_KV_HEADS, D_HEAD)),
    ]
    return pl.pallas_call(
        functools.partial(_inproj_kernel, tm=tm),
        grid=(nb, s // tm),
        in_specs=[
            pl.BlockSpec((1, tm, d), lambda b, i: (b, i, 0)),
            pl.BlockSpec((1, N_MOD, d), lambda b, i: (b, 0, 0)),
            pl.BlockSpec((1, d), const2),
            pl.BlockSpec((d, N_PROJ), const2),
            pl.BlockSpec((3, tm, LANES), lambda b, i: (0, i, 0)),
            pl.BlockSpec((3, tm, LANES), lambda b, i: (0, i, 0)),
            pl.BlockSpec((1, 256), const2),
            pl.BlockSpec((256, B_HEADS * LANES), const2),
            pl.BlockSpec((1, B_KV_LORA), const2),
            pl.BlockSpec((B_KV_LORA, B_HEADS * LANES), const2),
            pl.BlockSpec((B_HEADS, B_KV_LORA, LANES), const3),
            pl.BlockSpec((2, HG_W), const2),
            pl.BlockSpec((1, LANES), const2),
            pl.BlockSpec((1, LANES), const2),
        ],
        out_specs=[o[1] for o in outs],
        out_shape=[o[0] for o in outs],
        scratch_shapes=[pltpu.VMEM((tm, d), BF16), pltpu.VMEM((A_WIDTH // LANES, tm, LANES), F32)],
        compiler_params=_params(("parallel", "parallel")),
        name="mixer_inproj",
    )(x, mod, g_pre.reshape(1, d), wp, rope_b, rope_d, qn, wuq, kvn, wuk, wuv, lb, gqn, gkn)


def _s5_kernel(u2_ref, x0_ref, tb_ref, wst_ref, wout_ref, pq_ref, y_ref, xf_ref, carry, *, rows, reverse, nlev):
    i = pl.program_id(1)

    @pl.when(i == 0)
    def _():
        carry[...] = x0_ref[0]

    u2 = u2_ref[0]
    sloc = _dot(u2, wst_ref[...])
    ridx = lax.broadcasted_iota(jnp.int32, (rows, 1), 0)

    def cmul(xv, lev):
        p = pq_ref[pl.ds(2 * lev, 1), :]
        q = pq_ref[pl.ds(2 * lev + 1, 1), :]
        return xv * p + pltpu.roll(xv, S5_NSTATE, 1) * q

    if reverse:
        e = jnp.where(ridx == rows - 1, carry[...], pltpu.roll(sloc, rows - 1, 0))
    else:
        e = jnp.where(ridx == 0, carry[...], pltpu.roll(sloc, 1, 0))
    xin = e
    for lev in range(nlev):
        dist = 1 << lev
        if reverse:
            sh = jnp.where(ridx < rows - dist, pltpu.roll(xin, rows - dist, 0), 0.0)
        else:
            sh = jnp.where(ridx >= dist, pltpu.roll(xin, dist, 0), 0.0)
        xin = xin + cmul(sh, lev)

    last = 0 if reverse else rows - 1
    nxt = cmul(xin[last:last + 1, :], 0) + sloc[last:last + 1, :]
    carry[...] = nxt
    xf_ref[0] = nxt

    y2 = _dot(u2, tb_ref[...]) + _dot(xin.astype(BF16), wout_ref[...])
    for r in range(S5_R):
        for hf in range(A_WIDTH // LANES):
            lo = r * A_WIDTH + hf * LANES
            y_ref[0, hf, pl.ds(r, rows, stride=S5_R), :] = y2[:, lo:lo + LANES]


def _s5_scan(u2, x0, wts, reverse):
    tb, wst, wout, pq = wts
    nb, n2, _ = u2.shape
    rows = _pick(n2, (512, 256, 128, 64))
    nt = n2 // rows
    nlev = max(1, (rows - 1).bit_length())
    order = (lambda b, i: (b, nt - 1 - i, 0)) if reverse else (lambda b, i: (b, i, 0))
    order_out = (lambda b, i: (b, 0, nt - 1 - i, 0)) if reverse else (lambda b, i: (b, 0, i, 0))
    const2 = lambda b, i: (0, 0)
    y, xf = pl.pallas_call(
        functools.partial(_s5_kernel, rows=rows, reverse=reverse, nlev=nlev),
        grid=(nb, nt),
        in_specs=[
            pl.BlockSpec((1, rows, S5_LANES), order),
            pl.BlockSpec((1, 1, 2 * S5_NSTATE), lambda b, i: (b, 0, 0)),
            pl.BlockSpec((S5_LANES, S5_LANES), const2),
            pl.BlockSpec((S5_LANES, 2 * S5_NSTATE), const2),
            pl.BlockSpec((2 * S5_NSTATE, S5_LANES), const2),
            pl.BlockSpec(pq.shape, const2),
        ],
        out_specs=[
            pl.BlockSpec((1, A_WIDTH // LANES, rows * S5_R, LANES), order_out),
            pl.BlockSpec((1, 1, 2 * S5_NSTATE), lambda b, i: (b, 0, 0)),
        ],
        out_shape=[
            jax.ShapeDtypeStruct((nb, A_WIDTH // LANES, n2 * S5_R, LANES), F32),
            jax.ShapeDtypeStruct((nb, 1, 2 * S5_NSTATE), F32),
        ],
        scratch_shapes=[pltpu.VMEM((1, 2 * S5_NSTATE), F32)],
        compiler_params=_params(("parallel", "arbitrary")),
        name="s5_scan_rev" if reverse else "s5_scan_fwd",
    )(u2, x0, tb, wst, wout, pq)
    return y, xf


def _s5_weights(lam_re, lam_im, log_dt, b_re, b_im, c_re, c_im, reverse, nlev):
    g, n, r = A_GROUPS, A_STATE, S5_R
    lam_re = jnp.minimum(lam_re.astype(F32), -1e-4)
    lam_im = lam_im.astype(F32)
    dt = jnp.exp(log_dt.astype(F32))[:, None]
    mag = jnp.exp(lam_re * dt)
    a_re = mag * jnp.cos(lam_im * dt)
    a_im = mag * jnp.sin(lam_im * dt)
    den = lam_re * lam_re + lam_im * lam_im
    num_re = a_re - 1.0
    f_re = (num_re * lam_re + a_im * lam_im) / den
    f_im = (a_im * lam_re - num_re * lam_im) / den
    bb_re = f_re[..., None] * b_re - f_im[..., None] * b_im
    bb_im = f_re[..., None] * b_im + f_im[..., None] * b_re

    def cm(xr, xi, yr, yi):
        return xr * yr - xi * yi, xr * yi + xi * yr

    pr, pi = [jnp.ones_like(a_re)], [jnp.zeros_like(a_im)]
    for _ in range(r):
        nr, ni = cm(pr[-1], pi[-1], a_re, a_im)
        pr.append(nr)
        pi.append(ni)
    pr, pi = jnp.stack(pr), jnp.stack(pi)

    def lag(tau):
        xr, xi = cm(pr[tau][..., None], pi[tau][..., None], bb_re, bb_im)
        return jnp.einsum('gon,gnc->goc', c_re, xr) - jnp.einsum('gon,gnc->goc', c_im, xi)

    eye_g = jnp.eye(g, dtype=F32)
    zero_blk = jnp.zeros((g, A_GROUP, g, A_GROUP), F32)

    def group_diag(m):
        return jnp.einsum('gio,gh->giho', m, eye_g).reshape(g * A_GROUP, g * A_GROUP)

    blocks = []
    for r_in in range(r):
        row = []
        for r_out in range(r):
            tau = (r_in - r_out) if reverse else (r_out - r_in)
            if tau < 0:
                row.append(zero_blk.reshape(g * A_GROUP, g * A_GROUP))
            else:
                row.append(group_diag(jnp.swapaxes(lag(tau), 1, 2)))
        blocks.append(jnp.concatenate(row, axis=1))
    tb = jnp.concatenate(blocks, axis=0)

    wst_rows = []
    for r_in in range(r):
        steps = r_in if reverse else (r - 1 - r_in)
        xr, xi = cm(pr[steps][..., None], pi[steps][..., None], bb_re, bb_im)
        wr = jnp.einsum('gnc,gh->gchn', xr, eye_g).reshape(g * A_GROUP, g * n)
        wi = jnp.einsum('gnc,gh->gchn', xi, eye_g).reshape(g * A_GROUP, g * n)
        wst_rows.append(jnp.concatenate([wr, wi], axis=1))
    wst = jnp.concatenate(wst_rows, axis=0)

    wout_cols = []
    for r_out in range(r):
        steps = (r - r_out) if reverse else (r_out + 1)
        yr, yi = cm(c_re, c_im, pr[steps][:, None, :], pi[steps][:, None, :])
        wr = jnp.einsum('gon,gh->gnho', yr, eye_g).reshape(g * n, g * A_GROUP)
        wi = jnp.einsum('gon,gh->gnho', -yi, eye_g).reshape(g * n, g * A_GROUP)
        wout_cols.append(jnp.concatenate([wr, wi], axis=0))
    wout = jnp.concatenate(wout_cols, axis=1)

    lr, li = pr[r].reshape(1, g * n), pi[r].reshape(1, g * n)
    rows = []
    for _ in range(nlev):
        rows.append(jnp.concatenate([lr, lr], axis=1))
        rows.append(jnp.concatenate([-li, li], axis=1))
        lr, li = cm(lr, li, lr, li)
    pq = jnp.concatenate(rows, axis=0)
    return tb.astype(BF16), wst.astype(BF16), wout.astype(BF16), pq


def _split3(x):
    x1 = x.astype(BF16)
    r1 = x - x1.astype(F32)
    x2 = r1.astype(BF16)
    x3 = (r1 - x2.astype(F32)).astype(BF16)
    return x1, x2, x3


def _hgrn_kernel(q_ref, k_ref, v_ref, lf_ref, s0_ref, tri_ref, sel_ref, sgn_ref, msk_ref, hm_ref, bd_ref,
                 o_ref, sf_ref, st_scr, *, nlev):
    c = pl.program_id(2)

    @pl.when(c == 0)
    def _():
        st_scr[...] = s0_ref[0, 0]

    q = q_ref[0].astype(F32)
    k = k_ref[0, 0].astype(F32)
    vb = v_ref[0]
    lf = lf_ref[0, 0]
    tri = tri_ref[0]

    l1, l2, l3 = _split3(lf)
    cum = _dot(tri, l1) + _dot(tri, l2) + _dot(tri, l3)
    tot = jnp.sum(lf, axis=0, keepdims=True)

    mids = _dot(sel_ref[0], cum.astype(BF16))
    t = q.shape[0]
    kb = k.astype(BF16)
    qb = q.astype(BF16)
    heads = [hm_ref[pl.ds(hd, 1), :] > 0.5 for hd in range(C_HEADS)]

    att = [jnp.where(msk_ref[0, nlev] > 0.5, _dot_nt(qb, jnp.where(heads[hd], kb, jnp.zeros_like(kb))), 0.0)
           for hd in range(C_HEADS)]
    for lev in range(nlev):
        e = jnp.exp(sgn_ref[0, lev] * (cum - mids[lev * t:(lev + 1) * t, :]))
        qs = (q * e).astype(BF16)
        ks = (k * e).astype(BF16)
        m = msk_ref[0, lev] > 0.5
        for hd in range(C_HEADS):
            s = _dot_nt(qs, jnp.where(heads[hd], ks, jnp.zeros_like(ks)))
            att[hd] = att[hd] + jnp.where(m, s, 0.0)

    st = st_scr[...]
    o = _dot_nt((q * jnp.exp(cum)).astype(BF16), st.astype(BF16))
    for hd in range(C_HEADS):
        o = o + _dot(att[hd].astype(BF16), jnp.where(heads[hd], vb, jnp.zeros_like(vb)))
    o_ref[0, 0] = o

    kend = (k * jnp.exp(tot - cum)).astype(BF16)
    vt = vb.astype(F32).T.astype(BF16)
    new = st * jnp.exp(tot) + jnp.where(bd_ref[...] > 0.5, _dot(vt, kend), 0.0)
    st_scr[...] = new
    sf_ref[0, 0] = new


def _hgrn_consts():
    t = HG_T
    ti = jnp.arange(t)[:, None]
    si = jnp.arange(t)[None, :]
    tri, sel, sgn, msk = [], [], [], []
    for reverse in (False, True):
        tri.append((si >= ti) if reverse else (si <= ti))
        sels, sgns, msks = [], [], []
        for lev in range(HG_LEVELS):
            h = 1 << lev
            blk_t, blk_s = ti // (2 * h), si // (2 * h)
            hi_t, hi_s = (ti % (2 * h)) >= h, (si % (2 * h)) >= h
            if reverse:
                mid = blk_t * 2 * h + h
                q_role_t, k_role_s = ~hi_t, hi_s
            else:
                mid = blk_t * 2 * h + h - 1
                q_role_t, k_role_s = hi_t, ~hi_s
            sels.append(si == mid)
            sgns.append(jnp.where(q_role_t, 1.0, -1.0))
            msks.append((blk_t == blk_s) & q_role_t & k_role_s)
        msks.append(ti == si)
        sel.append(jnp.concatenate(sels, axis=0))
        sgn.append(jnp.stack(sgns))
        msk.append(jnp.stack(msks))
    lane_head = jnp.arange(HG_W) // C_DK
    hm = (lane_head[None, :] == jnp.arange(C_HEADS)[:, None]).astype(F32)
    bd = (lane_head[:, None] == lane_head[None, :]).astype(F32)
    return (jnp.stack(tri).astype(BF16), jnp.stack(sel).astype(BF16), jnp.stack(sgn).astype(F32),
            jnp.stack(msk).astype(F32), hm, bd)


def _hgrn_scan(q, k, v, lf, s0, consts):
    tri, sel, sgn, msk, hm, bd = consts
    nb, s, w = q.shape
    t = HG_T
    nc = s // t
    chunk = lambda d, b, c: jnp.where(d == 0, c, nc - 1 - c)
    const2 = lambda d, b, c: (0, 0)
    o, sf = pl.pallas_call(
        functools.partial(_hgrn_kernel, nlev=HG_LEVELS),
        grid=(2, nb, nc),
        in_specs=[
            pl.BlockSpec((1, t, w), lambda d, b, c: (b, chunk(d, b, c), 0)),
            pl.BlockSpec((1, 1, t, w), lambda d, b, c: (d, b, chunk(d, b, c), 0)),
            pl.BlockSpec((1, t, w), lambda d, b, c: (b, chunk(d, b, c), 0)),
            pl.BlockSpec((1, 1, t, w), lambda d, b, c: (d, b, chunk(d, b, c), 0)),
            pl.BlockSpec((1, 1, w, w), lambda d, b, c: (d, b, 0, 0)),
            pl.BlockSpec((1, t, t), lambda d, b, c: (d, 0, 0)),
            pl.BlockSpec((1, HG_LEVELS * t, t), lambda d, b, c: (d, 0, 0)),
            pl.BlockSpec((1, HG_LEVELS, t, 1), lambda d, b, c: (d, 0, 0, 0)),
            pl.BlockSpec((1, HG_LEVELS + 1, t, t), lambda d, b, c: (d, 0, 0, 0)),
            pl.BlockSpec((C_HEADS, w), const2),
            pl.BlockSpec((w, w), const2),
        ],
        out_specs=[
            pl.BlockSpec((1, 1, t, w), lambda d, b, c: (d, b, chunk(d, b, c), 0)),
            pl.BlockSpec((1, 1, w, w), lambda d, b, c: (d, b, 0, 0)),
        ],
        out_shape=[
            jax.ShapeDtypeStruct((2, nb, s, w), F32),
            jax.ShapeDtypeStruct((2, nb, w, w), F32),
        ],
        scratch_shapes=[pltpu.VMEM((w, w), F32)],
        compiler_params=_params(("parallel", "parallel", "arbitrary")),
        name="hgrn_scan",
    )(q, k, v, lf, s0, tri, sel, sgn.reshape(2, HG_LEVELS, t, 1), msk, hm, bd)
    return o, sf


def _attn_kernel(*refs, nseg, tks, g, tq):
    q_ref = refs[0]
    kv_refs = refs[1:1 + 2 * nseg]
    o_ref = refs[1 + 2 * nseg]
    dk = q_ref.shape[-1]
    dv = o_ref.shape[-1]
    n = g * tq
    q = q_ref[0].reshape(n, dk)

    def scores(k):
        return _dot_nt(k, q)

    def absorb(s, vt, carry):
        m, acc = carry
        m_new = jnp.maximum(m, jnp.max(s, axis=0, keepdims=True))
        alpha = jnp.exp2(m - m_new)
        p = jnp.exp2(s - m_new).astype(BF16)
        vt1 = jnp.concatenate([vt, jnp.ones((ATTN_ONES, vt.shape[1]), BF16)], axis=0)
        acc = alpha * acc + _dot(vt1, p)
        return m_new, acc

    carry = (jnp.full((1, n), -jnp.inf, F32), jnp.zeros((dv + ATTN_ONES, n), F32))
    pending = None
    for seg in range(nseg):
        k_ref, vt_ref = kv_refs[2 * seg], kv_refs[2 * seg + 1]
        tk = tks[seg]
        nk = k_ref.shape[2] // tk
        s_first = scores(k_ref[0, 0, 0:tk, :])
        if pending is not None:
            carry = absorb(*pending, carry)
        if nk == 1:
            pending = (s_first, vt_ref[0, 0])
            continue

        def body(j, c, k_ref=k_ref, vt_ref=vt_ref, tk=tk, nk=nk):
            s_cur, m, acc = c
            off_next = pl.multiple_of(jnp.minimum(j + 1, nk - 1) * tk, tk)
            s_next = scores(k_ref[0, 0, pl.ds(off_next, tk), :])
            off = pl.multiple_of(j * tk, tk)
            m, acc = absorb(s_cur, vt_ref[0, 0, :, pl.ds(off, tk)], (m, acc))
            return s_next, m, acc

        unroll = ATTN_UNROLL if nk % ATTN_UNROLL == 0 else 1
        _, m, acc = lax.fori_loop(0, nk, body, (s_first,) + carry, unroll=unroll)
        carry = (m, acc)
        pending = None
    m, acc = carry if pending is None else absorb(*pending, carry)
    out = acc[:dv] / acc[dv:dv + 1, :]
    out = jnp.concatenate([out, jnp.zeros((LANES - dv, n), F32)], axis=0).T
    o_ref[0] = out[:, :dv].reshape(g, tq, dv).astype(o_ref.dtype)


def _attention(q, kvs):
    nb, hq, sq, dk = q.shape
    hkv = kvs[0][0].shape[1]
    dv = kvs[0][1].shape[2]
    g = hq // hkv
    tq = _pick(sq, (512 // g, 256 // g, 128 // g))
    tks = tuple(_pick(k.shape[2], (512, 256, 128)) for k, _ in kvs)
    in_specs = [pl.BlockSpec((1, g, tq, dk), lambda b, h, i: (b, h, i, 0))]
    args = [q]
    for k, v in kvs:
        sk = k.shape[2]
        in_specs.append(pl.BlockSpec((1, 1, sk, dk), lambda b, h, i: (b, h, 0, 0)))
        in_specs.append(pl.BlockSpec((1, 1, dv, sk), lambda b, h, i: (b, h, 0, 0)))
        args += [k, v]
    return pl.pallas_call(
        functools.partial(_attn_kernel, nseg=len(kvs), tks=tks, g=g, tq=tq),
        grid=(nb, hkv, sq // tq),
        in_specs=in_specs,
        out_specs=pl.BlockSpec((1, g, tq, dv), lambda b, h, i: (b, h, i, 0)),
        out_shape=jax.ShapeDtypeStruct((nb, hq, sq, dv), BF16),
        compiler_params=_params(("parallel", "parallel", "arbitrary")),
        name="attention",
    )(*args)


def _merge_kernel(x_ref, mod_ref, gpre_ref, gpost_ref, wg_ref, wb_ref, wo_ref,
                  u_ref, yf_ref, yr_ref, sd_ref, wglu_ref, bo_ref, ho_ref, hg_ref, hn_ref, avg_ref, do_ref,
                  o_ref):
    x = x_ref[0]
    shift = mod_ref[0, pl.ds(3, 1), :]
    scale = mod_ref[0, pl.ds(4, 1), :]
    gate = mod_ref[0, pl.ds(5, 1), :]
    h = (_rms(x, gpre_ref[...]) * (1.0 + scale) + shift).astype(BF16)
    d = x.shape[-1]

    def branch_gate(i):
        return jax.nn.sigmoid(_dot(h, wg_ref[:, i * d:(i + 1) * d]))

    ysum = yf_ref[0] + yr_ref[0]
    y = sd_ref[...] * u_ref[0].astype(F32) + jnp.concatenate([ysum[hf] for hf in range(A_WIDTH // LANES)], axis=-1)
    ge = jax.nn.gelu(y)
    ya = ge * jax.nn.sigmoid(_dot(ge.astype(BF16), wglu_ref[...]))
    merged = branch_gate(0) * _dot(ya.astype(BF16), wb_ref[0])

    yb = _dot(bo_ref[0, 0], wb_ref[1, 0:B_V, :])
    for hd in range(1, B_HEADS):
        yb = yb + _dot(bo_ref[0, hd], wb_ref[1, hd * B_V:(hd + 1) * B_V, :])
    merged = merged + branch_gate(1) * yb

    o2 = ho_ref[0, 0] + ho_ref[1, 0]
    ms = _dot((o2 * o2).astype(BF16), avg_ref[...])
    gz = hg_ref[0].astype(F32)
    yc = o2 * lax.rsqrt(ms + EPS) * hn_ref[...] * (gz * jax.nn.sigmoid(gz))
    merged = merged + branch_gate(2) * _dot(yc.astype(BF16), wb_ref[2])

    yd = _dot(do_ref[0, 0], wb_ref[3, 0:D_HEAD, :])
    for hd in range(1, D_HEADS):
        yd = yd + _dot(do_ref[0, hd], wb_ref[3, hd * D_HEAD:(hd + 1) * D_HEAD, :])
    merged = merged + branch_gate(3) * yd

    yo = _dot(merged.astype(BF16), wo_ref[...])
    o_ref[0] = x + gate * _rms(yo, gpost_ref[...])


def _merge(x, mod, g_pre, g_post, wg, wb, wo, u, yf, yr, s5d, wglu, bo, ho, hg, hn, avg, do):
    nb, s, d = x.shape
    tm = _pick(s, (512, 256))
    const2 = lambda b, i: (0, 0)
    const3 = lambda b, i: (0, 0, 0)
    tok = lambda w: pl.BlockSpec((1, tm, w), lambda b, i: (b, i, 0))
    halves = pl.BlockSpec((1, A_WIDTH // LANES, tm, LANES), lambda b, i: (b, 0, i, 0))
    return pl.pallas_call(
        _merge_kernel,
        grid=(nb, s // tm),
        in_specs=[
            tok(d),
            pl.BlockSpec((1, N_MOD, d), lambda b, i: (b, 0, 0)),
            pl.BlockSpec((1, d), const2),
            pl.BlockSpec((1, d), const2),
            pl.BlockSpec((d, N_BRANCH * d), const2),
            pl.BlockSpec((N_BRANCH, BRANCH_W, d), const3),
            pl.BlockSpec((d, d), const2),
            tok(A_WIDTH), halves, halves,
            pl.BlockSpec((1, A_WIDTH), const2),
            pl.BlockSpec((A_WIDTH, A_WIDTH), const2),
            pl.BlockSpec((1, B_HEADS, tm, B_V), lambda b, i: (b, 0, i, 0)),
            pl.BlockSpec((2, 1, tm, HG_W), lambda b, i: (0, b, i, 0)),
            tok(HG_W),
            pl.BlockSpec((1, HG_W), const2),
            pl.BlockSpec((HG_W, HG_W), const2),
            pl.BlockSpec((1, D_HEADS, tm, D_HEAD), lambda b, i: (b, 0, i, 0)),
        ],
        out_specs=tok(d),
        out_shape=jax.ShapeDtypeStruct(x.shape, F32),
        compiler_params=_params(("parallel", "parallel")),
        name="merge_out",
    )(x, mod, g_pre.reshape(1, d), g_post.reshape(1, d), wg, wb, wo, u, yf, yr, s5d, wglu, bo, ho, hg, hn, avg, do)


def _pad_cols(w, width):
    return jnp.pad(w, ((0, 0), (0, width - w.shape[1])))


def _proj_weight(w_in_mix):
    offs = [0]
    for wdt in (A_WIDTH, B_Q_LORA, B_KV_LORA, B_ROPE, HG_W, HG_W, HG_W, HG_W, HG_W,
                D_HEADS * D_HEAD, D_KV_HEADS * D_HEAD, D_KV_HEADS * D_HEAD):
        offs.append(offs[-1] + wdt)
    p = [w_in_mix[:, offs[i]:offs[i + 1]] for i in range(12)]
    d = w_in_mix.shape[0]
    z = lambda n: jnp.zeros((d, n), w_in_mix.dtype)
    cols = [p[0], _pad_cols(p[1], 256), p[2],
            jnp.concatenate([z(B_NOPE), p[3], z(LANES - B_NOPE - B_ROPE)], axis=1),
            p[4], p[5], p[6], p[7], p[8]]
    for i, nh in ((9, D_HEADS), (10, D_KV_HEADS), (11, D_KV_HEADS)):
        for hd in range(nh):
            cols.append(_pad_cols(p[i][:, hd * D_HEAD:(hd + 1) * D_HEAD], LANES))
    return jnp.concatenate(cols, axis=1).astype(BF16)


def _rope_tables(n_tok, rot_dim, lane_off, identity):
    cos = jnp.ones((n_tok, LANES), F32)
    sin_a = jnp.zeros((n_tok, LANES), F32)
    sin_b = jnp.zeros((n_tok, LANES), F32)
    if not identity:
        n_rows = n_tok // GRID_W
        rows = jnp.repeat(jnp.arange(n_rows, dtype=F32), GRID_W)
        cols = jnp.tile(jnp.arange(GRID_W, dtype=F32), n_rows)
        half = rot_dim // 2
        inv = ROPE_THETA ** (-jnp.arange(0, half, 2, dtype=F32) / half)
        ang_r = rows[:, None] * inv
        ang_c = cols[:, None] * inv
        ang = jnp.concatenate([ang_r, ang_r, ang_c, ang_c], axis=-1)
        c, s = jnp.cos(ang), jnp.sin(ang)
        quarter = rot_dim // 4
        first = (jnp.arange(rot_dim) % (2 * quarter)) < quarter
        cos = cos.at[:, lane_off:lane_off + rot_dim].set(c)
        sin_a = sin_a.at[:, lane_off:lane_off + rot_dim].set(jnp.where(first, -s, 0.0))
        sin_b = sin_b.at[:, lane_off:lane_off + rot_dim].set(jnp.where(first, 0.0, s))
    return jnp.stack([cos, sin_a, sin_b])


def kernel(x, c, ctx, c_ctx, w_ada, b_ada, norm_pre, norm_post, ffn_w1, ffn_w3, ffn_w2, w_in,
           s5_lambda_re, s5_lambda_im, s5_log_dt, s5_b_re, s5_b_im, s5_c_re, s5_c_im, s5_d, s5_w_glu,
           mla_q_norm, mla_w_uq, mla_kv_norm, mla_w_ukv, hgrn_lb_raw, hgrn_o_norm,
           gqa_q_norm, gqa_k_norm, w_branch, w_out):
    nb, seq, d = x.shape
    n_ctx = ctx.shape[1]
    depth = w_ada.shape[0]
    n_mix = N_PROJ

    rows = max(8, -(-(nb + 1) // 8) * 8)
    cvec = jnp.zeros((rows, d), F32).at[:nb].set(c).at[nb].set(c_ctx)
    mod_all = _modulation(cvec, w_ada, b_ada).reshape(depth, rows, N_MOD, d)

    lb_step = jax.nn.softmax(hgrn_lb_raw.astype(F32), axis=1)
    lb_all = jnp.clip(jnp.cumsum(lb_step, axis=1) - lb_step[:, :1], 0.0, 1.0)

    rope_b_lat = _rope_tables(seq, B_ROPE, B_NOPE, False)
    rope_d_lat = _rope_tables(seq, D_HEAD, 0, False)
    rope_b_ctx = _rope_tables(n_ctx, B_ROPE, B_NOPE, True)
    rope_d_ctx = _rope_tables(n_ctx, D_HEAD, 0, True)
    hg_consts = _hgrn_consts()
    lane_head = jnp.arange(HG_W) // C_DV
    avg = (lane_head[:, None] == lane_head[None, :]).astype(BF16) * (1.0 / C_DV)
    n_mixcols = w_in.shape[-1] - N_BRANCH * d

    x_lat, x_ctx = x, ctx
    for layer in range(depth):
        last = layer == depth - 1
        mod_lat = mod_all[layer, :nb]
        mod_ctx = jnp.broadcast_to(mod_all[layer, nb:nb + 1], (nb, N_MOD, d))
        bf = lambda w: w.astype(BF16)
        ffn_a = (norm_pre[layer, 0], norm_post[layer, 0], bf(ffn_w1[layer, 0]), bf(ffn_w3[layer, 0]), bf(ffn_w2[layer, 0]))
        ffn_b = (norm_pre[layer, 2], norm_post[layer, 2], bf(ffn_w1[layer, 1]), bf(ffn_w3[layer, 1]), bf(ffn_w2[layer, 1]))

        x_lat = _ffn(x_lat, mod_lat, 0, *ffn_a)
        x_ctx = _ffn(x_ctx, mod_ctx, 0, *ffn_a)

        wp = _proj_weight(w_in[layer, :, :n_mixcols])
        wg = bf(w_in[layer, :, n_mixcols:])
        qn = _pad_cols(mla_q_norm[layer].reshape(1, B_Q_LORA), 256)
        wuq = mla_w_uq[layer].reshape(B_Q_LORA, B_HEADS, B_NOPE + B_ROPE)
        wuq = jnp.pad(wuq, ((0, 256 - B_Q_LORA), (0, 0), (0, LANES - B_NOPE - B_ROPE))).reshape(256, B_HEADS * LANES)
        wukv = mla_w_ukv[layer].reshape(B_KV_LORA, B_HEADS, B_NOPE + B_V)
        wuk = jnp.pad(wukv[:, :, :B_NOPE], ((0, 0), (0, 0), (0, LANES - B_NOPE))).reshape(B_KV_LORA, B_HEADS * LANES)
        wuv = jnp.pad(jnp.transpose(wukv[:, :, B_NOPE:], (1, 0, 2)), ((0, 0), (0, 0), (0, LANES - B_V)))
        kvn = mla_kv_norm[layer].reshape(1, B_KV_LORA)
        gqn = _pad_cols(gqa_q_norm[layer].reshape(1, D_HEAD), LANES)
        gkn = _pad_cols(gqa_k_norm[layer].reshape(1, D_HEAD), LANES)
        lb = lb_all[:, layer]
        proj_args = (norm_pre[layer, 1], wp)
        mla_args = (qn, bf(wuq), kvn, bf(wuk), bf(wuv), lb, gqn, gkn)

        pl_ = _inproj(x_lat, mod_lat, *proj_args, rope_b_lat, rope_d_lat, *mla_args)
        pc_ = _inproj(x_ctx, mod_ctx, *proj_args, rope_b_ctx, rope_d_ctx, *mla_args)
        (u_l, u2_l, bq_l, bk_l, bv_l, hq_l, hv_l, hk_l, hl_l, hg_l, dq_l, dk_l, dv_l) = pl_
        (u_c, u2_c, bq_c, bk_c, bv_c, hq_c, hv_c, hk_c, hl_c, hg_c, dq_c, dk_c, dv_c) = pc_

        ys_l, ys_c = [], []
        for dr, reverse in enumerate((False, True)):
            nlev = 9
            wts = _s5_weights(s5_lambda_re[layer, dr], s5_lambda_im[layer, dr], s5_log_dt[layer, dr],
                              s5_b_re[layer, dr], s5_b_im[layer, dr], s5_c_re[layer, dr], s5_c_im[layer, dr],
                              reverse, nlev)
            x0 = jnp.zeros((nb, 1, 2 * S5_NSTATE), F32)
            y_c, x_end = _s5_scan(u2_c, x0, wts, reverse)
            y_l, _ = _s5_scan(u2_l, x_end, wts, reverse)
            ys_l.append(y_l)
            ys_c.append(y_c)

        s0 = jnp.zeros((2, nb, HG_W, HG_W), F32)
        ho_c, s_ctx = _hgrn_scan(hq_c, hk_c, hv_c, hl_c, s0, hg_consts)
        ho_l, _ = _hgrn_scan(hq_l, hk_l, hv_l, hl_l, s_ctx, hg_consts)

        bo_l = _attention(bq_l, [(bk_c, bv_c), (bk_l, bv_l)])
        do_l = _attention(dq_l, [(dk_c, dv_c), (dk_l, dv_l)])

        merge_w = (norm_pre[layer, 1], norm_post[layer, 1], wg, bf(w_branch[layer]), bf(w_out[layer]))
        s5_ro = (s5_d[layer].reshape(1, A_WIDTH), bf(s5_w_glu[layer]))
        hn = jnp.tile(hgrn_o_norm[layer], C_HEADS).reshape(1, HG_W)
        x_lat_new = _merge(x_lat, mod_lat, *merge_w, u_l, ys_l[0], ys_l[1], *s5_ro, bo_l, ho_l, hg_l, hn, avg, do_l)
        if not last:
            bo_c = _attention(bq_c, [(bk_c, bv_c)])
            do_c = _attention(dq_c, [(dk_c, dv_c)])
            x_ctx = _merge(x_ctx, mod_ctx, *merge_w, u_c, ys_c[0], ys_c[1], *s5_ro, bo_c, ho_c, hg_c, hn, avg, do_c)
            x_ctx = _ffn(x_ctx, mod_ctx, 2, *ffn_b)
        x_lat = _ffn(x_lat_new, mod_lat, 2, *ffn_b)
    return x_lat
```

```python
import functools

import jax
import jax.numpy as jnp
from jax import lax
from jax.experimental import pallas as pl
from jax.experimental.pallas import tpu as pltpu

GRID_W = 64
FFN_RES_WEIGHT = 0.5
N_MOD = 9
EPS = 1e-6
ROPE_THETA = 10000.0
F_FLOOR = 1e-20

A_WIDTH = 256
A_GROUP = 16
A_GROUPS = A_WIDTH // A_GROUP
A_STATE = 64

B_HEADS = 4
B_NOPE = 64
B_ROPE = 32
B_V = 64
B_Q_LORA = 192
B_KV_LORA = 128

C_HEADS = 4
C_DK = 64
C_DV = 64

D_HEADS = 4
D_KV_HEADS = 2
D_HEAD = 64

N_BRANCH = 4
BRANCH_W = 256

LANES = 128
VMEM_LIMIT_BYTES = 56 * 1024 * 1024

S5_R = 4
S5_LANES = S5_R * A_WIDTH
S5_NSTATE = A_GROUPS * A_STATE
HG_T = 128
HG_LEVELS = 7
HG_W = C_HEADS * C_DK
ATTN_ONES = 16
ATTN_QUERIES = (1024, 512, 256, 128)
ATTN_UNROLL = 8

SLOT_S5 = 0
SLOT_CQ = 256
SLOT_CKV = 512
SLOT_KR = 640
SLOT_HQ = 768
SLOT_HV = 1024
SLOT_HF = 1280
SLOT_HB = 1536
SLOT_HG = 1792
SLOT_GQ = 2048
SLOT_GK = 2560
SLOT_GV = 2816
N_PROJ = 3072

BF16 = jnp.bfloat16
F32 = jnp.float32
LOG2E = 1.4426950408889634


def _params(sem):
    return pltpu.CompilerParams(dimension_semantics=sem, vmem_limit_bytes=VMEM_LIMIT_BYTES)


def _pick(n, candidates):
    for c in candidates:
        if n % c == 0:
            return c
    raise ValueError(f"no tile for {n} in {candidates}")


def _dot(a, b):
    return jnp.dot(a, b, preferred_element_type=F32)


def _dot_nt(a, b):
    return lax.dot_general(a, b, (((1,), (1,)), ((), ())), preferred_element_type=F32)


def _rms(x, g, n=None):
    n = x.shape[-1] if n is None else n
    ms = jnp.sum(x * x, axis=-1, keepdims=True) * (1.0 / n)
    return x * lax.rsqrt(ms + EPS) * g


def _mod_kernel(c_ref, w_ref, b_ref, o_ref):
    c = c_ref[...]
    a = (c * jax.nn.sigmoid(c)).astype(BF16)
    o_ref[0] = _dot(a, w_ref[0].astype(BF16)) + b_ref[0]


def _modulation(cvec, w_ada, b_ada):
    nl, d, nm = w_ada.shape
    rows = cvec.shape[0]
    tn = _pick(nm, (1152, 1024, 512, 256, 128))
    return pl.pallas_call(
        _mod_kernel,
        grid=(nl, nm // tn),
        in_specs=[
            pl.BlockSpec((rows, d), lambda l, n: (0, 0)),
            pl.BlockSpec((1, d, tn), lambda l, n: (l, 0, n)),
            pl.BlockSpec((1, 1, tn), lambda l, n: (l, 0, n)),
        ],
        out_specs=pl.BlockSpec((1, rows, tn), lambda l, n: (l, 0, n)),
        out_shape=jax.ShapeDtypeStruct((nl, rows, nm), F32),
        compiler_params=_params(("parallel", "parallel")),
        name="adaln_mod",
    )(cvec, w_ada, b_ada.reshape(nl, 1, nm))


def _ffn_kernel(x_ref, mod_ref, gpre_ref, gpost_ref, w1_ref, w3_ref, w2_ref, o_ref, h_scr, acc_scr, *, j, nf):
    f = pl.program_id(2)

    @pl.when(f == 0)
    def _():
        x = x_ref[0]
        shift = mod_ref[0, pl.ds(3 * j, 1), :]
        scale = mod_ref[0, pl.ds(3 * j + 1, 1), :]
        h = _rms(x, gpre_ref[...]) * (1.0 + scale) + shift
        h_scr[...] = h.astype(BF16)
        acc_scr[...] = jnp.zeros_like(acc_scr)

    h = h_scr[...]
    a = _dot(h, w1_ref[...])
    b = _dot(h, w3_ref[...])
    t = (a * jax.nn.sigmoid(a) * b).astype(BF16)
    acc_scr[...] += _dot(t, w2_ref[...])

    @pl.when(f == nf - 1)
    def _():
        gate = mod_ref[0, pl.ds(3 * j + 2, 1), :]
        y = _rms(acc_scr[...], gpost_ref[...])
        o_ref[0] = x_ref[0] + FFN_RES_WEIGHT * gate * y


def _ffn(x, mod, j, g_pre, g_post, w1, w3, w2):
    nb, s, d = x.shape
    dff = w1.shape[1]
    tm = _pick(s, (1024, 512, 256))
    tf = _pick(dff, (256, 128))
    nf = dff // tf
    return pl.pallas_call(
        functools.partial(_ffn_kernel, j=j, nf=nf),
        grid=(nb, s // tm, nf),
        in_specs=[
            pl.BlockSpec((1, tm, d), lambda b, i, f: (b, i, 0)),
            pl.BlockSpec((1, N_MOD, d), lambda b, i, f: (b, 0, 0)),
            pl.BlockSpec((1, d), lambda b, i, f: (0, 0)),
            pl.BlockSpec((1, d), lambda b, i, f: (0, 0)),
            pl.BlockSpec((d, tf), lambda b, i, f: (0, f)),
            pl.BlockSpec((d, tf), lambda b, i, f: (0, f)),
            pl.BlockSpec((tf, d), lambda b, i, f: (f, 0)),
        ],
        out_specs=pl.BlockSpec((1, tm, d), lambda b, i, f: (b, i, 0)),
        out_shape=jax.ShapeDtypeStruct(x.shape, F32),
        scratch_shapes=[pltpu.VMEM((tm, d), BF16), pltpu.VMEM((tm, d), F32)],
        compiler_params=_params(("parallel", "parallel", "arbitrary")),
        name="ffn_sublayer",
    )(x, mod, g_pre.reshape(1, d), g_post.reshape(1, d), w1, w3, w2)


def _rope(x, cos, sin_a, sin_b, quarter):
    w = x.shape[-1]
    return x * cos + pltpu.roll(x, w - quarter, 1) * sin_a + pltpu.roll(x, quarter, 1) * sin_b


def _inproj_kernel(x_ref, mod_ref, gpre_ref, w_ref, rb_ref, rd_ref, qn_ref, wuq_ref, kvn_ref, wuk_ref, wuv_ref,
                   lb_ref, gqn_ref, gkn_ref,
                   u_ref, u2_ref, bq_ref, bk_ref, bv_ref, hq_ref, hv_ref, hk_ref, hl_ref, hg_ref,
                   dq_ref, dk_ref, dv_ref, h_scr, u_scr, *, tm):
    x = x_ref[0]
    shift = mod_ref[0, pl.ds(3, 1), :]
    scale = mod_ref[0, pl.ds(4, 1), :]
    h_scr[...] = (_rms(x, gpre_ref[...]) * (1.0 + scale) + shift).astype(BF16)

    def proj(lo, width):
        return _dot(h_scr[...], w_ref[:, lo:lo + width])

    u = proj(SLOT_S5, A_WIDTH)
    u_ref[0] = u.astype(BF16)
    for hf in range(A_WIDTH // LANES):
        u_scr[hf] = u[:, hf * LANES:(hf + 1) * LANES]
    for r in range(S5_R):
        for hf in range(A_WIDTH // LANES):
            lo = r * A_WIDTH + hf * LANES
            u2_ref[0, :, lo:lo + LANES] = u_scr[hf, pl.ds(r, tm // S5_R, stride=S5_R), :].astype(BF16)

    cos_b, sa_b, sb_b = rb_ref[0], rb_ref[1], rb_ref[2]
    cq = _rms(proj(SLOT_CQ, 256), qn_ref[...], n=B_Q_LORA).astype(BF16)
    q = _dot(cq, wuq_ref[...])
    b_scale = (B_NOPE + B_ROPE) ** -0.5 * LOG2E
    for hd in range(B_HEADS):
        qh = _rope(q[:, hd * LANES:(hd + 1) * LANES], cos_b, sa_b, sb_b, B_ROPE // 4)
        bq_ref[0, hd] = (qh * b_scale).T.astype(BF16)
    ckv = _rms(proj(SLOT_CKV, B_KV_LORA), kvn_ref[...]).astype(BF16)
    kn = _dot(ckv, wuk_ref[...])
    kr = _rope(proj(SLOT_KR, LANES), cos_b, sa_b, sb_b, B_ROPE // 4)
    for hd in range(B_HEADS):
        bk_ref[0, hd] = (kn[:, hd * LANES:(hd + 1) * LANES] + kr).astype(BF16)
        bv_ref[0, hd] = _dot(ckv, wuv_ref[hd]).T[:B_V].astype(BF16)

    hq_ref[0] = proj(SLOT_HQ, HG_W).astype(BF16)
    hv_ref[0] = proj(SLOT_HV, HG_W).astype(BF16)
    hg_ref[0] = proj(SLOT_HG, HG_W).astype(BF16)
    for dr, slot in enumerate((SLOT_HF, SLOT_HB)):
        z = proj(slot, HG_W)
        lb = lb_ref[pl.ds(dr, 1), :]
        f = lb + (1.0 - lb) * jax.nn.sigmoid(z)
        hl_ref[dr, 0] = jnp.log(jnp.maximum(f, F_FLOOR))
        hk_ref[dr, 0] = ((1.0 - lb) * jax.nn.sigmoid(-z)).astype(BF16)

    cos_d, sa_d, sb_d = rd_ref[0], rd_ref[1], rd_ref[2]
    d_scale = D_HEAD ** -0.5 * LOG2E
    for hd in range(D_HEADS):
        qh = _rms(proj(SLOT_GQ + hd * LANES, LANES), gqn_ref[...], n=D_HEAD)
        qh = _rope(qh, cos_d, sa_d, sb_d, D_HEAD // 4) * d_scale
        dq_ref[0, hd] = qh.T[:D_HEAD].astype(BF16)
    for hd in range(D_KV_HEADS):
        kh = _rms(proj(SLOT_GK + hd * LANES, LANES), gkn_ref[...], n=D_HEAD)
        kh = _rope(kh, cos_d, sa_d, sb_d, D_HEAD // 4)
        dk_ref[0, hd] = kh[:, :D_HEAD].astype(BF16)
        dv_ref[0, hd] = proj(SLOT_GV + hd * LANES, LANES).T[:D_HEAD].astype(BF16)


def _inproj(x, mod, g_pre, wp, rope_b, rope_d, qn, wuq, kvn, wuk, wuv, lb, gqn, gkn):
    nb, s, d = x.shape
    tm = _pick(s, (512, 256))
    const2 = lambda b, i: (0, 0)
    const3 = lambda b, i: (0, 0, 0)
    tok = lambda w: pl.BlockSpec((1, tm, w), lambda b, i: (b, i, 0))
    headed = lambda nh, w: pl.BlockSpec((1, nh, tm, w), lambda b, i: (b, 0, i, 0))
    headed_t = lambda nh, w: pl.BlockSpec((1, nh, w, tm), lambda b, i: (b, 0, 0, i))
    dirtok = lambda w: pl.BlockSpec((2, 1, tm, w), lambda b, i: (0, b, i, 0))
    sd = jax.ShapeDtypeStruct
    outs = [
        (sd((nb, s, A_WIDTH), BF16), tok(A_WIDTH)),
        (sd((nb, s // S5_R, S5_LANES), BF16), pl.BlockSpec((1, tm // S5_R, S5_LANES), lambda b, i: (b, i, 0))),
        (sd((nb, B_HEADS, LANES, s), BF16), headed_t(B_HEADS, LANES)),
        (sd((nb, B_HEADS, s, LANES), BF16), headed(B_HEADS, LANES)),
        (sd((nb, B_HEADS, B_V, s), BF16), headed_t(B_HEADS, B_V)),
        (sd((nb, s, HG_W), BF16), tok(HG_W)),
        (sd((nb, s, HG_W), BF16), tok(HG_W)),
        (sd((2, nb, s, HG_W), BF16), dirtok(HG_W)),
        (sd((2, nb, s, HG_W), F32), dirtok(HG_W)),
        (sd((nb, s, HG_W), BF16), tok(HG_W)),
        (sd((nb, D_HEADS, D_HEAD, s), BF16), headed_t(D_HEADS, D_HEAD)),
        (sd((nb, D_KV_HEADS, s, D_HEAD), BF16), headed(D_KV_HEADS, D_HEAD)),
        (sd((nb, D_KV_HEADS, D_HEAD, s), BF16), headed_t(D_KV_HEADS, D_HEAD)),
    ]
    return pl.pallas_call(
        functools.partial(_inproj_kernel, tm=tm),
        grid=(nb, s // tm),
        in_specs=[
            pl.BlockSpec((1, tm, d), lambda b, i: (b, i, 0)),
            pl.BlockSpec((1, N_MOD, d), lambda b, i: (b, 0, 0)),
            pl.BlockSpec((1, d), const2),
            pl.BlockSpec((d, N_PROJ), const2),
            pl.BlockSpec((3, tm, LANES), lambda b, i: (0, i, 0)),
            pl.BlockSpec((3, tm, LANES), lambda b, i: (0, i, 0)),
            pl.BlockSpec((1, 256), const2),
            pl.BlockSpec((256, B_HEADS * LANES), const2),
            pl.BlockSpec((1, B_KV_LORA), const2),
            pl.BlockSpec((B_KV_LORA, B_HEADS * LANES), const2),
            pl.BlockSpec((B_HEADS, B_KV_LORA, LANES), const3),
            pl.BlockSpec((2, HG_W), const2),
            pl.BlockSpec((1, LANES), const2),
            pl.BlockSpec((1, LANES), const2),
        ],
        out_specs=[o[1] for o in outs],
        out_shape=[o[0] for o in outs],
        scratch_shapes=[pltpu.VMEM((tm, d), BF16), pltpu.VMEM((A_WIDTH // LANES, tm, LANES), F32)],
        compiler_params=_params(("parallel", "parallel")),
        name="mixer_inproj",
    )(x, mod, g_pre.reshape(1, d), wp, rope_b, rope_d, qn, wuq, kvn, wuk, wuv, lb, gqn, gkn)


def _s5_kernel(u2_ref, x0_ref, tb_ref, wst_ref, wout_ref, pq_ref, y_ref, xf_ref, carry, *, rows, reverse, nlev):
    i = pl.program_id(1)

    @pl.when(i == 0)
    def _():
        carry[...] = x0_ref[0]

    u2 = u2_ref[0]
    sloc = _dot(u2, wst_ref[...])
    ridx = lax.broadcasted_iota(jnp.int32, (rows, 1), 0)

    def cmul(xv, lev):
        p = pq_ref[pl.ds(2 * lev, 1), :]
        q = pq_ref[pl.ds(2 * lev + 1, 1), :]
        return xv * p + pltpu.roll(xv, S5_NSTATE, 1) * q

    if reverse:
        e = jnp.where(ridx == rows - 1, carry[...], pltpu.roll(sloc, rows - 1, 0))
    else:
        e = jnp.where(ridx == 0, carry[...], pltpu.roll(sloc, 1, 0))
    xin = e
    for lev in range(nlev):
        dist = 1 << lev
        if reverse:
            sh = jnp.where(ridx < rows - dist, pltpu.roll(xin, rows - dist, 0), 0.0)
        else:
            sh = jnp.where(ridx >= dist, pltpu.roll(xin, dist, 0), 0.0)
        xin = xin + cmul(sh, lev)

    last = 0 if reverse else rows - 1
    nxt = cmul(xin[last:last + 1, :], 0) + sloc[last:last + 1, :]
    carry[...] = nxt
    xf_ref[0] = nxt

    y2 = _dot(u2, tb_ref[...]) + _dot(xin.astype(BF16), wout_ref[...])
    for r in range(S5_R):
        for hf in range(A_WIDTH // LANES):
            lo = r * A_WIDTH + hf * LANES
            y_ref[0, hf, pl.ds(r, rows, stride=S5_R), :] = y2[:, lo:lo + LANES]


def _s5_scan(u2, x0, wts, reverse):
    tb, wst, wout, pq = wts
    nb, n2, _ = u2.shape
    rows = _pick(n2, (512, 256, 128, 64))
    nt = n2 // rows
    nlev = max(1, (rows - 1).bit_length())
    order = (lambda b, i: (b, nt - 1 - i, 0)) if reverse else (lambda b, i: (b, i, 0))
    order_out = (lambda b, i: (b, 0, nt - 1 - i, 0)) if reverse else (lambda b, i: (b, 0, i, 0))
    const2 = lambda b, i: (0, 0)
    y, xf = pl.pallas_call(
        functools.partial(_s5_kernel, rows=rows, reverse=reverse, nlev=nlev),
        grid=(nb, nt),
        in_specs=[
            pl.BlockSpec((1, rows, S5_LANES), order),
            pl.BlockSpec((1, 1, 2 * S5_NSTATE), lambda b, i: (b, 0, 0)),
            pl.BlockSpec((S5_LANES, S5_LANES), const2),
            pl.BlockSpec((S5_LANES, 2 * S5_NSTATE), const2),
            pl.BlockSpec((2 * S5_NSTATE, S5_LANES), const2),
            pl.BlockSpec(pq.shape, const2),
        ],
        out_specs=[
            pl.BlockSpec((1, A_WIDTH // LANES, rows * S5_R, LANES), order_out),
            pl.BlockSpec((1, 1, 2 * S5_NSTATE), lambda b, i: (b, 0, 0)),
        ],
        out_shape=[
            jax.ShapeDtypeStruct((nb, A_WIDTH // LANES, n2 * S5_R, LANES), F32),
            jax.ShapeDtypeStruct((nb, 1, 2 * S5_NSTATE), F32),
        ],
        scratch_shapes=[pltpu.VMEM((1, 2 * S5_NSTATE), F32)],
        compiler_params=_params(("parallel", "arbitrary")),
        name="s5_scan_rev" if reverse else "s5_scan_fwd",
    )(u2, x0, tb, wst, wout, pq)
    return y, xf


def _s5_weights(lam_re, lam_im, log_dt, b_re, b_im, c_re, c_im, reverse, nlev):
    g, n, r = A_GROUPS, A_STATE, S5_R
    lam_re = jnp.minimum(lam_re.astype(F32), -1e-4)
    lam_im = lam_im.astype(F32)
    dt = jnp.exp(log_dt.astype(F32))[:, None]
    mag = jnp.exp(lam_re * dt)
    a_re = mag * jnp.cos(lam_im * dt)
    a_im = mag * jnp.sin(lam_im * dt)
    den = lam_re * lam_re + lam_im * lam_im
    num_re = a_re - 1.0
    f_re = (num_re * lam_re + a_im * lam_im) / den
    f_im = (a_im * lam_re - num_re * lam_im) / den
    bb_re = f_re[..., None] * b_re - f_im[..., None] * b_im
    bb_im = f_re[..., None] * b_im + f_im[..., None] * b_re

    def cm(xr, xi, yr, yi):
        return xr * yr - xi * yi, xr * yi + xi * yr

    pr, pi = [jnp.ones_like(a_re)], [jnp.zeros_like(a_im)]
    for _ in range(r):
        nr, ni = cm(pr[-1], pi[-1], a_re, a_im)
        pr.append(nr)
        pi.append(ni)
    pr, pi = jnp.stack(pr), jnp.stack(pi)

    def lag(tau):
        xr, xi = cm(pr[tau][..., None], pi[tau][..., None], bb_re, bb_im)
        return jnp.einsum('gon,gnc->goc', c_re, xr) - jnp.einsum('gon,gnc->goc', c_im, xi)

    eye_g = jnp.eye(g, dtype=F32)
    zero_blk = jnp.zeros((g, A_GROUP, g, A_GROUP), F32)

    def group_diag(m):
        return jnp.einsum('gio,gh->giho', m, eye_g).reshape(g * A_GROUP, g * A_GROUP)

    blocks = []
    for r_in in range(r):
        row = []
        for r_out in range(r):
            tau = (r_in - r_out) if reverse else (r_out - r_in)
            if tau < 0:
                row.append(zero_blk.reshape(g * A_GROUP, g * A_GROUP))
            else:
                row.append(group_diag(jnp.swapaxes(lag(tau), 1, 2)))
        blocks.append(jnp.concatenate(row, axis=1))
    tb = jnp.concatenate(blocks, axis=0)

    wst_rows = []
    for r_in in range(r):
        steps = r_in if reverse else (r - 1 - r_in)
        xr, xi = cm(pr[steps][..., None], pi[steps][..., None], bb_re, bb_im)
        wr = jnp.einsum('gnc,gh->gchn', xr, eye_g).reshape(g * A_GROUP, g * n)
        wi = jnp.einsum('gnc,gh->gchn', xi, eye_g).reshape(g * A_GROUP, g * n)
        wst_rows.append(jnp.concatenate([wr, wi], axis=1))
    wst = jnp.concatenate(wst_rows, axis=0)

    wout_cols = []
    for r_out in range(r):
        steps = (r - r_out) if reverse else (r_out + 1)
        yr, yi = cm(c_re, c_im, pr[steps][:, None, :], pi[steps][:, None, :])
        wr = jnp.einsum('gon,gh->gnho', yr, eye_g).reshape(g * n, g * A_GROUP)
        wi = jnp.einsum('gon,gh->gnho', -yi, eye_g).reshape(g * n, g * A_GROUP)
        wout_cols.append(jnp.concatenate([wr, wi], axis=0))
    wout = jnp.concatenate(wout_cols, axis=1)

    lr, li = pr[r].reshape(1, g * n), pi[r].reshape(1, g * n)
    rows = []
    for _ in range(nlev):
        rows.append(jnp.concatenate([lr, lr], axis=1))
        rows.append(jnp.concatenate([-li, li], axis=1))
        lr, li = cm(lr, li, lr, li)
    pq = jnp.concatenate(rows, axis=0)
    return tb.astype(BF16), wst.astype(BF16), wout.astype(BF16), pq


def _split3(x):
    x1 = x.astype(BF16)
    r1 = x - x1.astype(F32)
    x2 = r1.astype(BF16)
    x3 = (r1 - x2.astype(F32)).astype(BF16)
    return x1, x2, x3


def _hgrn_chunk(q, k, vb, lf, st, tri, sel, sgn_ref, msk_ref, dr, heads, bd, nlev):
    l1, l2, l3 = _split3(lf)
    cum = _dot(tri, l1) + _dot(tri, l2) + _dot(tri, l3)
    tot = jnp.sum(lf, axis=0, keepdims=True)

    mids = _dot(sel, cum.astype(BF16))
    t = q.shape[0]

    def head_scores(qx, kx):
        qh = jnp.concatenate([jnp.where(heads[hd], qx, jnp.zeros_like(qx)) for hd in range(C_HEADS)], axis=0)
        s = _dot_nt(qh, kx)
        return [s[hd * t:(hd + 1) * t, :] for hd in range(C_HEADS)]

    diag = msk_ref[dr, nlev] > 0.5
    att = [jnp.where(diag, s, 0.0) for s in head_scores(q.astype(BF16), k.astype(BF16))]
    for lev in range(nlev):
        e = jnp.exp(sgn_ref[dr, lev] * (cum - mids[lev * t:(lev + 1) * t, :]))
        m = msk_ref[dr, lev] > 0.5
        for hd, s in enumerate(head_scores((q * e).astype(BF16), (k * e).astype(BF16))):
            att[hd] = att[hd] + jnp.where(m, s, 0.0)

    o = _dot_nt((q * jnp.exp(cum)).astype(BF16), st.astype(BF16))
    for hd in range(C_HEADS):
        o = o + _dot(att[hd].astype(BF16), jnp.where(heads[hd], vb, jnp.zeros_like(vb)))

    kend = (k * jnp.exp(tot - cum)).astype(BF16)
    vt = vb.astype(F32).T.astype(BF16)
    new = st * jnp.exp(tot) + jnp.where(bd, _dot(vt, kend), 0.0)
    return o, new


def _hgrn_kernel(qf_ref, qr_ref, kf_ref, kr_ref, vf_ref, vr_ref, lff_ref, lfr_ref, s0_ref,
                 tri_ref, sel_ref, sgn_ref, msk_ref, hm_ref, bd_ref,
                 of_ref, or_ref, sf_ref, st_scr, *, nlev, nb):
    c = pl.program_id(0)

    @pl.when(c == 0)
    def _():
        st_scr[...] = s0_ref[...]

    heads = [hm_ref[pl.ds(hd, 1), :] > 0.5 for hd in range(C_HEADS)]
    bd = bd_ref[...] > 0.5
    dirs = ((qf_ref, kf_ref, vf_ref, lff_ref, of_ref), (qr_ref, kr_ref, vr_ref, lfr_ref, or_ref))
    for dr, (q_ref, k_ref, v_ref, lf_ref, o_ref) in enumerate(dirs):
        for b in range(nb):
            o, new = _hgrn_chunk(q_ref[b].astype(F32), k_ref[0, b].astype(F32), v_ref[b], lf_ref[0, b],
                                 st_scr[dr, b], tri_ref[dr], sel_ref[dr], sgn_ref, msk_ref, dr, heads, bd, nlev)
            o_ref[b] = o
            st_scr[dr, b] = new
            sf_ref[dr, b] = new


def _hgrn_consts():
    t = HG_T
    ti = jnp.arange(t)[:, None]
    si = jnp.arange(t)[None, :]
    tri, sel, sgn, msk = [], [], [], []
    for reverse in (False, True):
        tri.append((si >= ti) if reverse else (si <= ti))
        sels, sgns, msks = [], [], []
        for lev in range(HG_LEVELS):
            h = 1 << lev
            blk_t, blk_s = ti // (2 * h), si // (2 * h)
            hi_t, hi_s = (ti % (2 * h)) >= h, (si % (2 * h)) >= h
            if reverse:
                mid = blk_t * 2 * h + h
                q_role_t, k_role_s = ~hi_t, hi_s
            else:
                mid = blk_t * 2 * h + h - 1
                q_role_t, k_role_s = hi_t, ~hi_s
            sels.append(si == mid)
            sgns.append(jnp.where(q_role_t, 1.0, -1.0))
            msks.append((blk_t == blk_s) & q_role_t & k_role_s)
        msks.append(ti == si)
        sel.append(jnp.concatenate(sels, axis=0))
        sgn.append(jnp.stack(sgns))
        msk.append(jnp.stack(msks))
    lane_head = jnp.arange(HG_W) // C_DK
    hm = (lane_head[None, :] == jnp.arange(C_HEADS)[:, None]).astype(F32)
    bd = (lane_head[:, None] == lane_head[None, :]).astype(F32)
    return (jnp.stack(tri).astype(BF16), jnp.stack(sel).astype(BF16), jnp.stack(sgn).astype(F32),
            jnp.stack(msk).astype(F32), hm, bd)


def _hgrn_scan(q, k, v, lf, s0, consts):
    tri, sel, sgn, msk, hm, bd = consts
    nb, s, w = q.shape
    t = HG_T
    nc = s // t
    fwd3 = pl.BlockSpec((nb, t, w), lambda c: (0, c, 0))
    rev3 = pl.BlockSpec((nb, t, w), lambda c: (0, nc - 1 - c, 0))
    fwd4 = pl.BlockSpec((1, nb, t, w), lambda c: (0, 0, c, 0))
    rev4 = pl.BlockSpec((1, nb, t, w), lambda c: (1, 0, nc - 1 - c, 0))
    whole = lambda a: pl.BlockSpec(a.shape, lambda c: (0,) * a.ndim)
    sgn = sgn.reshape(2, HG_LEVELS, t, 1)
    o_f, o_r, sf = pl.pallas_call(
        functools.partial(_hgrn_kernel, nlev=HG_LEVELS, nb=nb),
        grid=(nc,),
        in_specs=[fwd3, rev3, fwd4, rev4, fwd3, rev3, fwd4, rev4, whole(s0),
                  whole(tri), whole(sel), whole(sgn), whole(msk), whole(hm), whole(bd)],
        out_specs=[fwd3, rev3, whole(s0)],
        out_shape=[
            jax.ShapeDtypeStruct((nb, s, w), F32),
            jax.ShapeDtypeStruct((nb, s, w), F32),
            jax.ShapeDtypeStruct((2, nb, w, w), F32),
        ],
        scratch_shapes=[pltpu.VMEM((2, nb, w, w), F32)],
        compiler_params=_params(("arbitrary",)),
        name="hgrn_scan",
    )(q, q, k, k, v, v, lf, lf, s0, tri, sel, sgn, msk, hm, bd)
    return (o_f, o_r), sf


def _attn_kernel(*refs, nseg, tks, g, tq):
    q_ref = refs[0]
    kv_refs = refs[1:1 + 2 * nseg]
    o_ref = refs[1 + 2 * nseg]
    dv = o_ref.shape[-1]
    n = g * tq
    qt = jnp.concatenate([q_ref[0, hd] for hd in range(g)], axis=-1)

    def scores(k):
        return _dot(k, qt)

    def absorb(s, vt, carry):
        m, acc = carry
        m_new = jnp.maximum(m, jnp.max(s, axis=0, keepdims=True))
        alpha = jnp.exp2(m - m_new)
        p = jnp.exp2(s - m_new).astype(BF16)
        vt1 = jnp.concatenate([vt, jnp.ones((ATTN_ONES, vt.shape[1]), BF16)], axis=0)
        acc = alpha * acc + _dot(vt1, p)
        return m_new, acc

    carry = (jnp.full((1, n), -jnp.inf, F32), jnp.zeros((dv + ATTN_ONES, n), F32))
    pending = None
    for seg in range(nseg):
        k_ref, vt_ref = kv_refs[2 * seg], kv_refs[2 * seg + 1]
        tk = tks[seg]
        nk = k_ref.shape[2] // tk
        s_first = scores(k_ref[0, 0, 0:tk, :])
        if pending is not None:
            carry = absorb(*pending, carry)
        if nk == 1:
            pending = (s_first, vt_ref[0, 0])
            continue

        def body(j, c, k_ref=k_ref, vt_ref=vt_ref, tk=tk, nk=nk):
            s_cur, m, acc = c
            off_next = pl.multiple_of(jnp.minimum(j + 1, nk - 1) * tk, tk)
            s_next = scores(k_ref[0, 0, pl.ds(off_next, tk), :])
            off = pl.multiple_of(j * tk, tk)
            m, acc = absorb(s_cur, vt_ref[0, 0, :, pl.ds(off, tk)], (m, acc))
            return s_next, m, acc

        unroll = ATTN_UNROLL if nk % ATTN_UNROLL == 0 else 1
        _, m, acc = lax.fori_loop(0, nk, body, (s_first,) + carry, unroll=unroll)
        carry = (m, acc)
        pending = None
    m, acc = carry if pending is None else absorb(*pending, carry)
    out = acc[:dv] / acc[dv:dv + 1, :]
    out = jnp.concatenate([out, jnp.zeros((LANES - dv, n), F32)], axis=0).T
    o_ref[0] = out[:, :dv].reshape(g, tq, dv).astype(o_ref.dtype)


def _attention(q, kvs):
    nb, hq, dk, sq = q.shape
    hkv = kvs[0][0].shape[1]
    dv = kvs[0][1].shape[2]
    g = hq // hkv
    tq = _pick(sq, tuple(n // g for n in ATTN_QUERIES))
    tks = tuple(_pick(k.shape[2], (512, 256, 128)) for k, _ in kvs)
    in_specs = [pl.BlockSpec((1, g, dk, tq), lambda b, h, i: (b, h, 0, i))]
    args = [q]
    for k, v in kvs:
        sk = k.shape[2]
        in_specs.append(pl.BlockSpec((1, 1, sk, dk), lambda b, h, i: (b, h, 0, 0)))
        in_specs.append(pl.BlockSpec((1, 1, dv, sk), lambda b, h, i: (b, h, 0, 0)))
        args += [k, v]
    return pl.pallas_call(
        functools.partial(_attn_kernel, nseg=len(kvs), tks=tks, g=g, tq=tq),
        grid=(nb, hkv, sq // tq),
        in_specs=in_specs,
        out_specs=pl.BlockSpec((1, g, tq, dv), lambda b, h, i: (b, h, i, 0)),
        out_shape=jax.ShapeDtypeStruct((nb, hq, sq, dv), BF16),
        compiler_params=_params(("parallel", "parallel", "arbitrary")),
        name="attention",
    )(*args)


def _merge_kernel(x_ref, mod_ref, gpre_ref, gpost_ref, wg_ref, wb_ref, wo_ref,
                  u_ref, yf_ref, yr_ref, sd_ref, wglu_ref, bo_ref, hof_ref, hor_ref, hg_ref, hn_ref, avg_ref, do_ref,
                  o_ref):
    x = x_ref[0]
    shift = mod_ref[0, pl.ds(3, 1), :]
    scale = mod_ref[0, pl.ds(4, 1), :]
    gate = mod_ref[0, pl.ds(5, 1), :]
    h = (_rms(x, gpre_ref[...]) * (1.0 + scale) + shift).astype(BF16)
    d = x.shape[-1]

    def branch_gate(i):
        return jax.nn.sigmoid(_dot(h, wg_ref[:, i * d:(i + 1) * d]))

    ysum = yf_ref[0] + yr_ref[0]
    y = sd_ref[...] * u_ref[0].astype(F32) + jnp.concatenate([ysum[hf] for hf in range(A_WIDTH // LANES)], axis=-1)
    ge = jax.nn.gelu(y)
    ya = ge * jax.nn.sigmoid(_dot(ge.astype(BF16), wglu_ref[...]))
    merged = branch_gate(0) * _dot(ya.astype(BF16), wb_ref[0])

    yb = _dot(bo_ref[0, 0], wb_ref[1, 0:B_V, :])
    for hd in range(1, B_HEADS):
        yb = yb + _dot(bo_ref[0, hd], wb_ref[1, hd * B_V:(hd + 1) * B_V, :])
    merged = merged + branch_gate(1) * yb

    o2 = hof_ref[0] + hor_ref[0]
    ms = _dot((o2 * o2).astype(BF16), avg_ref[...])
    gz = hg_ref[0].astype(F32)
    yc = o2 * lax.rsqrt(ms + EPS) * hn_ref[...] * (gz * jax.nn.sigmoid(gz))
    merged = merged + branch_gate(2) * _dot(yc.astype(BF16), wb_ref[2])

    yd = _dot(do_ref[0, 0], wb_ref[3, 0:D_HEAD, :])
    for hd in range(1, D_HEADS):
        yd = yd + _dot(do_ref[0, hd], wb_ref[3, hd * D_HEAD:(hd + 1) * D_HEAD, :])
    merged = merged + branch_gate(3) * yd

    yo = _dot(merged.astype(BF16), wo_ref[...])
    o_ref[0] = x + gate * _rms(yo, gpost_ref[...])


def _merge(x, mod, g_pre, g_post, wg, wb, wo, u, yf, yr, s5d, wglu, bo, ho, hg, hn, avg, do):
    nb, s, d = x.shape
    tm = _pick(s, (512, 256))
    const2 = lambda b, i: (0, 0)
    const3 = lambda b, i: (0, 0, 0)
    tok = lambda w: pl.BlockSpec((1, tm, w), lambda b, i: (b, i, 0))
    halves = pl.BlockSpec((1, A_WIDTH // LANES, tm, LANES), lambda b, i: (b, 0, i, 0))
    return pl.pallas_call(
        _merge_kernel,
        grid=(nb, s // tm),
        in_specs=[
            tok(d),
            pl.BlockSpec((1, N_MOD, d), lambda b, i: (b, 0, 0)),
            pl.BlockSpec((1, d), const2),
            pl.BlockSpec((1, d), const2),
            pl.BlockSpec((d, N_BRANCH * d), const2),
            pl.BlockSpec((N_BRANCH, BRANCH_W, d), const3),
            pl.BlockSpec((d, d), const2),
            tok(A_WIDTH), halves, halves,
            pl.BlockSpec((1, A_WIDTH), const2),
            pl.BlockSpec((A_WIDTH, A_WIDTH), const2),
            pl.BlockSpec((1, B_HEADS, tm, B_V), lambda b, i: (b, 0, i, 0)),
            tok(HG_W), tok(HG_W),
            tok(HG_W),
            pl.BlockSpec((1, HG_W), const2),
            pl.BlockSpec((HG_W, HG_W), const2),
            pl.BlockSpec((1, D_HEADS, tm, D_HEAD), lambda b, i: (b, 0, i, 0)),
        ],
        out_specs=tok(d),
        out_shape=jax.ShapeDtypeStruct(x.shape, F32),
        compiler_params=_params(("parallel", "parallel")),
        name="merge_out",
    )(x, mod, g_pre.reshape(1, d), g_post.reshape(1, d), wg, wb, wo, u, yf, yr, s5d, wglu, bo, ho[0], ho[1], hg, hn,
      avg, do)


def _pad_cols(w, width):
    return jnp.pad(w, ((0, 0), (0, width - w.shape[1])))


def _proj_weight(w_in_mix):
    offs = [0]
    for wdt in (A_WIDTH, B_Q_LORA, B_KV_LORA, B_ROPE, HG_W, HG_W, HG_W, HG_W, HG_W,
                D_HEADS * D_HEAD, D_KV_HEADS * D_HEAD, D_KV_HEADS * D_HEAD):
        offs.append(offs[-1] + wdt)
    p = [w_in_mix[:, offs[i]:offs[i + 1]] for i in range(12)]
    d = w_in_mix.shape[0]
    z = lambda n: jnp.zeros((d, n), w_in_mix.dtype)
    cols = [p[0], _pad_cols(p[1], 256), p[2],
            jnp.concatenate([z(B_NOPE), p[3], z(LANES - B_NOPE - B_ROPE)], axis=1),
            p[4], p[5], p[6], p[7], p[8]]
    for i, nh in ((9, D_HEADS), (10, D_KV_HEADS), (11, D_KV_HEADS)):
        for hd in range(nh):
            cols.append(_pad_cols(p[i][:, hd * D_HEAD:(hd + 1) * D_HEAD], LANES))
    return jnp.concatenate(cols, axis=1).astype(BF16)


def _rope_tables(n_tok, rot_dim, lane_off, identity):
    cos = jnp.ones((n_tok, LANES), F32)
    sin_a = jnp.zeros((n_tok, LANES), F32)
    sin_b = jnp.zeros((n_tok, LANES), F32)
    if not identity:
        n_rows = n_tok // GRID_W
        rows = jnp.repeat(jnp.arange(n_rows, dtype=F32), GRID_W)
        cols = jnp.tile(jnp.arange(GRID_W, dtype=F32), n_rows)
        half = rot_dim // 2
        inv = ROPE_THETA ** (-jnp.arange(0, half, 2, dtype=F32) / half)
        ang_r = rows[:, None] * inv
        ang_c = cols[:, None] * inv
        ang = jnp.concatenate([ang_r, ang_r, ang_c, ang_c], axis=-1)
        c, s = jnp.cos(ang), jnp.sin(ang)
        quarter = rot_dim // 4
        first = (jnp.arange(rot_dim) % (2 * quarter)) < quarter
        cos = cos.at[:, lane_off:lane_off + rot_dim].set(c)
        sin_a = sin_a.at[:, lane_off:lane_off + rot_dim].set(jnp.where(first, -s, 0.0))
        sin_b = sin_b.at[:, lane_off:lane_off + rot_dim].set(jnp.where(first, 0.0, s))
    return jnp.stack([cos, sin_a, sin_b])


def kernel(x, c, ctx, c_ctx, w_ada, b_ada, norm_pre, norm_post, ffn_w1, ffn_w3, ffn_w2, w_in,
           s5_lambda_re, s5_lambda_im, s5_log_dt, s5_b_re, s5_b_im, s5_c_re, s5_c_im, s5_d, s5_w_glu,
           mla_q_norm, mla_w_uq, mla_kv_norm, mla_w_ukv, hgrn_lb_raw, hgrn_o_norm,
           gqa_q_norm, gqa_k_norm, w_branch, w_out):
    nb, seq, d = x.shape
    n_ctx = ctx.shape[1]
    depth = w_ada.shape[0]
    n_mix = N_PROJ

    rows = max(8, -(-(nb + 1) // 8) * 8)
    cvec = jnp.zeros((rows, d), F32).at[:nb].set(c).at[nb].set(c_ctx)
    mod_all = _modulation(cvec, w_ada, b_ada).reshape(depth, rows, N_MOD, d)

    lb_step = jax.nn.softmax(hgrn_lb_raw.astype(F32), axis=1)
    lb_all = jnp.clip(jnp.cumsum(lb_step, axis=1) - lb_step[:, :1], 0.0, 1.0)

    rope_b_lat = _rope_tables(seq, B_ROPE, B_NOPE, False)
    rope_d_lat = _rope_tables(seq, D_HEAD, 0, False)
    rope_b_ctx = _rope_tables(n_ctx, B_ROPE, B_NOPE, True)
    rope_d_ctx = _rope_tables(n_ctx, D_HEAD, 0, True)
    hg_consts = _hgrn_consts()
    lane_head = jnp.arange(HG_W) // C_DV
    avg = (lane_head[:, None] == lane_head[None, :]).astype(BF16) * (1.0 / C_DV)
    n_mixcols = w_in.shape[-1] - N_BRANCH * d

    x_lat, x_ctx = x, ctx
    for layer in range(depth):
        last = layer == depth - 1
        mod_lat = mod_all[layer, :nb]
        mod_ctx = jnp.broadcast_to(mod_all[layer, nb:nb + 1], (nb, N_MOD, d))
        bf = lambda w: w.astype(BF16)
        ffn_a = (norm_pre[layer, 0], norm_post[layer, 0], bf(ffn_w1[layer, 0]), bf(ffn_w3[layer, 0]), bf(ffn_w2[layer, 0]))
        ffn_b = (norm_pre[layer, 2], norm_post[layer, 2], bf(ffn_w1[layer, 1]), bf(ffn_w3[layer, 1]), bf(ffn_w2[layer, 1]))

        x_lat = _ffn(x_lat, mod_lat, 0, *ffn_a)
        x_ctx = _ffn(x_ctx, mod_ctx, 0, *ffn_a)

        wp = _proj_weight(w_in[layer, :, :n_mixcols])
        wg = bf(w_in[layer, :, n_mixcols:])
        qn = _pad_cols(mla_q_norm[layer].reshape(1, B_Q_LORA), 256)
        wuq = mla_w_uq[layer].reshape(B_Q_LORA, B_HEADS, B_NOPE + B_ROPE)
        wuq = jnp.pad(wuq, ((0, 256 - B_Q_LORA), (0, 0), (0, LANES - B_NOPE - B_ROPE))).reshape(256, B_HEADS * LANES)
        wukv = mla_w_ukv[layer].reshape(B_KV_LORA, B_HEADS, B_NOPE + B_V)
        wuk = jnp.pad(wukv[:, :, :B_NOPE], ((0, 0), (0, 0), (0, LANES - B_NOPE))).reshape(B_KV_LORA, B_HEADS * LANES)
        wuv = jnp.pad(jnp.transpose(wukv[:, :, B_NOPE:], (1, 0, 2)), ((0, 0), (0, 0), (0, LANES - B_V)))
        kvn = mla_kv_norm[layer].reshape(1, B_KV_LORA)
        gqn = _pad_cols(gqa_q_norm[layer].reshape(1, D_HEAD), LANES)
        gkn = _pad_cols(gqa_k_norm[layer].reshape(1, D_HEAD), LANES)
        lb = lb_all[:, layer]
        proj_args = (norm_pre[layer, 1], wp)
        mla_args = (qn, bf(wuq), kvn, bf(wuk), bf(wuv), lb, gqn, gkn)

        pl_ = _inproj(x_lat, mod_lat, *proj_args, rope_b_lat, rope_d_lat, *mla_args)
        pc_ = _inproj(x_ctx, mod_ctx, *proj_args, rope_b_ctx, rope_d_ctx, *mla_args)
        (u_l, u2_l, bq_l, bk_l, bv_l, hq_l, hv_l, hk_l, hl_l, hg_l, dq_l, dk_l, dv_l) = pl_
        (u_c, u2_c, bq_c, bk_c, bv_c, hq_c, hv_c, hk_c, hl_c, hg_c, dq_c, dk_c, dv_c) = pc_

        ys_l, ys_c = [], []
        for dr, reverse in enumerate((False, True)):
            nlev = 9
            wts = _s5_weights(s5_lambda_re[layer, dr], s5_lambda_im[layer, dr], s5_log_dt[layer, dr],
                              s5_b_re[layer, dr], s5_b_im[layer, dr], s5_c_re[layer, dr], s5_c_im[layer, dr],
                              reverse, nlev)
            x0 = jnp.zeros((nb, 1, 2 * S5_NSTATE), F32)
            y_c, x_end = _s5_scan(u2_c, x0, wts, reverse)
            y_l, _ = _s5_scan(u2_l, x_end, wts, reverse)
            ys_l.append(y_l)
            ys_c.append(y_c)

        s0 = jnp.zeros((2, nb, HG_W, HG_W), F32)
        ho_c, s_ctx = _hgrn_scan(hq_c, hk_c, hv_c, hl_c, s0, hg_consts)
        ho_l, _ = _hgrn_scan(hq_l, hk_l, hv_l, hl_l, s_ctx, hg_consts)

        bo_l = _attention(bq_l, [(bk_c, bv_c), (bk_l, bv_l)])
        do_l = _attention(dq_l, [(dk_c, dv_c), (dk_l, dv_l)])

        merge_w = (norm_pre[layer, 1], norm_post[layer, 1], wg, bf(w_branch[layer]), bf(w_out[layer]))
        s5_ro = (s5_d[layer].reshape(1, A_WIDTH), bf(s5_w_glu[layer]))
        hn = jnp.tile(hgrn_o_norm[layer], C_HEADS).reshape(1, HG_W)
        x_lat_new = _merge(x_lat, mod_lat, *merge_w, u_l, ys_l[0], ys_l[1], *s5_ro, bo_l, ho_l, hg_l, hn, avg, do_l)
        if not last:
            bo_c = _attention(bq_c, [(bk_c, bv_c)])
            do_c = _attention(dq_c, [(dk_c, dv_c)])
            x_ctx = _merge(x_ctx, mod_ctx, *merge_w, u_c, ys_c[0], ys_c[1], *s5_ro, bo_c, ho_c, hg_c, hn, avg, do_c)
            x_ctx = _ffn(x_ctx, mod_ctx, 2, *ffn_b)
        x_lat = _ffn(x_lat_new, mod_lat, 2, *ffn_b)
    return x_lat
```

```python
import functools

import jax
import jax.numpy as jnp
import numpy as np
from jax import lax
from jax.experimental import pallas as pl
from jax.experimental.pallas import tpu as pltpu

GRID_W = 64
FFN_RES_WEIGHT = 0.5
N_MOD = 9
EPS = 1e-6
ROPE_THETA = 10000.0
F_FLOOR = 1e-20

A_WIDTH = 256
A_GROUP = 16
A_GROUPS = A_WIDTH // A_GROUP
A_STATE = 64

B_HEADS = 4
B_NOPE = 64
B_ROPE = 32
B_V = 64
B_Q_LORA = 192
B_KV_LORA = 128

C_HEADS = 4
C_DK = 64
C_DV = 64

D_HEADS = 4
D_KV_HEADS = 2
D_HEAD = 64

N_BRANCH = 4
BRANCH_W = 256

LANES = 128
VMEM_LIMIT_BYTES = 56 * 1024 * 1024

S5_R = 4
S5_LANES = S5_R * A_WIDTH
S5_NSTATE = A_GROUPS * A_STATE
S5_MAX_ROWS = 512
S5_MAX_LEVELS = 9
HG_T = 128
HG_LEVELS = 7
HG_W = C_HEADS * C_DK
ATTN_ONES = 16
ATTN_QUERIES = (1024, 512, 256, 128)
ATTN_UNROLL = 8

SLOT_S5 = 0
SLOT_CQ = 256
SLOT_CKV = 512
SLOT_KR = 640
SLOT_HQ = 768
SLOT_HV = 1024
SLOT_HF = 1280
SLOT_HB = 1536
SLOT_HG = 1792
SLOT_GQ = 2048
SLOT_GK = 2560
SLOT_GV = 2816
N_PROJ = 3072

BF16 = jnp.bfloat16
F32 = jnp.float32
LOG2E = 1.4426950408889634


def _params(sem):
    return pltpu.CompilerParams(dimension_semantics=sem, vmem_limit_bytes=VMEM_LIMIT_BYTES)


def _pick(n, candidates):
    for c in candidates:
        if n % c == 0:
            return c
    raise ValueError(f"no tile for {n} in {candidates}")


def _dot(a, b):
    return jnp.dot(a, b, preferred_element_type=F32)


def _dot_nt(a, b):
    return lax.dot_general(a, b, (((1,), (1,)), ((), ())), preferred_element_type=F32)


def _rms(x, g, n=None):
    n = x.shape[-1] if n is None else n
    ms = jnp.sum(x * x, axis=-1, keepdims=True) * (1.0 / n)
    return x * lax.rsqrt(ms + EPS) * g


def _mod_kernel(c_ref, w_ref, b_ref, o_ref):
    c = c_ref[...]
    a = (c * jax.nn.sigmoid(c)).astype(BF16)
    o_ref[0] = _dot(a, w_ref[0].astype(BF16)) + b_ref[0]


def _modulation(cvec, w_ada, b_ada):
    nl, d, nm = w_ada.shape
    rows = cvec.shape[0]
    tn = _pick(nm, (1152, 1024, 512, 256, 128))
    return pl.pallas_call(
        _mod_kernel,
        grid=(nl, nm // tn),
        in_specs=[
            pl.BlockSpec((rows, d), lambda l, n: (0, 0)),
            pl.BlockSpec((1, d, tn), lambda l, n: (l, 0, n)),
            pl.BlockSpec((1, 1, tn), lambda l, n: (l, 0, n)),
        ],
        out_specs=pl.BlockSpec((1, rows, tn), lambda l, n: (l, 0, n)),
        out_shape=jax.ShapeDtypeStruct((nl, rows, nm), F32),
        compiler_params=_params(("parallel", "parallel")),
        name="adaln_mod",
    )(cvec, w_ada, b_ada.reshape(nl, 1, nm))


def _ffn_kernel(x_ref, mod_ref, gpre_ref, gpost_ref, w1_ref, w3_ref, w2_ref, o_ref, h_scr, acc_scr, *, j, nf):
    f = pl.program_id(2)

    @pl.when(f == 0)
    def _():
        x = x_ref[0]
        shift = mod_ref[0, pl.ds(3 * j, 1), :]
        scale = mod_ref[0, pl.ds(3 * j + 1, 1), :]
        h = _rms(x, gpre_ref[...]) * (1.0 + scale) + shift
        h_scr[...] = h.astype(BF16)
        acc_scr[...] = jnp.zeros_like(acc_scr)

    h = h_scr[...]
    a = _dot(h, w1_ref[...])
    b = _dot(h, w3_ref[...])
    t = (a * jax.nn.sigmoid(a) * b).astype(BF16)
    acc_scr[...] += _dot(t, w2_ref[...])

    @pl.when(f == nf - 1)
    def _():
        gate = mod_ref[0, pl.ds(3 * j + 2, 1), :]
        y = _rms(acc_scr[...], gpost_ref[...])
        o_ref[0] = x_ref[0] + FFN_RES_WEIGHT * gate * y


def _ffn_resident_kernel(x_ref, mod_ref, gpre_ref, gpost_ref, w1_ref, w3_ref, w2_ref, o_ref, *, j, tf):
    x = x_ref[0]
    shift = mod_ref[0, pl.ds(3 * j, 1), :]
    scale = mod_ref[0, pl.ds(3 * j + 1, 1), :]
    gate = mod_ref[0, pl.ds(3 * j + 2, 1), :]
    h = (_rms(x, gpre_ref[...]) * (1.0 + scale) + shift).astype(BF16)
    acc = None
    for f in range(w1_ref.shape[1] // tf):
        a = _dot(h, w1_ref[:, f * tf:(f + 1) * tf])
        b = _dot(h, w3_ref[:, f * tf:(f + 1) * tf])
        t = (a * jax.nn.sigmoid(a) * b).astype(BF16)
        part = _dot(t, w2_ref[f * tf:(f + 1) * tf, :])
        acc = part if acc is None else acc + part
    o_ref[0] = x + FFN_RES_WEIGHT * gate * _rms(acc, gpost_ref[...])


def _ffn_resident(x, mod, j, g_pre, g_post, w1, w3, w2):
    nb, s, d = x.shape
    dff = w1.shape[1]
    tm = _pick(s, (512, 256))
    tf = _pick(dff, (256, 128))
    once = pl.Buffered(1)
    return pl.pallas_call(
        functools.partial(_ffn_resident_kernel, j=j, tf=tf),
        grid=(nb, s // tm),
        in_specs=[
            pl.BlockSpec((1, tm, d), lambda b, i: (b, i, 0)),
            pl.BlockSpec((1, N_MOD, d), lambda b, i: (b, 0, 0)),
            pl.BlockSpec((1, d), lambda b, i: (0, 0)),
            pl.BlockSpec((1, d), lambda b, i: (0, 0)),
            pl.BlockSpec((d, dff), lambda b, i: (0, 0), pipeline_mode=once),
            pl.BlockSpec((d, dff), lambda b, i: (0, 0), pipeline_mode=once),
            pl.BlockSpec((dff, d), lambda b, i: (0, 0), pipeline_mode=once),
        ],
        out_specs=pl.BlockSpec((1, tm, d), lambda b, i: (b, i, 0)),
        out_shape=jax.ShapeDtypeStruct(x.shape, F32),
        compiler_params=_params(("parallel", "parallel")),
        name="ffn_resident",
    )(x, mod, g_pre.reshape(1, d), g_post.reshape(1, d), w1, w3, w2)


def _ffn(x, mod, j, g_pre, g_post, w1, w3, w2):
    nb, s, d = x.shape
    dff = w1.shape[1]
    tm = _pick(s, (1024, 512, 256))
    tf = _pick(dff, (256, 128))
    nf = dff // tf
    return pl.pallas_call(
        functools.partial(_ffn_kernel, j=j, nf=nf),
        grid=(nb, s // tm, nf),
        in_specs=[
            pl.BlockSpec((1, tm, d), lambda b, i, f: (b, i, 0)),
            pl.BlockSpec((1, N_MOD, d), lambda b, i, f: (b, 0, 0)),
            pl.BlockSpec((1, d), lambda b, i, f: (0, 0)),
            pl.BlockSpec((1, d), lambda b, i, f: (0, 0)),
            pl.BlockSpec((d, tf), lambda b, i, f: (0, f)),
            pl.BlockSpec((d, tf), lambda b, i, f: (0, f)),
            pl.BlockSpec((tf, d), lambda b, i, f: (f, 0)),
        ],
        out_specs=pl.BlockSpec((1, tm, d), lambda b, i, f: (b, i, 0)),
        out_shape=jax.ShapeDtypeStruct(x.shape, F32),
        scratch_shapes=[pltpu.VMEM((tm, d), BF16), pltpu.VMEM((tm, d), F32)],
        compiler_params=_params(("parallel", "parallel", "arbitrary")),
        name="ffn_sublayer",
    )(x, mod, g_pre.reshape(1, d), g_post.reshape(1, d), w1, w3, w2)


def _rope(x, cos, sin_a, sin_b, quarter):
    w = x.shape[-1]
    return x * cos + pltpu.roll(x, w - quarter, 1) * sin_a + pltpu.roll(x, quarter, 1) * sin_b


def _inproj_kernel(x_ref, mod_ref, gpre_ref, w_ref, rb_ref, rd_ref, qn_ref, wuq_ref, kvn_ref, wuk_ref, wuv_ref,
                   lb_ref, gqn_ref, gkn_ref,
                   u_ref, u2_ref, bq_ref, bk_ref, bv_ref, hq_ref, hv_ref, hk_ref, hl_ref, hg_ref,
                   dq_ref, dk_ref, dv_ref, h_scr, u_scr, *, tm):
    x = x_ref[0]
    shift = mod_ref[0, pl.ds(3, 1), :]
    scale = mod_ref[0, pl.ds(4, 1), :]
    h_scr[...] = (_rms(x, gpre_ref[...]) * (1.0 + scale) + shift).astype(BF16)

    def proj(lo, width):
        return _dot(h_scr[...], w_ref[:, lo:lo + width])

    cos_d, sa_d, sb_d = rd_ref[0], rd_ref[1], rd_ref[2]
    d_scale = D_HEAD ** -0.5 * LOG2E
    gq = proj(SLOT_GQ, D_HEADS * LANES)
    for hd in range(D_HEADS):
        qh = _rms(gq[:, hd * LANES:(hd + 1) * LANES], gqn_ref[...], n=D_HEAD)
        qh = _rope(qh, cos_d, sa_d, sb_d, D_HEAD // 4) * d_scale
        dq_ref[0, hd] = qh.T[:D_HEAD].astype(BF16)
    gkv = proj(SLOT_GK, 2 * D_KV_HEADS * LANES)
    for hd in range(D_KV_HEADS):
        kh = _rms(gkv[:, hd * LANES:(hd + 1) * LANES], gkn_ref[...], n=D_HEAD)
        kh = _rope(kh, cos_d, sa_d, sb_d, D_HEAD // 4)
        dk_ref[0, hd] = kh[:, :D_HEAD].astype(BF16)
        dv_ref[0, hd] = gkv[:, (D_KV_HEADS + hd) * LANES:(D_KV_HEADS + hd + 1) * LANES].T[:D_HEAD].astype(BF16)

    cos_b, sa_b, sb_b = rb_ref[0], rb_ref[1], rb_ref[2]
    cq = _rms(proj(SLOT_CQ, 256), qn_ref[...], n=B_Q_LORA).astype(BF16)
    q = _dot(cq, wuq_ref[...])
    b_scale = (B_NOPE + B_ROPE) ** -0.5 * LOG2E
    for hd in range(B_HEADS):
        qh = _rope(q[:, hd * LANES:(hd + 1) * LANES], cos_b, sa_b, sb_b, B_ROPE // 4)
        bq_ref[0, hd] = (qh * b_scale).T.astype(BF16)
    ckr = proj(SLOT_CKV, B_KV_LORA + LANES)
    ckv = _rms(ckr[:, :B_KV_LORA], kvn_ref[...]).astype(BF16)
    kn = _dot(ckv, wuk_ref[...])
    vn = _dot(ckv, wuv_ref[...])
    kr = _rope(ckr[:, B_KV_LORA:], cos_b, sa_b, sb_b, B_ROPE // 4)
    for hd in range(B_HEADS):
        bk_ref[0, hd] = (kn[:, hd * LANES:(hd + 1) * LANES] + kr).astype(BF16)
        bv_ref[0, hd] = vn[:, hd * LANES:(hd + 1) * LANES].T[:B_V].astype(BF16)

    hq_ref[0] = proj(SLOT_HQ, HG_W).astype(BF16)
    hv_ref[0] = proj(SLOT_HV, HG_W).astype(BF16)
    hg_ref[0] = proj(SLOT_HG, HG_W).astype(BF16)
    for dr, slot in enumerate((SLOT_HF, SLOT_HB)):
        z = proj(slot, HG_W)
        lb = lb_ref[pl.ds(dr, 1), :]
        f = lb + (1.0 - lb) * jax.nn.sigmoid(z)
        hl_ref[dr, 0] = jnp.log(jnp.maximum(f, F_FLOOR))
        hk_ref[dr, 0] = ((1.0 - lb) * jax.nn.sigmoid(-z)).astype(BF16)

    u = proj(SLOT_S5, A_WIDTH)
    u_ref[0] = u.astype(BF16)
    for hf in range(A_WIDTH // LANES):
        u_scr[hf] = u[:, hf * LANES:(hf + 1) * LANES]
    for r in range(S5_R):
        for hf in range(A_WIDTH // LANES):
            lo = r * A_WIDTH + hf * LANES
            u2_ref[0, :, lo:lo + LANES] = u_scr[hf, pl.ds(r, tm // S5_R, stride=S5_R), :].astype(BF16)


def _inproj(x, mod, g_pre, wp, rope_b, rope_d, qn, wuq, kvn, wuk, wuv, lb, gqn, gkn):
    nb, s, d = x.shape
    tm = _pick(s, (512, 256))
    const2 = lambda b, i: (0, 0)
    const3 = lambda b, i: (0, 0, 0)
    tok = lambda w: pl.BlockSpec((1, tm, w), lambda b, i: (b, i, 0))
    headed = lambda nh, w: pl.BlockSpec((1, nh, tm, w), lambda b, i: (b, 0, i, 0))
    headed_t = lambda nh, w: pl.BlockSpec((1, nh, w, tm), lambda b, i: (b, 0, 0, i))
    dirtok = lambda w: pl.BlockSpec((2, 1, tm, w), lambda b, i: (0, b, i, 0))
    sd = jax.ShapeDtypeStruct
    outs = [
        (sd((nb, s, A_WIDTH), BF16), tok(A_WIDTH)),
        (sd((nb, s // S5_R, S5_LANES), BF16), pl.BlockSpec((1, tm // S5_R, S5_LANES), lambda b, i: (b, i, 0))),
        (sd((nb, B_HEADS, LANES, s), BF16), headed_t(B_HEADS, LANES)),
        (sd((nb, B_HEADS, s, LANES), BF16), headed(B_HEADS, LANES)),
        (sd((nb, B_HEADS, B_V, s), BF16), headed_t(B_HEADS, B_V)),
        (sd((nb, s, HG_W), BF16), tok(HG_W)),
        (sd((nb, s, HG_W), BF16), tok(HG_W)),
        (sd((2, nb, s, HG_W), BF16), dirtok(HG_W)),
        (sd((2, nb, s, HG_W), F32), dirtok(HG_W)),
        (sd((nb, s, HG_W), BF16), tok(HG_W)),
        (sd((nb, D_HEADS, D_HEAD, s), BF16), headed_t(D_HEADS, D_HEAD)),
        (sd((nb, D_KV_HEADS, s, D_HEAD), BF16), headed(D_KV_HEADS, D_HEAD)),
        (sd((nb, D_KV_HEADS, D_HEAD, s), BF16), headed_t(D_KV_HEADS, D_HEAD)),
    ]
    return pl.pallas_call(
        functools.partial(_inproj_kernel, tm=tm),
        grid=(nb, s // tm),
        in_specs=[
            pl.BlockSpec((1, tm, d), lambda b, i: (b, i, 0)),
            pl.BlockSpec((1, N_MOD, d), lambda b, i: (b, 0, 0)),
            pl.BlockSpec((1, d), const2),
            pl.BlockSpec((d, N_PROJ), const2),
            pl.BlockSpec((3, tm, LANES), lambda b, i: (0, i, 0)),
            pl.BlockSpec((3, tm, LANES), lambda b, i: (0, i, 0)),
            pl.BlockSpec((1, 256), const2),
            pl.BlockSpec((256, B_HEADS * LANES), const2),
            pl.BlockSpec((1, B_KV_LORA), const2),
            pl.BlockSpec((B_KV_LORA, B_HEADS * LANES), const2),
            pl.BlockSpec((B_KV_LORA, B_HEADS * LANES), const2),
            pl.BlockSpec((2, HG_W), const2),
            pl.BlockSpec((1, LANES), const2),
            pl.BlockSpec((1, LANES), const2),
        ],
        out_specs=[o[1] for o in outs],
        out_shape=[o[0] for o in outs],
        scratch_shapes=[pltpu.VMEM((tm, d), BF16), pltpu.VMEM((A_WIDTH // LANES, tm, LANES), F32)],
        compiler_params=_params(("parallel", "parallel")),
        name="mixer_inproj",
    )(x, mod, g_pre.reshape(1, d), wp, rope_b, rope_d, qn, wuq, kvn, wuk, wuv, lb, gqn, gkn)


def _s5_kernel(u2_ref, x0_ref, tb_ref, wst_ref, wout_ref, pq_ref, y_ref, xf_ref, carry, *, rows, reverse, nlev):
    i = pl.program_id(1)

    @pl.when(i == 0)
    def _():
        carry[...] = x0_ref[0]

    u2 = u2_ref[0]
    sloc = _dot(u2, wst_ref[...])
    ridx = lax.broadcasted_iota(jnp.int32, (rows, 1), 0)

    def cmul(xv, lev):
        p = pq_ref[pl.ds(2 * lev, 1), :]
        q = pq_ref[pl.ds(2 * lev + 1, 1), :]
        return xv * p + pltpu.roll(xv, S5_NSTATE, 1) * q

    if reverse:
        e = jnp.where(ridx == rows - 1, carry[...], pltpu.roll(sloc, rows - 1, 0))
    else:
        e = jnp.where(ridx == 0, carry[...], pltpu.roll(sloc, 1, 0))
    xin = e
    for lev in range(nlev):
        dist = 1 << lev
        if reverse:
            sh = jnp.where(ridx < rows - dist, pltpu.roll(xin, rows - dist, 0), 0.0)
        else:
            sh = jnp.where(ridx >= dist, pltpu.roll(xin, dist, 0), 0.0)
        xin = xin + cmul(sh, lev)

    last = 0 if reverse else rows - 1
    nxt = cmul(xin[last:last + 1, :], 0) + sloc[last:last + 1, :]
    carry[...] = nxt
    xf_ref[0] = nxt

    y2 = _dot(u2, tb_ref[...]) + _dot(xin.astype(BF16), wout_ref[...])
    for r in range(S5_R):
        for hf in range(A_WIDTH // LANES):
            lo = r * A_WIDTH + hf * LANES
            y_ref[0, hf, pl.ds(r, rows, stride=S5_R), :] = y2[:, lo:lo + LANES]


def _s5_scan(u2, x0, wts, reverse):
    tb, wst, wout, pq = wts
    nb, n2, _ = u2.shape
    rows = _pick(n2, (S5_MAX_ROWS, 256, 128, 64))
    nt = n2 // rows
    nlev = max(1, (rows - 1).bit_length())
    order = (lambda b, i: (b, nt - 1 - i, 0)) if reverse else (lambda b, i: (b, i, 0))
    order_out = (lambda b, i: (b, 0, nt - 1 - i, 0)) if reverse else (lambda b, i: (b, 0, i, 0))
    const2 = lambda b, i: (0, 0)
    y, xf = pl.pallas_call(
        functools.partial(_s5_kernel, rows=rows, reverse=reverse, nlev=nlev),
        grid=(nb, nt),
        in_specs=[
            pl.BlockSpec((1, rows, S5_LANES), order),
            pl.BlockSpec((1, 1, 2 * S5_NSTATE), lambda b, i: (b, 0, 0)),
            pl.BlockSpec((S5_LANES, S5_LANES), const2),
            pl.BlockSpec((S5_LANES, 2 * S5_NSTATE), const2),
            pl.BlockSpec((2 * S5_NSTATE, S5_LANES), const2),
            pl.BlockSpec(pq.shape, const2),
        ],
        out_specs=[
            pl.BlockSpec((1, A_WIDTH // LANES, rows * S5_R, LANES), order_out),
            pl.BlockSpec((1, 1, 2 * S5_NSTATE), lambda b, i: (b, 0, 0)),
        ],
        out_shape=[
            jax.ShapeDtypeStruct((nb, A_WIDTH // LANES, n2 * S5_R, LANES), F32),
            jax.ShapeDtypeStruct((nb, 1, 2 * S5_NSTATE), F32),
        ],
        scratch_shapes=[pltpu.VMEM((1, 2 * S5_NSTATE), F32)],
        compiler_params=_params(("parallel", "arbitrary")),
        name="s5_scan_rev" if reverse else "s5_scan_fwd",
    )(u2, x0, tb, wst, wout, pq)
    return y, xf


def _s5_weights(lam_re, lam_im, log_dt, b_re, b_im, c_re, c_im, nlev):
    g, n, r = A_GROUPS, A_STATE, S5_R
    nl = lam_re.shape[0]
    lam_re = jnp.minimum(lam_re.astype(F32), -1e-4)
    lam_im = lam_im.astype(F32)
    dt = jnp.exp(log_dt.astype(F32))[..., None]
    mag = jnp.exp(lam_re * dt)
    a_re = mag * jnp.cos(lam_im * dt)
    a_im = mag * jnp.sin(lam_im * dt)
    den = lam_re * lam_re + lam_im * lam_im
    num_re = a_re - 1.0
    f_re = (num_re * lam_re + a_im * lam_im) / den
    f_im = (a_im * lam_re - num_re * lam_im) / den
    bb_re = f_re[..., None] * b_re - f_im[..., None] * b_im
    bb_im = f_re[..., None] * b_im + f_im[..., None] * b_re

    def cm(xr, xi, yr, yi):
        return xr * yr - xi * yi, xr * yi + xi * yr

    pr, pi = [jnp.ones_like(a_re)], [jnp.zeros_like(a_im)]
    for _ in range(r):
        nr, ni = cm(pr[-1], pi[-1], a_re, a_im)
        pr.append(nr)
        pi.append(ni)
    pr, pi = jnp.stack(pr), jnp.stack(pi)

    xr, xi = cm(pr[..., None], pi[..., None], bb_re, bb_im)
    yr, yi = cm(c_re, c_im, pr[:, :, :, :, None, :], pi[:, :, :, :, None, :])
    lag = jnp.einsum('ldgon,tldgnc->tldgoc', c_re, xr) - jnp.einsum('ldgon,tldgnc->tldgoc', c_im, xi)

    tau = np.arange(r + 1)
    ain, bout = np.arange(r)[:, None], np.arange(r)[None, :]
    sel_tb = np.stack([(bout - ain)[..., None] == tau, (ain - bout)[..., None] == tau]).astype(np.float32)
    sel_st = np.stack([(r - 1 - np.arange(r))[:, None] == tau, np.arange(r)[:, None] == tau]).astype(np.float32)
    sel_out = np.stack([(np.arange(r) + 1)[:, None] == tau, (r - np.arange(r))[:, None] == tau]).astype(np.float32)
    eye_g = jnp.eye(g, dtype=F32)

    tb = jnp.einsum('dabt,tldgoc,gh->ldagcbho', sel_tb, lag, eye_g).reshape(nl, 2, r * A_WIDTH, r * A_WIDTH)
    wst = jnp.concatenate([
        jnp.einsum('dat,tldgnc,gh->ldagchn', sel_st, xr, eye_g).reshape(nl, 2, r * A_WIDTH, g * n),
        jnp.einsum('dat,tldgnc,gh->ldagchn', sel_st, xi, eye_g).reshape(nl, 2, r * A_WIDTH, g * n)], axis=-1)
    wout = jnp.concatenate([
        jnp.einsum('dbt,tldgon,gh->ldgnbho', sel_out, yr, eye_g).reshape(nl, 2, g * n, r * A_WIDTH),
        jnp.einsum('dbt,tldgon,gh->ldgnbho', sel_out, -yi, eye_g).reshape(nl, 2, g * n, r * A_WIDTH)], axis=-2)

    lr, li = pr[r].reshape(nl, 2, 1, g * n), pi[r].reshape(nl, 2, 1, g * n)
    rows = []
    for _ in range(nlev):
        rows.append(jnp.concatenate([lr, lr], axis=-1))
        rows.append(jnp.concatenate([-li, li], axis=-1))
        lr, li = cm(lr, li, lr, li)
    pq = jnp.concatenate(rows, axis=-2)
    return tb.astype(BF16), wst.astype(BF16), wout.astype(BF16), pq


def _split3(x):
    x1 = x.astype(BF16)
    r1 = x - x1.astype(F32)
    x2 = r1.astype(BF16)
    x3 = (r1 - x2.astype(F32)).astype(BF16)
    return x1, x2, x3


def _hgrn_chunk(q, k, vb, lf, st, tri, sel, sgn_ref, msk_ref, dr, heads, bd, nlev):
    l1, l2, l3 = _split3(lf)
    cum = _dot(tri, l1) + _dot(tri, l2) + _dot(tri, l3)
    tot = jnp.sum(lf, axis=0, keepdims=True)

    mids = _dot(sel, cum.astype(BF16))
    t = q.shape[0]

    def head_scores(qx, kx):
        qh = jnp.concatenate([jnp.where(heads[hd], qx, jnp.zeros_like(qx)) for hd in range(C_HEADS)], axis=0)
        s = _dot_nt(qh, kx)
        return [s[hd * t:(hd + 1) * t, :] for hd in range(C_HEADS)]

    diag = msk_ref[dr, nlev] > 0.5
    att = [jnp.where(diag, s, 0.0) for s in head_scores(q.astype(BF16), k.astype(BF16))]
    for lev in range(nlev):
        e = jnp.exp(sgn_ref[dr, lev] * (cum - mids[lev * t:(lev + 1) * t, :]))
        m = msk_ref[dr, lev] > 0.5
        for hd, s in enumerate(head_scores((q * e).astype(BF16), (k * e).astype(BF16))):
            att[hd] = att[hd] + jnp.where(m, s, 0.0)

    o = _dot_nt((q * jnp.exp(cum)).astype(BF16), st.astype(BF16))
    for hd in range(C_HEADS):
        o = o + _dot(att[hd].astype(BF16), jnp.where(heads[hd], vb, jnp.zeros_like(vb)))

    kend = (k * jnp.exp(tot - cum)).astype(BF16)
    vt = vb.astype(F32).T.astype(BF16)
    new = st * jnp.exp(tot) + jnp.where(bd, _dot(vt, kend), 0.0)
    return o, new


def _hgrn_kernel(qf_ref, qr_ref, kf_ref, kr_ref, vf_ref, vr_ref, lff_ref, lfr_ref, s0_ref,
                 tri_ref, sel_ref, sgn_ref, msk_ref, hm_ref, bd_ref,
                 of_ref, or_ref, sf_ref, st_scr, *, nlev, nb):
    c = pl.program_id(0)

    @pl.when(c == 0)
    def _():
        st_scr[...] = s0_ref[...]

    heads = [hm_ref[pl.ds(hd, 1), :] > 0.5 for hd in range(C_HEADS)]
    bd = bd_ref[...] > 0.5
    dirs = ((qf_ref, kf_ref, vf_ref, lff_ref, of_ref), (qr_ref, kr_ref, vr_ref, lfr_ref, or_ref))
    for dr, (q_ref, k_ref, v_ref, lf_ref, o_ref) in enumerate(dirs):
        for b in range(nb):
            o, new = _hgrn_chunk(q_ref[b].astype(F32), k_ref[0, b].astype(F32), v_ref[b], lf_ref[0, b],
                                 st_scr[dr, b], tri_ref[dr], sel_ref[dr], sgn_ref, msk_ref, dr, heads, bd, nlev)
            o_ref[b] = o
            st_scr[dr, b] = new
            sf_ref[dr, b] = new


def _hgrn_consts():
    t = HG_T
    ti = jnp.arange(t)[:, None]
    si = jnp.arange(t)[None, :]
    tri, sel, sgn, msk = [], [], [], []
    for reverse in (False, True):
        tri.append((si >= ti) if reverse else (si <= ti))
        sels, sgns, msks = [], [], []
        for lev in range(HG_LEVELS):
            h = 1 << lev
            blk_t, blk_s = ti // (2 * h), si // (2 * h)
            hi_t, hi_s = (ti % (2 * h)) >= h, (si % (2 * h)) >= h
            if reverse:
                mid = blk_t * 2 * h + h
                q_role_t, k_role_s = ~hi_t, hi_s
            else:
                mid = blk_t * 2 * h + h - 1
                q_role_t, k_role_s = hi_t, ~hi_s
            sels.append(si == mid)
            sgns.append(jnp.where(q_role_t, 1.0, -1.0))
            msks.append((blk_t == blk_s) & q_role_t & k_role_s)
        msks.append(ti == si)
        sel.append(jnp.concatenate(sels, axis=0))
        sgn.append(jnp.stack(sgns))
        msk.append(jnp.stack(msks))
    lane_head = jnp.arange(HG_W) // C_DK
    hm = (lane_head[None, :] == jnp.arange(C_HEADS)[:, None]).astype(F32)
    bd = (lane_head[:, None] == lane_head[None, :]).astype(F32)
    return (jnp.stack(tri).astype(BF16), jnp.stack(sel).astype(BF16), jnp.stack(sgn).astype(F32),
            jnp.stack(msk).astype(F32), hm, bd)


def _hgrn_scan(q, k, v, lf, s0, consts):
    tri, sel, sgn, msk, hm, bd = consts
    nb, s, w = q.shape
    t = HG_T
    nc = s // t
    fwd3 = pl.BlockSpec((nb, t, w), lambda c: (0, c, 0))
    rev3 = pl.BlockSpec((nb, t, w), lambda c: (0, nc - 1 - c, 0))
    fwd4 = pl.BlockSpec((1, nb, t, w), lambda c: (0, 0, c, 0))
    rev4 = pl.BlockSpec((1, nb, t, w), lambda c: (1, 0, nc - 1 - c, 0))
    whole = lambda a: pl.BlockSpec(a.shape, lambda c: (0,) * a.ndim)
    sgn = sgn.reshape(2, HG_LEVELS, t, 1)
    o_f, o_r, sf = pl.pallas_call(
        functools.partial(_hgrn_kernel, nlev=HG_LEVELS, nb=nb),
        grid=(nc,),
        in_specs=[fwd3, rev3, fwd4, rev4, fwd3, rev3, fwd4, rev4, whole(s0),
                  whole(tri), whole(sel), whole(sgn), whole(msk), whole(hm), whole(bd)],
        out_specs=[fwd3, rev3, whole(s0)],
        out_shape=[
            jax.ShapeDtypeStruct((nb, s, w), F32),
            jax.ShapeDtypeStruct((nb, s, w), F32),
            jax.ShapeDtypeStruct((2, nb, w, w), F32),
        ],
        scratch_shapes=[pltpu.VMEM((2, nb, w, w), F32)],
        compiler_params=_params(("arbitrary",)),
        name="hgrn_scan",
    )(q, q, k, k, v, v, lf, lf, s0, tri, sel, sgn, msk, hm, bd)
    return (o_f, o_r), sf


def _attn_kernel(*refs, nseg, tks, g, tq):
    q_ref = refs[0]
    kv_refs = refs[1:1 + 2 * nseg]
    o_ref = refs[1 + 2 * nseg]
    dv = o_ref.shape[-1]
    n = g * tq
    qt = jnp.concatenate([q_ref[0, hd] for hd in range(g)], axis=-1)

    def scores(k):
        return _dot(k, qt)

    def absorb(s, vt, carry):
        m, acc = carry
        m_new = jnp.maximum(m, jnp.max(s, axis=0, keepdims=True))
        alpha = jnp.exp2(m - m_new)
        p = jnp.exp2(s - m_new).astype(BF16)
        vt1 = jnp.concatenate([vt, jnp.ones((ATTN_ONES, vt.shape[1]), BF16)], axis=0)
        acc = alpha * acc + _dot(vt1, p)
        return m_new, acc

    carry = (jnp.full((1, n), -jnp.inf, F32), jnp.zeros((dv + ATTN_ONES, n), F32))
    pending = None
    for seg in range(nseg):
        k_ref, vt_ref = kv_refs[2 * seg], kv_refs[2 * seg + 1]
        tk = tks[seg]
        nk = k_ref.shape[2] // tk
        s_first = scores(k_ref[0, 0, 0:tk, :])
        if pending is not None:
            carry = absorb(*pending, carry)
        if nk == 1:
            pending = (s_first, vt_ref[0, 0])
            continue

        def body(j, c, k_ref=k_ref, vt_ref=vt_ref, tk=tk, nk=nk):
            s_cur, m, acc = c
            off_next = pl.multiple_of(jnp.minimum(j + 1, nk - 1) * tk, tk)
            s_next = scores(k_ref[0, 0, pl.ds(off_next, tk), :])
            off = pl.multiple_of(j * tk, tk)
            m, acc = absorb(s_cur, vt_ref[0, 0, :, pl.ds(off, tk)], (m, acc))
            return s_next, m, acc

        unroll = ATTN_UNROLL if nk % ATTN_UNROLL == 0 else 1
        _, m, acc = lax.fori_loop(0, nk, body, (s_first,) + carry, unroll=unroll)
        carry = (m, acc)
        pending = None
    m, acc = carry if pending is None else absorb(*pending, carry)
    out = acc[:dv] / acc[dv:dv + 1, :]
    out = jnp.concatenate([out, jnp.zeros((LANES - dv, n), F32)], axis=0).T
    o_ref[0] = out[:, :dv].reshape(g, tq, dv).astype(o_ref.dtype)


def _attention(q, kvs):
    nb, hq, dk, sq = q.shape
    hkv = kvs[0][0].shape[1]
    dv = kvs[0][1].shape[2]
    g = hq // hkv
    tq = _pick(sq, tuple(n // g for n in ATTN_QUERIES))
    tks = tuple(_pick(k.shape[2], (512, 256, 128)) for k, _ in kvs)
    in_specs = [pl.BlockSpec((1, g, dk, tq), lambda b, h, i: (b, h, 0, i))]
    args = [q]
    for k, v in kvs:
        sk = k.shape[2]
        in_specs.append(pl.BlockSpec((1, 1, sk, dk), lambda b, h, i: (b, h, 0, 0)))
        in_specs.append(pl.BlockSpec((1, 1, dv, sk), lambda b, h, i: (b, h, 0, 0)))
        args += [k, v]
    return pl.pallas_call(
        functools.partial(_attn_kernel, nseg=len(kvs), tks=tks, g=g, tq=tq),
        grid=(nb, hkv, sq // tq),
        in_specs=in_specs,
        out_specs=pl.BlockSpec((1, g, tq, dv), lambda b, h, i: (b, h, i, 0)),
        out_shape=jax.ShapeDtypeStruct((nb, hq, sq, dv), BF16),
        compiler_params=_params(("parallel", "parallel", "arbitrary")),
        name="attention",
    )(*args)


def _merge_kernel(x_ref, mod_ref, gpre_ref, gpost_ref, wg_ref, wb_ref, wo_ref,
                  u_ref, yf_ref, yr_ref, sd_ref, wglu_ref, bo_ref, hof_ref, hor_ref, hg_ref, hn_ref, avg_ref, do_ref,
                  o_ref):
    x = x_ref[0]
    shift = mod_ref[0, pl.ds(3, 1), :]
    scale = mod_ref[0, pl.ds(4, 1), :]
    gate = mod_ref[0, pl.ds(5, 1), :]
    h = (_rms(x, gpre_ref[...]) * (1.0 + scale) + shift).astype(BF16)
    d = x.shape[-1]

    def branch_gate(i):
        return jax.nn.sigmoid(_dot(h, wg_ref[:, i * d:(i + 1) * d]))

    ysum = yf_ref[0] + yr_ref[0]
    y = sd_ref[...] * u_ref[0].astype(F32) + jnp.concatenate([ysum[hf] for hf in range(A_WIDTH // LANES)], axis=-1)
    ge = jax.nn.gelu(y)
    ya = ge * jax.nn.sigmoid(_dot(ge.astype(BF16), wglu_ref[...]))
    merged = branch_gate(0) * _dot(ya.astype(BF16), wb_ref[0])

    yb = jnp.concatenate([bo_ref[0, hd] for hd in range(B_HEADS)], axis=-1)
    merged = merged + branch_gate(1) * _dot(yb, wb_ref[1])

    o2 = hof_ref[0] + hor_ref[0]
    ms = _dot((o2 * o2).astype(BF16), avg_ref[...])
    gz = hg_ref[0].astype(F32)
    yc = o2 * lax.rsqrt(ms + EPS) * hn_ref[...] * (gz * jax.nn.sigmoid(gz))
    merged = merged + branch_gate(2) * _dot(yc.astype(BF16), wb_ref[2])

    yd = jnp.concatenate([do_ref[0, hd] for hd in range(D_HEADS)], axis=-1)
    merged = merged + branch_gate(3) * _dot(yd, wb_ref[3])

    yo = _dot(merged.astype(BF16), wo_ref[...])
    o_ref[0] = x + gate * _rms(yo, gpost_ref[...])


def _merge(x, mod, g_pre, g_post, wg, wb, wo, u, yf, yr, s5d, wglu, bo, ho, hg, hn, avg, do):
    nb, s, d = x.shape
    tm = _pick(s, (512, 256))
    const2 = lambda b, i: (0, 0)
    const3 = lambda b, i: (0, 0, 0)
    tok = lambda w: pl.BlockSpec((1, tm, w), lambda b, i: (b, i, 0))
    halves = pl.BlockSpec((1, A_WIDTH // LANES, tm, LANES), lambda b, i: (b, 0, i, 0))
    return pl.pallas_call(
        _merge_kernel,
        grid=(nb, s // tm),
        in_specs=[
            tok(d),
            pl.BlockSpec((1, N_MOD, d), lambda b, i: (b, 0, 0)),
            pl.BlockSpec((1, d), const2),
            pl.BlockSpec((1, d), const2),
            pl.BlockSpec((d, N_BRANCH * d), const2),
            pl.BlockSpec((N_BRANCH, BRANCH_W, d), const3),
            pl.BlockSpec((d, d), const2),
            tok(A_WIDTH), halves, halves,
            pl.BlockSpec((1, A_WIDTH), const2),
            pl.BlockSpec((A_WIDTH, A_WIDTH), const2),
            pl.BlockSpec((1, B_HEADS, tm, B_V), lambda b, i: (b, 0, i, 0)),
            tok(HG_W), tok(HG_W),
            tok(HG_W),
            pl.BlockSpec((1, HG_W), const2),
            pl.BlockSpec((HG_W, HG_W), const2),
            pl.BlockSpec((1, D_HEADS, tm, D_HEAD), lambda b, i: (b, 0, i, 0)),
        ],
        out_specs=tok(d),
        out_shape=jax.ShapeDtypeStruct(x.shape, F32),
        compiler_params=_params(("parallel", "parallel")),
        name="merge_out",
    )(x, mod, g_pre.reshape(1, d), g_post.reshape(1, d), wg, wb, wo, u, yf, yr, s5d, wglu, bo, ho[0], ho[1], hg, hn,
      avg, do)


def _pad_cols(w, width):
    return jnp.pad(w, ((0, 0), (0, width - w.shape[1])))


def _proj_weight(w_in_mix):
    offs = [0]
    for wdt in (A_WIDTH, B_Q_LORA, B_KV_LORA, B_ROPE, HG_W, HG_W, HG_W, HG_W, HG_W,
                D_HEADS * D_HEAD, D_KV_HEADS * D_HEAD, D_KV_HEADS * D_HEAD):
        offs.append(offs[-1] + wdt)
    p = [w_in_mix[:, offs[i]:offs[i + 1]] for i in range(12)]
    d = w_in_mix.shape[0]
    z = lambda n: jnp.zeros((d, n), w_in_mix.dtype)
    cols = [p[0], _pad_cols(p[1], 256), p[2],
            jnp.concatenate([z(B_NOPE), p[3], z(LANES - B_NOPE - B_ROPE)], axis=1),
            p[4], p[5], p[6], p[7], p[8]]
    for i, nh in ((9, D_HEADS), (10, D_KV_HEADS), (11, D_KV_HEADS)):
        for hd in range(nh):
            cols.append(_pad_cols(p[i][:, hd * D_HEAD:(hd + 1) * D_HEAD], LANES))
    return jnp.concatenate(cols, axis=1).astype(BF16)


def _rope_tables(n_tok, rot_dim, lane_off, identity):
    cos = jnp.ones((n_tok, LANES), F32)
    sin_a = jnp.zeros((n_tok, LANES), F32)
    sin_b = jnp.zeros((n_tok, LANES), F32)
    if not identity:
        n_rows = n_tok // GRID_W
        rows = jnp.repeat(jnp.arange(n_rows, dtype=F32), GRID_W)
        cols = jnp.tile(jnp.arange(GRID_W, dtype=F32), n_rows)
        half = rot_dim // 2
        inv = ROPE_THETA ** (-jnp.arange(0, half, 2, dtype=F32) / half)
        ang_r = rows[:, None] * inv
        ang_c = cols[:, None] * inv
        ang = jnp.concatenate([ang_r, ang_r, ang_c, ang_c], axis=-1)
        c, s = jnp.cos(ang), jnp.sin(ang)
        quarter = rot_dim // 4
        first = (np.arange(rot_dim) % (2 * quarter)) < quarter

        def place(a, fill):
            left = jnp.full((n_tok, lane_off), fill, F32)
            right = jnp.full((n_tok, LANES - lane_off - rot_dim), fill, F32)
            return jnp.concatenate([left, a, right], axis=1)

        cos = place(c, 1.0)
        sin_a = place(jnp.where(first, -s, 0.0), 0.0)
        sin_b = place(jnp.where(first, 0.0, s), 0.0)
    return jnp.stack([cos, sin_a, sin_b])


def kernel(x, c, ctx, c_ctx, w_ada, b_ada, norm_pre, norm_post, ffn_w1, ffn_w3, ffn_w2, w_in,
           s5_lambda_re, s5_lambda_im, s5_log_dt, s5_b_re, s5_b_im, s5_c_re, s5_c_im, s5_d, s5_w_glu,
           mla_q_norm, mla_w_uq, mla_kv_norm, mla_w_ukv, hgrn_lb_raw, hgrn_o_norm,
           gqa_q_norm, gqa_k_norm, w_branch, w_out):
    nb, seq, d = x.shape
    n_ctx = ctx.shape[1]
    depth = w_ada.shape[0]

    rows = max(8, -(-(nb + 1) // 8) * 8)
    cvec = jnp.zeros((rows, d), F32).at[:nb].set(c).at[nb].set(c_ctx)
    mod_all = _modulation(cvec, w_ada, b_ada).reshape(depth, rows, N_MOD, d)

    lb_step = jax.nn.softmax(hgrn_lb_raw.astype(F32), axis=1)
    lb_all = jnp.clip(jnp.cumsum(lb_step, axis=1) - lb_step[:, :1], 0.0, 1.0)

    rope_b_lat = _rope_tables(seq, B_ROPE, B_NOPE, False)
    rope_d_lat = _rope_tables(seq, D_HEAD, 0, False)
    rope_b_ctx = _rope_tables(n_ctx, B_ROPE, B_NOPE, True)
    rope_d_ctx = _rope_tables(n_ctx, D_HEAD, 0, True)
    hg_consts = _hgrn_consts()
    s5_w = _s5_weights(s5_lambda_re, s5_lambda_im, s5_log_dt, s5_b_re, s5_b_im, s5_c_re, s5_c_im, S5_MAX_LEVELS)
    lane_head = jnp.arange(HG_W) // C_DV
    avg = (lane_head[:, None] == lane_head[None, :]).astype(BF16) * (1.0 / C_DV)
    n_mixcols = w_in.shape[-1] - N_BRANCH * d

    x_lat, x_ctx = x, ctx
    for layer in range(depth):
        last = layer == depth - 1
        mod_lat = mod_all[layer, :nb]
        mod_ctx = jnp.broadcast_to(mod_all[layer, nb:nb + 1], (nb, N_MOD, d))
        bf = lambda w: w.astype(BF16)
        ffn_a = (norm_pre[layer, 0], norm_post[layer, 0], bf(ffn_w1[layer, 0]), bf(ffn_w3[layer, 0]), bf(ffn_w2[layer, 0]))
        ffn_b = (norm_pre[layer, 2], norm_post[layer, 2], bf(ffn_w1[layer, 1]), bf(ffn_w3[layer, 1]), bf(ffn_w2[layer, 1]))

        x_lat = _ffn_resident(x_lat, mod_lat, 0, *ffn_a)
        x_ctx = _ffn(x_ctx, mod_ctx, 0, *ffn_a)

        wp = _proj_weight(w_in[layer, :, :n_mixcols])
        wg = bf(w_in[layer, :, n_mixcols:])
        qn = _pad_cols(mla_q_norm[layer].reshape(1, B_Q_LORA), 256)
        wuq = mla_w_uq[layer].reshape(B_Q_LORA, B_HEADS, B_NOPE + B_ROPE)
        wuq = jnp.pad(wuq, ((0, 256 - B_Q_LORA), (0, 0), (0, LANES - B_NOPE - B_ROPE))).reshape(256, B_HEADS * LANES)
        wukv = mla_w_ukv[layer].reshape(B_KV_LORA, B_HEADS, B_NOPE + B_V)
        wuk = jnp.pad(wukv[:, :, :B_NOPE], ((0, 0), (0, 0), (0, LANES - B_NOPE))).reshape(B_KV_LORA, B_HEADS * LANES)
        wuv = jnp.pad(wukv[:, :, B_NOPE:], ((0, 0), (0, 0), (0, LANES - B_V))).reshape(B_KV_LORA, B_HEADS * LANES)
        kvn = mla_kv_norm[layer].reshape(1, B_KV_LORA)
        gqn = _pad_cols(gqa_q_norm[layer].reshape(1, D_HEAD), LANES)
        gkn = _pad_cols(gqa_k_norm[layer].reshape(1, D_HEAD), LANES)
        lb = lb_all[:, layer]
        proj_args = (norm_pre[layer, 1], wp)
        mla_args = (qn, bf(wuq), kvn, bf(wuk), bf(wuv), lb, gqn, gkn)

        pl_ = _inproj(x_lat, mod_lat, *proj_args, rope_b_lat, rope_d_lat, *mla_args)
        pc_ = _inproj(x_ctx, mod_ctx, *proj_args, rope_b_ctx, rope_d_ctx, *mla_args)
        (u_l, u2_l, bq_l, bk_l, bv_l, hq_l, hv_l, hk_l, hl_l, hg_l, dq_l, dk_l, dv_l) = pl_
        (u_c, u2_c, bq_c, bk_c, bv_c, hq_c, hv_c, hk_c, hl_c, hg_c, dq_c, dk_c, dv_c) = pc_

        ys_l, ys_c = [], []
        for dr, reverse in enumerate((False, True)):
            wts = tuple(w[layer, dr] for w in s5_w)
            x0 = jnp.zeros((nb, 1, 2 * S5_NSTATE), F32)
            y_c, x_end = _s5_scan(u2_c, x0, wts, reverse)
            y_l, _ = _s5_scan(u2_l, x_end, wts, reverse)
            ys_l.append(y_l)
            ys_c.append(y_c)

        s0 = jnp.zeros((2, nb, HG_W, HG_W), F32)
        ho_c, s_ctx = _hgrn_scan(hq_c, hk_c, hv_c, hl_c, s0, hg_consts)
        ho_l, _ = _hgrn_scan(hq_l, hk_l, hv_l, hl_l, s_ctx, hg_consts)

        bo_l = _attention(bq_l, [(bk_c, bv_c), (bk_l, bv_l)])
        do_l = _attention(dq_l, [(dk_c, dv_c), (dk_l, dv_l)])

        merge_w = (norm_pre[layer, 1], norm_post[layer, 1], wg, bf(w_branch[layer]), bf(w_out[layer]))
        s5_ro = (s5_d[layer].reshape(1, A_WIDTH), bf(s5_w_glu[layer]))
        hn = jnp.tile(hgrn_o_norm[layer], C_HEADS).reshape(1, HG_W)
        x_lat_new = _merge(x_lat, mod_lat, *merge_w, u_l, ys_l[0], ys_l[1], *s5_ro, bo_l, ho_l, hg_l, hn, avg, do_l)
        if not last:
            bo_c = _attention(bq_c, [(bk_c, bv_c)])
            do_c = _attention(dq_c, [(dk_c, dv_c)])
            x_ctx = _merge(x_ctx, mod_ctx, *merge_w, u_c, ys_c[0], ys_c[1], *s5_ro, bo_c, ho_c, hg_c, hn, avg, do_c)
            x_ctx = _ffn(x_ctx, mod_ctx, 2, *ffn_b)
        x_lat = _ffn_resident(x_lat_new, mod_lat, 2, *ffn_b)
    return x_lat
```

```python
import functools

import jax
import jax.numpy as jnp
import numpy as np
from jax import lax
from jax.experimental import pallas as pl
from jax.experimental.pallas import tpu as pltpu

GRID_W = 64
FFN_RES_WEIGHT = 0.5
N_MOD = 9
EPS = 1e-6
ROPE_THETA = 10000.0
F_FLOOR = 1e-20

A_WIDTH = 256
A_GROUP = 16
A_GROUPS = A_WIDTH // A_GROUP
A_STATE = 64

B_HEADS = 4
B_NOPE = 64
B_ROPE = 32
B_V = 64
B_Q_LORA = 192
B_KV_LORA = 128

C_HEADS = 4
C_DK = 64
C_DV = 64

D_HEADS = 4
D_KV_HEADS = 2
D_HEAD = 64

N_BRANCH = 4
BRANCH_W = 256

LANES = 128
VMEM_LIMIT_BYTES = 56 * 1024 * 1024

S5_R = 8
S5_LANES = S5_R * A_WIDTH
S5_NSTATE = A_GROUPS * A_STATE
S5_MAX_ROWS = 512
S5_MAX_LEVELS = 9
HG_T = 128
HG_LEVELS = 7
HG_W = C_HEADS * C_DK
ATTN_ONES = 16
ATTN_QUERIES = (1024, 512, 256, 128)
ATTN_UNROLL = 16

SLOT_S5 = 0
SLOT_CQ = 256
SLOT_CKV = 512
SLOT_KR = 640
SLOT_HQ = 768
SLOT_HV = 1024
SLOT_HF = 1280
SLOT_HB = 1536
SLOT_HG = 1792
SLOT_GQ = 2048
SLOT_GK = 2560
SLOT_GV = 2816
N_PROJ = 3072

BF16 = jnp.bfloat16
F32 = jnp.float32
LOG2E = 1.4426950408889634


def _params(sem, flags=None):
    return pltpu.CompilerParams(dimension_semantics=sem, vmem_limit_bytes=VMEM_LIMIT_BYTES, flags=flags)


def _pick(n, candidates):
    for c in candidates:
        if n % c == 0:
            return c
    raise ValueError(f"no tile for {n} in {candidates}")


def _dot(a, b):
    return jnp.dot(a, b, preferred_element_type=F32)


def _dot_nt(a, b):
    return lax.dot_general(a, b, (((1,), (1,)), ((), ())), preferred_element_type=F32)


def _rms(x, g, n=None):
    n = x.shape[-1] if n is None else n
    ms = jnp.sum(x * x, axis=-1, keepdims=True) * (1.0 / n)
    return x * lax.rsqrt(ms + EPS) * g


def _mod_kernel(c_ref, w_ref, b_ref, o_ref):
    c = c_ref[...]
    a = (c * jax.nn.sigmoid(c)).astype(BF16)
    o_ref[0] = _dot(a, w_ref[0].astype(BF16)) + b_ref[0]


def _modulation(cvec, w_ada, b_ada):
    nl, d, nm = w_ada.shape
    rows = cvec.shape[0]
    tn = _pick(nm, (1152, 1024, 512, 256, 128))
    return pl.pallas_call(
        _mod_kernel,
        grid=(nl, nm // tn),
        in_specs=[
            pl.BlockSpec((rows, d), lambda l, n: (0, 0)),
            pl.BlockSpec((1, d, tn), lambda l, n: (l, 0, n)),
            pl.BlockSpec((1, 1, tn), lambda l, n: (l, 0, n)),
        ],
        out_specs=pl.BlockSpec((1, rows, tn), lambda l, n: (l, 0, n)),
        out_shape=jax.ShapeDtypeStruct((nl, rows, nm), F32),
        compiler_params=_params(("parallel", "parallel")),
        name="adaln_mod",
    )(cvec, w_ada, b_ada.reshape(nl, 1, nm))


def _ffn_kernel(x_ref, mod_ref, gpre_ref, gpost_ref, w1_ref, w3_ref, w2_ref, o_ref, h_scr, acc_scr, *, j, nf):
    f = pl.program_id(2)

    @pl.when(f == 0)
    def _():
        x = x_ref[0]
        shift = mod_ref[0, pl.ds(3 * j, 1), :]
        scale = mod_ref[0, pl.ds(3 * j + 1, 1), :]
        h = _rms(x, gpre_ref[...]) * (1.0 + scale) + shift
        h_scr[...] = h.astype(BF16)
        acc_scr[...] = jnp.zeros_like(acc_scr)

    h = h_scr[...]
    a = _dot(h, w1_ref[...])
    b = _dot(h, w3_ref[...])
    t = (a * jax.nn.sigmoid(a) * b).astype(BF16)
    acc_scr[...] += _dot(t, w2_ref[...])

    @pl.when(f == nf - 1)
    def _():
        gate = mod_ref[0, pl.ds(3 * j + 2, 1), :]
        y = _rms(acc_scr[...], gpost_ref[...])
        o_ref[0] = x_ref[0] + FFN_RES_WEIGHT * gate * y


def _ffn_resident_kernel(x_ref, mod_ref, gpre_ref, gpost_ref, w1_ref, w3_ref, w2_ref, o_ref, *, j, tf):
    x = x_ref[0]
    shift = mod_ref[0, pl.ds(3 * j, 1), :]
    scale = mod_ref[0, pl.ds(3 * j + 1, 1), :]
    gate = mod_ref[0, pl.ds(3 * j + 2, 1), :]
    h = (_rms(x, gpre_ref[...]) * (1.0 + scale) + shift).astype(BF16)
    acc = None
    for f in range(w1_ref.shape[1] // tf):
        a = _dot(h, w1_ref[:, f * tf:(f + 1) * tf])
        b = _dot(h, w3_ref[:, f * tf:(f + 1) * tf])
        t = (a * jax.nn.sigmoid(a) * b).astype(BF16)
        part = _dot(t, w2_ref[f * tf:(f + 1) * tf, :])
        acc = part if acc is None else acc + part
    o_ref[0] = x + FFN_RES_WEIGHT * gate * _rms(acc, gpost_ref[...])


def _ffn_resident(x, mod, j, g_pre, g_post, w1, w3, w2):
    nb, s, d = x.shape
    dff = w1.shape[1]
    tm = _pick(s, (512, 256))
    tf = _pick(dff, (256, 128))
    once = pl.Buffered(1)
    return pl.pallas_call(
        functools.partial(_ffn_resident_kernel, j=j, tf=tf),
        grid=(nb, s // tm),
        in_specs=[
            pl.BlockSpec((1, tm, d), lambda b, i: (b, i, 0)),
            pl.BlockSpec((1, N_MOD, d), lambda b, i: (b, 0, 0)),
            pl.BlockSpec((1, d), lambda b, i: (0, 0)),
            pl.BlockSpec((1, d), lambda b, i: (0, 0)),
            pl.BlockSpec((d, dff), lambda b, i: (0, 0), pipeline_mode=once),
            pl.BlockSpec((d, dff), lambda b, i: (0, 0), pipeline_mode=once),
            pl.BlockSpec((dff, d), lambda b, i: (0, 0), pipeline_mode=once),
        ],
        out_specs=pl.BlockSpec((1, tm, d), lambda b, i: (b, i, 0)),
        out_shape=jax.ShapeDtypeStruct(x.shape, F32),
        compiler_params=_params(("parallel", "parallel")),
        name="ffn_resident",
    )(x, mod, g_pre.reshape(1, d), g_post.reshape(1, d), w1, w3, w2)


def _ffn(x, mod, j, g_pre, g_post, w1, w3, w2):
    nb, s, d = x.shape
    dff = w1.shape[1]
    tm = _pick(s, (1024, 512, 256))
    tf = _pick(dff, (256, 128))
    nf = dff // tf
    return pl.pallas_call(
        functools.partial(_ffn_kernel, j=j, nf=nf),
        grid=(nb, s // tm, nf),
        in_specs=[
            pl.BlockSpec((1, tm, d), lambda b, i, f: (b, i, 0)),
            pl.BlockSpec((1, N_MOD, d), lambda b, i, f: (b, 0, 0)),
            pl.BlockSpec((1, d), lambda b, i, f: (0, 0)),
            pl.BlockSpec((1, d), lambda b, i, f: (0, 0)),
            pl.BlockSpec((d, tf), lambda b, i, f: (0, f)),
            pl.BlockSpec((d, tf), lambda b, i, f: (0, f)),
            pl.BlockSpec((tf, d), lambda b, i, f: (f, 0)),
        ],
        out_specs=pl.BlockSpec((1, tm, d), lambda b, i, f: (b, i, 0)),
        out_shape=jax.ShapeDtypeStruct(x.shape, F32),
        scratch_shapes=[pltpu.VMEM((tm, d), BF16), pltpu.VMEM((tm, d), F32)],
        compiler_params=_params(("parallel", "parallel", "arbitrary")),
        name="ffn_sublayer",
    )(x, mod, g_pre.reshape(1, d), g_post.reshape(1, d), w1, w3, w2)


def _rope(x, cos, sin_a, sin_b, quarter):
    w = x.shape[-1]
    return x * cos + pltpu.roll(x, w - quarter, 1) * sin_a + pltpu.roll(x, quarter, 1) * sin_b


def _inproj_kernel(x_ref, mod_ref, gpre_ref, w_ref, rb_ref, rd_ref, qn_ref, wuq_ref, kvn_ref, wuk_ref, wuv_ref,
                   lb_ref, gqn_ref, gkn_ref,
                   u_ref, u2_ref, bq_ref, bk_ref, bv_ref, hq_ref, hv_ref, hk_ref, hl_ref, hg_ref,
                   dq_ref, dk_ref, dv_ref, h_scr, u_scr, *, tm):
    x = x_ref[0]
    shift = mod_ref[0, pl.ds(3, 1), :]
    scale = mod_ref[0, pl.ds(4, 1), :]
    h_scr[...] = (_rms(x, gpre_ref[...]) * (1.0 + scale) + shift).astype(BF16)

    def proj(lo, width):
        return _dot(h_scr[...], w_ref[:, lo:lo + width])

    cos_d, sa_d, sb_d = rd_ref[0], rd_ref[1], rd_ref[2]
    d_scale = D_HEAD ** -0.5 * LOG2E
    gq = proj(SLOT_GQ, D_HEADS * LANES)
    for hd in range(D_HEADS):
        qh = _rms(gq[:, hd * LANES:(hd + 1) * LANES], gqn_ref[...], n=D_HEAD)
        qh = _rope(qh, cos_d, sa_d, sb_d, D_HEAD // 4) * d_scale
        dq_ref[0, hd] = qh.T[:D_HEAD].astype(BF16)
    gkv = proj(SLOT_GK, 2 * D_KV_HEADS * LANES)
    for hd in range(D_KV_HEADS):
        kh = _rms(gkv[:, hd * LANES:(hd + 1) * LANES], gkn_ref[...], n=D_HEAD)
        kh = _rope(kh, cos_d, sa_d, sb_d, D_HEAD // 4)
        dk_ref[0, hd] = kh[:, :D_HEAD].astype(BF16)
        dv_ref[0, hd] = gkv[:, (D_KV_HEADS + hd) * LANES:(D_KV_HEADS + hd + 1) * LANES].T[:D_HEAD].astype(BF16)

    cos_b, sa_b, sb_b = rb_ref[0], rb_ref[1], rb_ref[2]
    cq = _rms(proj(SLOT_CQ, 256), qn_ref[...], n=B_Q_LORA).astype(BF16)
    q = _dot(cq, wuq_ref[...])
    b_scale = (B_NOPE + B_ROPE) ** -0.5 * LOG2E
    for hd in range(B_HEADS):
        qh = _rope(q[:, hd * LANES:(hd + 1) * LANES], cos_b, sa_b, sb_b, B_ROPE // 4)
        bq_ref[0, hd] = (qh * b_scale).T.astype(BF16)
    ckr = proj(SLOT_CKV, B_KV_LORA + LANES)
    ckv = _rms(ckr[:, :B_KV_LORA], kvn_ref[...]).astype(BF16)
    kn = _dot(ckv, wuk_ref[...])
    vn = _dot(ckv, wuv_ref[...])
    kr = _rope(ckr[:, B_KV_LORA:], cos_b, sa_b, sb_b, B_ROPE // 4)
    for hd in range(B_HEADS):
        bk_ref[0, hd] = (kn[:, hd * LANES:(hd + 1) * LANES] + kr).astype(BF16)
        bv_ref[0, hd] = vn[:, hd * LANES:(hd + 1) * LANES].T[:B_V].astype(BF16)

    hq_ref[0] = proj(SLOT_HQ, HG_W).astype(BF16)
    hv_ref[0] = proj(SLOT_HV, HG_W).astype(BF16)
    hg_ref[0] = proj(SLOT_HG, HG_W).astype(BF16)
    for dr, slot in enumerate((SLOT_HF, SLOT_HB)):
        z = proj(slot, HG_W)
        lb = lb_ref[pl.ds(dr, 1), :]
        f = lb + (1.0 - lb) * jax.nn.sigmoid(z)
        hl_ref[dr, 0] = jnp.log(jnp.maximum(f, F_FLOOR))
        hk_ref[dr, 0] = ((1.0 - lb) * jax.nn.sigmoid(-z)).astype(BF16)

    u = proj(SLOT_S5, A_WIDTH)
    u_ref[0] = u.astype(BF16)
    for hf in range(A_WIDTH // LANES):
        u_scr[hf] = u[:, hf * LANES:(hf + 1) * LANES]
    for r in range(S5_R):
        for hf in range(A_WIDTH // LANES):
            lo = r * A_WIDTH + hf * LANES
            u2_ref[0, :, lo:lo + LANES] = u_scr[hf, pl.ds(r, tm // S5_R, stride=S5_R), :].astype(BF16)


def _inproj(x, mod, g_pre, wp, rope_b, rope_d, qn, wuq, kvn, wuk, wuv, lb, gqn, gkn):
    nb, s, d = x.shape
    tm = _pick(s, (512, 256))
    const2 = lambda b, i: (0, 0)
    const3 = lambda b, i: (0, 0, 0)
    tok = lambda w: pl.BlockSpec((1, tm, w), lambda b, i: (b, i, 0))
    headed = lambda nh, w: pl.BlockSpec((1, nh, tm, w), lambda b, i: (b, 0, i, 0))
    headed_t = lambda nh, w: pl.BlockSpec((1, nh, w, tm), lambda b, i: (b, 0, 0, i))
    dirtok = lambda w: pl.BlockSpec((2, 1, tm, w), lambda b, i: (0, b, i, 0))
    sd = jax.ShapeDtypeStruct
    outs = [
        (sd((nb, s, A_WIDTH), BF16), tok(A_WIDTH)),
        (sd((nb, s // S5_R, S5_LANES), BF16), pl.BlockSpec((1, tm // S5_R, S5_LANES), lambda b, i: (b, i, 0))),
        (sd((nb, B_HEADS, LANES, s), BF16), headed_t(B_HEADS, LANES)),
        (sd((nb, B_HEADS, s, LANES), BF16), headed(B_HEADS, LANES)),
        (sd((nb, B_HEADS, B_V, s), BF16), headed_t(B_HEADS, B_V)),
        (sd((nb, s, HG_W), BF16), tok(HG_W)),
        (sd((nb, s, HG_W), BF16), tok(HG_W)),
        (sd((2, nb, s, HG_W), BF16), dirtok(HG_W)),
        (sd((2, nb, s, HG_W), F32), dirtok(HG_W)),
        (sd((nb, s, HG_W), BF16), tok(HG_W)),
        (sd((nb, D_HEADS, D_HEAD, s), BF16), headed_t(D_HEADS, D_HEAD)),
        (sd((nb, D_KV_HEADS, s, D_HEAD), BF16), headed(D_KV_HEADS, D_HEAD)),
        (sd((nb, D_KV_HEADS, D_HEAD, s), BF16), headed_t(D_KV_HEADS, D_HEAD)),
    ]
    return pl.pallas_call(
        functools.partial(_inproj_kernel, tm=tm),
        grid=(nb, s // tm),
        in_specs=[
            pl.BlockSpec((1, tm, d), lambda b, i: (b, i, 0)),
            pl.BlockSpec((1, N_MOD, d), lambda b, i: (b, 0, 0)),
            pl.BlockSpec((1, d), const2),
            pl.BlockSpec((d, N_PROJ), const2),
            pl.BlockSpec((3, tm, LANES), lambda b, i: (0, i, 0)),
            pl.BlockSpec((3, tm, LANES), lambda b, i: (0, i, 0)),
            pl.BlockSpec((1, 256), const2),
            pl.BlockSpec((256, B_HEADS * LANES), const2),
            pl.BlockSpec((1, B_KV_LORA), const2),
            pl.BlockSpec((B_KV_LORA, B_HEADS * LANES), const2),
            pl.BlockSpec((B_KV_LORA, B_HEADS * LANES), const2),
            pl.BlockSpec((2, HG_W), const2),
            pl.BlockSpec((1, LANES), const2),
            pl.BlockSpec((1, LANES), const2),
        ],
        out_specs=[o[1] for o in outs],
        out_shape=[o[0] for o in outs],
        scratch_shapes=[pltpu.VMEM((tm, d), BF16), pltpu.VMEM((A_WIDTH // LANES, tm, LANES), F32)],
        compiler_params=_params(("parallel", "parallel")),
        name="mixer_inproj",
    )(x, mod, g_pre.reshape(1, d), wp, rope_b, rope_d, qn, wuq, kvn, wuk, wuv, lb, gqn, gkn)


def _s5_kernel(u2_ref, x0_ref, tb_ref, wst_ref, wout_ref, pq_ref, y_ref, xf_ref, carry, *, rows, reverse, nlev):
    i = pl.program_id(1)

    @pl.when(i == 0)
    def _():
        carry[...] = x0_ref[0]

    u2 = u2_ref[0]
    sloc = _dot(u2, wst_ref[...])
    ridx = lax.broadcasted_iota(jnp.int32, (rows, 1), 0)

    def cmul(xv, lev):
        p = pq_ref[pl.ds(2 * lev, 1), :]
        q = pq_ref[pl.ds(2 * lev + 1, 1), :]
        return xv * p + pltpu.roll(xv, S5_NSTATE, 1) * q

    if reverse:
        e = jnp.where(ridx == rows - 1, carry[...], pltpu.roll(sloc, rows - 1, 0))
    else:
        e = jnp.where(ridx == 0, carry[...], pltpu.roll(sloc, 1, 0))
    xin = e
    for lev in range(nlev):
        dist = 1 << lev
        if reverse:
            sh = jnp.where(ridx < rows - dist, pltpu.roll(xin, rows - dist, 0), 0.0)
        else:
            sh = jnp.where(ridx >= dist, pltpu.roll(xin, dist, 0), 0.0)
        xin = xin + cmul(sh, lev)

    last = 0 if reverse else rows - 1
    nxt = cmul(xin[last:last + 1, :], 0) + sloc[last:last + 1, :]
    carry[...] = nxt
    xf_ref[0] = nxt

    y2 = _dot(u2, tb_ref[...]) + _dot(xin.astype(BF16), wout_ref[...])
    for r in range(S5_R):
        for hf in range(A_WIDTH // LANES):
            lo = r * A_WIDTH + hf * LANES
            y_ref[0, hf, pl.ds(r, rows, stride=S5_R), :] = y2[:, lo:lo + LANES]


def _s5_scan(u2, x0, wts, reverse):
    tb, wst, wout, pq = wts
    nb, n2, _ = u2.shape
    rows = _pick(n2, (S5_MAX_ROWS, 256, 128, 64, 32))
    nt = n2 // rows
    once = pl.Buffered(1)
    nlev = max(1, (rows - 1).bit_length())
    order = (lambda b, i: (b, nt - 1 - i, 0)) if reverse else (lambda b, i: (b, i, 0))
    order_out = (lambda b, i: (b, 0, nt - 1 - i, 0)) if reverse else (lambda b, i: (b, 0, i, 0))
    const2 = lambda b, i: (0, 0)
    y, xf = pl.pallas_call(
        functools.partial(_s5_kernel, rows=rows, reverse=reverse, nlev=nlev),
        grid=(nb, nt),
        in_specs=[
            pl.BlockSpec((1, rows, S5_LANES), order),
            pl.BlockSpec((1, 1, 2 * S5_NSTATE), lambda b, i: (b, 0, 0)),
            pl.BlockSpec((S5_LANES, S5_LANES), const2, pipeline_mode=once),
            pl.BlockSpec((S5_LANES, 2 * S5_NSTATE), const2, pipeline_mode=once),
            pl.BlockSpec((2 * S5_NSTATE, S5_LANES), const2, pipeline_mode=once),
            pl.BlockSpec(pq.shape, const2),
        ],
        out_specs=[
            pl.BlockSpec((1, A_WIDTH // LANES, rows * S5_R, LANES), order_out),
            pl.BlockSpec((1, 1, 2 * S5_NSTATE), lambda b, i: (b, 0, 0)),
        ],
        out_shape=[
            jax.ShapeDtypeStruct((nb, A_WIDTH // LANES, n2 * S5_R, LANES), F32),
            jax.ShapeDtypeStruct((nb, 1, 2 * S5_NSTATE), F32),
        ],
        scratch_shapes=[pltpu.VMEM((1, 2 * S5_NSTATE), F32)],
        compiler_params=_params(("parallel", "arbitrary")),
        name="s5_scan_rev" if reverse else "s5_scan_fwd",
    )(u2, x0, tb, wst, wout, pq)
    return y, xf


def _s5_weights(lam_re, lam_im, log_dt, b_re, b_im, c_re, c_im, nlev):
    g, n, r = A_GROUPS, A_STATE, S5_R
    nl = lam_re.shape[0]
    lam_re = jnp.minimum(lam_re.astype(F32), -1e-4)
    lam_im = lam_im.astype(F32)
    dt = jnp.exp(log_dt.astype(F32))[..., None]
    mag = jnp.exp(lam_re * dt)
    a_re = mag * jnp.cos(lam_im * dt)
    a_im = mag * jnp.sin(lam_im * dt)
    den = lam_re * lam_re + lam_im * lam_im
    num_re = a_re - 1.0
    f_re = (num_re * lam_re + a_im * lam_im) / den
    f_im = (a_im * lam_re - num_re * lam_im) / den
    bb_re = f_re[..., None] * b_re - f_im[..., None] * b_im
    bb_im = f_re[..., None] * b_im + f_im[..., None] * b_re

    def cm(xr, xi, yr, yi):
        return xr * yr - xi * yi, xr * yi + xi * yr

    pr, pi = [jnp.ones_like(a_re)], [jnp.zeros_like(a_im)]
    for _ in range(r):
        nr, ni = cm(pr[-1], pi[-1], a_re, a_im)
        pr.append(nr)
        pi.append(ni)
    pr, pi = jnp.stack(pr), jnp.stack(pi)

    xr, xi = cm(pr[..., None], pi[..., None], bb_re, bb_im)
    yr, yi = cm(c_re, c_im, pr[:, :, :, :, None, :], pi[:, :, :, :, None, :])
    lag = jnp.einsum('ldgon,tldgnc->tldgoc', c_re, xr) - jnp.einsum('ldgon,tldgnc->tldgoc', c_im, xi)

    tau = np.arange(r + 1)
    ain, bout = np.arange(r)[:, None], np.arange(r)[None, :]
    sel_tb = np.stack([(bout - ain)[..., None] == tau, (ain - bout)[..., None] == tau]).astype(np.float32)
    sel_st = np.stack([(r - 1 - np.arange(r))[:, None] == tau, np.arange(r)[:, None] == tau]).astype(np.float32)
    sel_out = np.stack([(np.arange(r) + 1)[:, None] == tau, (r - np.arange(r))[:, None] == tau]).astype(np.float32)
    eye_g = jnp.eye(g, dtype=F32)

    tb = jnp.einsum('dabt,tldgoc,gh->ldagcbho', sel_tb, lag, eye_g).reshape(nl, 2, r * A_WIDTH, r * A_WIDTH)
    wst = jnp.concatenate([
        jnp.einsum('dat,tldgnc,gh->ldagchn', sel_st, xr, eye_g).reshape(nl, 2, r * A_WIDTH, g * n),
        jnp.einsum('dat,tldgnc,gh->ldagchn', sel_st, xi, eye_g).reshape(nl, 2, r * A_WIDTH, g * n)], axis=-1)
    wout = jnp.concatenate([
        jnp.einsum('dbt,tldgon,gh->ldgnbho', sel_out, yr, eye_g).reshape(nl, 2, g * n, r * A_WIDTH),
        jnp.einsum('dbt,tldgon,gh->ldgnbho', sel_out, -yi, eye_g).reshape(nl, 2, g * n, r * A_WIDTH)], axis=-2)

    lr, li = pr[r].reshape(nl, 2, 1, g * n), pi[r].reshape(nl, 2, 1, g * n)
    rows = []
    for _ in range(nlev):
        rows.append(jnp.concatenate([lr, lr], axis=-1))
        rows.append(jnp.concatenate([-li, li], axis=-1))
        lr, li = cm(lr, li, lr, li)
    pq = jnp.concatenate(rows, axis=-2)
    return tb.astype(BF16), wst.astype(BF16), wout.astype(BF16), pq


def _split3(x):
    x1 = x.astype(BF16)
    r1 = x - x1.astype(F32)
    x2 = r1.astype(BF16)
    x3 = (r1 - x2.astype(F32)).astype(BF16)
    return x1, x2, x3


def _hgrn_chunk(q, k, vb, lf, st, tri, sel, sgn_ref, msk_ref, dr, heads, bd, nlev):
    l1, l2, l3 = _split3(lf)
    cum = _dot(tri, l1) + _dot(tri, l2) + _dot(tri, l3)
    tot = jnp.sum(lf, axis=0, keepdims=True)

    mids = _dot(sel, cum.astype(BF16))
    t = q.shape[0]

    def head_scores(qx, kx):
        qh = jnp.concatenate([jnp.where(heads[hd], qx, jnp.zeros_like(qx)) for hd in range(C_HEADS)], axis=0)
        s = _dot_nt(qh, kx)
        return [s[hd * t:(hd + 1) * t, :] for hd in range(C_HEADS)]

    diag = msk_ref[dr, nlev] > 0.5
    att = [jnp.where(diag, s, 0.0) for s in head_scores(q.astype(BF16), k.astype(BF16))]
    for lev in range(nlev):
        e = jnp.exp(sgn_ref[dr, lev] * (cum - mids[lev * t:(lev + 1) * t, :]))
        m = msk_ref[dr, lev] > 0.5
        for hd, s in enumerate(head_scores((q * e).astype(BF16), (k * e).astype(BF16))):
            att[hd] = att[hd] + jnp.where(m, s, 0.0)

    o = _dot_nt((q * jnp.exp(cum)).astype(BF16), st.astype(BF16))
    for hd in range(C_HEADS):
        o = o + _dot(att[hd].astype(BF16), jnp.where(heads[hd], vb, jnp.zeros_like(vb)))

    kend = (k * jnp.exp(tot - cum)).astype(BF16)
    vt = vb.astype(F32).T.astype(BF16)
    new = st * jnp.exp(tot) + jnp.where(bd, _dot(vt, kend), 0.0)
    return o, new


def _hgrn_kernel(qf_ref, qr_ref, kf_ref, kr_ref, vf_ref, vr_ref, lff_ref, lfr_ref, s0_ref,
                 tri_ref, sel_ref, sgn_ref, msk_ref, hm_ref, bd_ref,
                 of_ref, or_ref, sf_ref, st_scr, *, nlev, nb):
    c = pl.program_id(0)

    @pl.when(c == 0)
    def _():
        st_scr[...] = s0_ref[...]

    heads = [hm_ref[pl.ds(hd, 1), :] > 0.5 for hd in range(C_HEADS)]
    bd = bd_ref[...] > 0.5
    dirs = ((qf_ref, kf_ref, vf_ref, lff_ref, of_ref), (qr_ref, kr_ref, vr_ref, lfr_ref, or_ref))
    for dr, (q_ref, k_ref, v_ref, lf_ref, o_ref) in enumerate(dirs):
        for b in range(nb):
            o, new = _hgrn_chunk(q_ref[b].astype(F32), k_ref[0, b].astype(F32), v_ref[b], lf_ref[0, b],
                                 st_scr[dr, b], tri_ref[dr], sel_ref[dr], sgn_ref, msk_ref, dr, heads, bd, nlev)
            o_ref[b] = o
            st_scr[dr, b] = new
            sf_ref[dr, b] = new


def _hgrn_consts():
    t = HG_T
    ti = jnp.arange(t)[:, None]
    si = jnp.arange(t)[None, :]
    tri, sel, sgn, msk = [], [], [], []
    for reverse in (False, True):
        tri.append((si >= ti) if reverse else (si <= ti))
        sels, sgns, msks = [], [], []
        for lev in range(HG_LEVELS):
            h = 1 << lev
            blk_t, blk_s = ti // (2 * h), si // (2 * h)
            hi_t, hi_s = (ti % (2 * h)) >= h, (si % (2 * h)) >= h
            if reverse:
                mid = blk_t * 2 * h + h
                q_role_t, k_role_s = ~hi_t, hi_s
            else:
                mid = blk_t * 2 * h + h - 1
                q_role_t, k_role_s = hi_t, ~hi_s
            sels.append(si == mid)
            sgns.append(jnp.where(q_role_t, 1.0, -1.0))
            msks.append((blk_t == blk_s) & q_role_t & k_role_s)
        msks.append(ti == si)
        sel.append(jnp.concatenate(sels, axis=0))
        sgn.append(jnp.stack(sgns))
        msk.append(jnp.stack(msks))
    lane_head = jnp.arange(HG_W) // C_DK
    hm = (lane_head[None, :] == jnp.arange(C_HEADS)[:, None]).astype(F32)
    bd = (lane_head[:, None] == lane_head[None, :]).astype(F32)
    return (jnp.stack(tri).astype(BF16), jnp.stack(sel).astype(BF16), jnp.stack(sgn).astype(F32),
            jnp.stack(msk).astype(F32), hm, bd)


def _hgrn_scan(q, k, v, lf, s0, consts):
    tri, sel, sgn, msk, hm, bd = consts
    nb, s, w = q.shape
    t = HG_T
    nc = s // t
    fwd3 = pl.BlockSpec((nb, t, w), lambda c: (0, c, 0))
    rev3 = pl.BlockSpec((nb, t, w), lambda c: (0, nc - 1 - c, 0))
    fwd4 = pl.BlockSpec((1, nb, t, w), lambda c: (0, 0, c, 0))
    rev4 = pl.BlockSpec((1, nb, t, w), lambda c: (1, 0, nc - 1 - c, 0))
    whole = lambda a: pl.BlockSpec(a.shape, lambda c: (0,) * a.ndim)
    sgn = sgn.reshape(2, HG_LEVELS, t, 1)
    o_f, o_r, sf = pl.pallas_call(
        functools.partial(_hgrn_kernel, nlev=HG_LEVELS, nb=nb),
        grid=(nc,),
        in_specs=[fwd3, rev3, fwd4, rev4, fwd3, rev3, fwd4, rev4, whole(s0),
                  whole(tri), whole(sel), whole(sgn), whole(msk), whole(hm), whole(bd)],
        out_specs=[fwd3, rev3, whole(s0)],
        out_shape=[
            jax.ShapeDtypeStruct((nb, s, w), F32),
            jax.ShapeDtypeStruct((nb, s, w), F32),
            jax.ShapeDtypeStruct((2, nb, w, w), F32),
        ],
        scratch_shapes=[pltpu.VMEM((2, nb, w, w), F32)],
        compiler_params=_params(("arbitrary",)),
        name="hgrn_scan",
    )(q, q, k, k, v, v, lf, lf, s0, tri, sel, sgn, msk, hm, bd)
    return (o_f, o_r), sf


def _attn_kernel(*refs, nseg, tks, g, tq):
    q_ref = refs[0]
    kv_refs = refs[1:1 + 2 * nseg]
    o_ref = refs[1 + 2 * nseg]
    dv = o_ref.shape[-1]
    n = g * tq
    qt = jnp.concatenate([q_ref[0, hd] for hd in range(g)], axis=-1)

    def scores(k):
        s = _dot(k, qt)
        return s, jnp.max(s, axis=0, keepdims=True)

    def absorb(sm, vt, carry):
        s, smax = sm
        m, acc = carry
        m_new = jnp.maximum(m, smax)
        alpha = jnp.exp2(m - m_new)
        p = jnp.exp2(s - m_new).astype(BF16)
        vt1 = jnp.concatenate([vt, jnp.ones((ATTN_ONES, vt.shape[1]), BF16)], axis=0)
        acc = alpha * acc + _dot(vt1, p)
        return m_new, acc

    carry = (jnp.full((1, n), -jnp.inf, F32), jnp.zeros((dv + ATTN_ONES, n), F32))
    pending = None
    for seg in range(nseg):
        k_ref, vt_ref = kv_refs[2 * seg], kv_refs[2 * seg + 1]
        tk = tks[seg]
        nk = k_ref.shape[2] // tk
        s_first = scores(k_ref[0, 0, 0:tk, :])
        if pending is not None:
            carry = absorb(*pending, carry)
        if nk == 1:
            pending = (s_first, vt_ref[0, 0])
            continue

        def body(j, c, k_ref=k_ref, vt_ref=vt_ref, tk=tk, nk=nk):
            s_cur, smax_cur, m, acc = c
            off_next = pl.multiple_of(jnp.minimum(j + 1, nk - 1) * tk, tk)
            s_next, smax_next = scores(k_ref[0, 0, pl.ds(off_next, tk), :])
            off = pl.multiple_of(j * tk, tk)
            m, acc = absorb((s_cur, smax_cur), vt_ref[0, 0, :, pl.ds(off, tk)], (m, acc))
            return s_next, smax_next, m, acc

        unroll = ATTN_UNROLL if nk % ATTN_UNROLL == 0 else 1
        _, _, m, acc = lax.fori_loop(0, nk, body, s_first + carry, unroll=unroll)
        carry = (m, acc)
        pending = None
    m, acc = carry if pending is None else absorb(*pending, carry)
    out = acc[:dv] / acc[dv:dv + 1, :]
    out = jnp.concatenate([out, jnp.zeros((LANES - dv, n), F32)], axis=0).T
    o_ref[0] = out[:, :dv].reshape(g, tq, dv).astype(o_ref.dtype)


def _attention(q, kvs):
    nb, hq, dk, sq = q.shape
    hkv = kvs[0][0].shape[1]
    dv = kvs[0][1].shape[2]
    g = hq // hkv
    tq = _pick(sq, tuple(n // g for n in ATTN_QUERIES))
    tks = tuple(_pick(k.shape[2], (256, 128)) for k, _ in kvs)
    in_specs = [pl.BlockSpec((1, g, dk, tq), lambda b, h, i: (b, h, 0, i))]
    args = [q]
    for k, v in kvs:
        sk = k.shape[2]
        in_specs.append(pl.BlockSpec((1, 1, sk, dk), lambda b, h, i: (b, h, 0, 0)))
        in_specs.append(pl.BlockSpec((1, 1, dv, sk), lambda b, h, i: (b, h, 0, 0)))
        args += [k, v]
    return pl.pallas_call(
        functools.partial(_attn_kernel, nseg=len(kvs), tks=tks, g=g, tq=tq),
        grid=(nb, hkv, sq // tq),
        in_specs=in_specs,
        out_specs=pl.BlockSpec((1, g, tq, dv), lambda b, h, i: (b, h, i, 0)),
        out_shape=jax.ShapeDtypeStruct((nb, hq, sq, dv), BF16),
        compiler_params=_params(("parallel", "parallel", "arbitrary")),
        name="attention",
    )(*args)


def _merge_kernel(x_ref, mod_ref, gpre_ref, gpost_ref, wg_ref, wb_ref, wo_ref,
                  u_ref, yf_ref, yr_ref, sd_ref, wglu_ref, bo_ref, hof_ref, hor_ref, hg_ref, hn_ref, avg_ref, do_ref,
                  o_ref):
    x = x_ref[0]
    shift = mod_ref[0, pl.ds(3, 1), :]
    scale = mod_ref[0, pl.ds(4, 1), :]
    gate = mod_ref[0, pl.ds(5, 1), :]
    h = (_rms(x, gpre_ref[...]) * (1.0 + scale) + shift).astype(BF16)
    d = x.shape[-1]

    def branch_gate(i):
        return jax.nn.sigmoid(_dot(h, wg_ref[:, i * d:(i + 1) * d]))

    ysum = yf_ref[0] + yr_ref[0]
    y = sd_ref[...] * u_ref[0].astype(F32) + jnp.concatenate([ysum[hf] for hf in range(A_WIDTH // LANES)], axis=-1)
    ge = jax.nn.gelu(y)
    ya = ge * jax.nn.sigmoid(_dot(ge.astype(BF16), wglu_ref[...]))
    merged = branch_gate(0) * _dot(ya.astype(BF16), wb_ref[0])

    yb = jnp.concatenate([bo_ref[0, hd] for hd in range(B_HEADS)], axis=-1)
    merged = merged + branch_gate(1) * _dot(yb, wb_ref[1])

    o2 = hof_ref[0] + hor_ref[0]
    ms = _dot((o2 * o2).astype(BF16), avg_ref[...])
    gz = hg_ref[0].astype(F32)
    yc = o2 * lax.rsqrt(ms + EPS) * hn_ref[...] * (gz * jax.nn.sigmoid(gz))
    merged = merged + branch_gate(2) * _dot(yc.astype(BF16), wb_ref[2])

    yd = jnp.concatenate([do_ref[0, hd] for hd in range(D_HEADS)], axis=-1)
    merged = merged + branch_gate(3) * _dot(yd, wb_ref[3])

    yo = _dot(merged.astype(BF16), wo_ref[...])
    o_ref[0] = x + gate * _rms(yo, gpost_ref[...])


def _merge(x, mod, g_pre, g_post, wg, wb, wo, u, yf, yr, s5d, wglu, bo, ho, hg, hn, avg, do):
    nb, s, d = x.shape
    tm = _pick(s, (512, 256))
    const2 = lambda b, i: (0, 0)
    const3 = lambda b, i: (0, 0, 0)
    tok = lambda w: pl.BlockSpec((1, tm, w), lambda b, i: (b, i, 0))
    halves = pl.BlockSpec((1, A_WIDTH // LANES, tm, LANES), lambda b, i: (b, 0, i, 0))
    return pl.pallas_call(
        _merge_kernel,
        grid=(nb, s // tm),
        in_specs=[
            tok(d),
            pl.BlockSpec((1, N_MOD, d), lambda b, i: (b, 0, 0)),
            pl.BlockSpec((1, d), const2),
            pl.BlockSpec((1, d), const2),
            pl.BlockSpec((d, N_BRANCH * d), const2),
            pl.BlockSpec((N_BRANCH, BRANCH_W, d), const3),
            pl.BlockSpec((d, d), const2),
            tok(A_WIDTH), halves, halves,
            pl.BlockSpec((1, A_WIDTH), const2),
            pl.BlockSpec((A_WIDTH, A_WIDTH), const2),
            pl.BlockSpec((1, B_HEADS, tm, B_V), lambda b, i: (b, 0, i, 0)),
            tok(HG_W), tok(HG_W),
            tok(HG_W),
            pl.BlockSpec((1, HG_W), const2),
            pl.BlockSpec((HG_W, HG_W), const2),
            pl.BlockSpec((1, D_HEADS, tm, D_HEAD), lambda b, i: (b, 0, i, 0)),
        ],
        out_specs=tok(d),
        out_shape=jax.ShapeDtypeStruct(x.shape, F32),
        compiler_params=_params(("parallel", "parallel")),
        name="merge_out",
    )(x, mod, g_pre.reshape(1, d), g_post.reshape(1, d), wg, wb, wo, u, yf, yr, s5d, wglu, bo, ho[0], ho[1], hg, hn,
      avg, do)


def _pad_cols(w, width):
    return jnp.pad(w, ((0, 0), (0, width - w.shape[1])))


def _proj_weight(w_in_mix):
    offs = [0]
    for wdt in (A_WIDTH, B_Q_LORA, B_KV_LORA, B_ROPE, HG_W, HG_W, HG_W, HG_W, HG_W,
                D_HEADS * D_HEAD, D_KV_HEADS * D_HEAD, D_KV_HEADS * D_HEAD):
        offs.append(offs[-1] + wdt)
    p = [w_in_mix[:, offs[i]:offs[i + 1]] for i in range(12)]
    d = w_in_mix.shape[0]
    z = lambda n: jnp.zeros((d, n), w_in_mix.dtype)
    cols = [p[0], _pad_cols(p[1], 256), p[2],
            jnp.concatenate([z(B_NOPE), p[3], z(LANES - B_NOPE - B_ROPE)], axis=1),
            p[4], p[5], p[6], p[7], p[8]]
    for i, nh in ((9, D_HEADS), (10, D_KV_HEADS), (11, D_KV_HEADS)):
        for hd in range(nh):
            cols.append(_pad_cols(p[i][:, hd * D_HEAD:(hd + 1) * D_HEAD], LANES))
    return jnp.concatenate(cols, axis=1).astype(BF16)


def _rope_tables(n_tok, rot_dim, lane_off, identity):
    cos = jnp.ones((n_tok, LANES), F32)
    sin_a = jnp.zeros((n_tok, LANES), F32)
    sin_b = jnp.zeros((n_tok, LANES), F32)
    if not identity:
        n_rows = n_tok // GRID_W
        rows = jnp.repeat(jnp.arange(n_rows, dtype=F32), GRID_W)
        cols = jnp.tile(jnp.arange(GRID_W, dtype=F32), n_rows)
        half = rot_dim // 2
        inv = ROPE_THETA ** (-jnp.arange(0, half, 2, dtype=F32) / half)
        ang_r = rows[:, None] * inv
        ang_c = cols[:, None] * inv
        ang = jnp.concatenate([ang_r, ang_r, ang_c, ang_c], axis=-1)
        c, s = jnp.cos(ang), jnp.sin(ang)
        quarter = rot_dim // 4
        first = (np.arange(rot_dim) % (2 * quarter)) < quarter

        def place(a, fill):
            left = jnp.full((n_tok, lane_off), fill, F32)
            right = jnp.full((n_tok, LANES - lane_off - rot_dim), fill, F32)
            return jnp.concatenate([left, a, right], axis=1)

        cos = place(c, 1.0)
        sin_a = place(jnp.where(first, -s, 0.0), 0.0)
        sin_b = place(jnp.where(first, 0.0, s), 0.0)
    return jnp.stack([cos, sin_a, sin_b])


def kernel(x, c, ctx, c_ctx, w_ada, b_ada, norm_pre, norm_post, ffn_w1, ffn_w3, ffn_w2, w_in,
           s5_lambda_re, s5_lambda_im, s5_log_dt, s5_b_re, s5_b_im, s5_c_re, s5_c_im, s5_d, s5_w_glu,
           mla_q_norm, mla_w_uq, mla_kv_norm, mla_w_ukv, hgrn_lb_raw, hgrn_o_norm,
           gqa_q_norm, gqa_k_norm, w_branch, w_out):
    nb, seq, d = x.shape
    n_ctx = ctx.shape[1]
    depth = w_ada.shape[0]

    rows = max(8, -(-(nb + 1) // 8) * 8)
    cvec = jnp.zeros((rows, d), F32).at[:nb].set(c).at[nb].set(c_ctx)
    mod_all = _modulation(cvec, w_ada, b_ada).reshape(depth, rows, N_MOD, d)

    lb_step = jax.nn.softmax(hgrn_lb_raw.astype(F32), axis=1)
    lb_all = jnp.clip(jnp.cumsum(lb_step, axis=1) - lb_step[:, :1], 0.0, 1.0)

    rope_b_lat = _rope_tables(seq, B_ROPE, B_NOPE, False)
    rope_d_lat = _rope_tables(seq, D_HEAD, 0, False)
    rope_b_ctx = _rope_tables(n_ctx, B_ROPE, B_NOPE, True)
    rope_d_ctx = _rope_tables(n_ctx, D_HEAD, 0, True)
    hg_consts = _hgrn_consts()
    s5_w = _s5_weights(s5_lambda_re, s5_lambda_im, s5_log_dt, s5_b_re, s5_b_im, s5_c_re, s5_c_im, S5_MAX_LEVELS)
    lane_head = jnp.arange(HG_W) // C_DV
    avg = (lane_head[:, None] == lane_head[None, :]).astype(BF16) * (1.0 / C_DV)
    n_mixcols = w_in.shape[-1] - N_BRANCH * d

    x_lat, x_ctx = x, ctx
    for layer in range(depth):
        last = layer == depth - 1
        mod_lat = mod_all[layer, :nb]
        mod_ctx = jnp.broadcast_to(mod_all[layer, nb:nb + 1], (nb, N_MOD, d))
        bf = lambda w: w.astype(BF16)
        ffn_a = (norm_pre[layer, 0], norm_post[layer, 0], bf(ffn_w1[layer, 0]), bf(ffn_w3[layer, 0]), bf(ffn_w2[layer, 0]))
        ffn_b = (norm_pre[layer, 2], norm_post[layer, 2], bf(ffn_w1[layer, 1]), bf(ffn_w3[layer, 1]), bf(ffn_w2[layer, 1]))

        x_lat = _ffn_resident(x_lat, mod_lat, 0, *ffn_a)
        x_ctx = _ffn(x_ctx, mod_ctx, 0, *ffn_a)

        wp = _proj_weight(w_in[layer, :, :n_mixcols])
        wg = bf(w_in[layer, :, n_mixcols:])
        qn = _pad_cols(mla_q_norm[layer].reshape(1, B_Q_LORA), 256)
        wuq = mla_w_uq[layer].reshape(B_Q_LORA, B_HEADS, B_NOPE + B_ROPE)
        wuq = jnp.pad(wuq, ((0, 256 - B_Q_LORA), (0, 0), (0, LANES - B_NOPE - B_ROPE))).reshape(256, B_HEADS * LANES)
        wukv = mla_w_ukv[layer].reshape(B_KV_LORA, B_HEADS, B_NOPE + B_V)
        wuk = jnp.pad(wukv[:, :, :B_NOPE], ((0, 0), (0, 0), (0, LANES - B_NOPE))).reshape(B_KV_LORA, B_HEADS * LANES)
        wuv = jnp.pad(wukv[:, :, B_NOPE:], ((0, 0), (0, 0), (0, LANES - B_V))).reshape(B_KV_LORA, B_HEADS * LANES)
        kvn = mla_kv_norm[layer].reshape(1, B_KV_LORA)
        gqn = _pad_cols(gqa_q_norm[layer].reshape(1, D_HEAD), LANES)
        gkn = _pad_cols(gqa_k_norm[layer].reshape(1, D_HEAD), LANES)
        lb = lb_all[:, layer]
        proj_args = (norm_pre[layer, 1], wp)
        mla_args = (qn, bf(wuq), kvn, bf(wuk), bf(wuv), lb, gqn, gkn)

        pl_ = _inproj(x_lat, mod_lat, *proj_args, rope_b_lat, rope_d_lat, *mla_args)
        pc_ = _inproj(x_ctx, mod_ctx, *proj_args, rope_b_ctx, rope_d_ctx, *mla_args)
        (u_l, u2_l, bq_l, bk_l, bv_l, hq_l, hv_l, hk_l, hl_l, hg_l, dq_l, dk_l, dv_l) = pl_
        (u_c, u2_c, bq_c, bk_c, bv_c, hq_c, hv_c, hk_c, hl_c, hg_c, dq_c, dk_c, dv_c) = pc_

        ys_l, ys_c = [], []
        for dr, reverse in enumerate((False, True)):
            wts = tuple(w[layer, dr] for w in s5_w)
            x0 = jnp.zeros((nb, 1, 2 * S5_NSTATE), F32)
            y_c, x_end = _s5_scan(u2_c, x0, wts, reverse)
            y_l, _ = _s5_scan(u2_l, x_end, wts, reverse)
            ys_l.append(y_l)
            ys_c.append(y_c)

        s0 = jnp.zeros((2, nb, HG_W, HG_W), F32)
        ho_c, s_ctx = _hgrn_scan(hq_c, hk_c, hv_c, hl_c, s0, hg_consts)
        ho_l, _ = _hgrn_scan(hq_l, hk_l, hv_l, hl_l, s_ctx, hg_consts)

        bo_l = _attention(bq_l, [(bk_c, bv_c), (bk_l, bv_l)])
        do_l = _attention(dq_l, [(dk_c, dv_c), (dk_l, dv_l)])

        merge_w = (norm_pre[layer, 1], norm_post[layer, 1], wg, bf(w_branch[layer]), bf(w_out[layer]))
        s5_ro = (s5_d[layer].reshape(1, A_WIDTH), bf(s5_w_glu[layer]))
        hn = jnp.tile(hgrn_o_norm[layer], C_HEADS).reshape(1, HG_W)
        x_lat_new = _merge(x_lat, mod_lat, *merge_w, u_l, ys_l[0], ys_l[1], *s5_ro, bo_l, ho_l, hg_l, hn, avg, do_l)
        if not last:
            bo_c = _attention(bq_c, [(bk_c, bv_c)])
            do_c = _attention(dq_c, [(dk_c, dv_c)])
            x_ctx = _merge(x_ctx, mod_ctx, *merge_w, u_c, ys_c[0], ys_c[1], *s5_ro, bo_c, ho_c, hg_c, hn, avg, do_c)
            x_ctx = _ffn(x_ctx, mod_ctx, 2, *ffn_b)
        x_lat = _ffn_resident(x_lat_new, mod_lat, 2, *ffn_b)
    return x_lat
```

```python
import functools

import jax
import jax.numpy as jnp
import numpy as np
from jax import lax
from jax.experimental import pallas as pl
from jax.experimental.pallas import tpu as pltpu

GRID_W = 64
FFN_RES_WEIGHT = 0.5
N_MOD = 9
EPS = 1e-6
ROPE_THETA = 10000.0
F_FLOOR = 1e-20

A_WIDTH = 256
A_GROUP = 16
A_GROUPS = A_WIDTH // A_GROUP
A_STATE = 64

B_HEADS = 4
B_NOPE = 64
B_ROPE = 32
B_V = 64
B_Q_LORA = 192
B_KV_LORA = 128

C_HEADS = 4
C_DK = 64
C_DV = 64

D_HEADS = 4
D_KV_HEADS = 2
D_HEAD = 64

N_BRANCH = 4
BRANCH_W = 256

LANES = 128
VMEM_LIMIT_BYTES = 56 * 1024 * 1024

S5_R = 8
S5_LANES = S5_R * A_WIDTH
S5_NSTATE = A_GROUPS * A_STATE
S5_MAX_ROWS = 512
S5_MAX_LEVELS = 9
HG_T = 128
HG_LEVELS = 7
HG_W = C_HEADS * C_DK
ATTN_ONES = 16
ATTN_QUERIES = (1024, 512, 256, 128)
ATTN_SUB = 256
ATTN_UNROLL = 16

SLOT_S5 = 0
SLOT_CQ = 256
SLOT_CKV = 512
SLOT_KR = 640
SLOT_HQ = 768
SLOT_HV = 1024
SLOT_HF = 1280
SLOT_HB = 1536
SLOT_HG = 1792
SLOT_GQ = 2048
SLOT_GK = 2560
SLOT_GV = 2816
N_PROJ = 3072

BF16 = jnp.bfloat16
F32 = jnp.float32
LOG2E = 1.4426950408889634


def _params(sem, flags=None):
    return pltpu.CompilerParams(dimension_semantics=sem, vmem_limit_bytes=VMEM_LIMIT_BYTES, flags=flags)


def _pick(n, candidates):
    for c in candidates:
        if n % c == 0:
            return c
    raise ValueError(f"no tile for {n} in {candidates}")


def _dot(a, b):
    return jnp.dot(a, b, preferred_element_type=F32)


def _dot_nt(a, b):
    return lax.dot_general(a, b, (((1,), (1,)), ((), ())), preferred_element_type=F32)


def _rms(x, g, n=None):
    n = x.shape[-1] if n is None else n
    ms = jnp.sum(x * x, axis=-1, keepdims=True) * (1.0 / n)
    return x * lax.rsqrt(ms + EPS) * g


def _mod_kernel(c_ref, w_ref, b_ref, o_ref):
    c = c_ref[...]
    a = (c * jax.nn.sigmoid(c)).astype(BF16)
    o_ref[0] = _dot(a, w_ref[0].astype(BF16)) + b_ref[0]


def _modulation(cvec, w_ada, b_ada):
    nl, d, nm = w_ada.shape
    rows = cvec.shape[0]
    tn = _pick(nm, (1152, 1024, 512, 256, 128))
    return pl.pallas_call(
        _mod_kernel,
        grid=(nl, nm // tn),
        in_specs=[
            pl.BlockSpec((rows, d), lambda l, n: (0, 0)),
            pl.BlockSpec((1, d, tn), lambda l, n: (l, 0, n)),
            pl.BlockSpec((1, 1, tn), lambda l, n: (l, 0, n)),
        ],
        out_specs=pl.BlockSpec((1, rows, tn), lambda l, n: (l, 0, n)),
        out_shape=jax.ShapeDtypeStruct((nl, rows, nm), F32),
        compiler_params=_params(("parallel", "parallel")),
        name="adaln_mod",
    )(cvec, w_ada, b_ada.reshape(nl, 1, nm))


def _ffn_kernel(x_ref, mod_ref, gpre_ref, gpost_ref, w1_ref, w3_ref, w2_ref, o_ref, h_scr, acc_scr, *, j, nf):
    f = pl.program_id(2)

    @pl.when(f == 0)
    def _():
        x = x_ref[0]
        shift = mod_ref[0, pl.ds(3 * j, 1), :]
        scale = mod_ref[0, pl.ds(3 * j + 1, 1), :]
        h = _rms(x, gpre_ref[...]) * (1.0 + scale) + shift
        h_scr[...] = h.astype(BF16)
        acc_scr[...] = jnp.zeros_like(acc_scr)

    h = h_scr[...]
    a = _dot(h, w1_ref[...])
    b = _dot(h, w3_ref[...])
    t = (a * jax.nn.sigmoid(a) * b).astype(BF16)
    acc_scr[...] += _dot(t, w2_ref[...])

    @pl.when(f == nf - 1)
    def _():
        gate = mod_ref[0, pl.ds(3 * j + 2, 1), :]
        y = _rms(acc_scr[...], gpost_ref[...])
        o_ref[0] = x_ref[0] + FFN_RES_WEIGHT * gate * y


def _ffn_resident_kernel(x_ref, mod_ref, gpre_ref, gpost_ref, w1_ref, w3_ref, w2_ref, o_ref, *, j, tf):
    x = x_ref[0]
    shift = mod_ref[0, pl.ds(3 * j, 1), :]
    scale = mod_ref[0, pl.ds(3 * j + 1, 1), :]
    gate = mod_ref[0, pl.ds(3 * j + 2, 1), :]
    h = (_rms(x, gpre_ref[...]) * (1.0 + scale) + shift).astype(BF16)
    acc = None
    for f in range(w1_ref.shape[1] // tf):
        a = _dot(h, w1_ref[:, f * tf:(f + 1) * tf])
        b = _dot(h, w3_ref[:, f * tf:(f + 1) * tf])
        t = (a * jax.nn.sigmoid(a) * b).astype(BF16)
        part = _dot(t, w2_ref[f * tf:(f + 1) * tf, :])
        acc = part if acc is None else acc + part
    o_ref[0] = x + FFN_RES_WEIGHT * gate * _rms(acc, gpost_ref[...])


def _ffn_resident(x, mod, j, g_pre, g_post, w1, w3, w2):
    nb, s, d = x.shape
    dff = w1.shape[1]
    tm = _pick(s, (512, 256))
    tf = _pick(dff, (256, 128))
    once = pl.Buffered(1)
    return pl.pallas_call(
        functools.partial(_ffn_resident_kernel, j=j, tf=tf),
        grid=(nb, s // tm),
        in_specs=[
            pl.BlockSpec((1, tm, d), lambda b, i: (b, i, 0)),
            pl.BlockSpec((1, N_MOD, d), lambda b, i: (b, 0, 0)),
            pl.BlockSpec((1, d), lambda b, i: (0, 0)),
            pl.BlockSpec((1, d), lambda b, i: (0, 0)),
            pl.BlockSpec((d, dff), lambda b, i: (0, 0), pipeline_mode=once),
            pl.BlockSpec((d, dff), lambda b, i: (0, 0), pipeline_mode=once),
            pl.BlockSpec((dff, d), lambda b, i: (0, 0), pipeline_mode=once),
        ],
        out_specs=pl.BlockSpec((1, tm, d), lambda b, i: (b, i, 0)),
        out_shape=jax.ShapeDtypeStruct(x.shape, F32),
        compiler_params=_params(("parallel", "parallel")),
        name="ffn_resident",
    )(x, mod, g_pre.reshape(1, d), g_post.reshape(1, d), w1, w3, w2)


def _ffn(x, mod, j, g_pre, g_post, w1, w3, w2):
    nb, s, d = x.shape
    dff = w1.shape[1]
    tm = _pick(s, (1024, 512, 256))
    tf = _pick(dff, (256, 128))
    nf = dff // tf
    return pl.pallas_call(
        functools.partial(_ffn_kernel, j=j, nf=nf),
        grid=(nb, s // tm, nf),
        in_specs=[
            pl.BlockSpec((1, tm, d), lambda b, i, f: (b, i, 0)),
            pl.BlockSpec((1, N_MOD, d), lambda b, i, f: (b, 0, 0)),
            pl.BlockSpec((1, d), lambda b, i, f: (0, 0)),
            pl.BlockSpec((1, d), lambda b, i, f: (0, 0)),
            pl.BlockSpec((d, tf), lambda b, i, f: (0, f)),
            pl.BlockSpec((d, tf), lambda b, i, f: (0, f)),
            pl.BlockSpec((tf, d), lambda b, i, f: (f, 0)),
        ],
        out_specs=pl.BlockSpec((1, tm, d), lambda b, i, f: (b, i, 0)),
        out_shape=jax.ShapeDtypeStruct(x.shape, F32),
        scratch_shapes=[pltpu.VMEM((tm, d), BF16), pltpu.VMEM((tm, d), F32)],
        compiler_params=_params(("parallel", "parallel", "arbitrary")),
        name="ffn_sublayer",
    )(x, mod, g_pre.reshape(1, d), g_post.reshape(1, d), w1, w3, w2)


def _rope(x, cos, sin_a, sin_b, quarter):
    w = x.shape[-1]
    return x * cos + pltpu.roll(x, w - quarter, 1) * sin_a + pltpu.roll(x, quarter, 1) * sin_b


def _inproj_kernel(x_ref, mod_ref, gpre_ref, w_ref, rb_ref, rd_ref, qn_ref, wuq_ref, kvn_ref, wuk_ref, wuv_ref,
                   lb_ref, gqn_ref, gkn_ref,
                   u_ref, u2_ref, bq_ref, bk_ref, bv_ref, hq_ref, hv_ref, hk_ref, hl_ref, hg_ref,
                   dq_ref, dk_ref, dv_ref, h_scr, u_scr, *, tm):
    x = x_ref[0]
    shift = mod_ref[0, pl.ds(3, 1), :]
    scale = mod_ref[0, pl.ds(4, 1), :]
    h_scr[...] = (_rms(x, gpre_ref[...]) * (1.0 + scale) + shift).astype(BF16)

    def proj(lo, width):
        return _dot(h_scr[...], w_ref[:, lo:lo + width])

    cos_d, sa_d, sb_d = rd_ref[0], rd_ref[1], rd_ref[2]
    d_scale = D_HEAD ** -0.5 * LOG2E
    gq = proj(SLOT_GQ, D_HEADS * LANES)
    for hd in range(D_HEADS):
        qh = _rms(gq[:, hd * LANES:(hd + 1) * LANES], gqn_ref[...], n=D_HEAD)
        qh = _rope(qh, cos_d, sa_d, sb_d, D_HEAD // 4) * d_scale
        dq_ref[0, hd] = qh.T[:D_HEAD].astype(BF16)
    gkv = proj(SLOT_GK, 2 * D_KV_HEADS * LANES)
    for hd in range(D_KV_HEADS):
        kh = _rms(gkv[:, hd * LANES:(hd + 1) * LANES], gkn_ref[...], n=D_HEAD)
        kh = _rope(kh, cos_d, sa_d, sb_d, D_HEAD // 4)
        dk_ref[0, hd] = kh[:, :D_HEAD].astype(BF16)
        dv_ref[0, hd] = gkv[:, (D_KV_HEADS + hd) * LANES:(D_KV_HEADS + hd + 1) * LANES].T[:D_HEAD].astype(BF16)

    cos_b, sa_b, sb_b = rb_ref[0], rb_ref[1], rb_ref[2]
    cq = _rms(proj(SLOT_CQ, 256), qn_ref[...], n=B_Q_LORA).astype(BF16)
    q = _dot(cq, wuq_ref[...])
    b_scale = (B_NOPE + B_ROPE) ** -0.5 * LOG2E
    for hd in range(B_HEADS):
        qh = _rope(q[:, hd * LANES:(hd + 1) * LANES], cos_b, sa_b, sb_b, B_ROPE // 4)
        bq_ref[0, hd] = (qh * b_scale).T.astype(BF16)
    ckr = proj(SLOT_CKV, B_KV_LORA + LANES)
    ckv = _rms(ckr[:, :B_KV_LORA], kvn_ref[...]).astype(BF16)
    kn = _dot(ckv, wuk_ref[...])
    vn = _dot(ckv, wuv_ref[...])
    kr = _rope(ckr[:, B_KV_LORA:], cos_b, sa_b, sb_b, B_ROPE // 4)
    for hd in range(B_HEADS):
        bk_ref[0, hd] = (kn[:, hd * LANES:(hd + 1) * LANES] + kr).astype(BF16)
        bv_ref[0, hd] = vn[:, hd * LANES:(hd + 1) * LANES].T[:B_V].astype(BF16)

    hq_ref[0] = proj(SLOT_HQ, HG_W).astype(BF16)
    hv_ref[0] = proj(SLOT_HV, HG_W).astype(BF16)
    hg_ref[0] = proj(SLOT_HG, HG_W).astype(BF16)
    for dr, slot in enumerate((SLOT_HF, SLOT_HB)):
        z = proj(slot, HG_W)
        lb = lb_ref[pl.ds(dr, 1), :]
        f = lb + (1.0 - lb) * jax.nn.sigmoid(z)
        hl_ref[dr, 0] = jnp.log(jnp.maximum(f, F_FLOOR))
        hk_ref[dr, 0] = ((1.0 - lb) * jax.nn.sigmoid(-z)).astype(BF16)

    u = proj(SLOT_S5, A_WIDTH)
    u_ref[0] = u.astype(BF16)
    for hf in range(A_WIDTH // LANES):
        u_scr[hf] = u[:, hf * LANES:(hf + 1) * LANES]
    for r in range(S5_R):
        for hf in range(A_WIDTH // LANES):
            lo = r * A_WIDTH + hf * LANES
            u2_ref[0, :, lo:lo + LANES] = u_scr[hf, pl.ds(r, tm // S5_R, stride=S5_R), :].astype(BF16)


def _inproj(x, mod, g_pre, wp, rope_b, rope_d, qn, wuq, kvn, wuk, wuv, lb, gqn, gkn):
    nb, s, d = x.shape
    tm = _pick(s, (512, 256))
    const2 = lambda b, i: (0, 0)
    const3 = lambda b, i: (0, 0, 0)
    tok = lambda w: pl.BlockSpec((1, tm, w), lambda b, i: (b, i, 0))
    headed = lambda nh, w: pl.BlockSpec((1, nh, tm, w), lambda b, i: (b, 0, i, 0))
    headed_t = lambda nh, w: pl.BlockSpec((1, nh, w, tm), lambda b, i: (b, 0, 0, i))
    dirtok = lambda w: pl.BlockSpec((2, 1, tm, w), lambda b, i: (0, b, i, 0))
    sd = jax.ShapeDtypeStruct
    outs = [
        (sd((nb, s, A_WIDTH), BF16), tok(A_WIDTH)),
        (sd((nb, s // S5_R, S5_LANES), BF16), pl.BlockSpec((1, tm // S5_R, S5_LANES), lambda b, i: (b, i, 0))),
        (sd((nb, B_HEADS, LANES, s), BF16), headed_t(B_HEADS, LANES)),
        (sd((nb, B_HEADS, s, LANES), BF16), headed(B_HEADS, LANES)),
        (sd((nb, B_HEADS, B_V, s), BF16), headed_t(B_HEADS, B_V)),
        (sd((nb, s, HG_W), BF16), tok(HG_W)),
        (sd((nb, s, HG_W), BF16), tok(HG_W)),
        (sd((2, nb, s, HG_W), BF16), dirtok(HG_W)),
        (sd((2, nb, s, HG_W), F32), dirtok(HG_W)),
        (sd((nb, s, HG_W), BF16), tok(HG_W)),
        (sd((nb, D_HEADS, D_HEAD, s), BF16), headed_t(D_HEADS, D_HEAD)),
        (sd((nb, D_KV_HEADS, s, D_HEAD), BF16), headed(D_KV_HEADS, D_HEAD)),
        (sd((nb, D_KV_HEADS, D_HEAD, s), BF16), headed_t(D_KV_HEADS, D_HEAD)),
    ]
    return pl.pallas_call(
        functools.partial(_inproj_kernel, tm=tm),
        grid=(nb, s // tm),
        in_specs=[
            pl.BlockSpec((1, tm, d), lambda b, i: (b, i, 0)),
            pl.BlockSpec((1, N_MOD, d), lambda b, i: (b, 0, 0)),
            pl.BlockSpec((1, d), const2),
            pl.BlockSpec((d, N_PROJ), const2),
            pl.BlockSpec((3, tm, LANES), lambda b, i: (0, i, 0)),
            pl.BlockSpec((3, tm, LANES), lambda b, i: (0, i, 0)),
            pl.BlockSpec((1, 256), const2),
            pl.BlockSpec((256, B_HEADS * LANES), const2),
            pl.BlockSpec((1, B_KV_LORA), const2),
            pl.BlockSpec((B_KV_LORA, B_HEADS * LANES), const2),
            pl.BlockSpec((B_KV_LORA, B_HEADS * LANES), const2),
            pl.BlockSpec((2, HG_W), const2),
            pl.BlockSpec((1, LANES), const2),
            pl.BlockSpec((1, LANES), const2),
        ],
        out_specs=[o[1] for o in outs],
        out_shape=[o[0] for o in outs],
        scratch_shapes=[pltpu.VMEM((tm, d), BF16), pltpu.VMEM((A_WIDTH // LANES, tm, LANES), F32)],
        compiler_params=_params(("parallel", "parallel")),
        name="mixer_inproj",
    )(x, mod, g_pre.reshape(1, d), wp, rope_b, rope_d, qn, wuq, kvn, wuk, wuv, lb, gqn, gkn)


def _s5_kernel(u2_ref, x0_ref, tb_ref, wst_ref, wout_ref, pq_ref, y_ref, xf_ref, carry, *, rows, reverse, nlev):
    i = pl.program_id(1)

    @pl.when(i == 0)
    def _():
        carry[...] = x0_ref[0]

    u2 = u2_ref[0]
    sloc = _dot(u2, wst_ref[...])
    ridx = lax.broadcasted_iota(jnp.int32, (rows, 1), 0)

    def cmul(xv, lev):
        p = pq_ref[pl.ds(2 * lev, 1), :]
        q = pq_ref[pl.ds(2 * lev + 1, 1), :]
        return xv * p + pltpu.roll(xv, S5_NSTATE, 1) * q

    if reverse:
        e = jnp.where(ridx == rows - 1, carry[...], pltpu.roll(sloc, rows - 1, 0))
    else:
        e = jnp.where(ridx == 0, carry[...], pltpu.roll(sloc, 1, 0))
    xin = e
    for lev in range(nlev):
        dist = 1 << lev
        if reverse:
            sh = jnp.where(ridx < rows - dist, pltpu.roll(xin, rows - dist, 0), 0.0)
        else:
            sh = jnp.where(ridx >= dist, pltpu.roll(xin, dist, 0), 0.0)
        xin = xin + cmul(sh, lev)

    last = 0 if reverse else rows - 1
    nxt = cmul(xin[last:last + 1, :], 0) + sloc[last:last + 1, :]
    carry[...] = nxt
    xf_ref[0] = nxt

    y2 = _dot(u2, tb_ref[...]) + _dot(xin.astype(BF16), wout_ref[...])
    for r in range(S5_R):
        for hf in range(A_WIDTH // LANES):
            lo = r * A_WIDTH + hf * LANES
            y_ref[0, hf, pl.ds(r, rows, stride=S5_R), :] = y2[:, lo:lo + LANES]


def _s5_scan(u2, x0, wts, reverse):
    tb, wst, wout, pq = wts
    nb, n2, _ = u2.shape
    rows = _pick(n2, (S5_MAX_ROWS, 256, 128, 64, 32))
    nt = n2 // rows
    once = pl.Buffered(1)
    nlev = max(1, (rows - 1).bit_length())
    order = (lambda b, i: (b, nt - 1 - i, 0)) if reverse else (lambda b, i: (b, i, 0))
    order_out = (lambda b, i: (b, 0, nt - 1 - i, 0)) if reverse else (lambda b, i: (b, 0, i, 0))
    const2 = lambda b, i: (0, 0)
    y, xf = pl.pallas_call(
        functools.partial(_s5_kernel, rows=rows, reverse=reverse, nlev=nlev),
        grid=(nb, nt),
        in_specs=[
            pl.BlockSpec((1, rows, S5_LANES), order),
            pl.BlockSpec((1, 1, 2 * S5_NSTATE), lambda b, i: (b, 0, 0)),
            pl.BlockSpec((S5_LANES, S5_LANES), const2, pipeline_mode=once),
            pl.BlockSpec((S5_LANES, 2 * S5_NSTATE), const2, pipeline_mode=once),
            pl.BlockSpec((2 * S5_NSTATE, S5_LANES), const2, pipeline_mode=once),
            pl.BlockSpec(pq.shape, const2),
        ],
        out_specs=[
            pl.BlockSpec((1, A_WIDTH // LANES, rows * S5_R, LANES), order_out),
            pl.BlockSpec((1, 1, 2 * S5_NSTATE), lambda b, i: (b, 0, 0)),
        ],
        out_shape=[
            jax.ShapeDtypeStruct((nb, A_WIDTH // LANES, n2 * S5_R, LANES), F32),
            jax.ShapeDtypeStruct((nb, 1, 2 * S5_NSTATE), F32),
        ],
        scratch_shapes=[pltpu.VMEM((1, 2 * S5_NSTATE), F32)],
        compiler_params=_params(("parallel", "arbitrary")),
        name="s5_scan_rev" if reverse else "s5_scan_fwd",
    )(u2, x0, tb, wst, wout, pq)
    return y, xf


def _s5_weights(lam_re, lam_im, log_dt, b_re, b_im, c_re, c_im, nlev):
    g, n, r = A_GROUPS, A_STATE, S5_R
    nl = lam_re.shape[0]
    lam_re = jnp.minimum(lam_re.astype(F32), -1e-4)
    lam_im = lam_im.astype(F32)
    dt = jnp.exp(log_dt.astype(F32))[..., None]
    mag = jnp.exp(lam_re * dt)
    a_re = mag * jnp.cos(lam_im * dt)
    a_im = mag * jnp.sin(lam_im * dt)
    den = lam_re * lam_re + lam_im * lam_im
    num_re = a_re - 1.0
    f_re = (num_re * lam_re + a_im * lam_im) / den
    f_im = (a_im * lam_re - num_re * lam_im) / den
    bb_re = f_re[..., None] * b_re - f_im[..., None] * b_im
    bb_im = f_re[..., None] * b_im + f_im[..., None] * b_re

    def cm(xr, xi, yr, yi):
        return xr * yr - xi * yi, xr * yi + xi * yr

    pr, pi = [jnp.ones_like(a_re)], [jnp.zeros_like(a_im)]
    for _ in range(r):
        nr, ni = cm(pr[-1], pi[-1], a_re, a_im)
        pr.append(nr)
        pi.append(ni)
    pr, pi = jnp.stack(pr), jnp.stack(pi)

    xr, xi = cm(pr[..., None], pi[..., None], bb_re, bb_im)
    yr, yi = cm(c_re, c_im, pr[:, :, :, :, None, :], pi[:, :, :, :, None, :])
    lag = jnp.einsum('ldgon,tldgnc->tldgoc', c_re, xr) - jnp.einsum('ldgon,tldgnc->tldgoc', c_im, xi)

    eye_g = jnp.eye(g, dtype=F32)
    zero = lambda a: jnp.concatenate([a, jnp.zeros_like(a[:1])], axis=0)
    lag_bd = zero(jnp.einsum('tldgoc,gh->tldgcho', lag, eye_g).reshape(r + 1, nl, 2, A_WIDTH, A_WIDTH).astype(BF16))
    st_bd = jnp.concatenate([
        jnp.einsum('tldgnc,gh->tldgchn', xr, eye_g).reshape(r + 1, nl, 2, A_WIDTH, g * n),
        jnp.einsum('tldgnc,gh->tldgchn', xi, eye_g).reshape(r + 1, nl, 2, A_WIDTH, g * n)], axis=-1).astype(BF16)
    out_bd = jnp.concatenate([
        jnp.einsum('tldgon,gh->tldgnho', yr, eye_g).reshape(r + 1, nl, 2, g * n, A_WIDTH),
        jnp.einsum('tldgon,gh->tldgnho', -yi, eye_g).reshape(r + 1, nl, 2, g * n, A_WIDTH)], axis=-2).astype(BF16)

    ain, bout = np.arange(r)[:, None], np.arange(r)[None, :]
    fwd = np.where(bout >= ain, bout - ain, r + 1)
    tau_tb = (fwd, fwd.T)
    tau_st = (r - 1 - np.arange(r), np.arange(r))
    tau_out = (np.arange(r) + 1, r - np.arange(r))

    def per_dir(fn):
        return jnp.stack([fn(0), fn(1)], axis=1)

    tb = per_dir(lambda dr: jnp.transpose(lag_bd[tau_tb[dr], :, dr], (2, 0, 3, 1, 4))
                 .reshape(nl, r * A_WIDTH, r * A_WIDTH))
    wst = per_dir(lambda dr: jnp.transpose(st_bd[tau_st[dr], :, dr], (1, 0, 2, 3))
                  .reshape(nl, r * A_WIDTH, 2 * g * n))
    wout = per_dir(lambda dr: jnp.transpose(out_bd[tau_out[dr], :, dr], (1, 2, 0, 3))
                   .reshape(nl, 2 * g * n, r * A_WIDTH))

    lr, li = pr[r].reshape(nl, 2, 1, g * n), pi[r].reshape(nl, 2, 1, g * n)
    rows = []
    for _ in range(nlev):
        rows.append(jnp.concatenate([lr, lr], axis=-1))
        rows.append(jnp.concatenate([-li, li], axis=-1))
        lr, li = cm(lr, li, lr, li)
    pq = jnp.concatenate(rows, axis=-2)
    return tb.astype(BF16), wst.astype(BF16), wout.astype(BF16), pq


def _split3(x):
    x1 = x.astype(BF16)
    r1 = x - x1.astype(F32)
    x2 = r1.astype(BF16)
    x3 = (r1 - x2.astype(F32)).astype(BF16)
    return x1, x2, x3


def _hgrn_chunks(chains, tri_ref, sel_ref, sgn_ref, msk_ref, heads, bd, nlev):
    t = chains[0][0].shape[0]

    def head_scores(qx, kx):
        qh = jnp.concatenate([jnp.where(heads[hd], qx, jnp.zeros_like(qx)) for hd in range(C_HEADS)], axis=0)
        s = _dot_nt(qh, kx)
        return [s[hd * t:(hd + 1) * t, :] for hd in range(C_HEADS)]

    cums, tots = [], []
    for q, k, vb, lf, st, dr in chains:
        l1, l2, l3 = _split3(lf)
        tri = tri_ref[dr]
        cums.append(_dot(tri, l1) + _dot(tri, l2) + _dot(tri, l3))
        tots.append(jnp.sum(lf, axis=0, keepdims=True))
    mids = [_dot(sel_ref[c[5]], cum.astype(BF16)) for c, cum in zip(chains, cums)]

    atts = []
    for q, k, vb, lf, st, dr in chains:
        diag = msk_ref[dr, nlev] > 0.5
        atts.append([jnp.where(diag, s, 0.0) for s in head_scores(q.astype(BF16), k.astype(BF16))])
    for lev in range(nlev):
        for ci, (q, k, vb, lf, st, dr) in enumerate(chains):
            e = jnp.exp(sgn_ref[dr, lev] * (cums[ci] - mids[ci][lev * t:(lev + 1) * t, :]))
            m = msk_ref[dr, lev] > 0.5
            for hd, s in enumerate(head_scores((q * e).astype(BF16), (k * e).astype(BF16))):
                atts[ci][hd] = atts[ci][hd] + jnp.where(m, s, 0.0)

    outs = []
    for ci, (q, k, vb, lf, st, dr) in enumerate(chains):
        o = _dot_nt((q * jnp.exp(cums[ci])).astype(BF16), st.astype(BF16))
        for hd in range(C_HEADS):
            o = o + _dot(atts[ci][hd].astype(BF16), jnp.where(heads[hd], vb, jnp.zeros_like(vb)))
        kend = (k * jnp.exp(tots[ci] - cums[ci])).astype(BF16)
        vt = vb.astype(F32).T.astype(BF16)
        new = st * jnp.exp(tots[ci]) + jnp.where(bd, _dot(vt, kend), 0.0)
        outs.append((o, new))
    return outs


def _hgrn_kernel(qf_ref, qr_ref, kf_ref, kr_ref, vf_ref, vr_ref, lff_ref, lfr_ref, s0_ref,
                 tri_ref, sel_ref, sgn_ref, msk_ref, hm_ref, bd_ref,
                 of_ref, or_ref, sf_ref, st_scr, *, nlev, nb):
    c = pl.program_id(0)

    @pl.when(c == 0)
    def _():
        st_scr[...] = s0_ref[...]

    heads = [hm_ref[pl.ds(hd, 1), :] > 0.5 for hd in range(C_HEADS)]
    bd = bd_ref[...] > 0.5
    dirs = ((qf_ref, kf_ref, vf_ref, lff_ref, of_ref), (qr_ref, kr_ref, vr_ref, lfr_ref, or_ref))
    chains, sinks = [], []
    for dr, (q_ref, k_ref, v_ref, lf_ref, o_ref) in enumerate(dirs):
        for b in range(nb):
            chains.append((q_ref[b].astype(F32), k_ref[0, b].astype(F32), v_ref[b], lf_ref[0, b], st_scr[dr, b], dr))
            sinks.append((o_ref, dr, b))
    outs = _hgrn_chunks(chains, tri_ref, sel_ref, sgn_ref, msk_ref, heads, bd, nlev)
    for (o_ref, dr, b), (o, new) in zip(sinks, outs):
        o_ref[b] = o
        st_scr[dr, b] = new
        sf_ref[dr, b] = new


def _hgrn_consts():
    t = HG_T
    ti = jnp.arange(t)[:, None]
    si = jnp.arange(t)[None, :]
    tri, sel, sgn, msk = [], [], [], []
    for reverse in (False, True):
        tri.append((si >= ti) if reverse else (si <= ti))
        sels, sgns, msks = [], [], []
        for lev in range(HG_LEVELS):
            h = 1 << lev
            blk_t, blk_s = ti // (2 * h), si // (2 * h)
            hi_t, hi_s = (ti % (2 * h)) >= h, (si % (2 * h)) >= h
            if reverse:
                mid = blk_t * 2 * h + h
                q_role_t, k_role_s = ~hi_t, hi_s
            else:
                mid = blk_t * 2 * h + h - 1
                q_role_t, k_role_s = hi_t, ~hi_s
            sels.append(si == mid)
            sgns.append(jnp.where(q_role_t, 1.0, -1.0))
            msks.append((blk_t == blk_s) & q_role_t & k_role_s)
        msks.append(ti == si)
        sel.append(jnp.concatenate(sels, axis=0))
        sgn.append(jnp.stack(sgns))
        msk.append(jnp.stack(msks))
    lane_head = jnp.arange(HG_W) // C_DK
    hm = (lane_head[None, :] == jnp.arange(C_HEADS)[:, None]).astype(F32)
    bd = (lane_head[:, None] == lane_head[None, :]).astype(F32)
    return (jnp.stack(tri).astype(BF16), jnp.stack(sel).astype(BF16), jnp.stack(sgn).astype(F32),
            jnp.stack(msk).astype(F32), hm, bd)


def _hgrn_scan(q, k, v, lf, s0, consts):
    tri, sel, sgn, msk, hm, bd = consts
    nb, s, w = q.shape
    t = HG_T
    nc = s // t
    fwd3 = pl.BlockSpec((nb, t, w), lambda c: (0, c, 0))
    rev3 = pl.BlockSpec((nb, t, w), lambda c: (0, nc - 1 - c, 0))
    fwd4 = pl.BlockSpec((1, nb, t, w), lambda c: (0, 0, c, 0))
    rev4 = pl.BlockSpec((1, nb, t, w), lambda c: (1, 0, nc - 1 - c, 0))
    whole = lambda a: pl.BlockSpec(a.shape, lambda c: (0,) * a.ndim)
    sgn = sgn.reshape(2, HG_LEVELS, t, 1)
    o_f, o_r, sf = pl.pallas_call(
        functools.partial(_hgrn_kernel, nlev=HG_LEVELS, nb=nb),
        grid=(nc,),
        in_specs=[fwd3, rev3, fwd4, rev4, fwd3, rev3, fwd4, rev4, whole(s0),
                  whole(tri), whole(sel), whole(sgn), whole(msk), whole(hm), whole(bd)],
        out_specs=[fwd3, rev3, whole(s0)],
        out_shape=[
            jax.ShapeDtypeStruct((nb, s, w), F32),
            jax.ShapeDtypeStruct((nb, s, w), F32),
            jax.ShapeDtypeStruct((2, nb, w, w), F32),
        ],
        scratch_shapes=[pltpu.VMEM((2, nb, w, w), F32)],
        compiler_params=_params(("arbitrary",)),
        name="hgrn_scan",
    )(q, q, k, k, v, v, lf, lf, s0, tri, sel, sgn, msk, hm, bd)
    return (o_f, o_r), sf


def _attn_kernel(*refs, nseg, tks, g, tq):
    q_ref = refs[0]
    kv_refs = refs[1:1 + 2 * nseg]
    o_ref = refs[1 + 2 * nseg]
    dv = o_ref.shape[-1]
    n = g * tq
    qt = jnp.concatenate([q_ref[0, hd] for hd in range(g)], axis=-1)

    def scores(k):
        s = _dot(k, qt)
        sub = min(ATTN_SUB, s.shape[0])
        parts = [s[i:i + sub] for i in range(0, s.shape[0], sub)]
        return tuple(x for p in parts for x in (p, jnp.max(p, axis=0, keepdims=True)))

    def absorb(sm, vt, carry):
        m, acc = carry
        sub = sm[0].shape[0]
        for i in range(len(sm) // 2):
            s, smax = sm[2 * i], sm[2 * i + 1]
            m_new = jnp.maximum(m, smax)
            alpha = jnp.exp2(m - m_new)
            p = jnp.exp2(s - m_new).astype(BF16)
            vt1 = jnp.concatenate([vt[:, i * sub:(i + 1) * sub], jnp.ones((ATTN_ONES, sub), BF16)], axis=0)
            acc = alpha * acc + _dot(vt1, p)
            m = m_new
        return m, acc

    carry = (jnp.full((1, n), -jnp.inf, F32), jnp.zeros((dv + ATTN_ONES, n), F32))
    pending = None
    for seg in range(nseg):
        k_ref, vt_ref = kv_refs[2 * seg], kv_refs[2 * seg + 1]
        tk = tks[seg]
        nk = k_ref.shape[2] // tk
        s_first = scores(k_ref[0, 0, 0:tk, :])
        if pending is not None:
            carry = absorb(*pending, carry)
        if nk == 1:
            pending = (s_first, vt_ref[0, 0])
            continue

        def body(j, c, k_ref=k_ref, vt_ref=vt_ref, tk=tk, nk=nk):
            sm_cur, (m, acc) = c[:-2], c[-2:]
            off_next = pl.multiple_of(jnp.minimum(j + 1, nk - 1) * tk, tk)
            sm_next = scores(k_ref[0, 0, pl.ds(off_next, tk), :])
            off = pl.multiple_of(j * tk, tk)
            m, acc = absorb(sm_cur, vt_ref[0, 0, :, pl.ds(off, tk)], (m, acc))
            return sm_next + (m, acc)

        unroll = ATTN_UNROLL if nk % ATTN_UNROLL == 0 else 1
        m, acc = lax.fori_loop(0, nk, body, s_first + carry, unroll=unroll)[-2:]
        carry = (m, acc)
        pending = None
    m, acc = carry if pending is None else absorb(*pending, carry)
    out = acc[:dv] / acc[dv:dv + 1, :]
    out = jnp.concatenate([out, jnp.zeros((LANES - dv, n), F32)], axis=0).T
    o_ref[0] = out[:, :dv].reshape(g, tq, dv).astype(o_ref.dtype)


def _attention(q, kvs):
    nb, hq, dk, sq = q.shape
    hkv = kvs[0][0].shape[1]
    dv = kvs[0][1].shape[2]
    g = hq // hkv
    tq = _pick(sq, tuple(n // g for n in ATTN_QUERIES))
    tks = tuple(_pick(k.shape[2], (ATTN_SUB, 128)) for k, _ in kvs)
    in_specs = [pl.BlockSpec((1, g, dk, tq), lambda b, h, i: (b, h, 0, i))]
    args = [q]
    for k, v in kvs:
        sk = k.shape[2]
        in_specs.append(pl.BlockSpec((1, 1, sk, dk), lambda b, h, i: (b, h, 0, 0)))
        in_specs.append(pl.BlockSpec((1, 1, dv, sk), lambda b, h, i: (b, h, 0, 0)))
        args += [k, v]
    return pl.pallas_call(
        functools.partial(_attn_kernel, nseg=len(kvs), tks=tks, g=g, tq=tq),
        grid=(nb, hkv, sq // tq),
        in_specs=in_specs,
        out_specs=pl.BlockSpec((1, g, tq, dv), lambda b, h, i: (b, h, i, 0)),
        out_shape=jax.ShapeDtypeStruct((nb, hq, sq, dv), BF16),
        compiler_params=_params(("parallel", "parallel", "arbitrary")),
        name="attention",
    )(*args)


def _merge_kernel(x_ref, mod_ref, gpre_ref, gpost_ref, wg_ref, wb_ref, wo_ref,
                  u_ref, yf_ref, yr_ref, sd_ref, wglu_ref, bo_ref, hof_ref, hor_ref, hg_ref, hn_ref, avg_ref, do_ref,
                  o_ref):
    x = x_ref[0]
    shift = mod_ref[0, pl.ds(3, 1), :]
    scale = mod_ref[0, pl.ds(4, 1), :]
    gate = mod_ref[0, pl.ds(5, 1), :]
    h = (_rms(x, gpre_ref[...]) * (1.0 + scale) + shift).astype(BF16)
    d = x.shape[-1]

    def branch_gate(i):
        return jax.nn.sigmoid(_dot(h, wg_ref[:, i * d:(i + 1) * d]))

    ysum = yf_ref[0] + yr_ref[0]
    y = sd_ref[...] * u_ref[0].astype(F32) + jnp.concatenate([ysum[hf] for hf in range(A_WIDTH // LANES)], axis=-1)
    ge = jax.nn.gelu(y)
    ya = ge * jax.nn.sigmoid(_dot(ge.astype(BF16), wglu_ref[...]))
    merged = branch_gate(0) * _dot(ya.astype(BF16), wb_ref[0])

    yb = jnp.concatenate([bo_ref[0, hd] for hd in range(B_HEADS)], axis=-1)
    merged = merged + branch_gate(1) * _dot(yb, wb_ref[1])

    o2 = hof_ref[0] + hor_ref[0]
    ms = _dot((o2 * o2).astype(BF16), avg_ref[...])
    gz = hg_ref[0].astype(F32)
    yc = o2 * lax.rsqrt(ms + EPS) * hn_ref[...] * (gz * jax.nn.sigmoid(gz))
    merged = merged + branch_gate(2) * _dot(yc.astype(BF16), wb_ref[2])

    yd = jnp.concatenate([do_ref[0, hd] for hd in range(D_HEADS)], axis=-1)
    merged = merged + branch_gate(3) * _dot(yd, wb_ref[3])

    yo = _dot(merged.astype(BF16), wo_ref[...])
    o_ref[0] = x + gate * _rms(yo, gpost_ref[...])


def _merge(x, mod, g_pre, g_post, wg, wb, wo, u, yf, yr, s5d, wglu, bo, ho, hg, hn, avg, do):
    nb, s, d = x.shape
    tm = _pick(s, (512, 256))
    const2 = lambda b, i: (0, 0)
    const3 = lambda b, i: (0, 0, 0)
    tok = lambda w: pl.BlockSpec((1, tm, w), lambda b, i: (b, i, 0))
    halves = pl.BlockSpec((1, A_WIDTH // LANES, tm, LANES), lambda b, i: (b, 0, i, 0))
    return pl.pallas_call(
        _merge_kernel,
        grid=(nb, s // tm),
        in_specs=[
            tok(d),
            pl.BlockSpec((1, N_MOD, d), lambda b, i: (b, 0, 0)),
            pl.BlockSpec((1, d), const2),
            pl.BlockSpec((1, d), const2),
            pl.BlockSpec((d, N_BRANCH * d), const2),
            pl.BlockSpec((N_BRANCH, BRANCH_W, d), const3),
            pl.BlockSpec((d, d), const2),
            tok(A_WIDTH), halves, halves,
            pl.BlockSpec((1, A_WIDTH), const2),
            pl.BlockSpec((A_WIDTH, A_WIDTH), const2),
            pl.BlockSpec((1, B_HEADS, tm, B_V), lambda b, i: (b, 0, i, 0)),
            tok(HG_W), tok(HG_W),
            tok(HG_W),
            pl.BlockSpec((1, HG_W), const2),
            pl.BlockSpec((HG_W, HG_W), const2),
            pl.BlockSpec((1, D_HEADS, tm, D_HEAD), lambda b, i: (b, 0, i, 0)),
        ],
        out_specs=tok(d),
        out_shape=jax.ShapeDtypeStruct(x.shape, F32),
        compiler_params=_params(("parallel", "parallel")),
        name="merge_out",
    )(x, mod, g_pre.reshape(1, d), g_post.reshape(1, d), wg, wb, wo, u, yf, yr, s5d, wglu, bo, ho[0], ho[1], hg, hn,
      avg, do)


def _pad_cols(w, width):
    return jnp.pad(w, ((0, 0), (0, width - w.shape[1])))


def _proj_weight(w_in_mix):
    offs = [0]
    for wdt in (A_WIDTH, B_Q_LORA, B_KV_LORA, B_ROPE, HG_W, HG_W, HG_W, HG_W, HG_W,
                D_HEADS * D_HEAD, D_KV_HEADS * D_HEAD, D_KV_HEADS * D_HEAD):
        offs.append(offs[-1] + wdt)
    p = [w_in_mix[:, offs[i]:offs[i + 1]] for i in range(12)]
    d = w_in_mix.shape[0]
    z = lambda n: jnp.zeros((d, n), w_in_mix.dtype)
    cols = [p[0], _pad_cols(p[1], 256), p[2],
            jnp.concatenate([z(B_NOPE), p[3], z(LANES - B_NOPE - B_ROPE)], axis=1),
            p[4], p[5], p[6], p[7], p[8]]
    for i, nh in ((9, D_HEADS), (10, D_KV_HEADS), (11, D_KV_HEADS)):
        for hd in range(nh):
            cols.append(_pad_cols(p[i][:, hd * D_HEAD:(hd + 1) * D_HEAD], LANES))
    return jnp.concatenate(cols, axis=1).astype(BF16)


def _rope_tables(n_tok, rot_dim, lane_off, identity):
    cos = jnp.ones((n_tok, LANES), F32)
    sin_a = jnp.zeros((n_tok, LANES), F32)
    sin_b = jnp.zeros((n_tok, LANES), F32)
    if not identity:
        n_rows = n_tok // GRID_W
        rows = jnp.repeat(jnp.arange(n_rows, dtype=F32), GRID_W)
        cols = jnp.tile(jnp.arange(GRID_W, dtype=F32), n_rows)
        half = rot_dim // 2
        inv = ROPE_THETA ** (-jnp.arange(0, half, 2, dtype=F32) / half)
        ang_r = rows[:, None] * inv
        ang_c = cols[:, None] * inv
        ang = jnp.concatenate([ang_r, ang_r, ang_c, ang_c], axis=-1)
        c, s = jnp.cos(ang), jnp.sin(ang)
        quarter = rot_dim // 4
        first = (np.arange(rot_dim) % (2 * quarter)) < quarter

        def place(a, fill):
            left = jnp.full((n_tok, lane_off), fill, F32)
            right = jnp.full((n_tok, LANES - lane_off - rot_dim), fill, F32)
            return jnp.concatenate([left, a, right], axis=1)

        cos = place(c, 1.0)
        sin_a = place(jnp.where(first, -s, 0.0), 0.0)
        sin_b = place(jnp.where(first, 0.0, s), 0.0)
    return jnp.stack([cos, sin_a, sin_b])


def kernel(x, c, ctx, c_ctx, w_ada, b_ada, norm_pre, norm_post, ffn_w1, ffn_w3, ffn_w2, w_in,
           s5_lambda_re, s5_lambda_im, s5_log_dt, s5_b_re, s5_b_im, s5_c_re, s5_c_im, s5_d, s5_w_glu,
           mla_q_norm, mla_w_uq, mla_kv_norm, mla_w_ukv, hgrn_lb_raw, hgrn_o_norm,
           gqa_q_norm, gqa_k_norm, w_branch, w_out):
    nb, seq, d = x.shape
    n_ctx = ctx.shape[1]
    depth = w_ada.shape[0]

    rows = max(8, -(-(nb + 1) // 8) * 8)
    cvec = jnp.zeros((rows, d), F32).at[:nb].set(c).at[nb].set(c_ctx)
    mod_all = _modulation(cvec, w_ada, b_ada).reshape(depth, rows, N_MOD, d)

    lb_step = jax.nn.softmax(hgrn_lb_raw.astype(F32), axis=1)
    lb_all = jnp.clip(jnp.cumsum(lb_step, axis=1) - lb_step[:, :1], 0.0, 1.0)

    rope_b_lat = _rope_tables(seq, B_ROPE, B_NOPE, False)
    rope_d_lat = _rope_tables(seq, D_HEAD, 0, False)
    rope_b_ctx = _rope_tables(n_ctx, B_ROPE, B_NOPE, True)
    rope_d_ctx = _rope_tables(n_ctx, D_HEAD, 0, True)
    hg_consts = _hgrn_consts()
    s5_w = _s5_weights(s5_lambda_re, s5_lambda_im, s5_log_dt, s5_b_re, s5_b_im, s5_c_re, s5_c_im, S5_MAX_LEVELS)
    lane_head = jnp.arange(HG_W) // C_DV
    avg = (lane_head[:, None] == lane_head[None, :]).astype(BF16) * (1.0 / C_DV)
    n_mixcols = w_in.shape[-1] - N_BRANCH * d

    x_lat, x_ctx = x, ctx
    for layer in range(depth):
        last = layer == depth - 1
        mod_lat = mod_all[layer, :nb]
        mod_ctx = jnp.broadcast_to(mod_all[layer, nb:nb + 1], (nb, N_MOD, d))
        bf = lambda w: w.astype(BF16)
        ffn_a = (norm_pre[layer, 0], norm_post[layer, 0], bf(ffn_w1[layer, 0]), bf(ffn_w3[layer, 0]), bf(ffn_w2[layer, 0]))
        ffn_b = (norm_pre[layer, 2], norm_post[layer, 2], bf(ffn_w1[layer, 1]), bf(ffn_w3[layer, 1]), bf(ffn_w2[layer, 1]))

        x_lat = _ffn_resident(x_lat, mod_lat, 0, *ffn_a)
        x_ctx = _ffn(x_ctx, mod_ctx, 0, *ffn_a)

        wp = _proj_weight(w_in[layer, :, :n_mixcols])
        wg = bf(w_in[layer, :, n_mixcols:])
        qn = _pad_cols(mla_q_norm[layer].reshape(1, B_Q_LORA), 256)
        wuq = mla_w_uq[layer].reshape(B_Q_LORA, B_HEADS, B_NOPE + B_ROPE)
        wuq = jnp.pad(wuq, ((0, 256 - B_Q_LORA), (0, 0), (0, LANES - B_NOPE - B_ROPE))).reshape(256, B_HEADS * LANES)
        wukv = mla_w_ukv[layer].reshape(B_KV_LORA, B_HEADS, B_NOPE + B_V)
        wuk = jnp.pad(wukv[:, :, :B_NOPE], ((0, 0), (0, 0), (0, LANES - B_NOPE))).reshape(B_KV_LORA, B_HEADS * LANES)
        wuv = jnp.pad(wukv[:, :, B_NOPE:], ((0, 0), (0, 0), (0, LANES - B_V))).reshape(B_KV_LORA, B_HEADS * LANES)
        kvn = mla_kv_norm[layer].reshape(1, B_KV_LORA)
        gqn = _pad_cols(gqa_q_norm[layer].reshape(1, D_HEAD), LANES)
        gkn = _pad_cols(gqa_k_norm[layer].reshape(1, D_HEAD), LANES)
        lb = lb_all[:, layer]
        proj_args = (norm_pre[layer, 1], wp)
        mla_args = (qn, bf(wuq), kvn, bf(wuk), bf(wuv), lb, gqn, gkn)

        pl_ = _inproj(x_lat, mod_lat, *proj_args, rope_b_lat, rope_d_lat, *mla_args)
        pc_ = _inproj(x_ctx, mod_ctx, *proj_args, rope_b_ctx, rope_d_ctx, *mla_args)
        (u_l, u2_l, bq_l, bk_l, bv_l, hq_l, hv_l, hk_l, hl_l, hg_l, dq_l, dk_l, dv_l) = pl_
        (u_c, u2_c, bq_c, bk_c, bv_c, hq_c, hv_c, hk_c, hl_c, hg_c, dq_c, dk_c, dv_c) = pc_

        ys_l, ys_c = [], []
        for dr, reverse in enumerate((False, True)):
            wts = tuple(w[layer, dr] for w in s5_w)
            x0 = jnp.zeros((nb, 1, 2 * S5_NSTATE), F32)
            y_c, x_end = _s5_scan(u2_c, x0, wts, reverse)
            y_l, _ = _s5_scan(u2_l, x_end, wts, reverse)
            ys_l.append(y_l)
            ys_c.append(y_c)

        s0 = jnp.zeros((2, nb, HG_W, HG_W), F32)
        ho_c, s_ctx = _hgrn_scan(hq_c, hk_c, hv_c, hl_c, s0, hg_consts)
        ho_l, _ = _hgrn_scan(hq_l, hk_l, hv_l, hl_l, s_ctx, hg_consts)

        bo_l = _attention(bq_l, [(bk_c, bv_c), (bk_l, bv_l)])
        do_l = _attention(dq_l, [(dk_c, dv_c), (dk_l, dv_l)])

        merge_w = (norm_pre[layer, 1], norm_post[layer, 1], wg, bf(w_branch[layer]), bf(w_out[layer]))
        s5_ro = (s5_d[layer].reshape(1, A_WIDTH), bf(s5_w_glu[layer]))
        hn = jnp.tile(hgrn_o_norm[layer], C_HEADS).reshape(1, HG_W)
        x_lat_new = _merge(x_lat, mod_lat, *merge_w, u_l, ys_l[0], ys_l[1], *s5_ro, bo_l, ho_l, hg_l, hn, avg, do_l)
        if not last:
            bo_c = _attention(bq_c, [(bk_c, bv_c)])
            do_c = _attention(dq_c, [(dk_c, dv_c)])
            x_ctx = _merge(x_ctx, mod_ctx, *merge_w, u_c, ys_c[0], ys_c[1], *s5_ro, bo_c, ho_c, hg_c, hn, avg, do_c)
            x_ctx = _ffn(x_ctx, mod_ctx, 2, *ffn_b)
        x_lat = _ffn_resident(x_lat_new, mod_lat, 2, *ffn_b)
    return x_lat
```

```python
import functools

import jax
import jax.numpy as jnp
import numpy as np
from jax import lax
from jax.experimental import pallas as pl
from jax.experimental.pallas import tpu as pltpu

GRID_W = 64
FFN_RES_WEIGHT = 0.5
N_MOD = 9
EPS = 1e-6
ROPE_THETA = 10000.0
F_FLOOR = 1e-20

A_WIDTH = 256
A_GROUP = 16
A_GROUPS = A_WIDTH // A_GROUP
A_STATE = 64

B_HEADS = 4
B_NOPE = 64
B_ROPE = 32
B_V = 64
B_Q_LORA = 192
B_KV_LORA = 128

C_HEADS = 4
C_DK = 64
C_DV = 64

D_HEADS = 4
D_KV_HEADS = 2
D_HEAD = 64

N_BRANCH = 4
BRANCH_W = 256

LANES = 128
VMEM_LIMIT_BYTES = 56 * 1024 * 1024

S5_R = 8
S5_LANES = S5_R * A_WIDTH
S5_NSTATE = A_GROUPS * A_STATE
S5_MAX_ROWS = 512
S5_MAX_LEVELS = 9
HG_T = 128
HG_LEVELS = 7
HG_W = C_HEADS * C_DK
ATTN_ONES = 16
ATTN_QUERIES = (1024, 512, 256, 128)
ATTN_SUB = 256
ATTN_UNROLL = 16

SLOT_S5 = 0
SLOT_CQ = 256
SLOT_CKV = 512
SLOT_KR = 640
SLOT_HQ = 768
SLOT_HV = 1024
SLOT_HF = 1280
SLOT_HB = 1536
SLOT_HG = 1792
SLOT_GQ = 2048
SLOT_GK = 2560
SLOT_GV = 2816
N_PROJ = 3072

BF16 = jnp.bfloat16
F32 = jnp.float32
LOG2E = 1.4426950408889634


def _params(sem, flags=None):
    return pltpu.CompilerParams(dimension_semantics=sem, vmem_limit_bytes=VMEM_LIMIT_BYTES, flags=flags)


def _pick(n, candidates):
    for c in candidates:
        if n % c == 0:
            return c
    raise ValueError(f"no tile for {n} in {candidates}")


def _dot(a, b):
    return jnp.dot(a, b, preferred_element_type=F32)


def _dot_nt(a, b):
    return lax.dot_general(a, b, (((1,), (1,)), ((), ())), preferred_element_type=F32)


def _rms(x, g, n=None):
    n = x.shape[-1] if n is None else n
    ms = jnp.sum(x * x, axis=-1, keepdims=True) * (1.0 / n)
    return x * lax.rsqrt(ms + EPS) * g


def _mod_kernel(c_ref, w_ref, b_ref, o_ref):
    c = c_ref[...]
    a = (c * jax.nn.sigmoid(c)).astype(BF16)
    o_ref[0] = _dot(a, w_ref[0].astype(BF16)) + b_ref[0]


def _modulation(cvec, w_ada, b_ada):
    nl, d, nm = w_ada.shape
    rows = cvec.shape[0]
    tn = _pick(nm, (1152, 1024, 512, 256, 128))
    return pl.pallas_call(
        _mod_kernel,
        grid=(nl, nm // tn),
        in_specs=[
            pl.BlockSpec((rows, d), lambda l, n: (0, 0)),
            pl.BlockSpec((1, d, tn), lambda l, n: (l, 0, n)),
            pl.BlockSpec((1, 1, tn), lambda l, n: (l, 0, n)),
        ],
        out_specs=pl.BlockSpec((1, rows, tn), lambda l, n: (l, 0, n)),
        out_shape=jax.ShapeDtypeStruct((nl, rows, nm), F32),
        compiler_params=_params(("parallel", "parallel")),
        name="adaln_mod",
    )(cvec, w_ada, b_ada.reshape(nl, 1, nm))


def _ffn_kernel(x_ref, mod_ref, gpre_ref, gpost_ref, w1_ref, w3_ref, w2_ref, o_ref, *, j, tf):
    x = x_ref[0]
    shift = mod_ref[0, pl.ds(3 * j, 1), :]
    scale = mod_ref[0, pl.ds(3 * j + 1, 1), :]
    gate = mod_ref[0, pl.ds(3 * j + 2, 1), :]
    h = (_rms(x, gpre_ref[...]) * (1.0 + scale) + shift).astype(BF16)
    acc = None
    for f in range(w1_ref.shape[1] // tf):
        a = _dot(h, w1_ref[:, f * tf:(f + 1) * tf])
        b = _dot(h, w3_ref[:, f * tf:(f + 1) * tf])
        t = (a * jax.nn.sigmoid(a) * b).astype(BF16)
        part = _dot(t, w2_ref[f * tf:(f + 1) * tf, :])
        acc = part if acc is None else acc + part
    o_ref[0] = x + FFN_RES_WEIGHT * gate * _rms(acc, gpost_ref[...])


def _ffn(x, mod, j, g_pre, g_post, w1, w3, w2):
    nb, s, d = x.shape
    dff = w1.shape[1]
    tm = _pick(s, (512, 256))
    tf = _pick(dff, (256, 128))
    once = pl.Buffered(1)
    return pl.pallas_call(
        functools.partial(_ffn_kernel, j=j, tf=tf),
        grid=(nb, s // tm),
        in_specs=[
            pl.BlockSpec((1, tm, d), lambda b, i: (b, i, 0)),
            pl.BlockSpec((1, N_MOD, d), lambda b, i: (b, 0, 0)),
            pl.BlockSpec((1, d), lambda b, i: (0, 0)),
            pl.BlockSpec((1, d), lambda b, i: (0, 0)),
            pl.BlockSpec((d, dff), lambda b, i: (0, 0), pipeline_mode=once),
            pl.BlockSpec((d, dff), lambda b, i: (0, 0), pipeline_mode=once),
            pl.BlockSpec((dff, d), lambda b, i: (0, 0), pipeline_mode=once),
        ],
        out_specs=pl.BlockSpec((1, tm, d), lambda b, i: (b, i, 0)),
        out_shape=jax.ShapeDtypeStruct(x.shape, F32),
        compiler_params=_params(("parallel", "parallel")),
        name="ffn_sublayer",
    )(x, mod, g_pre.reshape(1, d), g_post.reshape(1, d), w1, w3, w2)


def _rope(x, cos, sin_a, sin_b, quarter):
    w = x.shape[-1]
    return x * cos + pltpu.roll(x, w - quarter, 1) * sin_a + pltpu.roll(x, quarter, 1) * sin_b


def _inproj_kernel(x_ref, mod_ref, gpre_ref, w_ref, rb_ref, rd_ref, qn_ref, wuq_ref, kvn_ref, wuk_ref, wuv_ref,
                   lb_ref, gqn_ref, gkn_ref,
                   u_ref, u2_ref, bq_ref, bk_ref, bv_ref, hq_ref, hv_ref, hk_ref, hl_ref, hg_ref,
                   dq_ref, dk_ref, dv_ref, h_scr, u_scr, *, tm):
    x = x_ref[0]
    shift = mod_ref[0, pl.ds(3, 1), :]
    scale = mod_ref[0, pl.ds(4, 1), :]
    h_scr[...] = (_rms(x, gpre_ref[...]) * (1.0 + scale) + shift).astype(BF16)

    def proj(lo, width):
        return _dot(h_scr[...], w_ref[:, lo:lo + width])

    cos_d, sa_d, sb_d = rd_ref[0], rd_ref[1], rd_ref[2]
    d_scale = D_HEAD ** -0.5 * LOG2E
    gq = proj(SLOT_GQ, D_HEADS * LANES)
    for hd in range(D_HEADS):
        qh = _rms(gq[:, hd * LANES:(hd + 1) * LANES], gqn_ref[...], n=D_HEAD)
        qh = _rope(qh, cos_d, sa_d, sb_d, D_HEAD // 4) * d_scale
        dq_ref[0, hd] = qh.T[:D_HEAD].astype(BF16)
    gkv = proj(SLOT_GK, 2 * D_KV_HEADS * LANES)
    for hd in range(D_KV_HEADS):
        kh = _rms(gkv[:, hd * LANES:(hd + 1) * LANES], gkn_ref[...], n=D_HEAD)
        kh = _rope(kh, cos_d, sa_d, sb_d, D_HEAD // 4)
        dk_ref[0, hd] = kh[:, :D_HEAD].astype(BF16)
        dv_ref[0, hd] = gkv[:, (D_KV_HEADS + hd) * LANES:(D_KV_HEADS + hd + 1) * LANES].T[:D_HEAD].astype(BF16)

    cos_b, sa_b, sb_b = rb_ref[0], rb_ref[1], rb_ref[2]
    cq = _rms(proj(SLOT_CQ, 256), qn_ref[...], n=B_Q_LORA).astype(BF16)
    q = _dot(cq, wuq_ref[...])
    b_scale = (B_NOPE + B_ROPE) ** -0.5 * LOG2E
    for hd in range(B_HEADS):
        qh = _rope(q[:, hd * LANES:(hd + 1) * LANES], cos_b, sa_b, sb_b, B_ROPE // 4)
        bq_ref[0, hd] = (qh * b_scale).T.astype(BF16)
    ckr = proj(SLOT_CKV, B_KV_LORA + LANES)
    ckv = _rms(ckr[:, :B_KV_LORA], kvn_ref[...]).astype(BF16)
    kn = _dot(ckv, wuk_ref[...])
    vn = _dot(ckv, wuv_ref[...])
    kr = _rope(ckr[:, B_KV_LORA:], cos_b, sa_b, sb_b, B_ROPE // 4)
    for hd in range(B_HEADS):
        bk_ref[0, hd] = (kn[:, hd * LANES:(hd + 1) * LANES] + kr).astype(BF16)
        bv_ref[0, hd] = vn[:, hd * LANES:(hd + 1) * LANES].T[:B_V].astype(BF16)

    hq_ref[0] = proj(SLOT_HQ, HG_W).astype(BF16)
    hv_ref[0] = proj(SLOT_HV, HG_W).astype(BF16)
    hg_ref[0] = proj(SLOT_HG, HG_W).astype(BF16)
    for dr, slot in enumerate((SLOT_HF, SLOT_HB)):
        z = proj(slot, HG_W)
        lb = lb_ref[pl.ds(dr, 1), :]
        f = lb + (1.0 - lb) * jax.nn.sigmoid(z)
        hl_ref[dr, 0] = jnp.log(jnp.maximum(f, F_FLOOR))
        hk_ref[dr, 0] = ((1.0 - lb) * jax.nn.sigmoid(-z)).astype(BF16)

    u = proj(SLOT_S5, A_WIDTH)
    u_ref[0] = u.astype(BF16)
    for hf in range(A_WIDTH // LANES):
        u_scr[hf] = u[:, hf * LANES:(hf + 1) * LANES]
    for r in range(S5_R):
        for hf in range(A_WIDTH // LANES):
            lo = r * A_WIDTH + hf * LANES
            u2_ref[0, :, lo:lo + LANES] = u_scr[hf, pl.ds(r, tm // S5_R, stride=S5_R), :].astype(BF16)


def _inproj(x, mod, g_pre, wp, rope_b, rope_d, qn, wuq, kvn, wuk, wuv, lb, gqn, gkn):
    nb, s, d = x.shape
    tm = _pick(s, (512, 256))
    const2 = lambda b, i: (0, 0)
    const3 = lambda b, i: (0, 0, 0)
    tok = lambda w: pl.BlockSpec((1, tm, w), lambda b, i: (b, i, 0))
    headed = lambda nh, w: pl.BlockSpec((1, nh, tm, w), lambda b, i: (b, 0, i, 0))
    headed_t = lambda nh, w: pl.BlockSpec((1, nh, w, tm), lambda b, i: (b, 0, 0, i))
    dirtok = lambda w: pl.BlockSpec((2, 1, tm, w), lambda b, i: (0, b, i, 0))
    sd = jax.ShapeDtypeStruct
    outs = [
        (sd((nb, s, A_WIDTH), BF16), tok(A_WIDTH)),
        (sd((nb, s // S5_R, S5_LANES), BF16), pl.BlockSpec((1, tm // S5_R, S5_LANES), lambda b, i: (b, i, 0))),
        (sd((nb, B_HEADS, LANES, s), BF16), headed_t(B_HEADS, LANES)),
        (sd((nb, B_HEADS, s, LANES), BF16), headed(B_HEADS, LANES)),
        (sd((nb, B_HEADS, B_V, s), BF16), headed_t(B_HEADS, B_V)),
        (sd((nb, s, HG_W), BF16), tok(HG_W)),
        (sd((nb, s, HG_W), BF16), tok(HG_W)),
        (sd((2, nb, s, HG_W), BF16), dirtok(HG_W)),
        (sd((2, nb, s, HG_W), F32), dirtok(HG_W)),
        (sd((nb, s, HG_W), BF16), tok(HG_W)),
        (sd((nb, D_HEADS, D_HEAD, s), BF16), headed_t(D_HEADS, D_HEAD)),
        (sd((nb, D_KV_HEADS, s, D_HEAD), BF16), headed(D_KV_HEADS, D_HEAD)),
        (sd((nb, D_KV_HEADS, D_HEAD, s), BF16), headed_t(D_KV_HEADS, D_HEAD)),
    ]
    return pl.pallas_call(
        functools.partial(_inproj_kernel, tm=tm),
        grid=(nb, s // tm),
        in_specs=[
            pl.BlockSpec((1, tm, d), lambda b, i: (b, i, 0)),
            pl.BlockSpec((1, N_MOD, d), lambda b, i: (b, 0, 0)),
            pl.BlockSpec((1, d), const2),
            pl.BlockSpec((d, N_PROJ), const2),
            pl.BlockSpec((3, tm, LANES), lambda b, i: (0, i, 0)),
            pl.BlockSpec((3, tm, LANES), lambda b, i: (0, i, 0)),
            pl.BlockSpec((1, 256), const2),
            pl.BlockSpec((256, B_HEADS * LANES), const2),
            pl.BlockSpec((1, B_KV_LORA), const2),
            pl.BlockSpec((B_KV_LORA, B_HEADS * LANES), const2),
            pl.BlockSpec((B_KV_LORA, B_HEADS * LANES), const2),
            pl.BlockSpec((2, HG_W), const2),
            pl.BlockSpec((1, LANES), const2),
            pl.BlockSpec((1, LANES), const2),
        ],
        out_specs=[o[1] for o in outs],
        out_shape=[o[0] for o in outs],
        scratch_shapes=[pltpu.VMEM((tm, d), BF16), pltpu.VMEM((A_WIDTH // LANES, tm, LANES), F32)],
        compiler_params=_params(("parallel", "parallel")),
        name="mixer_inproj",
    )(x, mod, g_pre.reshape(1, d), wp, rope_b, rope_d, qn, wuq, kvn, wuk, wuv, lb, gqn, gkn)


def _s5_kernel(u2_ref, x0_ref, lag_ref, wst_ref, wout_ref, pq_ref, y_ref, xf_ref, carry, *, rows, reverse, nlev):
    i = pl.program_id(1)

    @pl.when(i == 0)
    def _():
        carry[...] = x0_ref[0]

    u2 = u2_ref[0]
    sloc = _dot(u2, wst_ref[...])
    ridx = lax.broadcasted_iota(jnp.int32, (rows, 1), 0)

    def cmul(xv, lev):
        p = pq_ref[pl.ds(2 * lev, 1), :]
        q = pq_ref[pl.ds(2 * lev + 1, 1), :]
        return xv * p + pltpu.roll(xv, S5_NSTATE, 1) * q

    if reverse:
        e = jnp.where(ridx == rows - 1, carry[...], pltpu.roll(sloc, rows - 1, 0))
    else:
        e = jnp.where(ridx == 0, carry[...], pltpu.roll(sloc, 1, 0))
    xin = e
    for lev in range(nlev):
        dist = 1 << lev
        if reverse:
            sh = jnp.where(ridx < rows - dist, pltpu.roll(xin, rows - dist, 0), 0.0)
        else:
            sh = jnp.where(ridx >= dist, pltpu.roll(xin, dist, 0), 0.0)
        xin = xin + cmul(sh, lev)

    last = 0 if reverse else rows - 1
    nxt = cmul(xin[last:last + 1, :], 0) + sloc[last:last + 1, :]
    carry[...] = nxt
    xf_ref[0] = nxt

    xb = xin.astype(BF16)
    for b in range(S5_R):
        yb = _dot(xb, wout_ref[b])
        for a in (range(b, S5_R) if reverse else range(b + 1)):
            yb = yb + _dot(u2[:, a * A_WIDTH:(a + 1) * A_WIDTH], lag_ref[abs(b - a)])
        for hf in range(A_WIDTH // LANES):
            y_ref[0, hf, pl.ds(b, rows, stride=S5_R), :] = yb[:, hf * LANES:(hf + 1) * LANES]


def _s5_scan(u2, x0, wts, reverse):
    lag, wst, wout, pq = wts
    nb, n2, _ = u2.shape
    rows = _pick(n2, (S5_MAX_ROWS, 256, 128, 64, 32))
    nt = n2 // rows
    once = pl.Buffered(1)
    nlev = max(1, (rows - 1).bit_length())
    order = (lambda b, i: (b, nt - 1 - i, 0)) if reverse else (lambda b, i: (b, i, 0))
    order_out = (lambda b, i: (b, 0, nt - 1 - i, 0)) if reverse else (lambda b, i: (b, 0, i, 0))
    const2 = lambda b, i: (0, 0)
    y, xf = pl.pallas_call(
        functools.partial(_s5_kernel, rows=rows, reverse=reverse, nlev=nlev),
        grid=(nb, nt),
        in_specs=[
            pl.BlockSpec((1, rows, S5_LANES), order),
            pl.BlockSpec((1, 1, 2 * S5_NSTATE), lambda b, i: (b, 0, 0)),
            pl.BlockSpec((S5_R, A_WIDTH, A_WIDTH), lambda b, i: (0, 0, 0), pipeline_mode=once),
            pl.BlockSpec((S5_LANES, 2 * S5_NSTATE), const2, pipeline_mode=once),
            pl.BlockSpec((S5_R, 2 * S5_NSTATE, A_WIDTH), lambda b, i: (0, 0, 0), pipeline_mode=once),
            pl.BlockSpec(pq.shape, const2),
        ],
        out_specs=[
            pl.BlockSpec((1, A_WIDTH // LANES, rows * S5_R, LANES), order_out),
            pl.BlockSpec((1, 1, 2 * S5_NSTATE), lambda b, i: (b, 0, 0)),
        ],
        out_shape=[
            jax.ShapeDtypeStruct((nb, A_WIDTH // LANES, n2 * S5_R, LANES), F32),
            jax.ShapeDtypeStruct((nb, 1, 2 * S5_NSTATE), F32),
        ],
        scratch_shapes=[pltpu.VMEM((1, 2 * S5_NSTATE), F32)],
        compiler_params=_params(("parallel", "arbitrary")),
        name="s5_scan_rev" if reverse else "s5_scan_fwd",
    )(u2, x0, lag, wst, wout, pq)
    return y, xf


def _s5_weights(lam_re, lam_im, log_dt, b_re, b_im, c_re, c_im, nlev):
    g, n, r = A_GROUPS, A_STATE, S5_R
    nl = lam_re.shape[0]
    lam_re = jnp.minimum(lam_re.astype(F32), -1e-4)
    lam_im = lam_im.astype(F32)
    dt = jnp.exp(log_dt.astype(F32))[..., None]
    mag = jnp.exp(lam_re * dt)
    a_re = mag * jnp.cos(lam_im * dt)
    a_im = mag * jnp.sin(lam_im * dt)
    den = lam_re * lam_re + lam_im * lam_im
    num_re = a_re - 1.0
    f_re = (num_re * lam_re + a_im * lam_im) / den
    f_im = (a_im * lam_re - num_re * lam_im) / den
    bb_re = f_re[..., None] * b_re - f_im[..., None] * b_im
    bb_im = f_re[..., None] * b_im + f_im[..., None] * b_re

    def cm(xr, xi, yr, yi):
        return xr * yr - xi * yi, xr * yi + xi * yr

    pr, pi = [jnp.ones_like(a_re)], [jnp.zeros_like(a_im)]
    for _ in range(r):
        nr, ni = cm(pr[-1], pi[-1], a_re, a_im)
        pr.append(nr)
        pi.append(ni)
    pr, pi = jnp.stack(pr), jnp.stack(pi)

    xr, xi = cm(pr[..., None], pi[..., None], bb_re, bb_im)
    yr, yi = cm(c_re, c_im, pr[:, :, :, :, None, :], pi[:, :, :, :, None, :])
    lag = jnp.einsum('ldgon,tldgnc->tldgoc', c_re, xr) - jnp.einsum('ldgon,tldgnc->tldgoc', c_im, xi)

    eye_g = jnp.eye(g, dtype=F32)
    lag_bd = jnp.einsum('tldgoc,gh->ldtgcho', lag[:r], eye_g).reshape(nl, 2, r, A_WIDTH, A_WIDTH).astype(BF16)
    st_bd = jnp.concatenate([
        jnp.einsum('tldgnc,gh->tldgchn', xr, eye_g).reshape(r + 1, nl, 2, A_WIDTH, g * n),
        jnp.einsum('tldgnc,gh->tldgchn', xi, eye_g).reshape(r + 1, nl, 2, A_WIDTH, g * n)], axis=-1).astype(BF16)
    out_bd = jnp.concatenate([
        jnp.einsum('tldgon,gh->tldgnho', yr, eye_g).reshape(r + 1, nl, 2, g * n, A_WIDTH),
        jnp.einsum('tldgon,gh->tldgnho', -yi, eye_g).reshape(r + 1, nl, 2, g * n, A_WIDTH)], axis=-2).astype(BF16)

    tau_st = (r - 1 - np.arange(r), np.arange(r))
    tau_out = (np.arange(r) + 1, r - np.arange(r))

    def per_dir(fn):
        return jnp.stack([fn(0), fn(1)], axis=1)

    wst = per_dir(lambda dr: jnp.transpose(st_bd[tau_st[dr], :, dr], (1, 0, 2, 3))
                  .reshape(nl, r * A_WIDTH, 2 * g * n))
    wout = per_dir(lambda dr: jnp.transpose(out_bd[tau_out[dr], :, dr], (1, 0, 2, 3)))

    lr, li = pr[r].reshape(nl, 2, 1, g * n), pi[r].reshape(nl, 2, 1, g * n)
    rows = []
    for _ in range(nlev):
        rows.append(jnp.concatenate([lr, lr], axis=-1))
        rows.append(jnp.concatenate([-li, li], axis=-1))
        lr, li = cm(lr, li, lr, li)
    pq = jnp.concatenate(rows, axis=-2)
    return lag_bd, wst, wout, pq


def _split3(x):
    x1 = x.astype(BF16)
    r1 = x - x1.astype(F32)
    x2 = r1.astype(BF16)
    x3 = (r1 - x2.astype(F32)).astype(BF16)
    return x1, x2, x3


def _hgrn_chunks(chains, tri_ref, sel_ref, sgn_ref, msk_ref, heads, bd, nlev):
    t = chains[0][0].shape[0]

    def head_scores(qx, kx):
        qh = jnp.concatenate([jnp.where(heads[hd], qx, jnp.zeros_like(qx)) for hd in range(C_HEADS)], axis=0)
        s = _dot_nt(qh, kx)
        return [s[hd * t:(hd + 1) * t, :] for hd in range(C_HEADS)]

    cums, tots = [], []
    for q, k, vb, lf, st, dr in chains:
        l1, l2, l3 = _split3(lf)
        tri = tri_ref[dr]
        cums.append(_dot(tri, l1) + _dot(tri, l2) + _dot(tri, l3))
        tots.append(jnp.sum(lf, axis=0, keepdims=True))
    mids = [_dot(sel_ref[c[5]], cum.astype(BF16)) for c, cum in zip(chains, cums)]

    atts = []
    for q, k, vb, lf, st, dr in chains:
        diag = msk_ref[dr, nlev] > 0.5
        atts.append([jnp.where(diag, s, 0.0) for s in head_scores(q.astype(BF16), k.astype(BF16))])
    for lev in range(nlev):
        for ci, (q, k, vb, lf, st, dr) in enumerate(chains):
            e = jnp.exp(sgn_ref[dr, lev] * (cums[ci] - mids[ci][lev * t:(lev + 1) * t, :]))
            m = msk_ref[dr, lev] > 0.5
            for hd, s in enumerate(head_scores((q * e).astype(BF16), (k * e).astype(BF16))):
                atts[ci][hd] = atts[ci][hd] + jnp.where(m, s, 0.0)

    outs = []
    for ci, (q, k, vb, lf, st, dr) in enumerate(chains):
        o = _dot_nt((q * jnp.exp(cums[ci])).astype(BF16), st.astype(BF16))
        for hd in range(C_HEADS):
            o = o + _dot(atts[ci][hd].astype(BF16), jnp.where(heads[hd], vb, jnp.zeros_like(vb)))
        kend = (k * jnp.exp(tots[ci] - cums[ci])).astype(BF16)
        vt = vb.astype(F32).T.astype(BF16)
        new = st * jnp.exp(tots[ci]) + jnp.where(bd, _dot(vt, kend), 0.0)
        outs.append((o, new))
    return outs


def _hgrn_kernel(qf_ref, qr_ref, kf_ref, kr_ref, vf_ref, vr_ref, lff_ref, lfr_ref, s0_ref,
                 tri_ref, sel_ref, sgn_ref, msk_ref, hm_ref, bd_ref,
                 of_ref, or_ref, sf_ref, st_scr, *, nlev, nb):
    c = pl.program_id(0)

    @pl.when(c == 0)
    def _():
        st_scr[...] = s0_ref[...]

    heads = [hm_ref[pl.ds(hd, 1), :] > 0.5 for hd in range(C_HEADS)]
    bd = bd_ref[...] > 0.5
    dirs = ((qf_ref, kf_ref, vf_ref, lff_ref, of_ref), (qr_ref, kr_ref, vr_ref, lfr_ref, or_ref))
    chains, sinks = [], []
    for dr, (q_ref, k_ref, v_ref, lf_ref, o_ref) in enumerate(dirs):
        for b in range(nb):
            chains.append((q_ref[b].astype(F32), k_ref[0, b].astype(F32), v_ref[b], lf_ref[0, b], st_scr[dr, b], dr))
            sinks.append((o_ref, dr, b))
    outs = _hgrn_chunks(chains, tri_ref, sel_ref, sgn_ref, msk_ref, heads, bd, nlev)
    for (o_ref, dr, b), (o, new) in zip(sinks, outs):
        o_ref[b] = o
        st_scr[dr, b] = new
        sf_ref[dr, b] = new


def _hgrn_consts():
    t = HG_T
    ti = jnp.arange(t)[:, None]
    si = jnp.arange(t)[None, :]
    tri, sel, sgn, msk = [], [], [], []
    for reverse in (False, True):
        tri.append((si >= ti) if reverse else (si <= ti))
        sels, sgns, msks = [], [], []
        for lev in range(HG_LEVELS):
            h = 1 << lev
            blk_t, blk_s = ti // (2 * h), si // (2 * h)
            hi_t, hi_s = (ti % (2 * h)) >= h, (si % (2 * h)) >= h
            if reverse:
                mid = blk_t * 2 * h + h
                q_role_t, k_role_s = ~hi_t, hi_s
            else:
                mid = blk_t * 2 * h + h - 1
                q_role_t, k_role_s = hi_t, ~hi_s
            sels.append(si == mid)
            sgns.append(jnp.where(q_role_t, 1.0, -1.0))
            msks.append((blk_t == blk_s) & q_role_t & k_role_s)
        msks.append(ti == si)
        sel.append(jnp.concatenate(sels, axis=0))
        sgn.append(jnp.stack(sgns))
        msk.append(jnp.stack(msks))
    lane_head = jnp.arange(HG_W) // C_DK
    hm = (lane_head[None, :] == jnp.arange(C_HEADS)[:, None]).astype(F32)
    bd = (lane_head[:, None] == lane_head[None, :]).astype(F32)
    return (jnp.stack(tri).astype(BF16), jnp.stack(sel).astype(BF16), jnp.stack(sgn).astype(F32),
            jnp.stack(msk).astype(F32), hm, bd)


def _hgrn_scan(q, k, v, lf, s0, consts):
    tri, sel, sgn, msk, hm, bd = consts
    nb, s, w = q.shape
    t = HG_T
    nc = s // t
    fwd3 = pl.BlockSpec((nb, t, w), lambda c: (0, c, 0))
    rev3 = pl.BlockSpec((nb, t, w), lambda c: (0, nc - 1 - c, 0))
    fwd4 = pl.BlockSpec((1, nb, t, w), lambda c: (0, 0, c, 0))
    rev4 = pl.BlockSpec((1, nb, t, w), lambda c: (1, 0, nc - 1 - c, 0))
    whole = lambda a: pl.BlockSpec(a.shape, lambda c: (0,) * a.ndim)
    sgn = sgn.reshape(2, HG_LEVELS, t, 1)
    o_f, o_r, sf = pl.pallas_call(
        functools.partial(_hgrn_kernel, nlev=HG_LEVELS, nb=nb),
        grid=(nc,),
        in_specs=[fwd3, rev3, fwd4, rev4, fwd3, rev3, fwd4, rev4, whole(s0),
                  whole(tri), whole(sel), whole(sgn), whole(msk), whole(hm), whole(bd)],
        out_specs=[fwd3, rev3, whole(s0)],
        out_shape=[
            jax.ShapeDtypeStruct((nb, s, w), F32),
            jax.ShapeDtypeStruct((nb, s, w), F32),
            jax.ShapeDtypeStruct((2, nb, w, w), F32),
        ],
        scratch_shapes=[pltpu.VMEM((2, nb, w, w), F32)],
        compiler_params=_params(("arbitrary",)),
        name="hgrn_scan",
    )(q, q, k, k, v, v, lf, lf, s0, tri, sel, sgn, msk, hm, bd)
    return (o_f, o_r), sf


def _attn_kernel(*refs, nseg, tks, g, tq):
    q_ref = refs[0]
    kv_refs = refs[1:1 + 2 * nseg]
    o_ref = refs[1 + 2 * nseg]
    dv = o_ref.shape[-1]
    n = g * tq
    qt = jnp.concatenate([q_ref[0, hd] for hd in range(g)], axis=-1)

    def scores(k):
        s = _dot(k, qt)
        sub = min(ATTN_SUB, s.shape[0])
        parts = [s[i:i + sub] for i in range(0, s.shape[0], sub)]
        return tuple(x for p in parts for x in (p, jnp.max(p, axis=0, keepdims=True)))

    def absorb(sm, vt, carry):
        m, acc = carry
        sub = sm[0].shape[0]
        for i in range(len(sm) // 2):
            s, smax = sm[2 * i], sm[2 * i + 1]
            m_new = jnp.maximum(m, smax)
            alpha = jnp.exp2(m - m_new)
            p = jnp.exp2(s - m_new).astype(BF16)
            vt1 = jnp.concatenate([vt[:, i * sub:(i + 1) * sub], jnp.ones((ATTN_ONES, sub), BF16)], axis=0)
            acc = alpha * acc + _dot(vt1, p)
            m = m_new
        return m, acc

    carry = (jnp.full((1, n), -jnp.inf, F32), jnp.zeros((dv + ATTN_ONES, n), F32))
    pending = None
    for seg in range(nseg):
        k_ref, vt_ref = kv_refs[2 * seg], kv_refs[2 * seg + 1]
        tk = tks[seg]
        nk = k_ref.shape[2] // tk
        s_first = scores(k_ref[0, 0, 0:tk, :])
        if pending is not None:
            carry = absorb(*pending, carry)
        if nk == 1:
            pending = (s_first, vt_ref[0, 0])
            continue

        def body(j, c, k_ref=k_ref, vt_ref=vt_ref, tk=tk, nk=nk):
            sm_cur, (m, acc) = c[:-2], c[-2:]
            off_next = pl.multiple_of(jnp.minimum(j + 1, nk - 1) * tk, tk)
            sm_next = scores(k_ref[0, 0, pl.ds(off_next, tk), :])
            off = pl.multiple_of(j * tk, tk)
            m, acc = absorb(sm_cur, vt_ref[0, 0, :, pl.ds(off, tk)], (m, acc))
            return sm_next + (m, acc)

        unroll = ATTN_UNROLL if nk % ATTN_UNROLL == 0 else 1
        m, acc = lax.fori_loop(0, nk, body, s_first + carry, unroll=unroll)[-2:]
        carry = (m, acc)
        pending = None
    m, acc = carry if pending is None else absorb(*pending, carry)
    out = acc[:dv] / acc[dv:dv + 1, :]
    out = jnp.concatenate([out, jnp.zeros((LANES - dv, n), F32)], axis=0).T
    o_ref[0] = out[:, :dv].reshape(g, tq, dv).astype(o_ref.dtype)


def _attention(q, kvs):
    nb, hq, dk, sq = q.shape
    hkv = kvs[0][0].shape[1]
    dv = kvs[0][1].shape[2]
    g = hq // hkv
    tq = _pick(sq, tuple(n // g for n in ATTN_QUERIES))
    tks = tuple(_pick(k.shape[2], (ATTN_SUB, 128)) for k, _ in kvs)
    in_specs = [pl.BlockSpec((1, g, dk, tq), lambda b, h, i: (b, h, 0, i))]
    args = [q]
    for k, v in kvs:
        sk = k.shape[2]
        in_specs.append(pl.BlockSpec((1, 1, sk, dk), lambda b, h, i: (b, h, 0, 0)))
        in_specs.append(pl.BlockSpec((1, 1, dv, sk), lambda b, h, i: (b, h, 0, 0)))
        args += [k, v]
    return pl.pallas_call(
        functools.partial(_attn_kernel, nseg=len(kvs), tks=tks, g=g, tq=tq),
        grid=(nb, hkv, sq // tq),
        in_specs=in_specs,
        out_specs=pl.BlockSpec((1, g, tq, dv), lambda b, h, i: (b, h, i, 0)),
        out_shape=jax.ShapeDtypeStruct((nb, hq, sq, dv), BF16),
        compiler_params=_params(("parallel", "parallel", "arbitrary")),
        name="attention",
    )(*args)


def _merge_kernel(x_ref, mod_ref, gpre_ref, gpost_ref, wg_ref, wb_ref, wo_ref,
                  u_ref, yf_ref, yr_ref, sd_ref, wglu_ref, bo_ref, hof_ref, hor_ref, hg_ref, hn_ref, avg_ref, do_ref,
                  o_ref):
    x = x_ref[0]
    shift = mod_ref[0, pl.ds(3, 1), :]
    scale = mod_ref[0, pl.ds(4, 1), :]
    gate = mod_ref[0, pl.ds(5, 1), :]
    h = (_rms(x, gpre_ref[...]) * (1.0 + scale) + shift).astype(BF16)
    d = x.shape[-1]

    def branch_gate(i):
        return jax.nn.sigmoid(_dot(h, wg_ref[:, i * d:(i + 1) * d]))

    ysum = yf_ref[0] + yr_ref[0]
    y = sd_ref[...] * u_ref[0].astype(F32) + jnp.concatenate([ysum[hf] for hf in range(A_WIDTH // LANES)], axis=-1)
    ge = jax.nn.gelu(y)
    ya = ge * jax.nn.sigmoid(_dot(ge.astype(BF16), wglu_ref[...]))
    merged = branch_gate(0) * _dot(ya.astype(BF16), wb_ref[0])

    yb = jnp.concatenate([bo_ref[0, hd] for hd in range(B_HEADS)], axis=-1)
    merged = merged + branch_gate(1) * _dot(yb, wb_ref[1])

    o2 = hof_ref[0] + hor_ref[0]
    ms = _dot((o2 * o2).astype(BF16), avg_ref[...])
    gz = hg_ref[0].astype(F32)
    yc = o2 * lax.rsqrt(ms + EPS) * hn_ref[...] * (gz * jax.nn.sigmoid(gz))
    merged = merged + branch_gate(2) * _dot(yc.astype(BF16), wb_ref[2])

    yd = jnp.concatenate([do_ref[0, hd] for hd in range(D_HEADS)], axis=-1)
    merged = merged + branch_gate(3) * _dot(yd, wb_ref[3])

    yo = _dot(merged.astype(BF16), wo_ref[...])
    o_ref[0] = x + gate * _rms(yo, gpost_ref[...])


def _merge(x, mod, g_pre, g_post, wg, wb, wo, u, yf, yr, s5d, wglu, bo, ho, hg, hn, avg, do):
    nb, s, d = x.shape
    tm = _pick(s, (512, 256))
    const2 = lambda b, i: (0, 0)
    const3 = lambda b, i: (0, 0, 0)
    tok = lambda w: pl.BlockSpec((1, tm, w), lambda b, i: (b, i, 0))
    halves = pl.BlockSpec((1, A_WIDTH // LANES, tm, LANES), lambda b, i: (b, 0, i, 0))
    return pl.pallas_call(
        _merge_kernel,
        grid=(nb, s // tm),
        in_specs=[
            tok(d),
            pl.BlockSpec((1, N_MOD, d), lambda b, i: (b, 0, 0)),
            pl.BlockSpec((1, d), const2),
            pl.BlockSpec((1, d), const2),
            pl.BlockSpec((d, N_BRANCH * d), const2),
            pl.BlockSpec((N_BRANCH, BRANCH_W, d), const3),
            pl.BlockSpec((d, d), const2),
            tok(A_WIDTH), halves, halves,
            pl.BlockSpec((1, A_WIDTH), const2),
            pl.BlockSpec((A_WIDTH, A_WIDTH), const2),
            pl.BlockSpec((1, B_HEADS, tm, B_V), lambda b, i: (b, 0, i, 0)),
            tok(HG_W), tok(HG_W),
            tok(HG_W),
            pl.BlockSpec((1, HG_W), const2),
            pl.BlockSpec((HG_W, HG_W), const2),
            pl.BlockSpec((1, D_HEADS, tm, D_HEAD), lambda b, i: (b, 0, i, 0)),
        ],
        out_specs=tok(d),
        out_shape=jax.ShapeDtypeStruct(x.shape, F32),
        compiler_params=_params(("parallel", "parallel")),
        name="merge_out",
    )(x, mod, g_pre.reshape(1, d), g_post.reshape(1, d), wg, wb, wo, u, yf, yr, s5d, wglu, bo, ho[0], ho[1], hg, hn,
      avg, do)


def _pad_cols(w, width):
    return jnp.pad(w, ((0, 0), (0, width - w.shape[1])))


def _proj_weight(w_in_mix):
    offs = [0]
    for wdt in (A_WIDTH, B_Q_LORA, B_KV_LORA, B_ROPE, HG_W, HG_W, HG_W, HG_W, HG_W,
                D_HEADS * D_HEAD, D_KV_HEADS * D_HEAD, D_KV_HEADS * D_HEAD):
        offs.append(offs[-1] + wdt)
    p = [w_in_mix[:, offs[i]:offs[i + 1]] for i in range(12)]
    d = w_in_mix.shape[0]
    z = lambda n: jnp.zeros((d, n), w_in_mix.dtype)
    cols = [p[0], _pad_cols(p[1], 256), p[2],
            jnp.concatenate([z(B_NOPE), p[3], z(LANES - B_NOPE - B_ROPE)], axis=1),
            p[4], p[5], p[6], p[7], p[8]]
    for i, nh in ((9, D_HEADS), (10, D_KV_HEADS), (11, D_KV_HEADS)):
        for hd in range(nh):
            cols.append(_pad_cols(p[i][:, hd * D_HEAD:(hd + 1) * D_HEAD], LANES))
    return jnp.concatenate(cols, axis=1).astype(BF16)


def _rope_tables(n_tok, rot_dim, lane_off, identity):
    cos = jnp.ones((n_tok, LANES), F32)
    sin_a = jnp.zeros((n_tok, LANES), F32)
    sin_b = jnp.zeros((n_tok, LANES), F32)
    if not identity:
        n_rows = n_tok // GRID_W
        rows = jnp.repeat(jnp.arange(n_rows, dtype=F32), GRID_W)
        cols = jnp.tile(jnp.arange(GRID_W, dtype=F32), n_rows)
        half = rot_dim // 2
        inv = ROPE_THETA ** (-jnp.arange(0, half, 2, dtype=F32) / half)
        ang_r = rows[:, None] * inv
        ang_c = cols[:, None] * inv
        ang = jnp.concatenate([ang_r, ang_r, ang_c, ang_c], axis=-1)
        c, s = jnp.cos(ang), jnp.sin(ang)
        quarter = rot_dim // 4
        first = (np.arange(rot_dim) % (2 * quarter)) < quarter

        def place(a, fill):
            left = jnp.full((n_tok, lane_off), fill, F32)
            right = jnp.full((n_tok, LANES - lane_off - rot_dim), fill, F32)
            return jnp.concatenate([left, a, right], axis=1)

        cos = place(c, 1.0)
        sin_a = place(jnp.where(first, -s, 0.0), 0.0)
        sin_b = place(jnp.where(first, 0.0, s), 0.0)
    return jnp.stack([cos, sin_a, sin_b])


def kernel(x, c, ctx, c_ctx, w_ada, b_ada, norm_pre, norm_post, ffn_w1, ffn_w3, ffn_w2, w_in,
           s5_lambda_re, s5_lambda_im, s5_log_dt, s5_b_re, s5_b_im, s5_c_re, s5_c_im, s5_d, s5_w_glu,
           mla_q_norm, mla_w_uq, mla_kv_norm, mla_w_ukv, hgrn_lb_raw, hgrn_o_norm,
           gqa_q_norm, gqa_k_norm, w_branch, w_out):
    nb, seq, d = x.shape
    n_ctx = ctx.shape[1]
    depth = w_ada.shape[0]

    rows = max(8, -(-(nb + 1) // 8) * 8)
    cvec = jnp.zeros((rows, d), F32).at[:nb].set(c).at[nb].set(c_ctx)
    mod_all = _modulation(cvec, w_ada, b_ada).reshape(depth, rows, N_MOD, d)

    lb_step = jax.nn.softmax(hgrn_lb_raw.astype(F32), axis=1)
    lb_all = jnp.clip(jnp.cumsum(lb_step, axis=1) - lb_step[:, :1], 0.0, 1.0)

    rope_b_lat = _rope_tables(seq, B_ROPE, B_NOPE, False)
    rope_d_lat = _rope_tables(seq, D_HEAD, 0, False)
    rope_b_ctx = _rope_tables(n_ctx, B_ROPE, B_NOPE, True)
    rope_d_ctx = _rope_tables(n_ctx, D_HEAD, 0, True)
    hg_consts = _hgrn_consts()
    s5_w = _s5_weights(s5_lambda_re, s5_lambda_im, s5_log_dt, s5_b_re, s5_b_im, s5_c_re, s5_c_im, S5_MAX_LEVELS)
    lane_head = jnp.arange(HG_W) // C_DV
    avg = (lane_head[:, None] == lane_head[None, :]).astype(BF16) * (1.0 / C_DV)
    n_mixcols = w_in.shape[-1] - N_BRANCH * d

    x_lat, x_ctx = x, ctx
    for layer in range(depth):
        last = layer == depth - 1
        mod_lat = mod_all[layer, :nb]
        mod_ctx = jnp.broadcast_to(mod_all[layer, nb:nb + 1], (nb, N_MOD, d))
        bf = lambda w: w.astype(BF16)
        ffn_a = (norm_pre[layer, 0], norm_post[layer, 0], bf(ffn_w1[layer, 0]), bf(ffn_w3[layer, 0]), bf(ffn_w2[layer, 0]))
        ffn_b = (norm_pre[layer, 2], norm_post[layer, 2], bf(ffn_w1[layer, 1]), bf(ffn_w3[layer, 1]), bf(ffn_w2[layer, 1]))

        x_lat = _ffn(x_lat, mod_lat, 0, *ffn_a)
        x_ctx = _ffn(x_ctx, mod_ctx, 0, *ffn_a)

        wp = _proj_weight(w_in[layer, :, :n_mixcols])
        wg = bf(w_in[layer, :, n_mixcols:])
        qn = _pad_cols(mla_q_norm[layer].reshape(1, B_Q_LORA), 256)
        wuq = mla_w_uq[layer].reshape(B_Q_LORA, B_HEADS, B_NOPE + B_ROPE)
        wuq = jnp.pad(wuq, ((0, 256 - B_Q_LORA), (0, 0), (0, LANES - B_NOPE - B_ROPE))).reshape(256, B_HEADS * LANES)
        wukv = mla_w_ukv[layer].reshape(B_KV_LORA, B_HEADS, B_NOPE + B_V)
        wuk = jnp.pad(wukv[:, :, :B_NOPE], ((0, 0), (0, 0), (0, LANES - B_NOPE))).reshape(B_KV_LORA, B_HEADS * LANES)
        wuv = jnp.pad(wukv[:, :, B_NOPE:], ((0, 0), (0, 0), (0, LANES - B_V))).reshape(B_KV_LORA, B_HEADS * LANES)
        kvn = mla_kv_norm[layer].reshape(1, B_KV_LORA)
        gqn = _pad_cols(gqa_q_norm[layer].reshape(1, D_HEAD), LANES)
        gkn = _pad_cols(gqa_k_norm[layer].reshape(1, D_HEAD), LANES)
        lb = lb_all[:, layer]
        proj_args = (norm_pre[layer, 1], wp)
        mla_args = (qn, bf(wuq), kvn, bf(wuk), bf(wuv), lb, gqn, gkn)

        pl_ = _inproj(x_lat, mod_lat, *proj_args, rope_b_lat, rope_d_lat, *mla_args)
        pc_ = _inproj(x_ctx, mod_ctx, *proj_args, rope_b_ctx, rope_d_ctx, *mla_args)
        (u_l, u2_l, bq_l, bk_l, bv_l, hq_l, hv_l, hk_l, hl_l, hg_l, dq_l, dk_l, dv_l) = pl_
        (u_c, u2_c, bq_c, bk_c, bv_c, hq_c, hv_c, hk_c, hl_c, hg_c, dq_c, dk_c, dv_c) = pc_

        ys_l, ys_c = [], []
        for dr, reverse in enumerate((False, True)):
            wts = tuple(w[layer, dr] for w in s5_w)
            x0 = jnp.zeros((nb, 1, 2 * S5_NSTATE), F32)
            y_c, x_end = _s5_scan(u2_c, x0, wts, reverse)
            y_l, _ = _s5_scan(u2_l, x_end, wts, reverse)
            ys_l.append(y_l)
            ys_c.append(y_c)

        s0 = jnp.zeros((2, nb, HG_W, HG_W), F32)
        ho_c, s_ctx = _hgrn_scan(hq_c, hk_c, hv_c, hl_c, s0, hg_consts)
        ho_l, _ = _hgrn_scan(hq_l, hk_l, hv_l, hl_l, s_ctx, hg_consts)

        bo_l = _attention(bq_l, [(bk_c, bv_c), (bk_l, bv_l)])
        do_l = _attention(dq_l, [(dk_c, dv_c), (dk_l, dv_l)])

        merge_w = (norm_pre[layer, 1], norm_post[layer, 1], wg, bf(w_branch[layer]), bf(w_out[layer]))
        s5_ro = (s5_d[layer].reshape(1, A_WIDTH), bf(s5_w_glu[layer]))
        hn = jnp.tile(hgrn_o_norm[layer], C_HEADS).reshape(1, HG_W)
        x_lat_new = _merge(x_lat, mod_lat, *merge_w, u_l, ys_l[0], ys_l[1], *s5_ro, bo_l, ho_l, hg_l, hn, avg, do_l)
        if not last:
            bo_c = _attention(bq_c, [(bk_c, bv_c)])
            do_c = _attention(dq_c, [(dk_c, dv_c)])
            x_ctx = _merge(x_ctx, mod_ctx, *merge_w, u_c, ys_c[0], ys_c[1], *s5_ro, bo_c, ho_c, hg_c, hn, avg, do_c)
            x_ctx = _ffn(x_ctx, mod_ctx, 2, *ffn_b)
        x_lat = _ffn(x_lat_new, mod_lat, 2, *ffn_b)
    return x_lat
```

```python
import functools

import jax
import jax.numpy as jnp
import numpy as np
from jax import lax
from jax.experimental import pallas as pl
from jax.experimental.pallas import tpu as pltpu

GRID_W = 64
FFN_RES_WEIGHT = 0.5
N_MOD = 9
EPS = 1e-6
ROPE_THETA = 10000.0
F_FLOOR = 1e-20

A_WIDTH = 256
A_GROUP = 16
A_GROUPS = A_WIDTH // A_GROUP
A_STATE = 64

B_HEADS = 4
B_NOPE = 64
B_ROPE = 32
B_V = 64
B_Q_LORA = 192
B_KV_LORA = 128

C_HEADS = 4
C_DK = 64
C_DV = 64

D_HEADS = 4
D_KV_HEADS = 2
D_HEAD = 64

N_BRANCH = 4
BRANCH_W = 256

LANES = 128
VMEM_LIMIT_BYTES = 56 * 1024 * 1024

S5_R = 8
S5_LANES = S5_R * A_WIDTH
S5_NSTATE = A_GROUPS * A_STATE
S5_MAX_ROWS = 512
S5_MAX_LEVELS = 9
HG_T = 128
HG_LEVELS = 7
HG_W = C_HEADS * C_DK
ATTN_ONES = 16
ATTN_QUERIES = (1024, 512, 256, 128)
ATTN_SUB = 256
ATTN_UNROLL = 16

SLOT_S5 = 0
SLOT_CQ = 256
SLOT_CKV = 512
SLOT_KR = 640
SLOT_HQ = 768
SLOT_HV = 1024
SLOT_HF = 1280
SLOT_HB = 1536
SLOT_HG = 1792
SLOT_GQ = 2048
SLOT_GK = 2560
SLOT_GV = 2816
N_PROJ = 3072

BF16 = jnp.bfloat16
F32 = jnp.float32
LOG2E = 1.4426950408889634


def _params(sem, flags=None):
    return pltpu.CompilerParams(dimension_semantics=sem, vmem_limit_bytes=VMEM_LIMIT_BYTES, flags=flags)


def _pick(n, candidates):
    for c in candidates:
        if n % c == 0:
            return c
    raise ValueError(f"no tile for {n} in {candidates}")


def _dot(a, b):
    return jnp.dot(a, b, preferred_element_type=F32)


def _dot_nt(a, b):
    return lax.dot_general(a, b, (((1,), (1,)), ((), ())), preferred_element_type=F32)


def _rms(x, g, n=None):
    n = x.shape[-1] if n is None else n
    ms = jnp.sum(x * x, axis=-1, keepdims=True) * (1.0 / n)
    return x * lax.rsqrt(ms + EPS) * g


def _mod_kernel(c_ref, w_ref, b_ref, o_ref):
    c = c_ref[...]
    a = (c * jax.nn.sigmoid(c)).astype(BF16)
    o_ref[0] = _dot(a, w_ref[0].astype(BF16)) + b_ref[0]


def _modulation(cvec, w_ada, b_ada):
    nl, d, nm = w_ada.shape
    rows = cvec.shape[0]
    tn = _pick(nm, (1152, 1024, 512, 256, 128))
    return pl.pallas_call(
        _mod_kernel,
        grid=(nl, nm // tn),
        in_specs=[
            pl.BlockSpec((rows, d), lambda l, n: (0, 0)),
            pl.BlockSpec((1, d, tn), lambda l, n: (l, 0, n)),
            pl.BlockSpec((1, 1, tn), lambda l, n: (l, 0, n)),
        ],
        out_specs=pl.BlockSpec((1, rows, tn), lambda l, n: (l, 0, n)),
        out_shape=jax.ShapeDtypeStruct((nl, rows, nm), F32),
        compiler_params=_params(("parallel", "parallel")),
        name="adaln_mod",
    )(cvec, w_ada, b_ada.reshape(nl, 1, nm))


def _ffn_kernel(x_ref, mod_ref, gpre_ref, gpost_ref, w1_ref, w3_ref, w2_ref, o_ref, *, j, tf):
    x = x_ref[0]
    shift = mod_ref[0, pl.ds(3 * j, 1), :]
    scale = mod_ref[0, pl.ds(3 * j + 1, 1), :]
    gate = mod_ref[0, pl.ds(3 * j + 2, 1), :]
    h = (_rms(x, gpre_ref[...]) * (1.0 + scale) + shift).astype(BF16)
    acc = None
    for f in range(w1_ref.shape[1] // tf):
        a = _dot(h, w1_ref[:, f * tf:(f + 1) * tf])
        b = _dot(h, w3_ref[:, f * tf:(f + 1) * tf])
        t = (a * jax.nn.sigmoid(a) * b).astype(BF16)
        part = _dot(t, w2_ref[f * tf:(f + 1) * tf, :])
        acc = part if acc is None else acc + part
    o_ref[0] = x + FFN_RES_WEIGHT * gate * _rms(acc, gpost_ref[...])


def _ffn(x, mod, j, g_pre, g_post, w1, w3, w2, layer, which):
    nb, s, d = x.shape
    dff = w1.shape[-1]
    tm = _pick(s, (512, 256))
    tf = _pick(dff, (256, 128))
    once = pl.Buffered(1)
    pick = lambda b, i: (layer, which, 0, 0)
    return pl.pallas_call(
        functools.partial(_ffn_kernel, j=j, tf=tf),
        grid=(nb, s // tm),
        in_specs=[
            pl.BlockSpec((1, tm, d), lambda b, i: (b, i, 0)),
            pl.BlockSpec((1, N_MOD, d), lambda b, i: (b, 0, 0)),
            pl.BlockSpec((1, d), lambda b, i: (0, 0)),
            pl.BlockSpec((1, d), lambda b, i: (0, 0)),
            pl.BlockSpec((None, None, d, dff), pick, pipeline_mode=once),
            pl.BlockSpec((None, None, d, dff), pick, pipeline_mode=once),
            pl.BlockSpec((None, None, dff, d), pick, pipeline_mode=once),
        ],
        out_specs=pl.BlockSpec((1, tm, d), lambda b, i: (b, i, 0)),
        out_shape=jax.ShapeDtypeStruct(x.shape, F32),
        compiler_params=_params(("parallel", "parallel")),
        name="ffn_sublayer",
    )(x, mod, g_pre.reshape(1, d), g_post.reshape(1, d), w1, w3, w2)


def _rope(x, cos, sin_a, sin_b, quarter):
    w = x.shape[-1]
    return x * cos + pltpu.roll(x, w - quarter, 1) * sin_a + pltpu.roll(x, quarter, 1) * sin_b


def _inproj_kernel(x_ref, mod_ref, gpre_ref, w_ref, rb_ref, rd_ref, qn_ref, wuq_ref, kvn_ref, wuk_ref, wuv_ref,
                   lb_ref, gqn_ref, gkn_ref,
                   u_ref, u2_ref, bq_ref, bk_ref, bv_ref, hq_ref, hv_ref, hk_ref, hl_ref, hg_ref,
                   dq_ref, dk_ref, dv_ref, h_scr, u_scr, *, tm):
    x = x_ref[0]
    shift = mod_ref[0, pl.ds(3, 1), :]
    scale = mod_ref[0, pl.ds(4, 1), :]
    h_scr[...] = (_rms(x, gpre_ref[...]) * (1.0 + scale) + shift).astype(BF16)

    def proj(lo, width):
        return _dot(h_scr[...], w_ref[:, lo:lo + width])

    cos_d, sa_d, sb_d = rd_ref[0], rd_ref[1], rd_ref[2]
    d_scale = D_HEAD ** -0.5 * LOG2E
    gq = proj(SLOT_GQ, D_HEADS * LANES)
    for hd in range(D_HEADS):
        qh = _rms(gq[:, hd * LANES:(hd + 1) * LANES], gqn_ref[...], n=D_HEAD)
        qh = _rope(qh, cos_d, sa_d, sb_d, D_HEAD // 4) * d_scale
        dq_ref[0, hd] = qh.T[:D_HEAD].astype(BF16)
    gkv = proj(SLOT_GK, 2 * D_KV_HEADS * LANES)
    for hd in range(D_KV_HEADS):
        kh = _rms(gkv[:, hd * LANES:(hd + 1) * LANES], gkn_ref[...], n=D_HEAD)
        kh = _rope(kh, cos_d, sa_d, sb_d, D_HEAD // 4)
        dk_ref[0, hd] = kh[:, :D_HEAD].astype(BF16)
        dv_ref[0, hd] = gkv[:, (D_KV_HEADS + hd) * LANES:(D_KV_HEADS + hd + 1) * LANES].T[:D_HEAD].astype(BF16)

    cos_b, sa_b, sb_b = rb_ref[0], rb_ref[1], rb_ref[2]
    cq = _rms(proj(SLOT_CQ, 256), qn_ref[...], n=B_Q_LORA).astype(BF16)
    q = _dot(cq, wuq_ref[...])
    b_scale = (B_NOPE + B_ROPE) ** -0.5 * LOG2E
    for hd in range(B_HEADS):
        qh = _rope(q[:, hd * LANES:(hd + 1) * LANES], cos_b, sa_b, sb_b, B_ROPE // 4)
        bq_ref[0, hd] = (qh * b_scale).T.astype(BF16)
    ckr = proj(SLOT_CKV, B_KV_LORA + LANES)
    ckv = _rms(ckr[:, :B_KV_LORA], kvn_ref[...]).astype(BF16)
    kn = _dot(ckv, wuk_ref[...])
    vn = _dot(ckv, wuv_ref[...])
    kr = _rope(ckr[:, B_KV_LORA:], cos_b, sa_b, sb_b, B_ROPE // 4)
    for hd in range(B_HEADS):
        bk_ref[0, hd] = (kn[:, hd * LANES:(hd + 1) * LANES] + kr).astype(BF16)
        bv_ref[0, hd] = vn[:, hd * LANES:(hd + 1) * LANES].T[:B_V].astype(BF16)

    hq_ref[0] = proj(SLOT_HQ, HG_W).astype(BF16)
    hv_ref[0] = proj(SLOT_HV, HG_W).astype(BF16)
    hg_ref[0] = proj(SLOT_HG, HG_W).astype(BF16)
    for dr, slot in enumerate((SLOT_HF, SLOT_HB)):
        z = proj(slot, HG_W)
        lb = lb_ref[pl.ds(dr, 1), :]
        f = lb + (1.0 - lb) * jax.nn.sigmoid(z)
        hl_ref[dr, 0] = jnp.log(jnp.maximum(f, F_FLOOR))
        hk_ref[dr, 0] = ((1.0 - lb) * jax.nn.sigmoid(-z)).astype(BF16)

    u = proj(SLOT_S5, A_WIDTH)
    u_ref[0] = u.astype(BF16)
    for hf in range(A_WIDTH // LANES):
        u_scr[hf] = u[:, hf * LANES:(hf + 1) * LANES]
    for r in range(S5_R):
        for hf in range(A_WIDTH // LANES):
            lo = r * A_WIDTH + hf * LANES
            u2_ref[0, :, lo:lo + LANES] = u_scr[hf, pl.ds(r, tm // S5_R, stride=S5_R), :].astype(BF16)


def _inproj(x, mod, g_pre, wp, rope_b, rope_d, qn, wuq, kvn, wuk, wuv, lb, gqn, gkn):
    nb, s, d = x.shape
    tm = _pick(s, (512, 256))
    const2 = lambda b, i: (0, 0)
    const3 = lambda b, i: (0, 0, 0)
    tok = lambda w: pl.BlockSpec((1, tm, w), lambda b, i: (b, i, 0))
    headed = lambda nh, w: pl.BlockSpec((1, nh, tm, w), lambda b, i: (b, 0, i, 0))
    headed_t = lambda nh, w: pl.BlockSpec((1, nh, w, tm), lambda b, i: (b, 0, 0, i))
    dirtok = lambda w: pl.BlockSpec((2, 1, tm, w), lambda b, i: (0, b, i, 0))
    sd = jax.ShapeDtypeStruct
    outs = [
        (sd((nb, s, A_WIDTH), BF16), tok(A_WIDTH)),
        (sd((nb, s // S5_R, S5_LANES), BF16), pl.BlockSpec((1, tm // S5_R, S5_LANES), lambda b, i: (b, i, 0))),
        (sd((nb, B_HEADS, LANES, s), BF16), headed_t(B_HEADS, LANES)),
        (sd((nb, B_HEADS, s, LANES), BF16), headed(B_HEADS, LANES)),
        (sd((nb, B_HEADS, B_V, s), BF16), headed_t(B_HEADS, B_V)),
        (sd((nb, s, HG_W), BF16), tok(HG_W)),
        (sd((nb, s, HG_W), BF16), tok(HG_W)),
        (sd((2, nb, s, HG_W), BF16), dirtok(HG_W)),
        (sd((2, nb, s, HG_W), F32), dirtok(HG_W)),
        (sd((nb, s, HG_W), BF16), tok(HG_W)),
        (sd((nb, D_HEADS, D_HEAD, s), BF16), headed_t(D_HEADS, D_HEAD)),
        (sd((nb, D_KV_HEADS, s, D_HEAD), BF16), headed(D_KV_HEADS, D_HEAD)),
        (sd((nb, D_KV_HEADS, D_HEAD, s), BF16), headed_t(D_KV_HEADS, D_HEAD)),
    ]
    return pl.pallas_call(
        functools.partial(_inproj_kernel, tm=tm),
        grid=(nb, s // tm),
        in_specs=[
            pl.BlockSpec((1, tm, d), lambda b, i: (b, i, 0)),
            pl.BlockSpec((1, N_MOD, d), lambda b, i: (b, 0, 0)),
            pl.BlockSpec((1, d), const2),
            pl.BlockSpec((d, N_PROJ), const2),
            pl.BlockSpec((3, tm, LANES), lambda b, i: (0, i, 0)),
            pl.BlockSpec((3, tm, LANES), lambda b, i: (0, i, 0)),
            pl.BlockSpec((1, 256), const2),
            pl.BlockSpec((256, B_HEADS * LANES), const2),
            pl.BlockSpec((1, B_KV_LORA), const2),
            pl.BlockSpec((B_KV_LORA, B_HEADS * LANES), const2),
            pl.BlockSpec((B_KV_LORA, B_HEADS * LANES), const2),
            pl.BlockSpec((2, HG_W), const2),
            pl.BlockSpec((1, LANES), const2),
            pl.BlockSpec((1, LANES), const2),
        ],
        out_specs=[o[1] for o in outs],
        out_shape=[o[0] for o in outs],
        scratch_shapes=[pltpu.VMEM((tm, d), BF16), pltpu.VMEM((A_WIDTH // LANES, tm, LANES), F32)],
        compiler_params=_params(("parallel", "parallel")),
        name="mixer_inproj",
    )(x, mod, g_pre.reshape(1, d), wp, rope_b, rope_d, qn, wuq, kvn, wuk, wuv, lb, gqn, gkn)


def _s5_kernel(u2_ref, x0_ref, lag_ref, wst_ref, wout_ref, pq_ref, y_ref, xf_ref, carry, *, rows, reverse, nlev):
    i = pl.program_id(1)

    @pl.when(i == 0)
    def _():
        carry[...] = x0_ref[0]

    u2 = u2_ref[0]
    sloc = _dot(u2, wst_ref[...])
    ridx = lax.broadcasted_iota(jnp.int32, (rows, 1), 0)

    def cmul(xv, lev):
        p = pq_ref[pl.ds(2 * lev, 1), :]
        q = pq_ref[pl.ds(2 * lev + 1, 1), :]
        return xv * p + pltpu.roll(xv, S5_NSTATE, 1) * q

    if reverse:
        e = jnp.where(ridx == rows - 1, carry[...], pltpu.roll(sloc, rows - 1, 0))
    else:
        e = jnp.where(ridx == 0, carry[...], pltpu.roll(sloc, 1, 0))
    xin = e
    for lev in range(nlev):
        dist = 1 << lev
        if reverse:
            sh = jnp.where(ridx < rows - dist, pltpu.roll(xin, rows - dist, 0), 0.0)
        else:
            sh = jnp.where(ridx >= dist, pltpu.roll(xin, dist, 0), 0.0)
        xin = xin + cmul(sh, lev)

    last = 0 if reverse else rows - 1
    nxt = cmul(xin[last:last + 1, :], 0) + sloc[last:last + 1, :]
    carry[...] = nxt
    xf_ref[0] = nxt

    xb = xin.astype(BF16)
    for b in range(S5_R):
        yb = _dot(xb, wout_ref[b])
        for a in (range(b, S5_R) if reverse else range(b + 1)):
            yb = yb + _dot(u2[:, a * A_WIDTH:(a + 1) * A_WIDTH], lag_ref[abs(b - a)])
        for hf in range(A_WIDTH // LANES):
            y_ref[0, hf, pl.ds(b, rows, stride=S5_R), :] = yb[:, hf * LANES:(hf + 1) * LANES]


def _s5_scan(u2, x0, wts, reverse):
    lag, wst, wout, pq = wts
    nb, n2, _ = u2.shape
    rows = _pick(n2, (S5_MAX_ROWS, 256, 128, 64, 32))
    nt = n2 // rows
    once = pl.Buffered(1)
    nlev = max(1, (rows - 1).bit_length())
    order = (lambda b, i: (b, nt - 1 - i, 0)) if reverse else (lambda b, i: (b, i, 0))
    order_out = (lambda b, i: (b, 0, nt - 1 - i, 0)) if reverse else (lambda b, i: (b, 0, i, 0))
    const2 = lambda b, i: (0, 0)
    y, xf = pl.pallas_call(
        functools.partial(_s5_kernel, rows=rows, reverse=reverse, nlev=nlev),
        grid=(nb, nt),
        in_specs=[
            pl.BlockSpec((1, rows, S5_LANES), order),
            pl.BlockSpec((1, 1, 2 * S5_NSTATE), lambda b, i: (b, 0, 0)),
            pl.BlockSpec((S5_R, A_WIDTH, A_WIDTH), lambda b, i: (0, 0, 0), pipeline_mode=once),
            pl.BlockSpec((S5_LANES, 2 * S5_NSTATE), const2, pipeline_mode=once),
            pl.BlockSpec((S5_R, 2 * S5_NSTATE, A_WIDTH), lambda b, i: (0, 0, 0), pipeline_mode=once),
            pl.BlockSpec(pq.shape, const2),
        ],
        out_specs=[
            pl.BlockSpec((1, A_WIDTH // LANES, rows * S5_R, LANES), order_out),
            pl.BlockSpec((1, 1, 2 * S5_NSTATE), lambda b, i: (b, 0, 0)),
        ],
        out_shape=[
            jax.ShapeDtypeStruct((nb, A_WIDTH // LANES, n2 * S5_R, LANES), F32),
            jax.ShapeDtypeStruct((nb, 1, 2 * S5_NSTATE), F32),
        ],
        scratch_shapes=[pltpu.VMEM((1, 2 * S5_NSTATE), F32)],
        compiler_params=_params(("parallel", "arbitrary")),
        name="s5_scan_rev" if reverse else "s5_scan_fwd",
    )(u2, x0, lag, wst, wout, pq)
    return y, xf


def _s5_weights(lam_re, lam_im, log_dt, b_re, b_im, c_re, c_im, nlev):
    g, n, r = A_GROUPS, A_STATE, S5_R
    nl = lam_re.shape[0]
    lam_re = jnp.minimum(lam_re.astype(F32), -1e-4)
    lam_im = lam_im.astype(F32)
    dt = jnp.exp(log_dt.astype(F32))[..., None]
    mag = jnp.exp(lam_re * dt)
    a_re = mag * jnp.cos(lam_im * dt)
    a_im = mag * jnp.sin(lam_im * dt)
    den = lam_re * lam_re + lam_im * lam_im
    num_re = a_re - 1.0
    f_re = (num_re * lam_re + a_im * lam_im) / den
    f_im = (a_im * lam_re - num_re * lam_im) / den
    bb_re = f_re[..., None] * b_re - f_im[..., None] * b_im
    bb_im = f_re[..., None] * b_im + f_im[..., None] * b_re

    def cm(xr, xi, yr, yi):
        return xr * yr - xi * yi, xr * yi + xi * yr

    pr, pi = [jnp.ones_like(a_re)], [jnp.zeros_like(a_im)]
    for _ in range(r):
        nr, ni = cm(pr[-1], pi[-1], a_re, a_im)
        pr.append(nr)
        pi.append(ni)
    pr, pi = jnp.stack(pr), jnp.stack(pi)

    xr, xi = cm(pr[..., None], pi[..., None], bb_re, bb_im)
    yr, yi = cm(c_re, c_im, pr[:, :, :, :, None, :], pi[:, :, :, :, None, :])
    lag = jnp.einsum('ldgon,tldgnc->tldgoc', c_re, xr) - jnp.einsum('ldgon,tldgnc->tldgoc', c_im, xi)

    def group_diag(m):
        ni, nj = m.shape[-2:]
        wide = jnp.swapaxes(m, -3, -2).reshape(m.shape[:-3] + (1, ni, g * nj))
        keep = (np.arange(g)[:, None, None] == (np.arange(g * nj) // nj)[None, None, :])
        return jnp.where(keep, wide, 0.0).astype(BF16).reshape(m.shape[:-3] + (g * ni, g * nj))

    lag_bd = jnp.moveaxis(group_diag(jnp.swapaxes(lag[:r], -1, -2)), 0, 2)
    st_bd = jnp.concatenate([group_diag(jnp.swapaxes(xr, -1, -2)), group_diag(jnp.swapaxes(xi, -1, -2))], axis=-1)
    out_bd = jnp.concatenate([group_diag(jnp.swapaxes(yr, -1, -2)), group_diag(jnp.swapaxes(-yi, -1, -2))], axis=-2)

    tau_st = (r - 1 - np.arange(r), np.arange(r))
    tau_out = (np.arange(r) + 1, r - np.arange(r))

    def per_dir(fn):
        return jnp.stack([fn(0), fn(1)], axis=1)

    wst = per_dir(lambda dr: jnp.transpose(st_bd[tau_st[dr], :, dr], (1, 0, 2, 3))
                  .reshape(nl, r * A_WIDTH, 2 * g * n))
    wout = per_dir(lambda dr: jnp.transpose(out_bd[tau_out[dr], :, dr], (1, 0, 2, 3)))

    lr, li = pr[r].reshape(nl, 2, 1, g * n), pi[r].reshape(nl, 2, 1, g * n)
    rows = []
    for _ in range(nlev):
        rows.append(jnp.concatenate([lr, lr], axis=-1))
        rows.append(jnp.concatenate([-li, li], axis=-1))
        lr, li = cm(lr, li, lr, li)
    pq = jnp.concatenate(rows, axis=-2)
    return lag_bd, wst, wout, pq


def _split3(x):
    x1 = x.astype(BF16)
    r1 = x - x1.astype(F32)
    x2 = r1.astype(BF16)
    x3 = (r1 - x2.astype(F32)).astype(BF16)
    return x1, x2, x3


def _hgrn_chunks(chains, tri_ref, sel_ref, sgn_ref, msk_ref, heads, bd, nlev):
    t = chains[0][0].shape[0]

    def head_scores(qx, kx):
        qh = jnp.concatenate([jnp.where(heads[hd], qx, jnp.zeros_like(qx)) for hd in range(C_HEADS)], axis=0)
        s = _dot_nt(qh, kx)
        return [s[hd * t:(hd + 1) * t, :] for hd in range(C_HEADS)]

    cums, tots = [], []
    for q, k, vb, lf, st, dr in chains:
        l1, l2, l3 = _split3(lf)
        tri = tri_ref[dr]
        cums.append(_dot(tri, l1) + _dot(tri, l2) + _dot(tri, l3))
        tots.append(jnp.sum(lf, axis=0, keepdims=True))
    mids = [_dot(sel_ref[c[5]], cum.astype(BF16)) for c, cum in zip(chains, cums)]

    atts = []
    for q, k, vb, lf, st, dr in chains:
        diag = msk_ref[dr, nlev] > 0.5
        atts.append([jnp.where(diag, s, 0.0) for s in head_scores(q.astype(BF16), k.astype(BF16))])
    for lev in range(nlev):
        for ci, (q, k, vb, lf, st, dr) in enumerate(chains):
            e = jnp.exp(sgn_ref[dr, lev] * (cums[ci] - mids[ci][lev * t:(lev + 1) * t, :]))
            m = msk_ref[dr, lev] > 0.5
            for hd, s in enumerate(head_scores((q * e).astype(BF16), (k * e).astype(BF16))):
                atts[ci][hd] = atts[ci][hd] + jnp.where(m, s, 0.0)

    outs = []
    for ci, (q, k, vb, lf, st, dr) in enumerate(chains):
        o = _dot_nt((q * jnp.exp(cums[ci])).astype(BF16), st.astype(BF16))
        for hd in range(C_HEADS):
            o = o + _dot(atts[ci][hd].astype(BF16), jnp.where(heads[hd], vb, jnp.zeros_like(vb)))
        kend = (k * jnp.exp(tots[ci] - cums[ci])).astype(BF16)
        vt = vb.astype(F32).T.astype(BF16)
        new = st * jnp.exp(tots[ci]) + jnp.where(bd, _dot(vt, kend), 0.0)
        outs.append((o, new))
    return outs


def _hgrn_kernel(qf_ref, qr_ref, kf_ref, kr_ref, vf_ref, vr_ref, lff_ref, lfr_ref, s0_ref,
                 tri_ref, sel_ref, sgn_ref, msk_ref, hm_ref, bd_ref,
                 of_ref, or_ref, sf_ref, st_scr, *, nlev, nb):
    c = pl.program_id(0)

    @pl.when(c == 0)
    def _():
        st_scr[...] = s0_ref[...]

    heads = [hm_ref[pl.ds(hd, 1), :] > 0.5 for hd in range(C_HEADS)]
    bd = bd_ref[...] > 0.5
    dirs = ((qf_ref, kf_ref, vf_ref, lff_ref, of_ref), (qr_ref, kr_ref, vr_ref, lfr_ref, or_ref))
    chains, sinks = [], []
    for dr, (q_ref, k_ref, v_ref, lf_ref, o_ref) in enumerate(dirs):
        for b in range(nb):
            chains.append((q_ref[b].astype(F32), k_ref[0, b].astype(F32), v_ref[b], lf_ref[0, b], st_scr[dr, b], dr))
            sinks.append((o_ref, dr, b))
    outs = _hgrn_chunks(chains, tri_ref, sel_ref, sgn_ref, msk_ref, heads, bd, nlev)
    for (o_ref, dr, b), (o, new) in zip(sinks, outs):
        o_ref[b] = o
        st_scr[dr, b] = new
        sf_ref[dr, b] = new


def _hgrn_consts():
    t = HG_T
    ti = jnp.arange(t)[:, None]
    si = jnp.arange(t)[None, :]
    tri, sel, sgn, msk = [], [], [], []
    for reverse in (False, True):
        tri.append((si >= ti) if reverse else (si <= ti))
        sels, sgns, msks = [], [], []
        for lev in range(HG_LEVELS):
            h = 1 << lev
            blk_t, blk_s = ti // (2 * h), si // (2 * h)
            hi_t, hi_s = (ti % (2 * h)) >= h, (si % (2 * h)) >= h
            if reverse:
                mid = blk_t * 2 * h + h
                q_role_t, k_role_s = ~hi_t, hi_s
            else:
                mid = blk_t * 2 * h + h - 1
                q_role_t, k_role_s = hi_t, ~hi_s
            sels.append(si == mid)
            sgns.append(jnp.where(q_role_t, 1.0, -1.0))
            msks.append((blk_t == blk_s) & q_role_t & k_role_s)
        msks.append(ti == si)
        sel.append(jnp.concatenate(sels, axis=0))
        sgn.append(jnp.stack(sgns))
        msk.append(jnp.stack(msks))
    lane_head = jnp.arange(HG_W) // C_DK
    hm = (lane_head[None, :] == jnp.arange(C_HEADS)[:, None]).astype(F32)
    bd = (lane_head[:, None] == lane_head[None, :]).astype(F32)
    return (jnp.stack(tri).astype(BF16), jnp.stack(sel).astype(BF16), jnp.stack(sgn).astype(F32),
            jnp.stack(msk).astype(F32), hm, bd)


def _hgrn_scan(q, k, v, lf, s0, consts):
    tri, sel, sgn, msk, hm, bd = consts
    nb, s, w = q.shape
    t = HG_T
    nc = s // t
    fwd3 = pl.BlockSpec((nb, t, w), lambda c: (0, c, 0))
    rev3 = pl.BlockSpec((nb, t, w), lambda c: (0, nc - 1 - c, 0))
    fwd4 = pl.BlockSpec((1, nb, t, w), lambda c: (0, 0, c, 0))
    rev4 = pl.BlockSpec((1, nb, t, w), lambda c: (1, 0, nc - 1 - c, 0))
    whole = lambda a: pl.BlockSpec(a.shape, lambda c: (0,) * a.ndim)
    sgn = sgn.reshape(2, HG_LEVELS, t, 1)
    o_f, o_r, sf = pl.pallas_call(
        functools.partial(_hgrn_kernel, nlev=HG_LEVELS, nb=nb),
        grid=(nc,),
        in_specs=[fwd3, rev3, fwd4, rev4, fwd3, rev3, fwd4, rev4, whole(s0),
                  whole(tri), whole(sel), whole(sgn), whole(msk), whole(hm), whole(bd)],
        out_specs=[fwd3, rev3, whole(s0)],
        out_shape=[
            jax.ShapeDtypeStruct((nb, s, w), F32),
            jax.ShapeDtypeStruct((nb, s, w), F32),
            jax.ShapeDtypeStruct((2, nb, w, w), F32),
        ],
        scratch_shapes=[pltpu.VMEM((2, nb, w, w), F32)],
        compiler_params=_params(("arbitrary",)),
        name="hgrn_scan",
    )(q, q, k, k, v, v, lf, lf, s0, tri, sel, sgn, msk, hm, bd)
    return (o_f, o_r), sf


def _attn_kernel(*refs, nseg, tks, g, tq):
    q_ref = refs[0]
    kv_refs = refs[1:1 + 2 * nseg]
    o_ref = refs[1 + 2 * nseg]
    dv = o_ref.shape[-1]
    n = g * tq
    qt = jnp.concatenate([q_ref[0, hd] for hd in range(g)], axis=-1)

    def scores(k):
        s = _dot(k, qt)
        sub = min(ATTN_SUB, s.shape[0])
        parts = [s[i:i + sub] for i in range(0, s.shape[0], sub)]
        return tuple(x for p in parts for x in (p, jnp.max(p, axis=0, keepdims=True)))

    def absorb(sm, vt, carry):
        m, acc = carry
        sub = sm[0].shape[0]
        for i in range(len(sm) // 2):
            s, smax = sm[2 * i], sm[2 * i + 1]
            m_new = jnp.maximum(m, smax)
            alpha = jnp.exp2(m - m_new)
            p = jnp.exp2(s - m_new).astype(BF16)
            vt1 = jnp.concatenate([vt[:, i * sub:(i + 1) * sub], jnp.ones((ATTN_ONES, sub), BF16)], axis=0)
            acc = alpha * acc + _dot(vt1, p)
            m = m_new
        return m, acc

    carry = (jnp.full((1, n), -jnp.inf, F32), jnp.zeros((dv + ATTN_ONES, n), F32))
    pending = None
    for seg in range(nseg):
        k_ref, vt_ref = kv_refs[2 * seg], kv_refs[2 * seg + 1]
        tk = tks[seg]
        nk = k_ref.shape[2] // tk
        s_first = scores(k_ref[0, 0, 0:tk, :])
        if pending is not None:
            carry = absorb(*pending, carry)
        if nk == 1:
            pending = (s_first, vt_ref[0, 0])
            continue

        def body(j, c, k_ref=k_ref, vt_ref=vt_ref, tk=tk, nk=nk):
            sm_cur, (m, acc) = c[:-2], c[-2:]
            off_next = pl.multiple_of(jnp.minimum(j + 1, nk - 1) * tk, tk)
            sm_next = scores(k_ref[0, 0, pl.ds(off_next, tk), :])
            off = pl.multiple_of(j * tk, tk)
            m, acc = absorb(sm_cur, vt_ref[0, 0, :, pl.ds(off, tk)], (m, acc))
            return sm_next + (m, acc)

        unroll = ATTN_UNROLL if nk % ATTN_UNROLL == 0 else 1
        m, acc = lax.fori_loop(0, nk, body, s_first + carry, unroll=unroll)[-2:]
        carry = (m, acc)
        pending = None
    m, acc = carry if pending is None else absorb(*pending, carry)
    out = acc[:dv] / acc[dv:dv + 1, :]
    out = jnp.concatenate([out, jnp.zeros((LANES - dv, n), F32)], axis=0).T
    o_ref[0] = out[:, :dv].reshape(g, tq, dv).astype(o_ref.dtype)


def _attention(q, kvs):
    nb, hq, dk, sq = q.shape
    hkv = kvs[0][0].shape[1]
    dv = kvs[0][1].shape[2]
    g = hq // hkv
    tq = _pick(sq, tuple(n // g for n in ATTN_QUERIES))
    tks = tuple(_pick(k.shape[2], (ATTN_SUB, 128)) for k, _ in kvs)
    in_specs = [pl.BlockSpec((1, g, dk, tq), lambda b, h, i: (b, h, 0, i))]
    args = [q]
    for k, v in kvs:
        sk = k.shape[2]
        in_specs.append(pl.BlockSpec((1, 1, sk, dk), lambda b, h, i: (b, h, 0, 0)))
        in_specs.append(pl.BlockSpec((1, 1, dv, sk), lambda b, h, i: (b, h, 0, 0)))
        args += [k, v]
    return pl.pallas_call(
        functools.partial(_attn_kernel, nseg=len(kvs), tks=tks, g=g, tq=tq),
        grid=(nb, hkv, sq // tq),
        in_specs=in_specs,
        out_specs=pl.BlockSpec((1, g, tq, dv), lambda b, h, i: (b, h, i, 0)),
        out_shape=jax.ShapeDtypeStruct((nb, hq, sq, dv), BF16),
        compiler_params=_params(("parallel", "parallel", "arbitrary")),
        name="attention",
    )(*args)


def _merge_kernel(x_ref, mod_ref, gpre_ref, gpost_ref, wg_ref, wb_ref, wo_ref,
                  u_ref, yf_ref, yr_ref, sd_ref, wglu_ref, bo_ref, hof_ref, hor_ref, hg_ref, hn_ref, avg_ref, do_ref,
                  o_ref):
    x = x_ref[0]
    shift = mod_ref[0, pl.ds(3, 1), :]
    scale = mod_ref[0, pl.ds(4, 1), :]
    gate = mod_ref[0, pl.ds(5, 1), :]
    h = (_rms(x, gpre_ref[...]) * (1.0 + scale) + shift).astype(BF16)
    d = x.shape[-1]

    def branch_gate(i):
        return jax.nn.sigmoid(_dot(h, wg_ref[:, i * d:(i + 1) * d]))

    ysum = yf_ref[0] + yr_ref[0]
    y = sd_ref[...] * u_ref[0].astype(F32) + jnp.concatenate([ysum[hf] for hf in range(A_WIDTH // LANES)], axis=-1)
    ge = jax.nn.gelu(y)
    ya = ge * jax.nn.sigmoid(_dot(ge.astype(BF16), wglu_ref[...]))
    merged = branch_gate(0) * _dot(ya.astype(BF16), wb_ref[0])

    yb = jnp.concatenate([bo_ref[0, hd] for hd in range(B_HEADS)], axis=-1)
    merged = merged + branch_gate(1) * _dot(yb, wb_ref[1])

    o2 = hof_ref[0] + hor_ref[0]
    ms = _dot((o2 * o2).astype(BF16), avg_ref[...])
    gz = hg_ref[0].astype(F32)
    yc = o2 * lax.rsqrt(ms + EPS) * hn_ref[...] * (gz * jax.nn.sigmoid(gz))
    merged = merged + branch_gate(2) * _dot(yc.astype(BF16), wb_ref[2])

    yd = jnp.concatenate([do_ref[0, hd] for hd in range(D_HEADS)], axis=-1)
    merged = merged + branch_gate(3) * _dot(yd, wb_ref[3])

    yo = _dot(merged.astype(BF16), wo_ref[...])
    o_ref[0] = x + gate * _rms(yo, gpost_ref[...])


def _merge(x, mod, g_pre, g_post, wg, wb, wo, u, yf, yr, s5d, wglu, bo, ho, hg, hn, avg, do):
    nb, s, d = x.shape
    tm = _pick(s, (512, 256))
    const2 = lambda b, i: (0, 0)
    const3 = lambda b, i: (0, 0, 0)
    tok = lambda w: pl.BlockSpec((1, tm, w), lambda b, i: (b, i, 0))
    halves = pl.BlockSpec((1, A_WIDTH // LANES, tm, LANES), lambda b, i: (b, 0, i, 0))
    return pl.pallas_call(
        _merge_kernel,
        grid=(nb, s // tm),
        in_specs=[
            tok(d),
            pl.BlockSpec((1, N_MOD, d), lambda b, i: (b, 0, 0)),
            pl.BlockSpec((1, d), const2),
            pl.BlockSpec((1, d), const2),
            pl.BlockSpec((d, N_BRANCH * d), const2),
            pl.BlockSpec((N_BRANCH, BRANCH_W, d), const3),
            pl.BlockSpec((d, d), const2),
            tok(A_WIDTH), halves, halves,
            pl.BlockSpec((1, A_WIDTH), const2),
            pl.BlockSpec((A_WIDTH, A_WIDTH), const2),
            pl.BlockSpec((1, B_HEADS, tm, B_V), lambda b, i: (b, 0, i, 0)),
            tok(HG_W), tok(HG_W),
            tok(HG_W),
            pl.BlockSpec((1, HG_W), const2),
            pl.BlockSpec((HG_W, HG_W), const2),
            pl.BlockSpec((1, D_HEADS, tm, D_HEAD), lambda b, i: (b, 0, i, 0)),
        ],
        out_specs=tok(d),
        out_shape=jax.ShapeDtypeStruct(x.shape, F32),
        compiler_params=_params(("parallel", "parallel")),
        name="merge_out",
    )(x, mod, g_pre.reshape(1, d), g_post.reshape(1, d), wg, wb, wo, u, yf, yr, s5d, wglu, bo, ho[0], ho[1], hg, hn,
      avg, do)


def _pad_cols(w, width):
    return jnp.pad(w, ((0, 0), (0, width - w.shape[1])))


def _proj_weight(w_in_mix):
    offs = [0]
    for wdt in (A_WIDTH, B_Q_LORA, B_KV_LORA, B_ROPE, HG_W, HG_W, HG_W, HG_W, HG_W,
                D_HEADS * D_HEAD, D_KV_HEADS * D_HEAD, D_KV_HEADS * D_HEAD):
        offs.append(offs[-1] + wdt)
    p = [w_in_mix[:, offs[i]:offs[i + 1]] for i in range(12)]
    d = w_in_mix.shape[0]
    z = lambda n: jnp.zeros((d, n), w_in_mix.dtype)
    cols = [p[0], _pad_cols(p[1], 256), p[2],
            jnp.concatenate([z(B_NOPE), p[3], z(LANES - B_NOPE - B_ROPE)], axis=1),
            p[4], p[5], p[6], p[7], p[8]]
    for i, nh in ((9, D_HEADS), (10, D_KV_HEADS), (11, D_KV_HEADS)):
        for hd in range(nh):
            cols.append(_pad_cols(p[i][:, hd * D_HEAD:(hd + 1) * D_HEAD], LANES))
    return jnp.concatenate(cols, axis=1).astype(BF16)


def _rope_tables(n_tok, rot_dim, lane_off, identity):
    cos = jnp.ones((n_tok, LANES), F32)
    sin_a = jnp.zeros((n_tok, LANES), F32)
    sin_b = jnp.zeros((n_tok, LANES), F32)
    if not identity:
        n_rows = n_tok // GRID_W
        rows = jnp.repeat(jnp.arange(n_rows, dtype=F32), GRID_W)
        cols = jnp.tile(jnp.arange(GRID_W, dtype=F32), n_rows)
        half = rot_dim // 2
        inv = ROPE_THETA ** (-jnp.arange(0, half, 2, dtype=F32) / half)
        ang_r = rows[:, None] * inv
        ang_c = cols[:, None] * inv
        ang = jnp.concatenate([ang_r, ang_r, ang_c, ang_c], axis=-1)
        c, s = jnp.cos(ang), jnp.sin(ang)
        quarter = rot_dim // 4
        first = (np.arange(rot_dim) % (2 * quarter)) < quarter

        def place(a, fill):
            left = jnp.full((n_tok, lane_off), fill, F32)
            right = jnp.full((n_tok, LANES - lane_off - rot_dim), fill, F32)
            return jnp.concatenate([left, a, right], axis=1)

        cos = place(c, 1.0)
        sin_a = place(jnp.where(first, -s, 0.0), 0.0)
        sin_b = place(jnp.where(first, 0.0, s), 0.0)
    return jnp.stack([cos, sin_a, sin_b])


def kernel(x, c, ctx, c_ctx, w_ada, b_ada, norm_pre, norm_post, ffn_w1, ffn_w3, ffn_w2, w_in,
           s5_lambda_re, s5_lambda_im, s5_log_dt, s5_b_re, s5_b_im, s5_c_re, s5_c_im, s5_d, s5_w_glu,
           mla_q_norm, mla_w_uq, mla_kv_norm, mla_w_ukv, hgrn_lb_raw, hgrn_o_norm,
           gqa_q_norm, gqa_k_norm, w_branch, w_out):
    nb, seq, d = x.shape
    n_ctx = ctx.shape[1]
    depth = w_ada.shape[0]

    rows = max(8, -(-(nb + 1) // 8) * 8)
    cvec = jnp.zeros((rows, d), F32).at[:nb].set(c).at[nb].set(c_ctx)
    mod_all = _modulation(cvec, w_ada, b_ada).reshape(depth, rows, N_MOD, d)

    lb_step = jax.nn.softmax(hgrn_lb_raw.astype(F32), axis=1)
    lb_all = jnp.clip(jnp.cumsum(lb_step, axis=1) - lb_step[:, :1], 0.0, 1.0)

    rope_b_lat = _rope_tables(seq, B_ROPE, B_NOPE, False)
    rope_d_lat = _rope_tables(seq, D_HEAD, 0, False)
    rope_b_ctx = _rope_tables(n_ctx, B_ROPE, B_NOPE, True)
    rope_d_ctx = _rope_tables(n_ctx, D_HEAD, 0, True)
    hg_consts = _hgrn_consts()
    s5_w = _s5_weights(s5_lambda_re, s5_lambda_im, s5_log_dt, s5_b_re, s5_b_im, s5_c_re, s5_c_im, S5_MAX_LEVELS)
    lane_head = jnp.arange(HG_W) // C_DV
    avg = (lane_head[:, None] == lane_head[None, :]).astype(BF16) * (1.0 / C_DV)
    n_mixcols = w_in.shape[-1] - N_BRANCH * d
    ffn_w = (ffn_w1.astype(BF16), ffn_w3.astype(BF16), ffn_w2.astype(BF16))

    x_lat, x_ctx = x, ctx
    for layer in range(depth):
        last = layer == depth - 1
        mod_lat = mod_all[layer, :nb]
        mod_ctx = jnp.broadcast_to(mod_all[layer, nb:nb + 1], (nb, N_MOD, d))
        bf = lambda w: w.astype(BF16)
        ffn_a = (norm_pre[layer, 0], norm_post[layer, 0], *ffn_w, layer, 0)
        ffn_b = (norm_pre[layer, 2], norm_post[layer, 2], *ffn_w, layer, 1)

        x_lat = _ffn(x_lat, mod_lat, 0, *ffn_a)
        x_ctx = _ffn(x_ctx, mod_ctx, 0, *ffn_a)

        wp = _proj_weight(w_in[layer, :, :n_mixcols])
        wg = bf(w_in[layer, :, n_mixcols:])
        qn = _pad_cols(mla_q_norm[layer].reshape(1, B_Q_LORA), 256)
        wuq = mla_w_uq[layer].reshape(B_Q_LORA, B_HEADS, B_NOPE + B_ROPE)
        wuq = jnp.pad(wuq, ((0, 256 - B_Q_LORA), (0, 0), (0, LANES - B_NOPE - B_ROPE))).reshape(256, B_HEADS * LANES)
        wukv = mla_w_ukv[layer].reshape(B_KV_LORA, B_HEADS, B_NOPE + B_V)
        wuk = jnp.pad(wukv[:, :, :B_NOPE], ((0, 0), (0, 0), (0, LANES - B_NOPE))).reshape(B_KV_LORA, B_HEADS * LANES)
        wuv = jnp.pad(wukv[:, :, B_NOPE:], ((0, 0), (0, 0), (0, LANES - B_V))).reshape(B_KV_LORA, B_HEADS * LANES)
        kvn = mla_kv_norm[layer].reshape(1, B_KV_LORA)
        gqn = _pad_cols(gqa_q_norm[layer].reshape(1, D_HEAD), LANES)
        gkn = _pad_cols(gqa_k_norm[layer].reshape(1, D_HEAD), LANES)
        lb = lb_all[:, layer]
        proj_args = (norm_pre[layer, 1], wp)
        mla_args = (qn, bf(wuq), kvn, bf(wuk), bf(wuv), lb, gqn, gkn)

        pl_ = _inproj(x_lat, mod_lat, *proj_args, rope_b_lat, rope_d_lat, *mla_args)
        pc_ = _inproj(x_ctx, mod_ctx, *proj_args, rope_b_ctx, rope_d_ctx, *mla_args)
        (u_l, u2_l, bq_l, bk_l, bv_l, hq_l, hv_l, hk_l, hl_l, hg_l, dq_l, dk_l, dv_l) = pl_
        (u_c, u2_c, bq_c, bk_c, bv_c, hq_c, hv_c, hk_c, hl_c, hg_c, dq_c, dk_c, dv_c) = pc_

        ys_l, ys_c = [], []
        for dr, reverse in enumerate((False, True)):
            wts = tuple(w[layer, dr] for w in s5_w)
            x0 = jnp.zeros((nb, 1, 2 * S5_NSTATE), F32)
            y_c, x_end = _s5_scan(u2_c, x0, wts, reverse)
            y_l, _ = _s5_scan(u2_l, x_end, wts, reverse)
            ys_l.append(y_l)
            ys_c.append(y_c)

        s0 = jnp.zeros((2, nb, HG_W, HG_W), F32)
        ho_c, s_ctx = _hgrn_scan(hq_c, hk_c, hv_c, hl_c, s0, hg_consts)
        ho_l, _ = _hgrn_scan(hq_l, hk_l, hv_l, hl_l, s_ctx, hg_consts)

        bo_l = _attention(bq_l, [(bk_c, bv_c), (bk_l, bv_l)])
        do_l = _attention(dq_l, [(dk_c, dv_c), (dk_l, dv_l)])

        merge_w = (norm_pre[layer, 1], norm_post[layer, 1], wg, bf(w_branch[layer]), bf(w_out[layer]))
        s5_ro = (s5_d[layer].reshape(1, A_WIDTH), bf(s5_w_glu[layer]))
        hn = jnp.tile(hgrn_o_norm[layer], C_HEADS).reshape(1, HG_W)
        x_lat_new = _merge(x_lat, mod_lat, *merge_w, u_l, ys_l[0], ys_l[1], *s5_ro, bo_l, ho_l, hg_l, hn, avg, do_l)
        if not last:
            bo_c = _attention(bq_c, [(bk_c, bv_c)])
            do_c = _attention(dq_c, [(dk_c, dv_c)])
            x_ctx = _merge(x_ctx, mod_ctx, *merge_w, u_c, ys_c[0], ys_c[1], *s5_ro, bo_c, ho_c, hg_c, hn, avg, do_c)
            x_ctx = _ffn(x_ctx, mod_ctx, 2, *ffn_b)
        x_lat = _ffn(x_lat_new, mod_lat, 2, *ffn_b)
    return x_lat
```

```python
import functools

import jax
import jax.numpy as jnp
import numpy as np
from jax import lax
from jax.experimental import pallas as pl
from jax.experimental.pallas import tpu as pltpu

GRID_W = 64
FFN_RES_WEIGHT = 0.5
N_MOD = 9
EPS = 1e-6
ROPE_THETA = 10000.0
F_FLOOR = 1e-20

A_WIDTH = 256
A_GROUP = 16
A_GROUPS = A_WIDTH // A_GROUP
A_STATE = 64

B_HEADS = 4
B_NOPE = 64
B_ROPE = 32
B_V = 64
B_Q_LORA = 192
B_KV_LORA = 128

C_HEADS = 4
C_DK = 64
C_DV = 64

D_HEADS = 4
D_KV_HEADS = 2
D_HEAD = 64

N_BRANCH = 4
BRANCH_W = 256

LANES = 128
VMEM_LIMIT_BYTES = 56 * 1024 * 1024

S5_R = 8
S5_LANES = S5_R * A_WIDTH
S5_NSTATE = A_GROUPS * A_STATE
S5_MAX_ROWS = 512
S5_MAX_LEVELS = 9
HG_T = 128
HG_LEVELS = 7
HG_W = C_HEADS * C_DK
HG_TRIM_MIN = 16
ATTN_ONES = 16
ATTN_QUERIES = (1024, 512, 256, 128)
ATTN_SUB = 256
ATTN_UNROLL = 32

SLOT_S5 = 0
SLOT_CQ = 256
SLOT_CKV = 512
SLOT_KR = 640
SLOT_HQ = 768
SLOT_HV = 1024
SLOT_HF = 1280
SLOT_HB = 1536
SLOT_HG = 1792
SLOT_GQ = 2048
SLOT_GK = 2560
SLOT_GV = 2816
N_PROJ = 3072

BF16 = jnp.bfloat16
F32 = jnp.float32
LOG2E = 1.4426950408889634


def _params(sem, flags=None):
    return pltpu.CompilerParams(dimension_semantics=sem, vmem_limit_bytes=VMEM_LIMIT_BYTES, flags=flags)


def _pick(n, candidates):
    for c in candidates:
        if n % c == 0:
            return c
    raise ValueError(f"no tile for {n} in {candidates}")


def _dot(a, b):
    return jnp.dot(a, b, preferred_element_type=F32)


def _dot_nt(a, b):
    return lax.dot_general(a, b, (((1,), (1,)), ((), ())), preferred_element_type=F32)


def _rms(x, g, n=None):
    n = x.shape[-1] if n is None else n
    ms = jnp.sum(x * x, axis=-1, keepdims=True) * (1.0 / n)
    return x * lax.rsqrt(ms + EPS) * g


def _mod_kernel(c_ref, w_ref, b_ref, o_ref):
    c = c_ref[...]
    a = (c * jax.nn.sigmoid(c)).astype(BF16)
    o_ref[0] = _dot(a, w_ref[0].astype(BF16)) + b_ref[0]


def _modulation(cvec, w_ada, b_ada):
    nl, d, nm = w_ada.shape
    rows = cvec.shape[0]
    tn = _pick(nm, (1152, 1024, 512, 256, 128))
    return pl.pallas_call(
        _mod_kernel,
        grid=(nl, nm // tn),
        in_specs=[
            pl.BlockSpec((rows, d), lambda l, n: (0, 0)),
            pl.BlockSpec((1, d, tn), lambda l, n: (l, 0, n)),
            pl.BlockSpec((1, 1, tn), lambda l, n: (l, 0, n)),
        ],
        out_specs=pl.BlockSpec((1, rows, tn), lambda l, n: (l, 0, n)),
        out_shape=jax.ShapeDtypeStruct((nl, rows, nm), F32),
        compiler_params=_params(("parallel", "parallel")),
        name="adaln_mod",
    )(cvec, w_ada, b_ada.reshape(nl, 1, nm))


def _ffn_kernel(x_ref, mod_ref, gpre_ref, gpost_ref, w1_ref, w3_ref, w2_ref, o_ref, *, j, tf):
    x = x_ref[0]
    shift = mod_ref[0, pl.ds(3 * j, 1), :]
    scale = mod_ref[0, pl.ds(3 * j + 1, 1), :]
    gate = mod_ref[0, pl.ds(3 * j + 2, 1), :]
    h = (_rms(x, gpre_ref[...]) * (1.0 + scale) + shift).astype(BF16)
    acc = None
    for f in range(w1_ref.shape[1] // tf):
        a = _dot(h, w1_ref[:, f * tf:(f + 1) * tf])
        b = _dot(h, w3_ref[:, f * tf:(f + 1) * tf])
        t = (a * jax.nn.sigmoid(a) * b).astype(BF16)
        part = _dot(t, w2_ref[f * tf:(f + 1) * tf, :])
        acc = part if acc is None else acc + part
    o_ref[0] = x + FFN_RES_WEIGHT * gate * _rms(acc, gpost_ref[...])


def _ffn(x, mod, j, g_pre, g_post, w1, w3, w2, layer, which):
    nb, s, d = x.shape
    dff = w1.shape[-1]
    tm = _pick(s, (512, 256))
    tf = _pick(dff, (256, 128))
    once = pl.Buffered(1)
    pick = lambda b, i: (layer, which, 0, 0)
    return pl.pallas_call(
        functools.partial(_ffn_kernel, j=j, tf=tf),
        grid=(nb, s // tm),
        in_specs=[
            pl.BlockSpec((1, tm, d), lambda b, i: (b, i, 0)),
            pl.BlockSpec((1, N_MOD, d), lambda b, i: (b, 0, 0)),
            pl.BlockSpec((1, d), lambda b, i: (0, 0)),
            pl.BlockSpec((1, d), lambda b, i: (0, 0)),
            pl.BlockSpec((None, None, d, dff), pick, pipeline_mode=once),
            pl.BlockSpec((None, None, d, dff), pick, pipeline_mode=once),
            pl.BlockSpec((None, None, dff, d), pick, pipeline_mode=once),
        ],
        out_specs=pl.BlockSpec((1, tm, d), lambda b, i: (b, i, 0)),
        out_shape=jax.ShapeDtypeStruct(x.shape, F32),
        compiler_params=_params(("parallel", "parallel")),
        name="ffn_sublayer",
    )(x, mod, g_pre.reshape(1, d), g_post.reshape(1, d), w1, w3, w2)


def _rope(x, cos, sin_a, sin_b, quarter):
    w = x.shape[-1]
    return x * cos + pltpu.roll(x, w - quarter, 1) * sin_a + pltpu.roll(x, quarter, 1) * sin_b


def _inproj_kernel(x_ref, mod_ref, gpre_ref, w_ref, rb_ref, rd_ref, qn_ref, wuq_ref, kvn_ref, wuk_ref, wuv_ref,
                   lb_ref, gqn_ref, gkn_ref,
                   u_ref, u2_ref, bq_ref, bk_ref, bv_ref, hq_ref, hv_ref, hk_ref, hl_ref, hg_ref,
                   dq_ref, dk_ref, dv_ref, h_scr, u_scr, *, tm):
    x = x_ref[0]
    shift = mod_ref[0, pl.ds(3, 1), :]
    scale = mod_ref[0, pl.ds(4, 1), :]
    h_scr[...] = (_rms(x, gpre_ref[...]) * (1.0 + scale) + shift).astype(BF16)

    def proj(lo, width):
        return _dot(h_scr[...], w_ref[:, lo:lo + width])

    cos_d, sa_d, sb_d = rd_ref[0], rd_ref[1], rd_ref[2]
    d_scale = D_HEAD ** -0.5 * LOG2E
    gq = proj(SLOT_GQ, D_HEADS * LANES)
    for hd in range(D_HEADS):
        qh = _rms(gq[:, hd * LANES:(hd + 1) * LANES], gqn_ref[...], n=D_HEAD)
        qh = _rope(qh, cos_d, sa_d, sb_d, D_HEAD // 4) * d_scale
        dq_ref[0, hd] = qh.T[:D_HEAD].astype(BF16)
    gkv = proj(SLOT_GK, 2 * D_KV_HEADS * LANES)
    for hd in range(D_KV_HEADS):
        kh = _rms(gkv[:, hd * LANES:(hd + 1) * LANES], gkn_ref[...], n=D_HEAD)
        kh = _rope(kh, cos_d, sa_d, sb_d, D_HEAD // 4)
        dk_ref[0, hd] = kh[:, :D_HEAD].astype(BF16)
        dv_ref[0, hd] = gkv[:, (D_KV_HEADS + hd) * LANES:(D_KV_HEADS + hd + 1) * LANES].T[:D_HEAD].astype(BF16)

    cos_b, sa_b, sb_b = rb_ref[0], rb_ref[1], rb_ref[2]
    cq = _rms(proj(SLOT_CQ, 256), qn_ref[...], n=B_Q_LORA).astype(BF16)
    q = _dot(cq, wuq_ref[...])
    b_scale = (B_NOPE + B_ROPE) ** -0.5 * LOG2E
    for hd in range(B_HEADS):
        qh = _rope(q[:, hd * LANES:(hd + 1) * LANES], cos_b, sa_b, sb_b, B_ROPE // 4)
        bq_ref[0, hd] = (qh * b_scale).T.astype(BF16)
    ckr = proj(SLOT_CKV, B_KV_LORA + LANES)
    ckv = _rms(ckr[:, :B_KV_LORA], kvn_ref[...]).astype(BF16)
    kn = _dot(ckv, wuk_ref[...])
    vn = _dot(ckv, wuv_ref[...])
    kr = _rope(ckr[:, B_KV_LORA:], cos_b, sa_b, sb_b, B_ROPE // 4)
    for hd in range(B_HEADS):
        bk_ref[0, hd] = (kn[:, hd * LANES:(hd + 1) * LANES] + kr).astype(BF16)
        bv_ref[0, hd] = vn[:, hd * LANES:(hd + 1) * LANES].T[:B_V].astype(BF16)

    hq_ref[0] = proj(SLOT_HQ, HG_W).astype(BF16)
    hv_ref[0] = proj(SLOT_HV, HG_W).astype(BF16)
    hg_ref[0] = proj(SLOT_HG, HG_W).astype(BF16)
    for dr, slot in enumerate((SLOT_HF, SLOT_HB)):
        z = proj(slot, HG_W)
        lb = lb_ref[pl.ds(dr, 1), :]
        f = lb + (1.0 - lb) * jax.nn.sigmoid(z)
        hl_ref[dr, 0] = jnp.log(jnp.maximum(f, F_FLOOR))
        hk_ref[dr, 0] = ((1.0 - lb) * jax.nn.sigmoid(-z)).astype(BF16)

    u = proj(SLOT_S5, A_WIDTH)
    u_ref[0] = u.astype(BF16)
    for hf in range(A_WIDTH // LANES):
        u_scr[hf] = u[:, hf * LANES:(hf + 1) * LANES]
    for r in range(S5_R):
        for hf in range(A_WIDTH // LANES):
            lo = r * A_WIDTH + hf * LANES
            u2_ref[0, :, lo:lo + LANES] = u_scr[hf, pl.ds(r, tm // S5_R, stride=S5_R), :].astype(BF16)


def _inproj(x, mod, g_pre, wp, rope_b, rope_d, qn, wuq, kvn, wuk, wuv, lb, gqn, gkn):
    nb, s, d = x.shape
    tm = _pick(s, (512, 256))
    const2 = lambda b, i: (0, 0)
    tok = lambda w: pl.BlockSpec((1, tm, w), lambda b, i: (b, i, 0))
    headed = lambda nh, w: pl.BlockSpec((1, nh, tm, w), lambda b, i: (b, 0, i, 0))
    headed_t = lambda nh, w: pl.BlockSpec((1, nh, w, tm), lambda b, i: (b, 0, 0, i))
    dirtok = lambda w: pl.BlockSpec((2, 1, tm, w), lambda b, i: (0, b, i, 0))
    sd = jax.ShapeDtypeStruct
    outs = [
        (sd((nb, s, A_WIDTH), BF16), tok(A_WIDTH)),
        (sd((nb, s // S5_R, S5_LANES), BF16), pl.BlockSpec((1, tm // S5_R, S5_LANES), lambda b, i: (b, i, 0))),
        (sd((nb, B_HEADS, LANES, s), BF16), headed_t(B_HEADS, LANES)),
        (sd((nb, B_HEADS, s, LANES), BF16), headed(B_HEADS, LANES)),
        (sd((nb, B_HEADS, B_V, s), BF16), headed_t(B_HEADS, B_V)),
        (sd((nb, s, HG_W), BF16), tok(HG_W)),
        (sd((nb, s, HG_W), BF16), tok(HG_W)),
        (sd((2, nb, s, HG_W), BF16), dirtok(HG_W)),
        (sd((2, nb, s, HG_W), F32), dirtok(HG_W)),
        (sd((nb, s, HG_W), BF16), tok(HG_W)),
        (sd((nb, D_HEADS, D_HEAD, s), BF16), headed_t(D_HEADS, D_HEAD)),
        (sd((nb, D_KV_HEADS, s, D_HEAD), BF16), headed(D_KV_HEADS, D_HEAD)),
        (sd((nb, D_KV_HEADS, D_HEAD, s), BF16), headed_t(D_KV_HEADS, D_HEAD)),
    ]
    return pl.pallas_call(
        functools.partial(_inproj_kernel, tm=tm),
        grid=(nb, s // tm),
        in_specs=[
            pl.BlockSpec((1, tm, d), lambda b, i: (b, i, 0)),
            pl.BlockSpec((1, N_MOD, d), lambda b, i: (b, 0, 0)),
            pl.BlockSpec((1, d), const2),
            pl.BlockSpec((d, N_PROJ), const2),
            pl.BlockSpec((3, tm, LANES), lambda b, i: (0, i, 0)),
            pl.BlockSpec((3, tm, LANES), lambda b, i: (0, i, 0)),
            pl.BlockSpec((1, 256), const2),
            pl.BlockSpec((256, B_HEADS * LANES), const2),
            pl.BlockSpec((1, B_KV_LORA), const2),
            pl.BlockSpec((B_KV_LORA, B_HEADS * LANES), const2),
            pl.BlockSpec((B_KV_LORA, B_HEADS * LANES), const2),
            pl.BlockSpec((2, HG_W), const2),
            pl.BlockSpec((1, LANES), const2),
            pl.BlockSpec((1, LANES), const2),
        ],
        out_specs=[o[1] for o in outs],
        out_shape=[o[0] for o in outs],
        scratch_shapes=[pltpu.VMEM((tm, d), BF16), pltpu.VMEM((A_WIDTH // LANES, tm, LANES), F32)],
        compiler_params=_params(("parallel", "parallel")),
        name="mixer_inproj",
    )(x, mod, g_pre.reshape(1, d), wp, rope_b, rope_d, qn, wuq, kvn, wuk, wuv, lb, gqn, gkn)


def _s5_kernel(u2_ref, x0_ref, lag_ref, wst_ref, wout_ref, pq_ref, y_ref, xf_ref, carry, *, rows, reverse, nlev):
    i = pl.program_id(1)

    @pl.when(i == 0)
    def _():
        carry[...] = x0_ref[0]

    u2 = u2_ref[0]
    sloc = _dot(u2, wst_ref[...])
    ridx = lax.broadcasted_iota(jnp.int32, (rows, 1), 0)

    def cmul(xv, lev):
        p = pq_ref[pl.ds(2 * lev, 1), :]
        q = pq_ref[pl.ds(2 * lev + 1, 1), :]
        return xv * p + pltpu.roll(xv, S5_NSTATE, 1) * q

    if reverse:
        e = jnp.where(ridx == rows - 1, carry[...], pltpu.roll(sloc, rows - 1, 0))
    else:
        e = jnp.where(ridx == 0, carry[...], pltpu.roll(sloc, 1, 0))
    xin = e
    for lev in range(nlev):
        dist = 1 << lev
        if reverse:
            sh = jnp.where(ridx < rows - dist, pltpu.roll(xin, rows - dist, 0), 0.0)
        else:
            sh = jnp.where(ridx >= dist, pltpu.roll(xin, dist, 0), 0.0)
        xin = xin + cmul(sh, lev)

    last = 0 if reverse else rows - 1
    nxt = cmul(xin[last:last + 1, :], 0) + sloc[last:last + 1, :]
    carry[...] = nxt
    xf_ref[0] = nxt

    xb = xin.astype(BF16)
    for b in range(S5_R):
        yb = _dot(xb, wout_ref[b])
        for a in (range(b, S5_R) if reverse else range(b + 1)):
            yb = yb + _dot(u2[:, a * A_WIDTH:(a + 1) * A_WIDTH], lag_ref[abs(b - a)])
        for hf in range(A_WIDTH // LANES):
            y_ref[0, hf, pl.ds(b, rows, stride=S5_R), :] = yb[:, hf * LANES:(hf + 1) * LANES]


def _s5_scan(u2, x0, wts, reverse):
    lag, wst, wout, pq = wts
    nb, n2, _ = u2.shape
    rows = _pick(n2, (S5_MAX_ROWS, 256, 128, 64, 32))
    nt = n2 // rows
    once = pl.Buffered(1)
    nlev = max(1, (rows - 1).bit_length())
    order = (lambda b, i: (b, nt - 1 - i, 0)) if reverse else (lambda b, i: (b, i, 0))
    order_out = (lambda b, i: (b, 0, nt - 1 - i, 0)) if reverse else (lambda b, i: (b, 0, i, 0))
    const2 = lambda b, i: (0, 0)
    y, xf = pl.pallas_call(
        functools.partial(_s5_kernel, rows=rows, reverse=reverse, nlev=nlev),
        grid=(nb, nt),
        in_specs=[
            pl.BlockSpec((1, rows, S5_LANES), order),
            pl.BlockSpec((1, 1, 2 * S5_NSTATE), lambda b, i: (b, 0, 0)),
            pl.BlockSpec((S5_R, A_WIDTH, A_WIDTH), lambda b, i: (0, 0, 0), pipeline_mode=once),
            pl.BlockSpec((S5_LANES, 2 * S5_NSTATE), const2, pipeline_mode=once),
            pl.BlockSpec((S5_R, 2 * S5_NSTATE, A_WIDTH), lambda b, i: (0, 0, 0), pipeline_mode=once),
            pl.BlockSpec(pq.shape, const2),
        ],
        out_specs=[
            pl.BlockSpec((1, A_WIDTH // LANES, rows * S5_R, LANES), order_out),
            pl.BlockSpec((1, 1, 2 * S5_NSTATE), lambda b, i: (b, 0, 0)),
        ],
        out_shape=[
            jax.ShapeDtypeStruct((nb, A_WIDTH // LANES, n2 * S5_R, LANES), F32),
            jax.ShapeDtypeStruct((nb, 1, 2 * S5_NSTATE), F32),
        ],
        scratch_shapes=[pltpu.VMEM((1, 2 * S5_NSTATE), F32)],
        compiler_params=_params(("parallel", "arbitrary")),
        name="s5_scan_rev" if reverse else "s5_scan_fwd",
    )(u2, x0, lag, wst, wout, pq)
    return y, xf


def _s5_weights(lam_re, lam_im, log_dt, b_re, b_im, c_re, c_im, nlev):
    g, n, r = A_GROUPS, A_STATE, S5_R
    nl = lam_re.shape[0]
    lam_re = jnp.minimum(lam_re.astype(F32), -1e-4)
    lam_im = lam_im.astype(F32)
    dt = jnp.exp(log_dt.astype(F32))[..., None]
    mag = jnp.exp(lam_re * dt)
    a_re = mag * jnp.cos(lam_im * dt)
    a_im = mag * jnp.sin(lam_im * dt)
    den = lam_re * lam_re + lam_im * lam_im
    num_re = a_re - 1.0
    f_re = (num_re * lam_re + a_im * lam_im) / den
    f_im = (a_im * lam_re - num_re * lam_im) / den
    bb_re = f_re[..., None] * b_re - f_im[..., None] * b_im
    bb_im = f_re[..., None] * b_im + f_im[..., None] * b_re

    def cm(xr, xi, yr, yi):
        return xr * yr - xi * yi, xr * yi + xi * yr

    pr, pi = [jnp.ones_like(a_re)], [jnp.zeros_like(a_im)]
    for _ in range(r):
        nr, ni = cm(pr[-1], pi[-1], a_re, a_im)
        pr.append(nr)
        pi.append(ni)
    pr, pi = jnp.stack(pr), jnp.stack(pi)

    xr, xi = cm(pr[..., None], pi[..., None], bb_re, bb_im)
    yr, yi = cm(c_re, c_im, pr[:, :, :, :, None, :], pi[:, :, :, :, None, :])
    lag = jnp.einsum('ldgon,tldgnc->tldgoc', c_re, xr) - jnp.einsum('ldgon,tldgnc->tldgoc', c_im, xi)

    def group_diag(m):
        ni, nj = m.shape[-2:]
        wide = jnp.swapaxes(m, -3, -2).reshape(m.shape[:-3] + (1, ni, g * nj))
        keep = (np.arange(g)[:, None, None] == (np.arange(g * nj) // nj)[None, None, :])
        return jnp.where(keep, wide, 0.0).astype(BF16).reshape(m.shape[:-3] + (g * ni, g * nj))

    lag_bd = jnp.moveaxis(group_diag(jnp.swapaxes(lag[:r], -1, -2)), 0, 2)
    st_bd = jnp.concatenate([group_diag(jnp.swapaxes(xr, -1, -2)), group_diag(jnp.swapaxes(xi, -1, -2))], axis=-1)
    out_bd = jnp.concatenate([group_diag(jnp.swapaxes(yr, -1, -2)), group_diag(jnp.swapaxes(-yi, -1, -2))], axis=-2)

    tau_st = (r - 1 - np.arange(r), np.arange(r))
    tau_out = (np.arange(r) + 1, r - np.arange(r))

    def per_dir(fn):
        return jnp.stack([fn(0), fn(1)], axis=1)

    wst = per_dir(lambda dr: jnp.transpose(st_bd[tau_st[dr], :, dr], (1, 0, 2, 3))
                  .reshape(nl, r * A_WIDTH, 2 * g * n))
    wout = per_dir(lambda dr: jnp.transpose(out_bd[tau_out[dr], :, dr], (1, 0, 2, 3)))

    lr, li = pr[r].reshape(nl, 2, 1, g * n), pi[r].reshape(nl, 2, 1, g * n)
    rows = []
    for _ in range(nlev):
        rows.append(jnp.concatenate([lr, lr], axis=-1))
        rows.append(jnp.concatenate([-li, li], axis=-1))
        lr, li = cm(lr, li, lr, li)
    pq = jnp.concatenate(rows, axis=-2)
    return lag_bd, wst, wout, pq


def _split3(x):
    x1 = x.astype(BF16)
    r1 = x - x1.astype(F32)
    x2 = r1.astype(BF16)
    x3 = (r1 - x2.astype(F32)).astype(BF16)
    return x1, x2, x3


def _hgrn_chunks(chains, tri_ref, sel_ref, sgn_ref, msk_ref, heads, bd, nlev):
    t = chains[0][0].shape[0]

    def head_scores(qx, kx, half=None, upper=True):
        if half is not None:
            lo = half if upper else 0
            qx = jnp.concatenate([qx[b + lo:b + lo + half] for b in range(0, t, 2 * half)], axis=0)
        rows = qx.shape[0]
        qh = jnp.concatenate([jnp.where(heads[hd], qx, jnp.zeros_like(qx)) for hd in range(C_HEADS)], axis=0)
        s = _dot_nt(qh, kx)
        out = [s[hd * rows:(hd + 1) * rows, :] for hd in range(C_HEADS)]
        if half is None:
            return out
        zero = jnp.zeros((half, s.shape[1]), F32)
        full = []
        for sh in out:
            blocks = []
            for i in range(rows // half):
                piece = sh[i * half:(i + 1) * half]
                blocks += [zero, piece] if upper else [piece, zero]
            full.append(jnp.concatenate(blocks, axis=0))
        return full

    cums, tots = [], []
    for q, k, vb, lf, st, dr in chains:
        l1, l2, l3 = _split3(lf)
        tri = tri_ref[dr]
        cums.append(_dot(tri, l1) + _dot(tri, l2) + _dot(tri, l3))
        tots.append(jnp.sum(lf, axis=0, keepdims=True))
    mids = [_dot(sel_ref[c[5]], cum.astype(BF16)) for c, cum in zip(chains, cums)]

    atts = []
    for q, k, vb, lf, st, dr in chains:
        diag = msk_ref[dr, nlev] > 0.5
        atts.append([jnp.where(diag, s, 0.0) for s in head_scores(q.astype(BF16), k.astype(BF16))])
    for lev in range(nlev):
        for ci, (q, k, vb, lf, st, dr) in enumerate(chains):
            e = jnp.exp(sgn_ref[dr, lev] * (cums[ci] - mids[ci][lev * t:(lev + 1) * t, :]))
            m = msk_ref[dr, lev] > 0.5
            half = (1 << lev) if (1 << lev) >= HG_TRIM_MIN else None
            for hd, s in enumerate(head_scores((q * e).astype(BF16), (k * e).astype(BF16), half, upper=(dr == 0))):
                atts[ci][hd] = atts[ci][hd] + jnp.where(m, s, 0.0)

    outs = []
    for ci, (q, k, vb, lf, st, dr) in enumerate(chains):
        o = _dot_nt((q * jnp.exp(cums[ci])).astype(BF16), st.astype(BF16))
        att_wide = jnp.concatenate([a.astype(BF16) for a in atts[ci]], axis=1)
        v_heads = jnp.concatenate([jnp.where(heads[hd], vb, jnp.zeros_like(vb)) for hd in range(C_HEADS)], axis=0)
        o = o + _dot(att_wide, v_heads)
        kend = (k * jnp.exp(tots[ci] - cums[ci])).astype(BF16)
        vt = vb.astype(F32).T.astype(BF16)
        new = st * jnp.exp(tots[ci]) + jnp.where(bd, _dot(vt, kend), 0.0)
        outs.append((o, new))
    return outs


def _hgrn_kernel(qf_ref, qr_ref, kf_ref, kr_ref, vf_ref, vr_ref, lff_ref, lfr_ref, s0_ref,
                 tri_ref, sel_ref, sgn_ref, msk_ref, hm_ref, bd_ref,
                 of_ref, or_ref, sf_ref, st_scr, *, nlev, nb):
    c = pl.program_id(0)

    @pl.when(c == 0)
    def _():
        st_scr[...] = s0_ref[...]

    heads = [hm_ref[pl.ds(hd, 1), :] > 0.5 for hd in range(C_HEADS)]
    bd = bd_ref[...] > 0.5
    dirs = ((qf_ref, kf_ref, vf_ref, lff_ref, of_ref), (qr_ref, kr_ref, vr_ref, lfr_ref, or_ref))
    chains, sinks = [], []
    for dr, (q_ref, k_ref, v_ref, lf_ref, o_ref) in enumerate(dirs):
        for b in range(nb):
            chains.append((q_ref[b].astype(F32), k_ref[0, b].astype(F32), v_ref[b], lf_ref[0, b], st_scr[dr, b], dr))
            sinks.append((o_ref, dr, b))
    outs = _hgrn_chunks(chains, tri_ref, sel_ref, sgn_ref, msk_ref, heads, bd, nlev)
    for (o_ref, dr, b), (o, new) in zip(sinks, outs):
        o_ref[b] = o
        st_scr[dr, b] = new
        sf_ref[dr, b] = new


def _hgrn_consts():
    t = HG_T
    ti = jnp.arange(t)[:, None]
    si = jnp.arange(t)[None, :]
    tri, sel, sgn, msk = [], [], [], []
    for reverse in (False, True):
        tri.append((si >= ti) if reverse else (si <= ti))
        sels, sgns, msks = [], [], []
        for lev in range(HG_LEVELS):
            h = 1 << lev
            blk_t, blk_s = ti // (2 * h), si // (2 * h)
            hi_t, hi_s = (ti % (2 * h)) >= h, (si % (2 * h)) >= h
            if reverse:
                mid = blk_t * 2 * h + h
                q_role_t, k_role_s = ~hi_t, hi_s
            else:
                mid = blk_t * 2 * h + h - 1
                q_role_t, k_role_s = hi_t, ~hi_s
            sels.append(si == mid)
            sgns.append(jnp.where(q_role_t, 1.0, -1.0))
            msks.append((blk_t == blk_s) & q_role_t & k_role_s)
        msks.append(ti == si)
        sel.append(jnp.concatenate(sels, axis=0))
        sgn.append(jnp.stack(sgns))
        msk.append(jnp.stack(msks))
    lane_head = jnp.arange(HG_W) // C_DK
    hm = (lane_head[None, :] == jnp.arange(C_HEADS)[:, None]).astype(F32)
    bd = (lane_head[:, None] == lane_head[None, :]).astype(F32)
    return (jnp.stack(tri).astype(BF16), jnp.stack(sel).astype(BF16), jnp.stack(sgn).astype(F32),
            jnp.stack(msk).astype(F32), hm, bd)


def _hgrn_scan(q, k, v, lf, s0, consts):
    tri, sel, sgn, msk, hm, bd = consts
    nb, s, w = q.shape
    t = HG_T
    nc = s // t
    fwd3 = pl.BlockSpec((nb, t, w), lambda c: (0, c, 0))
    rev3 = pl.BlockSpec((nb, t, w), lambda c: (0, nc - 1 - c, 0))
    fwd4 = pl.BlockSpec((1, nb, t, w), lambda c: (0, 0, c, 0))
    rev4 = pl.BlockSpec((1, nb, t, w), lambda c: (1, 0, nc - 1 - c, 0))
    whole = lambda a: pl.BlockSpec(a.shape, lambda c: (0,) * a.ndim)
    sgn = sgn.reshape(2, HG_LEVELS, t, 1)
    o_f, o_r, sf = pl.pallas_call(
        functools.partial(_hgrn_kernel, nlev=HG_LEVELS, nb=nb),
        grid=(nc,),
        in_specs=[fwd3, rev3, fwd4, rev4, fwd3, rev3, fwd4, rev4, whole(s0),
                  whole(tri), whole(sel), whole(sgn), whole(msk), whole(hm), whole(bd)],
        out_specs=[fwd3, rev3, whole(s0)],
        out_shape=[
            jax.ShapeDtypeStruct((nb, s, w), F32),
            jax.ShapeDtypeStruct((nb, s, w), F32),
            jax.ShapeDtypeStruct((2, nb, w, w), F32),
        ],
        scratch_shapes=[pltpu.VMEM((2, nb, w, w), F32)],
        compiler_params=_params(("arbitrary",)),
        name="hgrn_scan",
    )(q, q, k, k, v, v, lf, lf, s0, tri, sel, sgn, msk, hm, bd)
    return (o_f, o_r), sf


def _attn_kernel(*refs, nseg, tks, g, tq):
    q_ref = refs[0]
    kv_refs = refs[1:1 + 2 * nseg]
    o_ref = refs[1 + 2 * nseg]
    dv = o_ref.shape[-1]
    n = g * tq
    qt = jnp.concatenate([q_ref[0, hd] for hd in range(g)], axis=-1)

    def scores(k):
        s = _dot(k, qt)
        sub = min(ATTN_SUB, s.shape[0])
        parts = [s[i:i + sub] for i in range(0, s.shape[0], sub)]
        return tuple(x for p in parts for x in (p, jnp.max(p, axis=0, keepdims=True)))

    def absorb(sm, vt, carry):
        m, acc = carry
        sub = sm[0].shape[0]
        for i in range(len(sm) // 2):
            s, smax = sm[2 * i], sm[2 * i + 1]
            m_new = jnp.maximum(m, smax)
            alpha = jnp.exp2(m - m_new)
            p = jnp.exp2(s - m_new).astype(BF16)
            vt1 = jnp.concatenate([vt[:, i * sub:(i + 1) * sub], jnp.ones((ATTN_ONES, sub), BF16)], axis=0)
            acc = alpha * acc + _dot(vt1, p)
            m = m_new
        return m, acc

    carry = (jnp.full((1, n), -jnp.inf, F32), jnp.zeros((dv + ATTN_ONES, n), F32))
    pending = None
    for seg in range(nseg):
        k_ref, vt_ref = kv_refs[2 * seg], kv_refs[2 * seg + 1]
        tk = tks[seg]
        nk = k_ref.shape[2] // tk
        s_first = scores(k_ref[0, 0, 0:tk, :])
        if pending is not None:
            carry = absorb(*pending, carry)
        if nk == 1:
            pending = (s_first, vt_ref[0, 0])
            continue

        def body(j, c, k_ref=k_ref, vt_ref=vt_ref, tk=tk, nk=nk):
            sm_cur, (m, acc) = c[:-2], c[-2:]
            off_next = pl.multiple_of(jnp.minimum(j + 1, nk - 1) * tk, tk)
            sm_next = scores(k_ref[0, 0, pl.ds(off_next, tk), :])
            off = pl.multiple_of(j * tk, tk)
            m, acc = absorb(sm_cur, vt_ref[0, 0, :, pl.ds(off, tk)], (m, acc))
            return sm_next + (m, acc)

        unroll = ATTN_UNROLL if nk % ATTN_UNROLL == 0 else 1
        m, acc = lax.fori_loop(0, nk, body, s_first + carry, unroll=unroll)[-2:]
        carry = (m, acc)
        pending = None
    m, acc = carry if pending is None else absorb(*pending, carry)
    out = acc[:dv] / acc[dv:dv + 1, :]
    out = jnp.concatenate([out, jnp.zeros((LANES - dv, n), F32)], axis=0).T
    o_ref[0] = out[:, :dv].reshape(g, tq, dv).astype(o_ref.dtype)


def _attention(q, kvs):
    nb, hq, dk, sq = q.shape
    hkv = kvs[0][0].shape[1]
    dv = kvs[0][1].shape[2]
    g = hq // hkv
    tq = _pick(sq, tuple(n // g for n in ATTN_QUERIES))
    tks = tuple(_pick(k.shape[2], (ATTN_SUB, 128)) for k, _ in kvs)
    in_specs = [pl.BlockSpec((1, g, dk, tq), lambda b, h, i: (b, h, 0, i))]
    args = [q]
    for k, v in kvs:
        sk = k.shape[2]
        in_specs.append(pl.BlockSpec((1, 1, sk, dk), lambda b, h, i: (b, h, 0, 0)))
        in_specs.append(pl.BlockSpec((1, 1, dv, sk), lambda b, h, i: (b, h, 0, 0)))
        args += [k, v]
    return pl.pallas_call(
        functools.partial(_attn_kernel, nseg=len(kvs), tks=tks, g=g, tq=tq),
        grid=(nb, hkv, sq // tq),
        in_specs=in_specs,
        out_specs=pl.BlockSpec((1, g, tq, dv), lambda b, h, i: (b, h, i, 0)),
        out_shape=jax.ShapeDtypeStruct((nb, hq, sq, dv), BF16),
        compiler_params=_params(("parallel", "parallel", "arbitrary")),
        name="attention",
    )(*args)


def _merge_kernel(x_ref, mod_ref, gpre_ref, gpost_ref, wg_ref, wb_ref, wo_ref,
                  u_ref, yf_ref, yr_ref, sd_ref, wglu_ref, bo_ref, hof_ref, hor_ref, hg_ref, hn_ref, avg_ref, do_ref,
                  o_ref):
    x = x_ref[0]
    shift = mod_ref[0, pl.ds(3, 1), :]
    scale = mod_ref[0, pl.ds(4, 1), :]
    gate = mod_ref[0, pl.ds(5, 1), :]
    h = (_rms(x, gpre_ref[...]) * (1.0 + scale) + shift).astype(BF16)
    d = x.shape[-1]

    def branch_gate(i):
        return jax.nn.sigmoid(_dot(h, wg_ref[:, i * d:(i + 1) * d]))

    ysum = yf_ref[0] + yr_ref[0]
    y = sd_ref[...] * u_ref[0].astype(F32) + jnp.concatenate([ysum[hf] for hf in range(A_WIDTH // LANES)], axis=-1)
    ge = jax.nn.gelu(y)
    ya = ge * jax.nn.sigmoid(_dot(ge.astype(BF16), wglu_ref[...]))
    merged = branch_gate(0) * _dot(ya.astype(BF16), wb_ref[0])

    yb = jnp.concatenate([bo_ref[0, hd] for hd in range(B_HEADS)], axis=-1)
    merged = merged + branch_gate(1) * _dot(yb, wb_ref[1])

    o2 = hof_ref[0] + hor_ref[0]
    ms = _dot((o2 * o2).astype(BF16), avg_ref[...])
    gz = hg_ref[0].astype(F32)
    yc = o2 * lax.rsqrt(ms + EPS) * hn_ref[...] * (gz * jax.nn.sigmoid(gz))
    merged = merged + branch_gate(2) * _dot(yc.astype(BF16), wb_ref[2])

    yd = jnp.concatenate([do_ref[0, hd] for hd in range(D_HEADS)], axis=-1)
    merged = merged + branch_gate(3) * _dot(yd, wb_ref[3])

    yo = _dot(merged.astype(BF16), wo_ref[...])
    o_ref[0] = x + gate * _rms(yo, gpost_ref[...])


def _merge(x, mod, g_pre, g_post, wg, wb, wo, u, yf, yr, s5d, wglu, bo, ho, hg, hn, avg, do):
    nb, s, d = x.shape
    tm = _pick(s, (512, 256))
    const2 = lambda b, i: (0, 0)
    const3 = lambda b, i: (0, 0, 0)
    tok = lambda w: pl.BlockSpec((1, tm, w), lambda b, i: (b, i, 0))
    halves = pl.BlockSpec((1, A_WIDTH // LANES, tm, LANES), lambda b, i: (b, 0, i, 0))
    return pl.pallas_call(
        _merge_kernel,
        grid=(nb, s // tm),
        in_specs=[
            tok(d),
            pl.BlockSpec((1, N_MOD, d), lambda b, i: (b, 0, 0)),
            pl.BlockSpec((1, d), const2),
            pl.BlockSpec((1, d), const2),
            pl.BlockSpec((d, N_BRANCH * d), const2),
            pl.BlockSpec((N_BRANCH, BRANCH_W, d), const3),
            pl.BlockSpec((d, d), const2),
            tok(A_WIDTH), halves, halves,
            pl.BlockSpec((1, A_WIDTH), const2),
            pl.BlockSpec((A_WIDTH, A_WIDTH), const2),
            pl.BlockSpec((1, B_HEADS, tm, B_V), lambda b, i: (b, 0, i, 0)),
            tok(HG_W), tok(HG_W),
            tok(HG_W),
            pl.BlockSpec((1, HG_W), const2),
            pl.BlockSpec((HG_W, HG_W), const2),
            pl.BlockSpec((1, D_HEADS, tm, D_HEAD), lambda b, i: (b, 0, i, 0)),
        ],
        out_specs=tok(d),
        out_shape=jax.ShapeDtypeStruct(x.shape, F32),
        compiler_params=_params(("parallel", "parallel")),
        name="merge_out",
    )(x, mod, g_pre.reshape(1, d), g_post.reshape(1, d), wg, wb, wo, u, yf, yr, s5d, wglu, bo, ho[0], ho[1], hg, hn,
      avg, do)


def _pad_cols(w, width):
    return jnp.pad(w, ((0, 0), (0, width - w.shape[1])))


def _proj_weight(w_in_mix):
    offs = [0]
    for wdt in (A_WIDTH, B_Q_LORA, B_KV_LORA, B_ROPE, HG_W, HG_W, HG_W, HG_W, HG_W,
                D_HEADS * D_HEAD, D_KV_HEADS * D_HEAD, D_KV_HEADS * D_HEAD):
        offs.append(offs[-1] + wdt)
    p = [w_in_mix[:, offs[i]:offs[i + 1]] for i in range(12)]
    d = w_in_mix.shape[0]
    z = lambda n: jnp.zeros((d, n), w_in_mix.dtype)
    cols = [p[0], _pad_cols(p[1], 256), p[2],
            jnp.concatenate([z(B_NOPE), p[3], z(LANES - B_NOPE - B_ROPE)], axis=1),
            p[4], p[5], p[6], p[7], p[8]]
    for i, nh in ((9, D_HEADS), (10, D_KV_HEADS), (11, D_KV_HEADS)):
        for hd in range(nh):
            cols.append(_pad_cols(p[i][:, hd * D_HEAD:(hd + 1) * D_HEAD], LANES))
    return jnp.concatenate(cols, axis=1).astype(BF16)


def _rope_tables(n_tok, rot_dim, lane_off, identity):
    cos = jnp.ones((n_tok, LANES), F32)
    sin_a = jnp.zeros((n_tok, LANES), F32)
    sin_b = jnp.zeros((n_tok, LANES), F32)
    if not identity:
        n_rows = n_tok // GRID_W
        rows = jnp.repeat(jnp.arange(n_rows, dtype=F32), GRID_W)
        cols = jnp.tile(jnp.arange(GRID_W, dtype=F32), n_rows)
        half = rot_dim // 2
        inv = ROPE_THETA ** (-jnp.arange(0, half, 2, dtype=F32) / half)
        ang_r = rows[:, None] * inv
        ang_c = cols[:, None] * inv
        ang = jnp.concatenate([ang_r, ang_r, ang_c, ang_c], axis=-1)
        c, s = jnp.cos(ang), jnp.sin(ang)
        quarter = rot_dim // 4
        first = (np.arange(rot_dim) % (2 * quarter)) < quarter

        def place(a, fill):
            left = jnp.full((n_tok, lane_off), fill, F32)
            right = jnp.full((n_tok, LANES - lane_off - rot_dim), fill, F32)
            return jnp.concatenate([left, a, right], axis=1)

        cos = place(c, 1.0)
        sin_a = place(jnp.where(first, -s, 0.0), 0.0)
        sin_b = place(jnp.where(first, 0.0, s), 0.0)
    return jnp.stack([cos, sin_a, sin_b])


def kernel(x, c, ctx, c_ctx, w_ada, b_ada, norm_pre, norm_post, ffn_w1, ffn_w3, ffn_w2, w_in,
           s5_lambda_re, s5_lambda_im, s5_log_dt, s5_b_re, s5_b_im, s5_c_re, s5_c_im, s5_d, s5_w_glu,
           mla_q_norm, mla_w_uq, mla_kv_norm, mla_w_ukv, hgrn_lb_raw, hgrn_o_norm,
           gqa_q_norm, gqa_k_norm, w_branch, w_out):
    nb, seq, d = x.shape
    n_ctx = ctx.shape[1]
    depth = w_ada.shape[0]

    rows = max(8, -(-(nb + 1) // 8) * 8)
    cvec = jnp.zeros((rows, d), F32).at[:nb].set(c).at[nb].set(c_ctx)
    mod_all = _modulation(cvec, w_ada, b_ada).reshape(depth, rows, N_MOD, d)

    lb_step = jax.nn.softmax(hgrn_lb_raw.astype(F32), axis=1)
    lb_all = jnp.clip(jnp.cumsum(lb_step, axis=1) - lb_step[:, :1], 0.0, 1.0)

    rope_b_lat = _rope_tables(seq, B_ROPE, B_NOPE, False)
    rope_d_lat = _rope_tables(seq, D_HEAD, 0, False)
    rope_b_ctx = _rope_tables(n_ctx, B_ROPE, B_NOPE, True)
    rope_d_ctx = _rope_tables(n_ctx, D_HEAD, 0, True)
    hg_consts = _hgrn_consts()
    s5_w = _s5_weights(s5_lambda_re, s5_lambda_im, s5_log_dt, s5_b_re, s5_b_im, s5_c_re, s5_c_im, S5_MAX_LEVELS)
    lane_head = jnp.arange(HG_W) // C_DV
    avg = (lane_head[:, None] == lane_head[None, :]).astype(BF16) * (1.0 / C_DV)
    n_mixcols = w_in.shape[-1] - N_BRANCH * d
    ffn_w = (ffn_w1.astype(BF16), ffn_w3.astype(BF16), ffn_w2.astype(BF16))

    x_lat, x_ctx = x, ctx
    for layer in range(depth):
        last = layer == depth - 1
        mod_lat = mod_all[layer, :nb]
        mod_ctx = jnp.broadcast_to(mod_all[layer, nb:nb + 1], (nb, N_MOD, d))
        bf = lambda w: w.astype(BF16)
        ffn_a = (norm_pre[layer, 0], norm_post[layer, 0], *ffn_w, layer, 0)
        ffn_b = (norm_pre[layer, 2], norm_post[layer, 2], *ffn_w, layer, 1)

        x_lat = _ffn(x_lat, mod_lat, 0, *ffn_a)
        x_ctx = _ffn(x_ctx, mod_ctx, 0, *ffn_a)

        wp = _proj_weight(w_in[layer, :, :n_mixcols])
        wg = bf(w_in[layer, :, n_mixcols:])
        qn = _pad_cols(mla_q_norm[layer].reshape(1, B_Q_LORA), 256)
        wuq = mla_w_uq[layer].reshape(B_Q_LORA, B_HEADS, B_NOPE + B_ROPE)
        wuq = jnp.pad(wuq, ((0, 256 - B_Q_LORA), (0, 0), (0, LANES - B_NOPE - B_ROPE))).reshape(256, B_HEADS * LANES)
        wukv = mla_w_ukv[layer].reshape(B_KV_LORA, B_HEADS, B_NOPE + B_V)
        wuk = jnp.pad(wukv[:, :, :B_NOPE], ((0, 0), (0, 0), (0, LANES - B_NOPE))).reshape(B_KV_LORA, B_HEADS * LANES)
        wuv = jnp.pad(wukv[:, :, B_NOPE:], ((0, 0), (0, 0), (0, LANES - B_V))).reshape(B_KV_LORA, B_HEADS * LANES)
        kvn = mla_kv_norm[layer].reshape(1, B_KV_LORA)
        gqn = _pad_cols(gqa_q_norm[layer].reshape(1, D_HEAD), LANES)
        gkn = _pad_cols(gqa_k_norm[layer].reshape(1, D_HEAD), LANES)
        lb = lb_all[:, layer]
        proj_args = (norm_pre[layer, 1], wp)
        mla_args = (qn, bf(wuq), kvn, bf(wuk), bf(wuv), lb, gqn, gkn)

        pl_ = _inproj(x_lat, mod_lat, *proj_args, rope_b_lat, rope_d_lat, *mla_args)
        pc_ = _inproj(x_ctx, mod_ctx, *proj_args, rope_b_ctx, rope_d_ctx, *mla_args)
        (u_l, u2_l, bq_l, bk_l, bv_l, hq_l, hv_l, hk_l, hl_l, hg_l, dq_l, dk_l, dv_l) = pl_
        (u_c, u2_c, bq_c, bk_c, bv_c, hq_c, hv_c, hk_c, hl_c, hg_c, dq_c, dk_c, dv_c) = pc_

        ys_l, ys_c = [], []
        for dr, reverse in enumerate((False, True)):
            wts = tuple(w[layer, dr] for w in s5_w)
            x0 = jnp.zeros((nb, 1, 2 * S5_NSTATE), F32)
            y_c, x_end = _s5_scan(u2_c, x0, wts, reverse)
            y_l, _ = _s5_scan(u2_l, x_end, wts, reverse)
            ys_l.append(y_l)
            ys_c.append(y_c)

        s0 = jnp.zeros((2, nb, HG_W, HG_W), F32)
        ho_c, s_ctx = _hgrn_scan(hq_c, hk_c, hv_c, hl_c, s0, hg_consts)
        ho_l, _ = _hgrn_scan(hq_l, hk_l, hv_l, hl_l, s_ctx, hg_consts)

        bo_l = _attention(bq_l, [(bk_c, bv_c), (bk_l, bv_l)])
        do_l = _attention(dq_l, [(dk_c, dv_c), (dk_l, dv_l)])

        merge_w = (norm_pre[layer, 1], norm_post[layer, 1], wg, bf(w_branch[layer]), bf(w_out[layer]))
        s5_ro = (s5_d[layer].reshape(1, A_WIDTH), bf(s5_w_glu[layer]))
        hn = jnp.tile(hgrn_o_norm[layer], C_HEADS).reshape(1, HG_W)
        x_lat_new = _merge(x_lat, mod_lat, *merge_w, u_l, ys_l[0], ys_l[1], *s5_ro, bo_l, ho_l, hg_l, hn, avg, do_l)
        if not last:
            bo_c = _attention(bq_c, [(bk_c, bv_c)])
            do_c = _attention(dq_c, [(dk_c, dv_c)])
            x_ctx = _merge(x_ctx, mod_ctx, *merge_w, u_c, ys_c[0], ys_c[1], *s5_ro, bo_c, ho_c, hg_c, hn, avg, do_c)
            x_ctx = _ffn(x_ctx, mod_ctx, 2, *ffn_b)
        x_lat = _ffn(x_lat_new, mod_lat, 2, *ffn_b)
    return x_lat
```

```python
import functools

import jax
import jax.numpy as jnp
import numpy as np
from jax import lax
from jax.experimental import pallas as pl
from jax.experimental.pallas import tpu as pltpu

GRID_W = 64
FFN_RES_WEIGHT = 0.5
N_MOD = 9
EPS = 1e-6
ROPE_THETA = 10000.0
F_FLOOR = 1e-20

A_WIDTH = 256
A_GROUP = 16
A_GROUPS = A_WIDTH // A_GROUP
A_STATE = 64

B_HEADS = 4
B_NOPE = 64
B_ROPE = 32
B_V = 64
B_Q_LORA = 192
B_KV_LORA = 128

C_HEADS = 4
C_DK = 64
C_DV = 64

D_HEADS = 4
D_KV_HEADS = 2
D_HEAD = 64

N_BRANCH = 4
BRANCH_W = 256

LANES = 128
VMEM_LIMIT_BYTES = 56 * 1024 * 1024

S5_R = 8
S5_LANES = S5_R * A_WIDTH
S5_NSTATE = A_GROUPS * A_STATE
S5_MAX_ROWS = 512
S5_MAX_LEVELS = 9
HG_T = 128
HG_LEVELS = 7
HG_W = C_HEADS * C_DK
HG_TRIM_MIN = 8
ATTN_ONES = 16
ATTN_QUERIES = (1024, 512, 256, 128)
ATTN_SUB = 256
ATTN_UNROLL = 32

SLOT_S5 = 0
SLOT_CQ = 256
SLOT_CKV = 512
SLOT_KR = 640
SLOT_HQ = 768
SLOT_HV = 1024
SLOT_HF = 1280
SLOT_HB = 1536
SLOT_HG = 1792
SLOT_GQ = 2048
SLOT_GK = 2560
SLOT_GV = 2816
N_PROJ = 3072

BF16 = jnp.bfloat16
F32 = jnp.float32
LOG2E = 1.4426950408889634


def _params(sem, flags=None):
    return pltpu.CompilerParams(dimension_semantics=sem, vmem_limit_bytes=VMEM_LIMIT_BYTES, flags=flags)


def _pick(n, candidates):
    for c in candidates:
        if n % c == 0:
            return c
    raise ValueError(f"no tile for {n} in {candidates}")


def _dot(a, b):
    return jnp.dot(a, b, preferred_element_type=F32)


def _dot_nt(a, b):
    return lax.dot_general(a, b, (((1,), (1,)), ((), ())), preferred_element_type=F32)


def _rms(x, g, n=None):
    n = x.shape[-1] if n is None else n
    ms = jnp.sum(x * x, axis=-1, keepdims=True) * (1.0 / n)
    return x * lax.rsqrt(ms + EPS) * g


def _mod_kernel(c_ref, w_ref, b_ref, o_ref):
    c = c_ref[...]
    a = (c * jax.nn.sigmoid(c)).astype(BF16)
    o_ref[0] = _dot(a, w_ref[0].astype(BF16)) + b_ref[0]


def _modulation(cvec, w_ada, b_ada):
    nl, d, nm = w_ada.shape
    rows = cvec.shape[0]
    tn = _pick(nm, (1152, 1024, 512, 256, 128))
    return pl.pallas_call(
        _mod_kernel,
        grid=(nl, nm // tn),
        in_specs=[
            pl.BlockSpec((rows, d), lambda l, n: (0, 0)),
            pl.BlockSpec((1, d, tn), lambda l, n: (l, 0, n)),
            pl.BlockSpec((1, 1, tn), lambda l, n: (l, 0, n)),
        ],
        out_specs=pl.BlockSpec((1, rows, tn), lambda l, n: (l, 0, n)),
        out_shape=jax.ShapeDtypeStruct((nl, rows, nm), F32),
        compiler_params=_params(("parallel", "parallel")),
        name="adaln_mod",
    )(cvec, w_ada, b_ada.reshape(nl, 1, nm))


def _ffn_kernel(x_ref, mod_ref, gpre_ref, gpost_ref, w1_ref, w3_ref, w2_ref, o_ref, *, j, tf):
    x = x_ref[0]
    shift = mod_ref[0, pl.ds(3 * j, 1), :]
    scale = mod_ref[0, pl.ds(3 * j + 1, 1), :]
    gate = mod_ref[0, pl.ds(3 * j + 2, 1), :]
    h = (_rms(x, gpre_ref[...]) * (1.0 + scale) + shift).astype(BF16)
    acc = None
    for f in range(w1_ref.shape[1] // tf):
        a = _dot(h, w1_ref[:, f * tf:(f + 1) * tf])
        b = _dot(h, w3_ref[:, f * tf:(f + 1) * tf])
        t = (a * jax.nn.sigmoid(a) * b).astype(BF16)
        part = _dot(t, w2_ref[f * tf:(f + 1) * tf, :])
        acc = part if acc is None else acc + part
    o_ref[0] = x + FFN_RES_WEIGHT * gate * _rms(acc, gpost_ref[...])


def _ffn(x, mod, j, g_pre, g_post, w1, w3, w2, layer, which):
    nb, s, d = x.shape
    dff = w1.shape[-1]
    tm = _pick(s, (512, 256))
    tf = _pick(dff, (256, 128))
    once = pl.Buffered(1)
    pick = lambda b, i: (layer, which, 0, 0)
    return pl.pallas_call(
        functools.partial(_ffn_kernel, j=j, tf=tf),
        grid=(nb, s // tm),
        in_specs=[
            pl.BlockSpec((1, tm, d), lambda b, i: (b, i, 0)),
            pl.BlockSpec((1, N_MOD, d), lambda b, i: (b, 0, 0)),
            pl.BlockSpec((1, d), lambda b, i: (0, 0)),
            pl.BlockSpec((1, d), lambda b, i: (0, 0)),
            pl.BlockSpec((None, None, d, dff), pick, pipeline_mode=once),
            pl.BlockSpec((None, None, d, dff), pick, pipeline_mode=once),
            pl.BlockSpec((None, None, dff, d), pick, pipeline_mode=once),
        ],
        out_specs=pl.BlockSpec((1, tm, d), lambda b, i: (b, i, 0)),
        out_shape=jax.ShapeDtypeStruct(x.shape, F32),
        compiler_params=_params(("parallel", "parallel")),
        name="ffn_sublayer",
    )(x, mod, g_pre.reshape(1, d), g_post.reshape(1, d), w1, w3, w2)


def _rope(x, cos, sin_a, sin_b, quarter):
    w = x.shape[-1]
    return x * cos + pltpu.roll(x, w - quarter, 1) * sin_a + pltpu.roll(x, quarter, 1) * sin_b


def _inproj_kernel(x_ref, mod_ref, gpre_ref, w_ref, rb_ref, rd_ref, qn_ref, wuq_ref, kvn_ref, wuk_ref, wuv_ref,
                   lb_ref, gqn_ref, gkn_ref,
                   u_ref, u2_ref, bq_ref, bk_ref, bv_ref, hq_ref, hv_ref, hk_ref, hl_ref, hg_ref,
                   dq_ref, dk_ref, dv_ref, h_scr, u_scr, *, tm):
    x = x_ref[0]
    shift = mod_ref[0, pl.ds(3, 1), :]
    scale = mod_ref[0, pl.ds(4, 1), :]
    h_scr[...] = (_rms(x, gpre_ref[...]) * (1.0 + scale) + shift).astype(BF16)

    def proj(lo, width):
        return _dot(h_scr[...], w_ref[:, lo:lo + width])

    cos_d, sa_d, sb_d = rd_ref[0], rd_ref[1], rd_ref[2]
    d_scale = D_HEAD ** -0.5 * LOG2E
    gq = proj(SLOT_GQ, D_HEADS * LANES)
    for hd in range(D_HEADS):
        qh = _rms(gq[:, hd * LANES:(hd + 1) * LANES], gqn_ref[...], n=D_HEAD)
        qh = _rope(qh, cos_d, sa_d, sb_d, D_HEAD // 4) * d_scale
        dq_ref[0, hd] = qh.T[:D_HEAD].astype(BF16)
    gkv = proj(SLOT_GK, 2 * D_KV_HEADS * LANES)
    for hd in range(D_KV_HEADS):
        kh = _rms(gkv[:, hd * LANES:(hd + 1) * LANES], gkn_ref[...], n=D_HEAD)
        kh = _rope(kh, cos_d, sa_d, sb_d, D_HEAD // 4)
        dk_ref[0, hd] = kh[:, :D_HEAD].astype(BF16)
        dv_ref[0, hd] = gkv[:, (D_KV_HEADS + hd) * LANES:(D_KV_HEADS + hd + 1) * LANES].T[:D_HEAD].astype(BF16)

    cos_b, sa_b, sb_b = rb_ref[0], rb_ref[1], rb_ref[2]
    cq = _rms(proj(SLOT_CQ, 256), qn_ref[...], n=B_Q_LORA).astype(BF16)
    q = _dot(cq, wuq_ref[...])
    b_scale = (B_NOPE + B_ROPE) ** -0.5 * LOG2E
    for hd in range(B_HEADS):
        qh = _rope(q[:, hd * LANES:(hd + 1) * LANES], cos_b, sa_b, sb_b, B_ROPE // 4)
        bq_ref[0, hd] = (qh * b_scale).T.astype(BF16)
    ckr = proj(SLOT_CKV, B_KV_LORA + LANES)
    ckv = _rms(ckr[:, :B_KV_LORA], kvn_ref[...]).astype(BF16)
    kn = _dot(ckv, wuk_ref[...])
    vn = _dot(ckv, wuv_ref[...])
    kr = _rope(ckr[:, B_KV_LORA:], cos_b, sa_b, sb_b, B_ROPE // 4)
    for hd in range(B_HEADS):
        bk_ref[0, hd] = (kn[:, hd * LANES:(hd + 1) * LANES] + kr).astype(BF16)
        bv_ref[0, hd] = vn[:, hd * LANES:(hd + 1) * LANES].T[:B_V].astype(BF16)

    hq_ref[0] = proj(SLOT_HQ, HG_W).astype(BF16)
    hv_ref[0] = proj(SLOT_HV, HG_W).astype(BF16)
    hg_ref[0] = proj(SLOT_HG, HG_W).astype(BF16)
    for dr, slot in enumerate((SLOT_HF, SLOT_HB)):
        z = proj(slot, HG_W)
        lb = lb_ref[pl.ds(dr, 1), :]
        f = lb + (1.0 - lb) * jax.nn.sigmoid(z)
        hl_ref[dr, 0] = jnp.log(jnp.maximum(f, F_FLOOR))
        hk_ref[dr, 0] = ((1.0 - lb) * jax.nn.sigmoid(-z)).astype(BF16)

    u = proj(SLOT_S5, A_WIDTH)
    u_ref[0] = u.astype(BF16)
    for hf in range(A_WIDTH // LANES):
        u_scr[hf] = u[:, hf * LANES:(hf + 1) * LANES]
    for r in range(S5_R):
        for hf in range(A_WIDTH // LANES):
            lo = r * A_WIDTH + hf * LANES
            u2_ref[0, :, lo:lo + LANES] = u_scr[hf, pl.ds(r, tm // S5_R, stride=S5_R), :].astype(BF16)


def _inproj(x, mod, g_pre, wp, rope_b, rope_d, qn, wuq, kvn, wuk, wuv, lb, gqn, gkn):
    nb, s, d = x.shape
    tm = _pick(s, (512, 256))
    const2 = lambda b, i: (0, 0)
    tok = lambda w: pl.BlockSpec((1, tm, w), lambda b, i: (b, i, 0))
    headed = lambda nh, w: pl.BlockSpec((1, nh, tm, w), lambda b, i: (b, 0, i, 0))
    headed_t = lambda nh, w: pl.BlockSpec((1, nh, w, tm), lambda b, i: (b, 0, 0, i))
    dirtok = lambda w: pl.BlockSpec((2, 1, tm, w), lambda b, i: (0, b, i, 0))
    sd = jax.ShapeDtypeStruct
    outs = [
        (sd((nb, s, A_WIDTH), BF16), tok(A_WIDTH)),
        (sd((nb, s // S5_R, S5_LANES), BF16), pl.BlockSpec((1, tm // S5_R, S5_LANES), lambda b, i: (b, i, 0))),
        (sd((nb, B_HEADS, LANES, s), BF16), headed_t(B_HEADS, LANES)),
        (sd((nb, B_HEADS, s, LANES), BF16), headed(B_HEADS, LANES)),
        (sd((nb, B_HEADS, B_V, s), BF16), headed_t(B_HEADS, B_V)),
        (sd((nb, s, HG_W), BF16), tok(HG_W)),
        (sd((nb, s, HG_W), BF16), tok(HG_W)),
        (sd((2, nb, s, HG_W), BF16), dirtok(HG_W)),
        (sd((2, nb, s, HG_W), F32), dirtok(HG_W)),
        (sd((nb, s, HG_W), BF16), tok(HG_W)),
        (sd((nb, D_HEADS, D_HEAD, s), BF16), headed_t(D_HEADS, D_HEAD)),
        (sd((nb, D_KV_HEADS, s, D_HEAD), BF16), headed(D_KV_HEADS, D_HEAD)),
        (sd((nb, D_KV_HEADS, D_HEAD, s), BF16), headed_t(D_KV_HEADS, D_HEAD)),
    ]
    return pl.pallas_call(
        functools.partial(_inproj_kernel, tm=tm),
        grid=(nb, s // tm),
        in_specs=[
            pl.BlockSpec((1, tm, d), lambda b, i: (b, i, 0)),
            pl.BlockSpec((1, N_MOD, d), lambda b, i: (b, 0, 0)),
            pl.BlockSpec((1, d), const2),
            pl.BlockSpec((d, N_PROJ), const2),
            pl.BlockSpec((3, tm, LANES), lambda b, i: (0, i, 0)),
            pl.BlockSpec((3, tm, LANES), lambda b, i: (0, i, 0)),
            pl.BlockSpec((1, 256), const2),
            pl.BlockSpec((256, B_HEADS * LANES), const2),
            pl.BlockSpec((1, B_KV_LORA), const2),
            pl.BlockSpec((B_KV_LORA, B_HEADS * LANES), const2),
            pl.BlockSpec((B_KV_LORA, B_HEADS * LANES), const2),
            pl.BlockSpec((2, HG_W), const2),
            pl.BlockSpec((1, LANES), const2),
            pl.BlockSpec((1, LANES), const2),
        ],
        out_specs=[o[1] for o in outs],
        out_shape=[o[0] for o in outs],
        scratch_shapes=[pltpu.VMEM((tm, d), BF16), pltpu.VMEM((A_WIDTH // LANES, tm, LANES), F32)],
        compiler_params=_params(("parallel", "parallel")),
        name="mixer_inproj",
    )(x, mod, g_pre.reshape(1, d), wp, rope_b, rope_d, qn, wuq, kvn, wuk, wuv, lb, gqn, gkn)


def _s5_kernel(u2_ref, x0_ref, lag_ref, wst_ref, wout_ref, pq_ref, y_ref, xf_ref, carry, *, rows, reverse, nlev):
    i = pl.program_id(1)

    @pl.when(i == 0)
    def _():
        carry[...] = x0_ref[0]

    u2 = u2_ref[0]
    sloc = _dot(u2, wst_ref[...])
    y_local = []
    for b in range(S5_R):
        acc = None
        for a in (range(b, S5_R) if reverse else range(b + 1)):
            part = _dot(u2[:, a * A_WIDTH:(a + 1) * A_WIDTH], lag_ref[abs(b - a)])
            acc = part if acc is None else acc + part
        y_local.append(acc)
    ridx = lax.broadcasted_iota(jnp.int32, (rows, 1), 0)

    def cmul(xv, lev):
        p = pq_ref[pl.ds(2 * lev, 1), :]
        q = pq_ref[pl.ds(2 * lev + 1, 1), :]
        return xv * p + pltpu.roll(xv, S5_NSTATE, 1) * q

    if reverse:
        e = jnp.where(ridx == rows - 1, carry[...], pltpu.roll(sloc, rows - 1, 0))
    else:
        e = jnp.where(ridx == 0, carry[...], pltpu.roll(sloc, 1, 0))
    xin = e
    for lev in range(nlev):
        dist = 1 << lev
        if reverse:
            sh = jnp.where(ridx < rows - dist, pltpu.roll(xin, rows - dist, 0), 0.0)
        else:
            sh = jnp.where(ridx >= dist, pltpu.roll(xin, dist, 0), 0.0)
        xin = xin + cmul(sh, lev)

    last = 0 if reverse else rows - 1
    nxt = cmul(xin[last:last + 1, :], 0) + sloc[last:last + 1, :]
    carry[...] = nxt
    xf_ref[0] = nxt

    xb = xin.astype(BF16)
    for b in range(S5_R):
        yb = y_local[b] + _dot(xb, wout_ref[b])
        for hf in range(A_WIDTH // LANES):
            y_ref[0, hf, pl.ds(b, rows, stride=S5_R), :] = yb[:, hf * LANES:(hf + 1) * LANES]


def _s5_scan(u2, x0, wts, reverse):
    lag, wst, wout, pq = wts
    nb, n2, _ = u2.shape
    rows = _pick(n2, (S5_MAX_ROWS, 256, 128, 64, 32))
    nt = n2 // rows
    once = pl.Buffered(1)
    nlev = max(1, (rows - 1).bit_length())
    order = (lambda b, i: (b, nt - 1 - i, 0)) if reverse else (lambda b, i: (b, i, 0))
    order_out = (lambda b, i: (b, 0, nt - 1 - i, 0)) if reverse else (lambda b, i: (b, 0, i, 0))
    const2 = lambda b, i: (0, 0)
    y, xf = pl.pallas_call(
        functools.partial(_s5_kernel, rows=rows, reverse=reverse, nlev=nlev),
        grid=(nb, nt),
        in_specs=[
            pl.BlockSpec((1, rows, S5_LANES), order),
            pl.BlockSpec((1, 1, 2 * S5_NSTATE), lambda b, i: (b, 0, 0)),
            pl.BlockSpec((S5_R, A_WIDTH, A_WIDTH), lambda b, i: (0, 0, 0), pipeline_mode=once),
            pl.BlockSpec((S5_LANES, 2 * S5_NSTATE), const2, pipeline_mode=once),
            pl.BlockSpec((S5_R, 2 * S5_NSTATE, A_WIDTH), lambda b, i: (0, 0, 0), pipeline_mode=once),
            pl.BlockSpec(pq.shape, const2),
        ],
        out_specs=[
            pl.BlockSpec((1, A_WIDTH // LANES, rows * S5_R, LANES), order_out),
            pl.BlockSpec((1, 1, 2 * S5_NSTATE), lambda b, i: (b, 0, 0)),
        ],
        out_shape=[
            jax.ShapeDtypeStruct((nb, A_WIDTH // LANES, n2 * S5_R, LANES), F32),
            jax.ShapeDtypeStruct((nb, 1, 2 * S5_NSTATE), F32),
        ],
        scratch_shapes=[pltpu.VMEM((1, 2 * S5_NSTATE), F32)],
        compiler_params=_params(("parallel", "arbitrary")),
        name="s5_scan_rev" if reverse else "s5_scan_fwd",
    )(u2, x0, lag, wst, wout, pq)
    return y, xf


def _s5_weights(lam_re, lam_im, log_dt, b_re, b_im, c_re, c_im, nlev):
    g, n, r = A_GROUPS, A_STATE, S5_R
    nl = lam_re.shape[0]
    lam_re = jnp.minimum(lam_re.astype(F32), -1e-4)
    lam_im = lam_im.astype(F32)
    dt = jnp.exp(log_dt.astype(F32))[..., None]
    mag = jnp.exp(lam_re * dt)
    a_re = mag * jnp.cos(lam_im * dt)
    a_im = mag * jnp.sin(lam_im * dt)
    den = lam_re * lam_re + lam_im * lam_im
    num_re = a_re - 1.0
    f_re = (num_re * lam_re + a_im * lam_im) / den
    f_im = (a_im * lam_re - num_re * lam_im) / den
    bb_re = f_re[..., None] * b_re - f_im[..., None] * b_im
    bb_im = f_re[..., None] * b_im + f_im[..., None] * b_re

    def cm(xr, xi, yr, yi):
        return xr * yr - xi * yi, xr * yi + xi * yr

    pr, pi = [jnp.ones_like(a_re)], [jnp.zeros_like(a_im)]
    for _ in range(r):
        nr, ni = cm(pr[-1], pi[-1], a_re, a_im)
        pr.append(nr)
        pi.append(ni)
    pr, pi = jnp.stack(pr), jnp.stack(pi)

    xr, xi = cm(pr[..., None], pi[..., None], bb_re, bb_im)
    yr, yi = cm(c_re, c_im, pr[:, :, :, :, None, :], pi[:, :, :, :, None, :])
    lag = jnp.einsum('ldgon,tldgnc->tldgoc', c_re, xr) - jnp.einsum('ldgon,tldgnc->tldgoc', c_im, xi)

    def group_diag(m):
        ni, nj = m.shape[-2:]
        wide = jnp.swapaxes(m, -3, -2).reshape(m.shape[:-3] + (1, ni, g * nj))
        keep = (np.arange(g)[:, None, None] == (np.arange(g * nj) // nj)[None, None, :])
        return jnp.where(keep, wide, 0.0).astype(BF16).reshape(m.shape[:-3] + (g * ni, g * nj))

    lag_bd = jnp.moveaxis(group_diag(jnp.swapaxes(lag[:r], -1, -2)), 0, 2)
    st_bd = jnp.concatenate([group_diag(jnp.swapaxes(xr, -1, -2)), group_diag(jnp.swapaxes(xi, -1, -2))], axis=-1)
    out_bd = jnp.concatenate([group_diag(jnp.swapaxes(yr, -1, -2)), group_diag(jnp.swapaxes(-yi, -1, -2))], axis=-2)

    tau_st = (r - 1 - np.arange(r), np.arange(r))
    tau_out = (np.arange(r) + 1, r - np.arange(r))

    def per_dir(fn):
        return jnp.stack([fn(0), fn(1)], axis=1)

    wst = per_dir(lambda dr: jnp.transpose(st_bd[tau_st[dr], :, dr], (1, 0, 2, 3))
                  .reshape(nl, r * A_WIDTH, 2 * g * n))
    wout = per_dir(lambda dr: jnp.transpose(out_bd[tau_out[dr], :, dr], (1, 0, 2, 3)))

    lr, li = pr[r].reshape(nl, 2, 1, g * n), pi[r].reshape(nl, 2, 1, g * n)
    rows = []
    for _ in range(nlev):
        rows.append(jnp.concatenate([lr, lr], axis=-1))
        rows.append(jnp.concatenate([-li, li], axis=-1))
        lr, li = cm(lr, li, lr, li)
    pq = jnp.concatenate(rows, axis=-2)
    return lag_bd, wst, wout, pq


def _split3(x):
    x1 = x.astype(BF16)
    r1 = x - x1.astype(F32)
    x2 = r1.astype(BF16)
    x3 = (r1 - x2.astype(F32)).astype(BF16)
    return x1, x2, x3


def _hgrn_chunks(chains, tri_ref, sel_ref, sgn_ref, msk_ref, heads, bd, nlev):
    t = chains[0][0].shape[0]

    def head_scores(qx, kx, half=None, upper=True):
        if half is not None:
            lo = half if upper else 0
            qx = jnp.concatenate([qx[b + lo:b + lo + half] for b in range(0, t, 2 * half)], axis=0)
        qx = qx.astype(BF16)
        rows = qx.shape[0]
        qh = jnp.concatenate([jnp.where(heads[hd], qx, jnp.zeros_like(qx)) for hd in range(C_HEADS)], axis=0)
        s = _dot_nt(qh, kx)
        out = [s[hd * rows:(hd + 1) * rows, :] for hd in range(C_HEADS)]
        if half is None:
            return out
        zero = jnp.zeros((half, s.shape[1]), F32)
        full = []
        for sh in out:
            blocks = []
            for i in range(rows // half):
                piece = sh[i * half:(i + 1) * half]
                blocks += [zero, piece] if upper else [piece, zero]
            full.append(jnp.concatenate(blocks, axis=0))
        return full

    cums, tots = [], []
    for q, k, vb, lf, st, dr in chains:
        l1, l2, l3 = _split3(lf)
        tri = tri_ref[dr]
        cums.append(_dot(tri, l1) + _dot(tri, l2) + _dot(tri, l3))
        tots.append(jnp.sum(lf, axis=0, keepdims=True))
    mids = [_dot(sel_ref[c[5]], cum.astype(BF16)) for c, cum in zip(chains, cums)]

    atts = []
    for q, k, vb, lf, st, dr in chains:
        diag = msk_ref[dr, nlev] > 0.5
        atts.append([jnp.where(diag, s, 0.0) for s in head_scores(q, k.astype(BF16))])
    for lev in range(nlev):
        for ci, (q, k, vb, lf, st, dr) in enumerate(chains):
            e = jnp.exp(sgn_ref[dr, lev] * (cums[ci] - mids[ci][lev * t:(lev + 1) * t, :]))
            m = msk_ref[dr, lev] > 0.5
            half = (1 << lev) if (1 << lev) >= HG_TRIM_MIN else None
            for hd, s in enumerate(head_scores(q * e, (k * e).astype(BF16), half, upper=(dr == 0))):
                atts[ci][hd] = atts[ci][hd] + jnp.where(m, s, 0.0)

    outs = []
    for ci, (q, k, vb, lf, st, dr) in enumerate(chains):
        o = _dot_nt((q * jnp.exp(cums[ci])).astype(BF16), st.astype(BF16))
        att_wide = jnp.concatenate([a.astype(BF16) for a in atts[ci]], axis=1)
        v_heads = jnp.concatenate([jnp.where(heads[hd], vb, jnp.zeros_like(vb)) for hd in range(C_HEADS)], axis=0)
        o = o + _dot(att_wide, v_heads)
        kend = (k * jnp.exp(tots[ci] - cums[ci])).astype(BF16)
        vt = vb.astype(F32).T.astype(BF16)
        new = st * jnp.exp(tots[ci]) + jnp.where(bd, _dot(vt, kend), 0.0)
        outs.append((o, new))
    return outs


def _hgrn_kernel(qf_ref, qr_ref, kf_ref, kr_ref, vf_ref, vr_ref, lff_ref, lfr_ref, s0_ref,
                 tri_ref, sel_ref, sgn_ref, msk_ref, hm_ref, bd_ref,
                 of_ref, or_ref, sf_ref, st_scr, *, nlev, nb):
    c = pl.program_id(0)

    @pl.when(c == 0)
    def _():
        st_scr[...] = s0_ref[...]

    heads = [hm_ref[pl.ds(hd, 1), :] > 0.5 for hd in range(C_HEADS)]
    bd = bd_ref[...] > 0.5
    dirs = ((qf_ref, kf_ref, vf_ref, lff_ref, of_ref), (qr_ref, kr_ref, vr_ref, lfr_ref, or_ref))
    chains, sinks = [], []
    for dr, (q_ref, k_ref, v_ref, lf_ref, o_ref) in enumerate(dirs):
        for b in range(nb):
            chains.append((q_ref[b].astype(F32), k_ref[0, b].astype(F32), v_ref[b], lf_ref[0, b], st_scr[dr, b], dr))
            sinks.append((o_ref, dr, b))
    outs = _hgrn_chunks(chains, tri_ref, sel_ref, sgn_ref, msk_ref, heads, bd, nlev)
    for (o_ref, dr, b), (o, new) in zip(sinks, outs):
        o_ref[b] = o
        st_scr[dr, b] = new
        sf_ref[dr, b] = new


def _hgrn_consts():
    t = HG_T
    ti = jnp.arange(t)[:, None]
    si = jnp.arange(t)[None, :]
    tri, sel, sgn, msk = [], [], [], []
    for reverse in (False, True):
        tri.append((si >= ti) if reverse else (si <= ti))
        sels, sgns, msks = [], [], []
        for lev in range(HG_LEVELS):
            h = 1 << lev
            blk_t, blk_s = ti // (2 * h), si // (2 * h)
            hi_t, hi_s = (ti % (2 * h)) >= h, (si % (2 * h)) >= h
            if reverse:
                mid = blk_t * 2 * h + h
                q_role_t, k_role_s = ~hi_t, hi_s
            else:
                mid = blk_t * 2 * h + h - 1
                q_role_t, k_role_s = hi_t, ~hi_s
            sels.append(si == mid)
            sgns.append(jnp.where(q_role_t, 1.0, -1.0))
            msks.append((blk_t == blk_s) & q_role_t & k_role_s)
        msks.append(ti == si)
        sel.append(jnp.concatenate(sels, axis=0))
        sgn.append(jnp.stack(sgns))
        msk.append(jnp.stack(msks))
    lane_head = jnp.arange(HG_W) // C_DK
    hm = (lane_head[None, :] == jnp.arange(C_HEADS)[:, None]).astype(F32)
    bd = (lane_head[:, None] == lane_head[None, :]).astype(F32)
    return (jnp.stack(tri).astype(BF16), jnp.stack(sel).astype(BF16), jnp.stack(sgn).astype(F32),
            jnp.stack(msk).astype(F32), hm, bd)


def _hgrn_scan(q, k, v, lf, s0, consts):
    tri, sel, sgn, msk, hm, bd = consts
    nb, s, w = q.shape
    t = HG_T
    nc = s // t
    fwd3 = pl.BlockSpec((nb, t, w), lambda c: (0, c, 0))
    rev3 = pl.BlockSpec((nb, t, w), lambda c: (0, nc - 1 - c, 0))
    fwd4 = pl.BlockSpec((1, nb, t, w), lambda c: (0, 0, c, 0))
    rev4 = pl.BlockSpec((1, nb, t, w), lambda c: (1, 0, nc - 1 - c, 0))
    whole = lambda a: pl.BlockSpec(a.shape, lambda c: (0,) * a.ndim)
    sgn = sgn.reshape(2, HG_LEVELS, t, 1)
    o_f, o_r, sf = pl.pallas_call(
        functools.partial(_hgrn_kernel, nlev=HG_LEVELS, nb=nb),
        grid=(nc,),
        in_specs=[fwd3, rev3, fwd4, rev4, fwd3, rev3, fwd4, rev4, whole(s0),
                  whole(tri), whole(sel), whole(sgn), whole(msk), whole(hm), whole(bd)],
        out_specs=[fwd3, rev3, whole(s0)],
        out_shape=[
            jax.ShapeDtypeStruct((nb, s, w), F32),
            jax.ShapeDtypeStruct((nb, s, w), F32),
            jax.ShapeDtypeStruct((2, nb, w, w), F32),
        ],
        scratch_shapes=[pltpu.VMEM((2, nb, w, w), F32)],
        compiler_params=_params(("arbitrary",)),
        name="hgrn_scan",
    )(q, q, k, k, v, v, lf, lf, s0, tri, sel, sgn, msk, hm, bd)
    return (o_f, o_r), sf


def _attn_kernel(*refs, nseg, tks, g, tq):
    q_ref = refs[0]
    kv_refs = refs[1:1 + 2 * nseg]
    o_ref = refs[1 + 2 * nseg]
    dv = o_ref.shape[-1]
    n = g * tq
    qt = jnp.concatenate([q_ref[0, hd] for hd in range(g)], axis=-1)

    def scores(k):
        s = _dot(k, qt)
        sub = min(ATTN_SUB, s.shape[0])
        parts = [s[i:i + sub] for i in range(0, s.shape[0], sub)]
        return tuple(x for p in parts for x in (p, jnp.max(p, axis=0, keepdims=True)))

    def absorb(sm, vt, carry):
        m, acc = carry
        sub = sm[0].shape[0]
        for i in range(len(sm) // 2):
            s, smax = sm[2 * i], sm[2 * i + 1]
            m_new = jnp.maximum(m, smax)
            alpha = jnp.exp2(m - m_new)
            p = jnp.exp2(s - m_new).astype(BF16)
            vt1 = jnp.concatenate([vt[:, i * sub:(i + 1) * sub], jnp.ones((ATTN_ONES, sub), BF16)], axis=0)
            acc = alpha * acc + _dot(vt1, p)
            m = m_new
        return m, acc

    carry = (jnp.full((1, n), -jnp.inf, F32), jnp.zeros((dv + ATTN_ONES, n), F32))
    pending = None
    for seg in range(nseg):
        k_ref, vt_ref = kv_refs[2 * seg], kv_refs[2 * seg + 1]
        tk = tks[seg]
        nk = k_ref.shape[2] // tk
        s_first = scores(k_ref[0, 0, 0:tk, :])
        if pending is not None:
            carry = absorb(*pending, carry)
        if nk == 1:
            pending = (s_first, vt_ref[0, 0])
            continue

        def body(j, c, k_ref=k_ref, vt_ref=vt_ref, tk=tk, nk=nk):
            sm_cur, (m, acc) = c[:-2], c[-2:]
            off_next = pl.multiple_of(jnp.minimum(j + 1, nk - 1) * tk, tk)
            sm_next = scores(k_ref[0, 0, pl.ds(off_next, tk), :])
            off = pl.multiple_of(j * tk, tk)
            m, acc = absorb(sm_cur, vt_ref[0, 0, :, pl.ds(off, tk)], (m, acc))
            return sm_next + (m, acc)

        unroll = ATTN_UNROLL if nk % ATTN_UNROLL == 0 else 1
        m, acc = lax.fori_loop(0, nk, body, s_first + carry, unroll=unroll)[-2:]
        carry = (m, acc)
        pending = None
    m, acc = carry if pending is None else absorb(*pending, carry)
    out = acc[:dv] / acc[dv:dv + 1, :]
    out = jnp.concatenate([out, jnp.zeros((LANES - dv, n), F32)], axis=0).T
    o_ref[0] = out[:, :dv].reshape(g, tq, dv).astype(o_ref.dtype)


def _attention(q, kvs):
    nb, hq, dk, sq = q.shape
    hkv = kvs[0][0].shape[1]
    dv = kvs[0][1].shape[2]
    g = hq // hkv
    tq = _pick(sq, tuple(n // g for n in ATTN_QUERIES))
    tks = tuple(_pick(k.shape[2], (ATTN_SUB, 128)) for k, _ in kvs)
    in_specs = [pl.BlockSpec((1, g, dk, tq), lambda b, h, i: (b, h, 0, i))]
    args = [q]
    for k, v in kvs:
        sk = k.shape[2]
        in_specs.append(pl.BlockSpec((1, 1, sk, dk), lambda b, h, i: (b, h, 0, 0)))
        in_specs.append(pl.BlockSpec((1, 1, dv, sk), lambda b, h, i: (b, h, 0, 0)))
        args += [k, v]
    return pl.pallas_call(
        functools.partial(_attn_kernel, nseg=len(kvs), tks=tks, g=g, tq=tq),
        grid=(nb, hkv, sq // tq),
        in_specs=in_specs,
        out_specs=pl.BlockSpec((1, g, tq, dv), lambda b, h, i: (b, h, i, 0)),
        out_shape=jax.ShapeDtypeStruct((nb, hq, sq, dv), BF16),
        compiler_params=_params(("parallel", "parallel", "arbitrary")),
        name="attention",
    )(*args)


def _merge_kernel(x_ref, mod_ref, gpre_ref, gpost_ref, wg_ref, wb_ref, wo_ref,
                  u_ref, yf_ref, yr_ref, sd_ref, wglu_ref, bo_ref, hof_ref, hor_ref, hg_ref, hn_ref, avg_ref, do_ref,
                  o_ref):
    x = x_ref[0]
    shift = mod_ref[0, pl.ds(3, 1), :]
    scale = mod_ref[0, pl.ds(4, 1), :]
    gate = mod_ref[0, pl.ds(5, 1), :]
    h = (_rms(x, gpre_ref[...]) * (1.0 + scale) + shift).astype(BF16)
    d = x.shape[-1]

    def branch_gate(i):
        return jax.nn.sigmoid(_dot(h, wg_ref[:, i * d:(i + 1) * d]))

    ysum = yf_ref[0] + yr_ref[0]
    y = sd_ref[...] * u_ref[0].astype(F32) + jnp.concatenate([ysum[hf] for hf in range(A_WIDTH // LANES)], axis=-1)
    ge = jax.nn.gelu(y)
    ya = ge * jax.nn.sigmoid(_dot(ge.astype(BF16), wglu_ref[...]))
    merged = branch_gate(0) * _dot(ya.astype(BF16), wb_ref[0])

    yb = jnp.concatenate([bo_ref[0, hd] for hd in range(B_HEADS)], axis=-1)
    merged = merged + branch_gate(1) * _dot(yb, wb_ref[1])

    o2 = hof_ref[0] + hor_ref[0]
    ms = _dot((o2 * o2).astype(BF16), avg_ref[...])
    gz = hg_ref[0].astype(F32)
    yc = o2 * lax.rsqrt(ms + EPS) * hn_ref[...] * (gz * jax.nn.sigmoid(gz))
    merged = merged + branch_gate(2) * _dot(yc.astype(BF16), wb_ref[2])

    yd = jnp.concatenate([do_ref[0, hd] for hd in range(D_HEADS)], axis=-1)
    merged = merged + branch_gate(3) * _dot(yd, wb_ref[3])

    yo = _dot(merged.astype(BF16), wo_ref[...])
    o_ref[0] = x + gate * _rms(yo, gpost_ref[...])


def _merge(x, mod, g_pre, g_post, wg, wb, wo, u, yf, yr, s5d, wglu, bo, ho, hg, hn, avg, do):
    nb, s, d = x.shape
    tm = _pick(s, (512, 256))
    const2 = lambda b, i: (0, 0)
    const3 = lambda b, i: (0, 0, 0)
    tok = lambda w: pl.BlockSpec((1, tm, w), lambda b, i: (b, i, 0))
    halves = pl.BlockSpec((1, A_WIDTH // LANES, tm, LANES), lambda b, i: (b, 0, i, 0))
    return pl.pallas_call(
        _merge_kernel,
        grid=(nb, s // tm),
        in_specs=[
            tok(d),
            pl.BlockSpec((1, N_MOD, d), lambda b, i: (b, 0, 0)),
            pl.BlockSpec((1, d), const2),
            pl.BlockSpec((1, d), const2),
            pl.BlockSpec((d, N_BRANCH * d), const2),
            pl.BlockSpec((N_BRANCH, BRANCH_W, d), const3),
            pl.BlockSpec((d, d), const2),
            tok(A_WIDTH), halves, halves,
            pl.BlockSpec((1, A_WIDTH), const2),
            pl.BlockSpec((A_WIDTH, A_WIDTH), const2),
            pl.BlockSpec((1, B_HEADS, tm, B_V), lambda b, i: (b, 0, i, 0)),
            tok(HG_W), tok(HG_W),
            tok(HG_W),
            pl.BlockSpec((1, HG_W), const2),
            pl.BlockSpec((HG_W, HG_W), const2),
            pl.BlockSpec((1, D_HEADS, tm, D_HEAD), lambda b, i: (b, 0, i, 0)),
        ],
        out_specs=tok(d),
        out_shape=jax.ShapeDtypeStruct(x.shape, F32),
        compiler_params=_params(("parallel", "parallel")),
        name="merge_out",
    )(x, mod, g_pre.reshape(1, d), g_post.reshape(1, d), wg, wb, wo, u, yf, yr, s5d, wglu, bo, ho[0], ho[1], hg, hn,
      avg, do)


def _pad_cols(w, width):
    return jnp.pad(w, ((0, 0), (0, width - w.shape[1])))


def _proj_weight(w_in_mix):
    offs = [0]
    for wdt in (A_WIDTH, B_Q_LORA, B_KV_LORA, B_ROPE, HG_W, HG_W, HG_W, HG_W, HG_W,
                D_HEADS * D_HEAD, D_KV_HEADS * D_HEAD, D_KV_HEADS * D_HEAD):
        offs.append(offs[-1] + wdt)
    p = [w_in_mix[:, offs[i]:offs[i + 1]] for i in range(12)]
    d = w_in_mix.shape[0]
    z = lambda n: jnp.zeros((d, n), w_in_mix.dtype)
    cols = [p[0], _pad_cols(p[1], 256), p[2],
            jnp.concatenate([z(B_NOPE), p[3], z(LANES - B_NOPE - B_ROPE)], axis=1),
            p[4], p[5], p[6], p[7], p[8]]
    for i, nh in ((9, D_HEADS), (10, D_KV_HEADS), (11, D_KV_HEADS)):
        for hd in range(nh):
            cols.append(_pad_cols(p[i][:, hd * D_HEAD:(hd + 1) * D_HEAD], LANES))
    return jnp.concatenate(cols, axis=1).astype(BF16)


def _rope_tables(n_tok, rot_dim, lane_off, identity):
    cos = jnp.ones((n_tok, LANES), F32)
    sin_a = jnp.zeros((n_tok, LANES), F32)
    sin_b = jnp.zeros((n_tok, LANES), F32)
    if not identity:
        n_rows = n_tok // GRID_W
        rows = jnp.repeat(jnp.arange(n_rows, dtype=F32), GRID_W)
        cols = jnp.tile(jnp.arange(GRID_W, dtype=F32), n_rows)
        half = rot_dim // 2
        inv = ROPE_THETA ** (-jnp.arange(0, half, 2, dtype=F32) / half)
        ang_r = rows[:, None] * inv
        ang_c = cols[:, None] * inv
        ang = jnp.concatenate([ang_r, ang_r, ang_c, ang_c], axis=-1)
        c, s = jnp.cos(ang), jnp.sin(ang)
        quarter = rot_dim // 4
        first = (np.arange(rot_dim) % (2 * quarter)) < quarter

        def place(a, fill):
            left = jnp.full((n_tok, lane_off), fill, F32)
            right = jnp.full((n_tok, LANES - lane_off - rot_dim), fill, F32)
            return jnp.concatenate([left, a, right], axis=1)

        cos = place(c, 1.0)
        sin_a = place(jnp.where(first, -s, 0.0), 0.0)
        sin_b = place(jnp.where(first, 0.0, s), 0.0)
    return jnp.stack([cos, sin_a, sin_b])


def kernel(x, c, ctx, c_ctx, w_ada, b_ada, norm_pre, norm_post, ffn_w1, ffn_w3, ffn_w2, w_in,
           s5_lambda_re, s5_lambda_im, s5_log_dt, s5_b_re, s5_b_im, s5_c_re, s5_c_im, s5_d, s5_w_glu,
           mla_q_norm, mla_w_uq, mla_kv_norm, mla_w_ukv, hgrn_lb_raw, hgrn_o_norm,
           gqa_q_norm, gqa_k_norm, w_branch, w_out):
    nb, seq, d = x.shape
    n_ctx = ctx.shape[1]
    depth = w_ada.shape[0]

    rows = max(8, -(-(nb + 1) // 8) * 8)
    cvec = jnp.zeros((rows, d), F32).at[:nb].set(c).at[nb].set(c_ctx)
    mod_all = _modulation(cvec, w_ada, b_ada).reshape(depth, rows, N_MOD, d)

    lb_step = jax.nn.softmax(hgrn_lb_raw.astype(F32), axis=1)
    lb_all = jnp.clip(jnp.cumsum(lb_step, axis=1) - lb_step[:, :1], 0.0, 1.0)

    rope_b_lat = _rope_tables(seq, B_ROPE, B_NOPE, False)
    rope_d_lat = _rope_tables(seq, D_HEAD, 0, False)
    rope_b_ctx = _rope_tables(n_ctx, B_ROPE, B_NOPE, True)
    rope_d_ctx = _rope_tables(n_ctx, D_HEAD, 0, True)
    hg_consts = _hgrn_consts()
    s5_w = _s5_weights(s5_lambda_re, s5_lambda_im, s5_log_dt, s5_b_re, s5_b_im, s5_c_re, s5_c_im, S5_MAX_LEVELS)
    lane_head = jnp.arange(HG_W) // C_DV
    avg = (lane_head[:, None] == lane_head[None, :]).astype(BF16) * (1.0 / C_DV)
    n_mixcols = w_in.shape[-1] - N_BRANCH * d
    ffn_w = (ffn_w1.astype(BF16), ffn_w3.astype(BF16), ffn_w2.astype(BF16))

    x_lat, x_ctx = x, ctx
    for layer in range(depth):
        last = layer == depth - 1
        mod_lat = mod_all[layer, :nb]
        mod_ctx = jnp.broadcast_to(mod_all[layer, nb:nb + 1], (nb, N_MOD, d))
        bf = lambda w: w.astype(BF16)
        ffn_a = (norm_pre[layer, 0], norm_post[layer, 0], *ffn_w, layer, 0)
        ffn_b = (norm_pre[layer, 2], norm_post[layer, 2], *ffn_w, layer, 1)

        x_lat = _ffn(x_lat, mod_lat, 0, *ffn_a)
        x_ctx = _ffn(x_ctx, mod_ctx, 0, *ffn_a)

        wp = _proj_weight(w_in[layer, :, :n_mixcols])
        wg = bf(w_in[layer, :, n_mixcols:])
        qn = _pad_cols(mla_q_norm[layer].reshape(1, B_Q_LORA), 256)
        wuq = mla_w_uq[layer].reshape(B_Q_LORA, B_HEADS, B_NOPE + B_ROPE)
        wuq = jnp.pad(wuq, ((0, 256 - B_Q_LORA), (0, 0), (0, LANES - B_NOPE - B_ROPE))).reshape(256, B_HEADS * LANES)
        wukv = mla_w_ukv[layer].reshape(B_KV_LORA, B_HEADS, B_NOPE + B_V)
        wuk = jnp.pad(wukv[:, :, :B_NOPE], ((0, 0), (0, 0), (0, LANES - B_NOPE))).reshape(B_KV_LORA, B_HEADS * LANES)
        wuv = jnp.pad(wukv[:, :, B_NOPE:], ((0, 0), (0, 0), (0, LANES - B_V))).reshape(B_KV_LORA, B_HEADS * LANES)
        kvn = mla_kv_norm[layer].reshape(1, B_KV_LORA)
        gqn = _pad_cols(gqa_q_norm[layer].reshape(1, D_HEAD), LANES)
        gkn = _pad_cols(gqa_k_norm[layer].reshape(1, D_HEAD), LANES)
        lb = lb_all[:, layer]
        proj_args = (norm_pre[layer, 1], wp)
        mla_args = (qn, bf(wuq), kvn, bf(wuk), bf(wuv), lb, gqn, gkn)

        pl_ = _inproj(x_lat, mod_lat, *proj_args, rope_b_lat, rope_d_lat, *mla_args)
        pc_ = _inproj(x_ctx, mod_ctx, *proj_args, rope_b_ctx, rope_d_ctx, *mla_args)
        (u_l, u2_l, bq_l, bk_l, bv_l, hq_l, hv_l, hk_l, hl_l, hg_l, dq_l, dk_l, dv_l) = pl_
        (u_c, u2_c, bq_c, bk_c, bv_c, hq_c, hv_c, hk_c, hl_c, hg_c, dq_c, dk_c, dv_c) = pc_

        ys_l, ys_c = [], []
        for dr, reverse in enumerate((False, True)):
            wts = tuple(w[layer, dr] for w in s5_w)
            x0 = jnp.zeros((nb, 1, 2 * S5_NSTATE), F32)
            y_c, x_end = _s5_scan(u2_c, x0, wts, reverse)
            y_l, _ = _s5_scan(u2_l, x_end, wts, reverse)
            ys_l.append(y_l)
            ys_c.append(y_c)

        s0 = jnp.zeros((2, nb, HG_W, HG_W), F32)
        ho_c, s_ctx = _hgrn_scan(hq_c, hk_c, hv_c, hl_c, s0, hg_consts)
        ho_l, _ = _hgrn_scan(hq_l, hk_l, hv_l, hl_l, s_ctx, hg_consts)

        bo_l = _attention(bq_l, [(bk_c, bv_c), (bk_l, bv_l)])
        do_l = _attention(dq_l, [(dk_c, dv_c), (dk_l, dv_l)])

        merge_w = (norm_pre[layer, 1], norm_post[layer, 1], wg, bf(w_branch[layer]), bf(w_out[layer]))
        s5_ro = (s5_d[layer].reshape(1, A_WIDTH), bf(s5_w_glu[layer]))
        hn = jnp.tile(hgrn_o_norm[layer], C_HEADS).reshape(1, HG_W)
        x_lat_new = _merge(x_lat, mod_lat, *merge_w, u_l, ys_l[0], ys_l[1], *s5_ro, bo_l, ho_l, hg_l, hn, avg, do_l)
        if not last:
            bo_c = _attention(bq_c, [(bk_c, bv_c)])
            do_c = _attention(dq_c, [(dk_c, dv_c)])
            x_ctx = _merge(x_ctx, mod_ctx, *merge_w, u_c, ys_c[0], ys_c[1], *s5_ro, bo_c, ho_c, hg_c, hn, avg, do_c)
            x_ctx = _ffn(x_ctx, mod_ctx, 2, *ffn_b)
        x_lat = _ffn(x_lat_new, mod_lat, 2, *ffn_b)
    return x_lat
```

```python
import functools

import jax
import jax.numpy as jnp
import numpy as np
from jax import lax
from jax.experimental import pallas as pl
from jax.experimental.pallas import tpu as pltpu

GRID_W = 64
FFN_RES_WEIGHT = 0.5
N_MOD = 9
EPS = 1e-6
ROPE_THETA = 10000.0
F_FLOOR = 1e-20

A_WIDTH = 256
A_GROUP = 16
A_GROUPS = A_WIDTH // A_GROUP
A_STATE = 64

B_HEADS = 4
B_NOPE = 64
B_ROPE = 32
B_V = 64
B_Q_LORA = 192
B_KV_LORA = 128

C_HEADS = 4
C_DK = 64
C_DV = 64

D_HEADS = 4
D_KV_HEADS = 2
D_HEAD = 64

N_BRANCH = 4
BRANCH_W = 256

LANES = 128
VMEM_LIMIT_BYTES = 56 * 1024 * 1024

S5_R = 8
S5_LANES = S5_R * A_WIDTH
S5_NSTATE = A_GROUPS * A_STATE
S5_MAX_ROWS = 512
S5_MAX_LEVELS = 9
HG_T = 128
HG_LEVELS = 7
HG_W = C_HEADS * C_DK
HG_TRIM_MIN = 8
ATTN_ONES = 16
ATTN_QUERIES = (1024, 512, 256, 128)
ATTN_SUB = 256
ATTN_AHEAD = 2

SLOT_S5 = 0
SLOT_CQ = 256
SLOT_CKV = 512
SLOT_KR = 640
SLOT_HQ = 768
SLOT_HV = 1024
SLOT_HF = 1280
SLOT_HB = 1536
SLOT_HG = 1792
SLOT_GQ = 2048
SLOT_GK = 2560
SLOT_GV = 2816
N_PROJ = 3072

BF16 = jnp.bfloat16
F32 = jnp.float32
LOG2E = 1.4426950408889634


def _params(sem, flags=None):
    return pltpu.CompilerParams(dimension_semantics=sem, vmem_limit_bytes=VMEM_LIMIT_BYTES, flags=flags)


def _pick(n, candidates):
    for c in candidates:
        if n % c == 0:
            return c
    raise ValueError(f"no tile for {n} in {candidates}")


def _dot(a, b):
    return jnp.dot(a, b, preferred_element_type=F32)


def _dot_nt(a, b):
    return lax.dot_general(a, b, (((1,), (1,)), ((), ())), preferred_element_type=F32)


def _rms(x, g, n=None):
    n = x.shape[-1] if n is None else n
    ms = jnp.sum(x * x, axis=-1, keepdims=True) * (1.0 / n)
    return x * lax.rsqrt(ms + EPS) * g


def _mod_kernel(c_ref, w_ref, b_ref, o_ref):
    c = c_ref[...]
    a = (c * jax.nn.sigmoid(c)).astype(BF16)
    o_ref[0] = _dot(a, w_ref[0].astype(BF16)) + b_ref[0]


def _modulation(cvec, w_ada, b_ada):
    nl, d, nm = w_ada.shape
    rows = cvec.shape[0]
    tn = _pick(nm, (1152, 1024, 512, 256, 128))
    return pl.pallas_call(
        _mod_kernel,
        grid=(nl, nm // tn),
        in_specs=[
            pl.BlockSpec((rows, d), lambda l, n: (0, 0)),
            pl.BlockSpec((1, d, tn), lambda l, n: (l, 0, n)),
            pl.BlockSpec((1, 1, tn), lambda l, n: (l, 0, n)),
        ],
        out_specs=pl.BlockSpec((1, rows, tn), lambda l, n: (l, 0, n)),
        out_shape=jax.ShapeDtypeStruct((nl, rows, nm), F32),
        compiler_params=_params(("parallel", "parallel")),
        name="adaln_mod",
    )(cvec, w_ada, b_ada.reshape(nl, 1, nm))


def _ffn_kernel(x_ref, mod_ref, gpre_ref, gpost_ref, w1_ref, w3_ref, w2_ref, o_ref, *, j, tf):
    x = x_ref[0]
    shift = mod_ref[0, pl.ds(3 * j, 1), :]
    scale = mod_ref[0, pl.ds(3 * j + 1, 1), :]
    gate = mod_ref[0, pl.ds(3 * j + 2, 1), :]
    h = (_rms(x, gpre_ref[...]) * (1.0 + scale) + shift).astype(BF16)
    acc = None
    for f in range(w1_ref.shape[1] // tf):
        a = _dot(h, w1_ref[:, f * tf:(f + 1) * tf])
        b = _dot(h, w3_ref[:, f * tf:(f + 1) * tf])
        t = (a * jax.nn.sigmoid(a) * b).astype(BF16)
        part = _dot(t, w2_ref[f * tf:(f + 1) * tf, :])
        acc = part if acc is None else acc + part
    o_ref[0] = x + FFN_RES_WEIGHT * gate * _rms(acc, gpost_ref[...])


def _ffn(x, mod, j, g_pre, g_post, w1, w3, w2, layer, which):
    nb, s, d = x.shape
    dff = w1.shape[-1]
    tm = _pick(s, (512, 256))
    tf = _pick(dff, (256, 128))
    once = pl.Buffered(1)
    pick = lambda b, i: (layer, which, 0, 0)
    return pl.pallas_call(
        functools.partial(_ffn_kernel, j=j, tf=tf),
        grid=(nb, s // tm),
        in_specs=[
            pl.BlockSpec((1, tm, d), lambda b, i: (b, i, 0)),
            pl.BlockSpec((1, N_MOD, d), lambda b, i: (b, 0, 0)),
            pl.BlockSpec((1, d), lambda b, i: (0, 0)),
            pl.BlockSpec((1, d), lambda b, i: (0, 0)),
            pl.BlockSpec((None, None, d, dff), pick, pipeline_mode=once),
            pl.BlockSpec((None, None, d, dff), pick, pipeline_mode=once),
            pl.BlockSpec((None, None, dff, d), pick, pipeline_mode=once),
        ],
        out_specs=pl.BlockSpec((1, tm, d), lambda b, i: (b, i, 0)),
        out_shape=jax.ShapeDtypeStruct(x.shape, F32),
        compiler_params=_params(("parallel", "parallel")),
        name="ffn_sublayer",
    )(x, mod, g_pre.reshape(1, d), g_post.reshape(1, d), w1, w3, w2)


def _rope(x, cos, sin_a, sin_b, quarter):
    w = x.shape[-1]
    return x * cos + pltpu.roll(x, w - quarter, 1) * sin_a + pltpu.roll(x, quarter, 1) * sin_b


def _inproj_kernel(x_ref, mod_ref, gpre_ref, w_ref, rb_ref, rd_ref, qn_ref, wuq_ref, kvn_ref, wuk_ref, wuv_ref,
                   lb_ref, gqn_ref, gkn_ref,
                   u_ref, u2_ref, bq_ref, bk_ref, bv_ref, hq_ref, hv_ref, hk_ref, hl_ref, hg_ref,
                   dq_ref, dk_ref, dv_ref, h_scr, u_scr, *, tm):
    x = x_ref[0]
    shift = mod_ref[0, pl.ds(3, 1), :]
    scale = mod_ref[0, pl.ds(4, 1), :]
    h_scr[...] = (_rms(x, gpre_ref[...]) * (1.0 + scale) + shift).astype(BF16)

    def proj(lo, width):
        return _dot(h_scr[...], w_ref[:, lo:lo + width])

    cos_d, sa_d, sb_d = rd_ref[0], rd_ref[1], rd_ref[2]
    d_scale = D_HEAD ** -0.5 * LOG2E
    gq = proj(SLOT_GQ, D_HEADS * LANES)
    for hd in range(D_HEADS):
        qh = _rms(gq[:, hd * LANES:(hd + 1) * LANES], gqn_ref[...], n=D_HEAD)
        qh = _rope(qh, cos_d, sa_d, sb_d, D_HEAD // 4) * d_scale
        dq_ref[0, hd] = qh.T[:D_HEAD].astype(BF16)
    gkv = proj(SLOT_GK, 2 * D_KV_HEADS * LANES)
    for hd in range(D_KV_HEADS):
        kh = _rms(gkv[:, hd * LANES:(hd + 1) * LANES], gkn_ref[...], n=D_HEAD)
        kh = _rope(kh, cos_d, sa_d, sb_d, D_HEAD // 4)
        dk_ref[0, hd] = kh[:, :D_HEAD].astype(BF16)
        dv_ref[0, hd] = gkv[:, (D_KV_HEADS + hd) * LANES:(D_KV_HEADS + hd + 1) * LANES].T[:D_HEAD].astype(BF16)

    cos_b, sa_b, sb_b = rb_ref[0], rb_ref[1], rb_ref[2]
    cq = _rms(proj(SLOT_CQ, 256), qn_ref[...], n=B_Q_LORA).astype(BF16)
    q = _dot(cq, wuq_ref[...])
    b_scale = (B_NOPE + B_ROPE) ** -0.5 * LOG2E
    for hd in range(B_HEADS):
        qh = _rope(q[:, hd * LANES:(hd + 1) * LANES], cos_b, sa_b, sb_b, B_ROPE // 4)
        bq_ref[0, hd] = (qh * b_scale).T.astype(BF16)
    ckr = proj(SLOT_CKV, B_KV_LORA + LANES)
    ckv = _rms(ckr[:, :B_KV_LORA], kvn_ref[...]).astype(BF16)
    kn = _dot(ckv, wuk_ref[...])
    vn = _dot(ckv, wuv_ref[...])
    kr = _rope(ckr[:, B_KV_LORA:], cos_b, sa_b, sb_b, B_ROPE // 4)
    for hd in range(B_HEADS):
        bk_ref[0, hd] = (kn[:, hd * LANES:(hd + 1) * LANES] + kr).astype(BF16)
        bv_ref[0, hd] = vn[:, hd * LANES:(hd + 1) * LANES].T[:B_V].astype(BF16)

    hq_ref[0] = proj(SLOT_HQ, HG_W).astype(BF16)
    hv_ref[0] = proj(SLOT_HV, HG_W).astype(BF16)
    hg_ref[0] = proj(SLOT_HG, HG_W).astype(BF16)
    for dr, slot in enumerate((SLOT_HF, SLOT_HB)):
        z = proj(slot, HG_W)
        lb = lb_ref[pl.ds(dr, 1), :]
        f = lb + (1.0 - lb) * jax.nn.sigmoid(z)
        hl_ref[dr, 0] = jnp.log(jnp.maximum(f, F_FLOOR))
        hk_ref[dr, 0] = ((1.0 - lb) * jax.nn.sigmoid(-z)).astype(BF16)

    u = proj(SLOT_S5, A_WIDTH)
    u_ref[0] = u.astype(BF16)
    for hf in range(A_WIDTH // LANES):
        u_scr[hf] = u[:, hf * LANES:(hf + 1) * LANES]
    for r in range(S5_R):
        for hf in range(A_WIDTH // LANES):
            lo = r * A_WIDTH + hf * LANES
            u2_ref[0, :, lo:lo + LANES] = u_scr[hf, pl.ds(r, tm // S5_R, stride=S5_R), :].astype(BF16)


def _inproj(x, mod, g_pre, wp, rope_b, rope_d, qn, wuq, kvn, wuk, wuv, lb, gqn, gkn):
    nb, s, d = x.shape
    tm = _pick(s, (512, 256))
    const2 = lambda b, i: (0, 0)
    tok = lambda w: pl.BlockSpec((1, tm, w), lambda b, i: (b, i, 0))
    headed = lambda nh, w: pl.BlockSpec((1, nh, tm, w), lambda b, i: (b, 0, i, 0))
    headed_t = lambda nh, w: pl.BlockSpec((1, nh, w, tm), lambda b, i: (b, 0, 0, i))
    dirtok = lambda w: pl.BlockSpec((2, 1, tm, w), lambda b, i: (0, b, i, 0))
    sd = jax.ShapeDtypeStruct
    outs = [
        (sd((nb, s, A_WIDTH), BF16), tok(A_WIDTH)),
        (sd((nb, s // S5_R, S5_LANES), BF16), pl.BlockSpec((1, tm // S5_R, S5_LANES), lambda b, i: (b, i, 0))),
        (sd((nb, B_HEADS, LANES, s), BF16), headed_t(B_HEADS, LANES)),
        (sd((nb, B_HEADS, s, LANES), BF16), headed(B_HEADS, LANES)),
        (sd((nb, B_HEADS, B_V, s), BF16), headed_t(B_HEADS, B_V)),
        (sd((nb, s, HG_W), BF16), tok(HG_W)),
        (sd((nb, s, HG_W), BF16), tok(HG_W)),
        (sd((2, nb, s, HG_W), BF16), dirtok(HG_W)),
        (sd((2, nb, s, HG_W), F32), dirtok(HG_W)),
        (sd((nb, s, HG_W), BF16), tok(HG_W)),
        (sd((nb, D_HEADS, D_HEAD, s), BF16), headed_t(D_HEADS, D_HEAD)),
        (sd((nb, D_KV_HEADS, s, D_HEAD), BF16), headed(D_KV_HEADS, D_HEAD)),
        (sd((nb, D_KV_HEADS, D_HEAD, s), BF16), headed_t(D_KV_HEADS, D_HEAD)),
    ]
    return pl.pallas_call(
        functools.partial(_inproj_kernel, tm=tm),
        grid=(nb, s // tm),
        in_specs=[
            pl.BlockSpec((1, tm, d), lambda b, i: (b, i, 0)),
            pl.BlockSpec((1, N_MOD, d), lambda b, i: (b, 0, 0)),
            pl.BlockSpec((1, d), const2),
            pl.BlockSpec((d, N_PROJ), const2),
            pl.BlockSpec((3, tm, LANES), lambda b, i: (0, i, 0)),
            pl.BlockSpec((3, tm, LANES), lambda b, i: (0, i, 0)),
            pl.BlockSpec((1, 256), const2),
            pl.BlockSpec((256, B_HEADS * LANES), const2),
            pl.BlockSpec((1, B_KV_LORA), const2),
            pl.BlockSpec((B_KV_LORA, B_HEADS * LANES), const2),
            pl.BlockSpec((B_KV_LORA, B_HEADS * LANES), const2),
            pl.BlockSpec((2, HG_W), const2),
            pl.BlockSpec((1, LANES), const2),
            pl.BlockSpec((1, LANES), const2),
        ],
        out_specs=[o[1] for o in outs],
        out_shape=[o[0] for o in outs],
        scratch_shapes=[pltpu.VMEM((tm, d), BF16), pltpu.VMEM((A_WIDTH // LANES, tm, LANES), F32)],
        compiler_params=_params(("parallel", "parallel")),
        name="mixer_inproj",
    )(x, mod, g_pre.reshape(1, d), wp, rope_b, rope_d, qn, wuq, kvn, wuk, wuv, lb, gqn, gkn)


def _s5_kernel(u2_ref, x0_ref, lag_ref, wst_ref, wout_ref, pq_ref, y_ref, xf_ref, carry, *, rows, reverse, nlev):
    i = pl.program_id(1)

    @pl.when(i == 0)
    def _():
        carry[...] = x0_ref[0]

    u2 = u2_ref[0]
    sloc = _dot(u2, wst_ref[...])
    y_local = []
    for b in range(S5_R):
        acc = None
        for a in (range(b, S5_R) if reverse else range(b + 1)):
            part = _dot(u2[:, a * A_WIDTH:(a + 1) * A_WIDTH], lag_ref[abs(b - a)])
            acc = part if acc is None else acc + part
        y_local.append(acc)
    ridx = lax.broadcasted_iota(jnp.int32, (rows, 1), 0)

    def cmul(xv, lev):
        p = pq_ref[pl.ds(2 * lev, 1), :]
        q = pq_ref[pl.ds(2 * lev + 1, 1), :]
        return xv * p + pltpu.roll(xv, S5_NSTATE, 1) * q

    if reverse:
        e = jnp.where(ridx == rows - 1, carry[...], pltpu.roll(sloc, rows - 1, 0))
    else:
        e = jnp.where(ridx == 0, carry[...], pltpu.roll(sloc, 1, 0))
    xin = e
    for lev in range(nlev):
        dist = 1 << lev
        if reverse:
            sh = jnp.where(ridx < rows - dist, pltpu.roll(xin, rows - dist, 0), 0.0)
        else:
            sh = jnp.where(ridx >= dist, pltpu.roll(xin, dist, 0), 0.0)
        xin = xin + cmul(sh, lev)

    last = 0 if reverse else rows - 1
    nxt = cmul(xin[last:last + 1, :], 0) + sloc[last:last + 1, :]
    carry[...] = nxt
    xf_ref[0] = nxt

    xb = xin.astype(BF16)
    for b in range(S5_R):
        yb = y_local[b] + _dot(xb, wout_ref[b])
        for hf in range(A_WIDTH // LANES):
            y_ref[0, hf, pl.ds(b, rows, stride=S5_R), :] = yb[:, hf * LANES:(hf + 1) * LANES]


def _s5_scan(u2, x0, wts, reverse):
    lag, wst, wout, pq = wts
    nb, n2, _ = u2.shape
    rows = _pick(n2, (S5_MAX_ROWS, 256, 128, 64, 32))
    nt = n2 // rows
    once = pl.Buffered(1)
    nlev = max(1, (rows - 1).bit_length())
    order = (lambda b, i: (b, nt - 1 - i, 0)) if reverse else (lambda b, i: (b, i, 0))
    order_out = (lambda b, i: (b, 0, nt - 1 - i, 0)) if reverse else (lambda b, i: (b, 0, i, 0))
    const2 = lambda b, i: (0, 0)
    y, xf = pl.pallas_call(
        functools.partial(_s5_kernel, rows=rows, reverse=reverse, nlev=nlev),
        grid=(nb, nt),
        in_specs=[
            pl.BlockSpec((1, rows, S5_LANES), order),
            pl.BlockSpec((1, 1, 2 * S5_NSTATE), lambda b, i: (b, 0, 0)),
            pl.BlockSpec((S5_R, A_WIDTH, A_WIDTH), lambda b, i: (0, 0, 0), pipeline_mode=once),
            pl.BlockSpec((S5_LANES, 2 * S5_NSTATE), const2, pipeline_mode=once),
            pl.BlockSpec((S5_R, 2 * S5_NSTATE, A_WIDTH), lambda b, i: (0, 0, 0), pipeline_mode=once),
            pl.BlockSpec(pq.shape, const2),
        ],
        out_specs=[
            pl.BlockSpec((1, A_WIDTH // LANES, rows * S5_R, LANES), order_out),
            pl.BlockSpec((1, 1, 2 * S5_NSTATE), lambda b, i: (b, 0, 0)),
        ],
        out_shape=[
            jax.ShapeDtypeStruct((nb, A_WIDTH // LANES, n2 * S5_R, LANES), F32),
            jax.ShapeDtypeStruct((nb, 1, 2 * S5_NSTATE), F32),
        ],
        scratch_shapes=[pltpu.VMEM((1, 2 * S5_NSTATE), F32)],
        compiler_params=_params(("parallel", "arbitrary")),
        name="s5_scan_rev" if reverse else "s5_scan_fwd",
    )(u2, x0, lag, wst, wout, pq)
    return y, xf


def _s5_weights(lam_re, lam_im, log_dt, b_re, b_im, c_re, c_im, nlev):
    g, n, r = A_GROUPS, A_STATE, S5_R
    nl = lam_re.shape[0]
    lam_re = jnp.minimum(lam_re.astype(F32), -1e-4)
    lam_im = lam_im.astype(F32)
    dt = jnp.exp(log_dt.astype(F32))[..., None]
    mag = jnp.exp(lam_re * dt)
    a_re = mag * jnp.cos(lam_im * dt)
    a_im = mag * jnp.sin(lam_im * dt)
    den = lam_re * lam_re + lam_im * lam_im
    num_re = a_re - 1.0
    f_re = (num_re * lam_re + a_im * lam_im) / den
    f_im = (a_im * lam_re - num_re * lam_im) / den
    bb_re = f_re[..., None] * b_re - f_im[..., None] * b_im
    bb_im = f_re[..., None] * b_im + f_im[..., None] * b_re

    def cm(xr, xi, yr, yi):
        return xr * yr - xi * yi, xr * yi + xi * yr

    pr, pi = [jnp.ones_like(a_re)], [jnp.zeros_like(a_im)]
    for _ in range(r):
        nr, ni = cm(pr[-1], pi[-1], a_re, a_im)
        pr.append(nr)
        pi.append(ni)
    pr, pi = jnp.stack(pr), jnp.stack(pi)

    xr, xi = cm(pr[..., None], pi[..., None], bb_re, bb_im)
    yr, yi = cm(c_re, c_im, pr[:, :, :, :, None, :], pi[:, :, :, :, None, :])
    lag = jnp.einsum('ldgon,tldgnc->tldgoc', c_re, xr) - jnp.einsum('ldgon,tldgnc->tldgoc', c_im, xi)

    def group_diag(m):
        ni, nj = m.shape[-2:]
        wide = jnp.swapaxes(m, -3, -2).reshape(m.shape[:-3] + (1, ni, g * nj))
        keep = (np.arange(g)[:, None, None] == (np.arange(g * nj) // nj)[None, None, :])
        return jnp.where(keep, wide, 0.0).astype(BF16).reshape(m.shape[:-3] + (g * ni, g * nj))

    lag_bd = jnp.moveaxis(group_diag(jnp.swapaxes(lag[:r], -1, -2)), 0, 2)
    st_bd = jnp.concatenate([group_diag(jnp.swapaxes(xr, -1, -2)), group_diag(jnp.swapaxes(xi, -1, -2))], axis=-1)
    out_bd = jnp.concatenate([group_diag(jnp.swapaxes(yr, -1, -2)), group_diag(jnp.swapaxes(-yi, -1, -2))], axis=-2)

    tau_st = (r - 1 - np.arange(r), np.arange(r))
    tau_out = (np.arange(r) + 1, r - np.arange(r))

    def per_dir(fn):
        return jnp.stack([fn(0), fn(1)], axis=1)

    wst = per_dir(lambda dr: jnp.transpose(st_bd[tau_st[dr], :, dr], (1, 0, 2, 3))
                  .reshape(nl, r * A_WIDTH, 2 * g * n))
    wout = per_dir(lambda dr: jnp.transpose(out_bd[tau_out[dr], :, dr], (1, 0, 2, 3)))

    lr, li = pr[r].reshape(nl, 2, 1, g * n), pi[r].reshape(nl, 2, 1, g * n)
    rows = []
    for _ in range(nlev):
        rows.append(jnp.concatenate([lr, lr], axis=-1))
        rows.append(jnp.concatenate([-li, li], axis=-1))
        lr, li = cm(lr, li, lr, li)
    pq = jnp.concatenate(rows, axis=-2)
    return lag_bd, wst, wout, pq


def _split3(x):
    x1 = x.astype(BF16)
    r1 = x - x1.astype(F32)
    x2 = r1.astype(BF16)
    x3 = (r1 - x2.astype(F32)).astype(BF16)
    return x1, x2, x3


def _hgrn_chunks(chains, tri_ref, sel_ref, sgn_ref, msk_ref, heads, bd, nlev):
    t = chains[0][0].shape[0]

    def head_scores(qx, kx, half=None, upper=True):
        if half is not None:
            lo = half if upper else 0
            qx = jnp.concatenate([qx[b + lo:b + lo + half] for b in range(0, t, 2 * half)], axis=0)
        qx = qx.astype(BF16)
        rows = qx.shape[0]
        qh = jnp.concatenate([jnp.where(heads[hd], qx, jnp.zeros_like(qx)) for hd in range(C_HEADS)], axis=0)
        s = _dot_nt(qh, kx)
        out = [s[hd * rows:(hd + 1) * rows, :] for hd in range(C_HEADS)]
        if half is None:
            return out
        zero = jnp.zeros((half, s.shape[1]), F32)
        full = []
        for sh in out:
            blocks = []
            for i in range(rows // half):
                piece = sh[i * half:(i + 1) * half]
                blocks += [zero, piece] if upper else [piece, zero]
            full.append(jnp.concatenate(blocks, axis=0))
        return full

    cums, tots = [], []
    for q, k, vb, lf, st, dr in chains:
        l1, l2, l3 = _split3(lf)
        tri = tri_ref[dr]
        cums.append(_dot(tri, l1) + _dot(tri, l2) + _dot(tri, l3))
        tots.append(jnp.sum(lf, axis=0, keepdims=True))
    mids = [_dot(sel_ref[c[5]], cum.astype(BF16)) for c, cum in zip(chains, cums)]

    atts = []
    for q, k, vb, lf, st, dr in chains:
        diag = msk_ref[dr, nlev] > 0.5
        atts.append([jnp.where(diag, s, 0.0) for s in head_scores(q, k.astype(BF16))])
    for lev in range(nlev):
        for ci, (q, k, vb, lf, st, dr) in enumerate(chains):
            e = jnp.exp(sgn_ref[dr, lev] * (cums[ci] - mids[ci][lev * t:(lev + 1) * t, :]))
            m = msk_ref[dr, lev] > 0.5
            half = (1 << lev) if (1 << lev) >= HG_TRIM_MIN else None
            for hd, s in enumerate(head_scores(q * e, (k * e).astype(BF16), half, upper=(dr == 0))):
                atts[ci][hd] = atts[ci][hd] + jnp.where(m, s, 0.0)

    outs = []
    for ci, (q, k, vb, lf, st, dr) in enumerate(chains):
        o = _dot_nt((q * jnp.exp(cums[ci])).astype(BF16), st.astype(BF16))
        att_wide = jnp.concatenate([a.astype(BF16) for a in atts[ci]], axis=1)
        v_heads = jnp.concatenate([jnp.where(heads[hd], vb, jnp.zeros_like(vb)) for hd in range(C_HEADS)], axis=0)
        o = o + _dot(att_wide, v_heads)
        kend = (k * jnp.exp(tots[ci] - cums[ci])).astype(BF16)
        vt = vb.astype(F32).T.astype(BF16)
        new = st * jnp.exp(tots[ci]) + jnp.where(bd, _dot(vt, kend), 0.0)
        outs.append((o, new))
    return outs


def _hgrn_kernel(qf_ref, qr_ref, kf_ref, kr_ref, vf_ref, vr_ref, lff_ref, lfr_ref, s0_ref,
                 tri_ref, sel_ref, sgn_ref, msk_ref, hm_ref, bd_ref,
                 of_ref, or_ref, sf_ref, st_scr, *, nlev, nb):
    c = pl.program_id(0)

    @pl.when(c == 0)
    def _():
        st_scr[...] = s0_ref[...]

    heads = [hm_ref[pl.ds(hd, 1), :] > 0.5 for hd in range(C_HEADS)]
    bd = bd_ref[...] > 0.5
    dirs = ((qf_ref, kf_ref, vf_ref, lff_ref, of_ref), (qr_ref, kr_ref, vr_ref, lfr_ref, or_ref))
    chains, sinks = [], []
    for dr, (q_ref, k_ref, v_ref, lf_ref, o_ref) in enumerate(dirs):
        for b in range(nb):
            chains.append((q_ref[b].astype(F32), k_ref[0, b].astype(F32), v_ref[b], lf_ref[0, b], st_scr[dr, b], dr))
            sinks.append((o_ref, dr, b))
    outs = _hgrn_chunks(chains, tri_ref, sel_ref, sgn_ref, msk_ref, heads, bd, nlev)
    for (o_ref, dr, b), (o, new) in zip(sinks, outs):
        o_ref[b] = o
        st_scr[dr, b] = new
        sf_ref[dr, b] = new


def _hgrn_consts():
    t = HG_T
    ti = jnp.arange(t)[:, None]
    si = jnp.arange(t)[None, :]
    tri, sel, sgn, msk = [], [], [], []
    for reverse in (False, True):
        tri.append((si >= ti) if reverse else (si <= ti))
        sels, sgns, msks = [], [], []
        for lev in range(HG_LEVELS):
            h = 1 << lev
            blk_t, blk_s = ti // (2 * h), si // (2 * h)
            hi_t, hi_s = (ti % (2 * h)) >= h, (si % (2 * h)) >= h
            if reverse:
                mid = blk_t * 2 * h + h
                q_role_t, k_role_s = ~hi_t, hi_s
            else:
                mid = blk_t * 2 * h + h - 1
                q_role_t, k_role_s = hi_t, ~hi_s
            sels.append(si == mid)
            sgns.append(jnp.where(q_role_t, 1.0, -1.0))
            msks.append((blk_t == blk_s) & q_role_t & k_role_s)
        msks.append(ti == si)
        sel.append(jnp.concatenate(sels, axis=0))
        sgn.append(jnp.stack(sgns))
        msk.append(jnp.stack(msks))
    lane_head = jnp.arange(HG_W) // C_DK
    hm = (lane_head[None, :] == jnp.arange(C_HEADS)[:, None]).astype(F32)
    bd = (lane_head[:, None] == lane_head[None, :]).astype(F32)
    return (jnp.stack(tri).astype(BF16), jnp.stack(sel).astype(BF16), jnp.stack(sgn).astype(F32),
            jnp.stack(msk).astype(F32), hm, bd)


def _hgrn_scan(q, k, v, lf, s0, consts):
    tri, sel, sgn, msk, hm, bd = consts
    nb, s, w = q.shape
    t = HG_T
    nc = s // t
    fwd3 = pl.BlockSpec((nb, t, w), lambda c: (0, c, 0))
    rev3 = pl.BlockSpec((nb, t, w), lambda c: (0, nc - 1 - c, 0))
    fwd4 = pl.BlockSpec((1, nb, t, w), lambda c: (0, 0, c, 0))
    rev4 = pl.BlockSpec((1, nb, t, w), lambda c: (1, 0, nc - 1 - c, 0))
    whole = lambda a: pl.BlockSpec(a.shape, lambda c: (0,) * a.ndim)
    sgn = sgn.reshape(2, HG_LEVELS, t, 1)
    o_f, o_r, sf = pl.pallas_call(
        functools.partial(_hgrn_kernel, nlev=HG_LEVELS, nb=nb),
        grid=(nc,),
        in_specs=[fwd3, rev3, fwd4, rev4, fwd3, rev3, fwd4, rev4, whole(s0),
                  whole(tri), whole(sel), whole(sgn), whole(msk), whole(hm), whole(bd)],
        out_specs=[fwd3, rev3, whole(s0)],
        out_shape=[
            jax.ShapeDtypeStruct((nb, s, w), F32),
            jax.ShapeDtypeStruct((nb, s, w), F32),
            jax.ShapeDtypeStruct((2, nb, w, w), F32),
        ],
        scratch_shapes=[pltpu.VMEM((2, nb, w, w), F32)],
        compiler_params=_params(("arbitrary",)),
        name="hgrn_scan",
    )(q, q, k, k, v, v, lf, lf, s0, tri, sel, sgn, msk, hm, bd)
    return (o_f, o_r), sf


def _attn_kernel(*refs, nseg, tks, g, tq):
    q_ref = refs[0]
    kv_refs = refs[1:1 + 2 * nseg]
    o_ref = refs[1 + 2 * nseg]
    dv = o_ref.shape[-1]
    n = g * tq
    qt = jnp.concatenate([q_ref[0, hd] for hd in range(g)], axis=-1)

    def scores(k):
        s = _dot(k, qt)
        sub = min(ATTN_SUB, s.shape[0])
        parts = [s[i:i + sub] for i in range(0, s.shape[0], sub)]
        return tuple(x for p in parts for x in (p, jnp.max(p, axis=0, keepdims=True)))

    def absorb(sm, vt, carry):
        m, acc = carry
        sub = sm[0].shape[0]
        for i in range(len(sm) // 2):
            s, smax = sm[2 * i], sm[2 * i + 1]
            m_new = jnp.maximum(m, smax)
            alpha = jnp.exp2(m - m_new)
            p = jnp.exp2(s - m_new).astype(BF16)
            vt1 = jnp.concatenate([vt[:, i * sub:(i + 1) * sub], jnp.ones((ATTN_ONES, sub), BF16)], axis=0)
            acc = alpha * acc + _dot(vt1, p)
            m = m_new
        return m, acc

    carry = (jnp.full((1, n), -jnp.inf, F32), jnp.zeros((dv + ATTN_ONES, n), F32))
    ahead = []
    for seg in range(nseg):
        k_ref, vt_ref = kv_refs[2 * seg], kv_refs[2 * seg + 1]
        tk = tks[seg]
        for off in range(0, k_ref.shape[2], tk):
            ahead.append((scores(k_ref[0, 0, off:off + tk, :]), vt_ref[0, 0, :, off:off + tk]))
            if len(ahead) > ATTN_AHEAD:
                carry = absorb(*ahead.pop(0), carry)
    for item in ahead:
        carry = absorb(*item, carry)
    m, acc = carry
    out = acc[:dv] / acc[dv:dv + 1, :]
    out = jnp.concatenate([out, jnp.zeros((LANES - dv, n), F32)], axis=0).T
    o_ref[0] = out[:, :dv].reshape(g, tq, dv).astype(o_ref.dtype)


def _attention(q, kvs):
    nb, hq, dk, sq = q.shape
    hkv = kvs[0][0].shape[1]
    dv = kvs[0][1].shape[2]
    g = hq // hkv
    tq = _pick(sq, tuple(n // g for n in ATTN_QUERIES))
    tks = tuple(_pick(k.shape[2], (ATTN_SUB, 128)) for k, _ in kvs)
    in_specs = [pl.BlockSpec((1, g, dk, tq), lambda b, h, i: (b, h, 0, i))]
    args = [q]
    for k, v in kvs:
        sk = k.shape[2]
        in_specs.append(pl.BlockSpec((1, 1, sk, dk), lambda b, h, i: (b, h, 0, 0)))
        in_specs.append(pl.BlockSpec((1, 1, dv, sk), lambda b, h, i: (b, h, 0, 0)))
        args += [k, v]
    return pl.pallas_call(
        functools.partial(_attn_kernel, nseg=len(kvs), tks=tks, g=g, tq=tq),
        grid=(nb, hkv, sq // tq),
        in_specs=in_specs,
        out_specs=pl.BlockSpec((1, g, tq, dv), lambda b, h, i: (b, h, i, 0)),
        out_shape=jax.ShapeDtypeStruct((nb, hq, sq, dv), BF16),
        compiler_params=_params(("parallel", "parallel", "arbitrary")),
        name="attention",
    )(*args)


def _merge_kernel(x_ref, mod_ref, gpre_ref, gpost_ref, wg_ref, wb_ref, wo_ref,
                  u_ref, yf_ref, yr_ref, sd_ref, wglu_ref, bo_ref, hof_ref, hor_ref, hg_ref, hn_ref, avg_ref, do_ref,
                  o_ref):
    x = x_ref[0]
    shift = mod_ref[0, pl.ds(3, 1), :]
    scale = mod_ref[0, pl.ds(4, 1), :]
    gate = mod_ref[0, pl.ds(5, 1), :]
    h = (_rms(x, gpre_ref[...]) * (1.0 + scale) + shift).astype(BF16)
    d = x.shape[-1]

    def branch_gate(i):
        return jax.nn.sigmoid(_dot(h, wg_ref[:, i * d:(i + 1) * d]))

    ysum = yf_ref[0] + yr_ref[0]
    y = sd_ref[...] * u_ref[0].astype(F32) + jnp.concatenate([ysum[hf] for hf in range(A_WIDTH // LANES)], axis=-1)
    ge = jax.nn.gelu(y)
    ya = ge * jax.nn.sigmoid(_dot(ge.astype(BF16), wglu_ref[...]))
    merged = branch_gate(0) * _dot(ya.astype(BF16), wb_ref[0])

    yb = jnp.concatenate([bo_ref[0, hd] for hd in range(B_HEADS)], axis=-1)
    merged = merged + branch_gate(1) * _dot(yb, wb_ref[1])

    o2 = hof_ref[0] + hor_ref[0]
    ms = _dot((o2 * o2).astype(BF16), avg_ref[...])
    gz = hg_ref[0].astype(F32)
    yc = o2 * lax.rsqrt(ms + EPS) * hn_ref[...] * (gz * jax.nn.sigmoid(gz))
    merged = merged + branch_gate(2) * _dot(yc.astype(BF16), wb_ref[2])

    yd = jnp.concatenate([do_ref[0, hd] for hd in range(D_HEADS)], axis=-1)
    merged = merged + branch_gate(3) * _dot(yd, wb_ref[3])

    yo = _dot(merged.astype(BF16), wo_ref[...])
    o_ref[0] = x + gate * _rms(yo, gpost_ref[...])


def _merge(x, mod, g_pre, g_post, wg, wb, wo, u, yf, yr, s5d, wglu, bo, ho, hg, hn, avg, do):
    nb, s, d = x.shape
    tm = _pick(s, (512, 256))
    const2 = lambda b, i: (0, 0)
    const3 = lambda b, i: (0, 0, 0)
    tok = lambda w: pl.BlockSpec((1, tm, w), lambda b, i: (b, i, 0))
    halves = pl.BlockSpec((1, A_WIDTH // LANES, tm, LANES), lambda b, i: (b, 0, i, 0))
    return pl.pallas_call(
        _merge_kernel,
        grid=(nb, s // tm),
        in_specs=[
            tok(d),
            pl.BlockSpec((1, N_MOD, d), lambda b, i: (b, 0, 0)),
            pl.BlockSpec((1, d), const2),
            pl.BlockSpec((1, d), const2),
            pl.BlockSpec((d, N_BRANCH * d), const2),
            pl.BlockSpec((N_BRANCH, BRANCH_W, d), const3),
            pl.BlockSpec((d, d), const2),
            tok(A_WIDTH), halves, halves,
            pl.BlockSpec((1, A_WIDTH), const2),
            pl.BlockSpec((A_WIDTH, A_WIDTH), const2),
            pl.BlockSpec((1, B_HEADS, tm, B_V), lambda b, i: (b, 0, i, 0)),
            tok(HG_W), tok(HG_W),
            tok(HG_W),
            pl.BlockSpec((1, HG_W), const2),
            pl.BlockSpec((HG_W, HG_W), const2),
            pl.BlockSpec((1, D_HEADS, tm, D_HEAD), lambda b, i: (b, 0, i, 0)),
        ],
        out_specs=tok(d),
        out_shape=jax.ShapeDtypeStruct(x.shape, F32),
        compiler_params=_params(("parallel", "parallel")),
        name="merge_out",
    )(x, mod, g_pre.reshape(1, d), g_post.reshape(1, d), wg, wb, wo, u, yf, yr, s5d, wglu, bo, ho[0], ho[1], hg, hn,
      avg, do)


def _pad_cols(w, width):
    return jnp.pad(w, ((0, 0), (0, width - w.shape[1])))


def _proj_weight(w_in_mix):
    offs = [0]
    for wdt in (A_WIDTH, B_Q_LORA, B_KV_LORA, B_ROPE, HG_W, HG_W, HG_W, HG_W, HG_W,
                D_HEADS * D_HEAD, D_KV_HEADS * D_HEAD, D_KV_HEADS * D_HEAD):
        offs.append(offs[-1] + wdt)
    p = [w_in_mix[:, offs[i]:offs[i + 1]] for i in range(12)]
    d = w_in_mix.shape[0]
    z = lambda n: jnp.zeros((d, n), w_in_mix.dtype)
    cols = [p[0], _pad_cols(p[1], 256), p[2],
            jnp.concatenate([z(B_NOPE), p[3], z(LANES - B_NOPE - B_ROPE)], axis=1),
            p[4], p[5], p[6], p[7], p[8]]
    for i, nh in ((9, D_HEADS), (10, D_KV_HEADS), (11, D_KV_HEADS)):
        for hd in range(nh):
            cols.append(_pad_cols(p[i][:, hd * D_HEAD:(hd + 1) * D_HEAD], LANES))
    return jnp.concatenate(cols, axis=1).astype(BF16)


def _rope_tables(n_tok, rot_dim, lane_off, identity):
    cos = jnp.ones((n_tok, LANES), F32)
    sin_a = jnp.zeros((n_tok, LANES), F32)
    sin_b = jnp.zeros((n_tok, LANES), F32)
    if not identity:
        n_rows = n_tok // GRID_W
        rows = jnp.repeat(jnp.arange(n_rows, dtype=F32), GRID_W)
        cols = jnp.tile(jnp.arange(GRID_W, dtype=F32), n_rows)
        half = rot_dim // 2
        inv = ROPE_THETA ** (-jnp.arange(0, half, 2, dtype=F32) / half)
        ang_r = rows[:, None] * inv
        ang_c = cols[:, None] * inv
        ang = jnp.concatenate([ang_r, ang_r, ang_c, ang_c], axis=-1)
        c, s = jnp.cos(ang), jnp.sin(ang)
        quarter = rot_dim // 4
        first = (np.arange(rot_dim) % (2 * quarter)) < quarter

        def place(a, fill):
            left = jnp.full((n_tok, lane_off), fill, F32)
            right = jnp.full((n_tok, LANES - lane_off - rot_dim), fill, F32)
            return jnp.concatenate([left, a, right], axis=1)

        cos = place(c, 1.0)
        sin_a = place(jnp.where(first, -s, 0.0), 0.0)
        sin_b = place(jnp.where(first, 0.0, s), 0.0)
    return jnp.stack([cos, sin_a, sin_b])


def kernel(x, c, ctx, c_ctx, w_ada, b_ada, norm_pre, norm_post, ffn_w1, ffn_w3, ffn_w2, w_in,
           s5_lambda_re, s5_lambda_im, s5_log_dt, s5_b_re, s5_b_im, s5_c_re, s5_c_im, s5_d, s5_w_glu,
           mla_q_norm, mla_w_uq, mla_kv_norm, mla_w_ukv, hgrn_lb_raw, hgrn_o_norm,
           gqa_q_norm, gqa_k_norm, w_branch, w_out):
    nb, seq, d = x.shape
    n_ctx = ctx.shape[1]
    depth = w_ada.shape[0]

    rows = max(8, -(-(nb + 1) // 8) * 8)
    cvec = jnp.zeros((rows, d), F32).at[:nb].set(c).at[nb].set(c_ctx)
    mod_all = _modulation(cvec, w_ada, b_ada).reshape(depth, rows, N_MOD, d)

    lb_step = jax.nn.softmax(hgrn_lb_raw.astype(F32), axis=1)
    lb_all = jnp.clip(jnp.cumsum(lb_step, axis=1) - lb_step[:, :1], 0.0, 1.0)

    rope_b_lat = _rope_tables(seq, B_ROPE, B_NOPE, False)
    rope_d_lat = _rope_tables(seq, D_HEAD, 0, False)
    rope_b_ctx = _rope_tables(n_ctx, B_ROPE, B_NOPE, True)
    rope_d_ctx = _rope_tables(n_ctx, D_HEAD, 0, True)
    hg_consts = _hgrn_consts()
    s5_w = _s5_weights(s5_lambda_re, s5_lambda_im, s5_log_dt, s5_b_re, s5_b_im, s5_c_re, s5_c_im, S5_MAX_LEVELS)
    lane_head = jnp.arange(HG_W) // C_DV
    avg = (lane_head[:, None] == lane_head[None, :]).astype(BF16) * (1.0 / C_DV)
    n_mixcols = w_in.shape[-1] - N_BRANCH * d
    ffn_w = (ffn_w1.astype(BF16), ffn_w3.astype(BF16), ffn_w2.astype(BF16))

    x_lat, x_ctx = x, ctx
    for layer in range(depth):
        last = layer == depth - 1
        mod_lat = mod_all[layer, :nb]
        mod_ctx = jnp.broadcast_to(mod_all[layer, nb:nb + 1], (nb, N_MOD, d))
        bf = lambda w: w.astype(BF16)
        ffn_a = (norm_pre[layer, 0], norm_post[layer, 0], *ffn_w, layer, 0)
        ffn_b = (norm_pre[layer, 2], norm_post[layer, 2], *ffn_w, layer, 1)

        x_lat = _ffn(x_lat, mod_lat, 0, *ffn_a)
        x_ctx = _ffn(x_ctx, mod_ctx, 0, *ffn_a)

        wp = _proj_weight(w_in[layer, :, :n_mixcols])
        wg = bf(w_in[layer, :, n_mixcols:])
        qn = _pad_cols(mla_q_norm[layer].reshape(1, B_Q_LORA), 256)
        wuq = mla_w_uq[layer].reshape(B_Q_LORA, B_HEADS, B_NOPE + B_ROPE)
        wuq = jnp.pad(wuq, ((0, 256 - B_Q_LORA), (0, 0), (0, LANES - B_NOPE - B_ROPE))).reshape(256, B_HEADS * LANES)
        wukv = mla_w_ukv[layer].reshape(B_KV_LORA, B_HEADS, B_NOPE + B_V)
        wuk = jnp.pad(wukv[:, :, :B_NOPE], ((0, 0), (0, 0), (0, LANES - B_NOPE))).reshape(B_KV_LORA, B_HEADS * LANES)
        wuv = jnp.pad(wukv[:, :, B_NOPE:], ((0, 0), (0, 0), (0, LANES - B_V))).reshape(B_KV_LORA, B_HEADS * LANES)
        kvn = mla_kv_norm[layer].reshape(1, B_KV_LORA)
        gqn = _pad_cols(gqa_q_norm[layer].reshape(1, D_HEAD), LANES)
        gkn = _pad_cols(gqa_k_norm[layer].reshape(1, D_HEAD), LANES)
        lb = lb_all[:, layer]
        proj_args = (norm_pre[layer, 1], wp)
        mla_args = (qn, bf(wuq), kvn, bf(wuk), bf(wuv), lb, gqn, gkn)

        pl_ = _inproj(x_lat, mod_lat, *proj_args, rope_b_lat, rope_d_lat, *mla_args)
        pc_ = _inproj(x_ctx, mod_ctx, *proj_args, rope_b_ctx, rope_d_ctx, *mla_args)
        (u_l, u2_l, bq_l, bk_l, bv_l, hq_l, hv_l, hk_l, hl_l, hg_l, dq_l, dk_l, dv_l) = pl_
        (u_c, u2_c, bq_c, bk_c, bv_c, hq_c, hv_c, hk_c, hl_c, hg_c, dq_c, dk_c, dv_c) = pc_

        ys_l, ys_c = [], []
        for dr, reverse in enumerate((False, True)):
            wts = tuple(w[layer, dr] for w in s5_w)
            x0 = jnp.zeros((nb, 1, 2 * S5_NSTATE), F32)
            y_c, x_end = _s5_scan(u2_c, x0, wts, reverse)
            y_l, _ = _s5_scan(u2_l, x_end, wts, reverse)
            ys_l.append(y_l)
            ys_c.append(y_c)

        s0 = jnp.zeros((2, nb, HG_W, HG_W), F32)
        ho_c, s_ctx = _hgrn_scan(hq_c, hk_c, hv_c, hl_c, s0, hg_consts)
        ho_l, _ = _hgrn_scan(hq_l, hk_l, hv_l, hl_l, s_ctx, hg_consts)

        bo_l = _attention(bq_l, [(bk_c, bv_c), (bk_l, bv_l)])
        do_l = _attention(dq_l, [(dk_c, dv_c), (dk_l, dv_l)])

        merge_w = (norm_pre[layer, 1], norm_post[layer, 1], wg, bf(w_branch[layer]), bf(w_out[layer]))
        s5_ro = (s5_d[layer].reshape(1, A_WIDTH), bf(s5_w_glu[layer]))
        hn = jnp.tile(hgrn_o_norm[layer], C_HEADS).reshape(1, HG_W)
        x_lat_new = _merge(x_lat, mod_lat, *merge_w, u_l, ys_l[0], ys_l[1], *s5_ro, bo_l, ho_l, hg_l, hn, avg, do_l)
        if not last:
            bo_c = _attention(bq_c, [(bk_c, bv_c)])
            do_c = _attention(dq_c, [(dk_c, dv_c)])
            x_ctx = _merge(x_ctx, mod_ctx, *merge_w, u_c, ys_c[0], ys_c[1], *s5_ro, bo_c, ho_c, hg_c, hn, avg, do_c)
            x_ctx = _ffn(x_ctx, mod_ctx, 2, *ffn_b)
        x_lat = _ffn(x_lat_new, mod_lat, 2, *ffn_b)
    return x_lat
```

```python
import functools

import jax
import jax.numpy as jnp
import numpy as np
from jax import lax
from jax.experimental import pallas as pl
from jax.experimental.pallas import tpu as pltpu

GRID_W = 64
FFN_RES_WEIGHT = 0.5
N_MOD = 9
EPS = 1e-6
ROPE_THETA = 10000.0
F_FLOOR = 1e-20

A_WIDTH = 256
A_GROUP = 16
A_GROUPS = A_WIDTH // A_GROUP
A_STATE = 64

B_HEADS = 4
B_NOPE = 64
B_ROPE = 32
B_V = 64
B_Q_LORA = 192
B_KV_LORA = 128

C_HEADS = 4
C_DK = 64
C_DV = 64

D_HEADS = 4
D_KV_HEADS = 2
D_HEAD = 64

N_BRANCH = 4
BRANCH_W = 256

LANES = 128
VMEM_LIMIT_BYTES = 56 * 1024 * 1024

S5_R = 8
S5_LANES = S5_R * A_WIDTH
S5_NSTATE = A_GROUPS * A_STATE
S5_MAX_ROWS = 512
S5_MAX_LEVELS = 9
HG_T = 128
HG_LEVELS = 7
HG_W = C_HEADS * C_DK
HG_TRIM_MIN = 8
ATTN_ONES = 16
ATTN_QUERIES = (512, 256, 128)
ATTN_SUB = 256

SLOT_S5 = 0
SLOT_CQ = 256
SLOT_CKV = 512
SLOT_KR = 640
SLOT_HQ = 768
SLOT_HV = 1024
SLOT_HF = 1280
SLOT_HB = 1536
SLOT_HG = 1792
SLOT_GQ = 2048
SLOT_GK = 2560
SLOT_GV = 2816
N_PROJ = 3072

BF16 = jnp.bfloat16
F32 = jnp.float32
LOG2E = 1.4426950408889634


def _params(sem, flags=None):
    return pltpu.CompilerParams(dimension_semantics=sem, vmem_limit_bytes=VMEM_LIMIT_BYTES, flags=flags)


def _pick(n, candidates):
    for c in candidates:
        if n % c == 0:
            return c
    raise ValueError(f"no tile for {n} in {candidates}")


def _dot(a, b):
    return jnp.dot(a, b, preferred_element_type=F32)


def _dot_nt(a, b):
    return lax.dot_general(a, b, (((1,), (1,)), ((), ())), preferred_element_type=F32)


def _rms(x, g, n=None):
    n = x.shape[-1] if n is None else n
    ms = jnp.sum(x * x, axis=-1, keepdims=True) * (1.0 / n)
    return x * lax.rsqrt(ms + EPS) * g


def _mod_kernel(c_ref, w_ref, b_ref, o_ref):
    c = c_ref[...]
    a = (c * jax.nn.sigmoid(c)).astype(BF16)
    o_ref[0] = _dot(a, w_ref[0].astype(BF16)) + b_ref[0]


def _modulation(cvec, w_ada, b_ada):
    nl, d, nm = w_ada.shape
    rows = cvec.shape[0]
    tn = _pick(nm, (1152, 1024, 512, 256, 128))
    return pl.pallas_call(
        _mod_kernel,
        grid=(nl, nm // tn),
        in_specs=[
            pl.BlockSpec((rows, d), lambda l, n: (0, 0)),
            pl.BlockSpec((1, d, tn), lambda l, n: (l, 0, n)),
            pl.BlockSpec((1, 1, tn), lambda l, n: (l, 0, n)),
        ],
        out_specs=pl.BlockSpec((1, rows, tn), lambda l, n: (l, 0, n)),
        out_shape=jax.ShapeDtypeStruct((nl, rows, nm), F32),
        compiler_params=_params(("parallel", "parallel")),
        name="adaln_mod",
    )(cvec, w_ada, b_ada.reshape(nl, 1, nm))


def _ffn_kernel(x_ref, mod_ref, gpre_ref, gpost_ref, w1_ref, w3_ref, w2_ref, o_ref, *, j, tf):
    x = x_ref[0]
    shift = mod_ref[0, pl.ds(3 * j, 1), :]
    scale = mod_ref[0, pl.ds(3 * j + 1, 1), :]
    gate = mod_ref[0, pl.ds(3 * j + 2, 1), :]
    h = (_rms(x, gpre_ref[...]) * (1.0 + scale) + shift).astype(BF16)
    acc = None
    for f in range(w1_ref.shape[1] // tf):
        a = _dot(h, w1_ref[:, f * tf:(f + 1) * tf])
        b = _dot(h, w3_ref[:, f * tf:(f + 1) * tf])
        t = (a * jax.nn.sigmoid(a) * b).astype(BF16)
        part = _dot(t, w2_ref[f * tf:(f + 1) * tf, :])
        acc = part if acc is None else acc + part
    o_ref[0] = x + FFN_RES_WEIGHT * gate * _rms(acc, gpost_ref[...])


def _ffn(x, mod, j, g_pre, g_post, w1, w3, w2, layer, which):
    nb, s, d = x.shape
    dff = w1.shape[-1]
    tm = _pick(s, (512, 256))
    tf = _pick(dff, (256, 128))
    once = pl.Buffered(1)
    pick = lambda b, i: (layer, which, 0, 0)
    return pl.pallas_call(
        functools.partial(_ffn_kernel, j=j, tf=tf),
        grid=(nb, s // tm),
        in_specs=[
            pl.BlockSpec((1, tm, d), lambda b, i: (b, i, 0)),
            pl.BlockSpec((1, N_MOD, d), lambda b, i: (b, 0, 0)),
            pl.BlockSpec((1, d), lambda b, i: (0, 0)),
            pl.BlockSpec((1, d), lambda b, i: (0, 0)),
            pl.BlockSpec((None, None, d, dff), pick, pipeline_mode=once),
            pl.BlockSpec((None, None, d, dff), pick, pipeline_mode=once),
            pl.BlockSpec((None, None, dff, d), pick, pipeline_mode=once),
        ],
        out_specs=pl.BlockSpec((1, tm, d), lambda b, i: (b, i, 0)),
        out_shape=jax.ShapeDtypeStruct(x.shape, F32),
        compiler_params=_params(("parallel", "parallel")),
        name="ffn_sublayer",
    )(x, mod, g_pre.reshape(1, d), g_post.reshape(1, d), w1, w3, w2)


def _rope(x, cos, sin_a, sin_b, quarter):
    w = x.shape[-1]
    return x * cos + pltpu.roll(x, w - quarter, 1) * sin_a + pltpu.roll(x, quarter, 1) * sin_b


def _inproj_kernel(x_ref, mod_ref, gpre_ref, w_ref, rb_ref, rd_ref, qn_ref, wuq_ref, kvn_ref, wuk_ref, wuv_ref,
                   lb_ref, gqn_ref, gkn_ref,
                   u_ref, u2_ref, bq_ref, bk_ref, bv_ref, hq_ref, hv_ref, hk_ref, hl_ref, hg_ref,
                   dq_ref, dk_ref, dv_ref, h_scr, u_scr, *, tm):
    x = x_ref[0]
    shift = mod_ref[0, pl.ds(3, 1), :]
    scale = mod_ref[0, pl.ds(4, 1), :]
    h_scr[...] = (_rms(x, gpre_ref[...]) * (1.0 + scale) + shift).astype(BF16)

    def proj(lo, width):
        return _dot(h_scr[...], w_ref[:, lo:lo + width])

    cos_d, sa_d, sb_d = rd_ref[0], rd_ref[1], rd_ref[2]
    d_scale = D_HEAD ** -0.5 * LOG2E
    gq = proj(SLOT_GQ, D_HEADS * LANES)
    for hd in range(D_HEADS):
        qh = _rms(gq[:, hd * LANES:(hd + 1) * LANES], gqn_ref[...], n=D_HEAD)
        qh = _rope(qh, cos_d, sa_d, sb_d, D_HEAD // 4) * d_scale
        dq_ref[0, hd] = qh.T[:D_HEAD].astype(BF16)
    gkv = proj(SLOT_GK, 2 * D_KV_HEADS * LANES)
    for hd in range(D_KV_HEADS):
        kh = _rms(gkv[:, hd * LANES:(hd + 1) * LANES], gkn_ref[...], n=D_HEAD)
        kh = _rope(kh, cos_d, sa_d, sb_d, D_HEAD // 4)
        dk_ref[0, hd] = kh[:, :D_HEAD].astype(BF16)
        dv_ref[0, hd] = gkv[:, (D_KV_HEADS + hd) * LANES:(D_KV_HEADS + hd + 1) * LANES].T[:D_HEAD].astype(BF16)

    cos_b, sa_b, sb_b = rb_ref[0], rb_ref[1], rb_ref[2]
    cq = _rms(proj(SLOT_CQ, 256), qn_ref[...], n=B_Q_LORA).astype(BF16)
    q = _dot(cq, wuq_ref[...])
    b_scale = (B_NOPE + B_ROPE) ** -0.5 * LOG2E
    for hd in range(B_HEADS):
        qh = _rope(q[:, hd * LANES:(hd + 1) * LANES], cos_b, sa_b, sb_b, B_ROPE // 4)
        bq_ref[0, hd] = (qh * b_scale).T.astype(BF16)
    ckr = proj(SLOT_CKV, B_KV_LORA + LANES)
    ckv = _rms(ckr[:, :B_KV_LORA], kvn_ref[...]).astype(BF16)
    kn = _dot(ckv, wuk_ref[...])
    vn = _dot(ckv, wuv_ref[...])
    kr = _rope(ckr[:, B_KV_LORA:], cos_b, sa_b, sb_b, B_ROPE // 4)
    for hd in range(B_HEADS):
        bk_ref[0, hd] = (kn[:, hd * LANES:(hd + 1) * LANES] + kr).astype(BF16)
        bv_ref[0, hd] = vn[:, hd * LANES:(hd + 1) * LANES].T[:B_V].astype(BF16)

    hq_ref[0] = proj(SLOT_HQ, HG_W).astype(BF16)
    hv_ref[0] = proj(SLOT_HV, HG_W).astype(BF16)
    hg_ref[0] = proj(SLOT_HG, HG_W).astype(BF16)
    for dr, slot in enumerate((SLOT_HF, SLOT_HB)):
        z = proj(slot, HG_W)
        lb = lb_ref[pl.ds(dr, 1), :]
        f = lb + (1.0 - lb) * jax.nn.sigmoid(z)
        hl_ref[dr, 0] = jnp.log(jnp.maximum(f, F_FLOOR))
        hk_ref[dr, 0] = ((1.0 - lb) * jax.nn.sigmoid(-z)).astype(BF16)

    u = proj(SLOT_S5, A_WIDTH)
    u_ref[0] = u.astype(BF16)
    for hf in range(A_WIDTH // LANES):
        u_scr[hf] = u[:, hf * LANES:(hf + 1) * LANES]
    for r in range(S5_R):
        for hf in range(A_WIDTH // LANES):
            lo = r * A_WIDTH + hf * LANES
            u2_ref[0, :, lo:lo + LANES] = u_scr[hf, pl.ds(r, tm // S5_R, stride=S5_R), :].astype(BF16)


def _inproj(x, mod, g_pre, wp, rope_b, rope_d, qn, wuq, kvn, wuk, wuv, lb, gqn, gkn):
    nb, s, d = x.shape
    tm = _pick(s, (512, 256))
    const2 = lambda b, i: (0, 0)
    tok = lambda w: pl.BlockSpec((1, tm, w), lambda b, i: (b, i, 0))
    headed = lambda nh, w: pl.BlockSpec((1, nh, tm, w), lambda b, i: (b, 0, i, 0))
    headed_t = lambda nh, w: pl.BlockSpec((1, nh, w, tm), lambda b, i: (b, 0, 0, i))
    dirtok = lambda w: pl.BlockSpec((2, 1, tm, w), lambda b, i: (0, b, i, 0))
    sd = jax.ShapeDtypeStruct
    outs = [
        (sd((nb, s, A_WIDTH), BF16), tok(A_WIDTH)),
        (sd((nb, s // S5_R, S5_LANES), BF16), pl.BlockSpec((1, tm // S5_R, S5_LANES), lambda b, i: (b, i, 0))),
        (sd((nb, B_HEADS, LANES, s), BF16), headed_t(B_HEADS, LANES)),
        (sd((nb, B_HEADS, s, LANES), BF16), headed(B_HEADS, LANES)),
        (sd((nb, B_HEADS, B_V, s), BF16), headed_t(B_HEADS, B_V)),
        (sd((nb, s, HG_W), BF16), tok(HG_W)),
        (sd((nb, s, HG_W), BF16), tok(HG_W)),
        (sd((2, nb, s, HG_W), BF16), dirtok(HG_W)),
        (sd((2, nb, s, HG_W), F32), dirtok(HG_W)),
        (sd((nb, s, HG_W), BF16), tok(HG_W)),
        (sd((nb, D_HEADS, D_HEAD, s), BF16), headed_t(D_HEADS, D_HEAD)),
        (sd((nb, D_KV_HEADS, s, D_HEAD), BF16), headed(D_KV_HEADS, D_HEAD)),
        (sd((nb, D_KV_HEADS, D_HEAD, s), BF16), headed_t(D_KV_HEADS, D_HEAD)),
    ]
    return pl.pallas_call(
        functools.partial(_inproj_kernel, tm=tm),
        grid=(nb, s // tm),
        in_specs=[
            pl.BlockSpec((1, tm, d), lambda b, i: (b, i, 0)),
            pl.BlockSpec((1, N_MOD, d), lambda b, i: (b, 0, 0)),
            pl.BlockSpec((1, d), const2),
            pl.BlockSpec((d, N_PROJ), const2),
            pl.BlockSpec((3, tm, LANES), lambda b, i: (0, i, 0)),
            pl.BlockSpec((3, tm, LANES), lambda b, i: (0, i, 0)),
            pl.BlockSpec((1, 256), const2),
            pl.BlockSpec((256, B_HEADS * LANES), const2),
            pl.BlockSpec((1, B_KV_LORA), const2),
            pl.BlockSpec((B_KV_LORA, B_HEADS * LANES), const2),
            pl.BlockSpec((B_KV_LORA, B_HEADS * LANES), const2),
            pl.BlockSpec((2, HG_W), const2),
            pl.BlockSpec((1, LANES), const2),
            pl.BlockSpec((1, LANES), const2),
        ],
        out_specs=[o[1] for o in outs],
        out_shape=[o[0] for o in outs],
        scratch_shapes=[pltpu.VMEM((tm, d), BF16), pltpu.VMEM((A_WIDTH // LANES, tm, LANES), F32)],
        compiler_params=_params(("parallel", "parallel")),
        name="mixer_inproj",
    )(x, mod, g_pre.reshape(1, d), wp, rope_b, rope_d, qn, wuq, kvn, wuk, wuv, lb, gqn, gkn)


def _s5_kernel(u2_ref, x0_ref, lag_ref, wst_ref, wout_ref, pq_ref, y_ref, xf_ref, carry, *, rows, reverse, nlev):
    i = pl.program_id(1)

    @pl.when(i == 0)
    def _():
        carry[...] = x0_ref[0]

    u2 = u2_ref[0]
    sloc = _dot(u2, wst_ref[...])
    y_local = []
    for b in range(S5_R):
        acc = None
        for a in (range(b, S5_R) if reverse else range(b + 1)):
            part = _dot(u2[:, a * A_WIDTH:(a + 1) * A_WIDTH], lag_ref[abs(b - a)])
            acc = part if acc is None else acc + part
        y_local.append(acc)
    ridx = lax.broadcasted_iota(jnp.int32, (rows, 1), 0)

    def cmul(xv, lev):
        p = pq_ref[pl.ds(2 * lev, 1), :]
        q = pq_ref[pl.ds(2 * lev + 1, 1), :]
        return xv * p + pltpu.roll(xv, S5_NSTATE, 1) * q

    if reverse:
        e = jnp.where(ridx == rows - 1, carry[...], pltpu.roll(sloc, rows - 1, 0))
    else:
        e = jnp.where(ridx == 0, carry[...], pltpu.roll(sloc, 1, 0))
    xin = e
    for lev in range(nlev):
        dist = 1 << lev
        if reverse:
            sh = jnp.where(ridx < rows - dist, pltpu.roll(xin, rows - dist, 0), 0.0)
        else:
            sh = jnp.where(ridx >= dist, pltpu.roll(xin, dist, 0), 0.0)
        xin = xin + cmul(sh, lev)

    last = 0 if reverse else rows - 1
    nxt = cmul(xin[last:last + 1, :], 0) + sloc[last:last + 1, :]
    carry[...] = nxt
    xf_ref[0] = nxt

    xb = xin.astype(BF16)
    for b in range(S5_R):
        yb = y_local[b] + _dot(xb, wout_ref[b])
        for hf in range(A_WIDTH // LANES):
            y_ref[0, hf, pl.ds(b, rows, stride=S5_R), :] = yb[:, hf * LANES:(hf + 1) * LANES]


def _s5_scan(u2, x0, wts, reverse):
    lag, wst, wout, pq = wts
    nb, n2, _ = u2.shape
    rows = _pick(n2, (S5_MAX_ROWS, 256, 128, 64, 32))
    nt = n2 // rows
    once = pl.Buffered(1)
    nlev = max(1, (rows - 1).bit_length())
    order = (lambda b, i: (b, nt - 1 - i, 0)) if reverse else (lambda b, i: (b, i, 0))
    order_out = (lambda b, i: (b, 0, nt - 1 - i, 0)) if reverse else (lambda b, i: (b, 0, i, 0))
    const2 = lambda b, i: (0, 0)
    y, xf = pl.pallas_call(
        functools.partial(_s5_kernel, rows=rows, reverse=reverse, nlev=nlev),
        grid=(nb, nt),
        in_specs=[
            pl.BlockSpec((1, rows, S5_LANES), order),
            pl.BlockSpec((1, 1, 2 * S5_NSTATE), lambda b, i: (b, 0, 0)),
            pl.BlockSpec((S5_R, A_WIDTH, A_WIDTH), lambda b, i: (0, 0, 0), pipeline_mode=once),
            pl.BlockSpec((S5_LANES, 2 * S5_NSTATE), const2, pipeline_mode=once),
            pl.BlockSpec((S5_R, 2 * S5_NSTATE, A_WIDTH), lambda b, i: (0, 0, 0), pipeline_mode=once),
            pl.BlockSpec(pq.shape, const2),
        ],
        out_specs=[
            pl.BlockSpec((1, A_WIDTH // LANES, rows * S5_R, LANES), order_out),
            pl.BlockSpec((1, 1, 2 * S5_NSTATE), lambda b, i: (b, 0, 0)),
        ],
        out_shape=[
            jax.ShapeDtypeStruct((nb, A_WIDTH // LANES, n2 * S5_R, LANES), F32),
            jax.ShapeDtypeStruct((nb, 1, 2 * S5_NSTATE), F32),
        ],
        scratch_shapes=[pltpu.VMEM((1, 2 * S5_NSTATE), F32)],
        compiler_params=_params(("parallel", "arbitrary")),
        name="s5_scan_rev" if reverse else "s5_scan_fwd",
    )(u2, x0, lag, wst, wout, pq)
    return y, xf


def _s5_weights(lam_re, lam_im, log_dt, b_re, b_im, c_re, c_im, nlev):
    g, n, r = A_GROUPS, A_STATE, S5_R
    nl = lam_re.shape[0]
    lam_re = jnp.minimum(lam_re.astype(F32), -1e-4)
    lam_im = lam_im.astype(F32)
    dt = jnp.exp(log_dt.astype(F32))[..., None]
    mag = jnp.exp(lam_re * dt)
    a_re = mag * jnp.cos(lam_im * dt)
    a_im = mag * jnp.sin(lam_im * dt)
    den = lam_re * lam_re + lam_im * lam_im
    num_re = a_re - 1.0
    f_re = (num_re * lam_re + a_im * lam_im) / den
    f_im = (a_im * lam_re - num_re * lam_im) / den
    bb_re = f_re[..., None] * b_re - f_im[..., None] * b_im
    bb_im = f_re[..., None] * b_im + f_im[..., None] * b_re

    def cm(xr, xi, yr, yi):
        return xr * yr - xi * yi, xr * yi + xi * yr

    pr, pi = [jnp.ones_like(a_re)], [jnp.zeros_like(a_im)]
    for _ in range(r):
        nr, ni = cm(pr[-1], pi[-1], a_re, a_im)
        pr.append(nr)
        pi.append(ni)
    pr, pi = jnp.stack(pr), jnp.stack(pi)

    xr, xi = cm(pr[..., None], pi[..., None], bb_re, bb_im)
    yr, yi = cm(c_re, c_im, pr[:, :, :, :, None, :], pi[:, :, :, :, None, :])
    lag = jnp.einsum('ldgon,tldgnc->tldgoc', c_re, xr) - jnp.einsum('ldgon,tldgnc->tldgoc', c_im, xi)

    def group_diag(m):
        ni, nj = m.shape[-2:]
        wide = jnp.swapaxes(m, -3, -2).reshape(m.shape[:-3] + (1, ni, g * nj))
        keep = (np.arange(g)[:, None, None] == (np.arange(g * nj) // nj)[None, None, :])
        return jnp.where(keep, wide, 0.0).astype(BF16).reshape(m.shape[:-3] + (g * ni, g * nj))

    lag_bd = jnp.moveaxis(group_diag(jnp.swapaxes(lag[:r], -1, -2)), 0, 2)
    st_bd = jnp.concatenate([group_diag(jnp.swapaxes(xr, -1, -2)), group_diag(jnp.swapaxes(xi, -1, -2))], axis=-1)
    out_bd = jnp.concatenate([group_diag(jnp.swapaxes(yr, -1, -2)), group_diag(jnp.swapaxes(-yi, -1, -2))], axis=-2)

    tau_st = (r - 1 - np.arange(r), np.arange(r))
    tau_out = (np.arange(r) + 1, r - np.arange(r))

    def per_dir(fn):
        return jnp.stack([fn(0), fn(1)], axis=1)

    wst = per_dir(lambda dr: jnp.transpose(st_bd[tau_st[dr], :, dr], (1, 0, 2, 3))
                  .reshape(nl, r * A_WIDTH, 2 * g * n))
    wout = per_dir(lambda dr: jnp.transpose(out_bd[tau_out[dr], :, dr], (1, 0, 2, 3)))

    lr, li = pr[r].reshape(nl, 2, 1, g * n), pi[r].reshape(nl, 2, 1, g * n)
    rows = []
    for _ in range(nlev):
        rows.append(jnp.concatenate([lr, lr], axis=-1))
        rows.append(jnp.concatenate([-li, li], axis=-1))
        lr, li = cm(lr, li, lr, li)
    pq = jnp.concatenate(rows, axis=-2)
    return lag_bd, wst, wout, pq


def _split3(x):
    x1 = x.astype(BF16)
    r1 = x - x1.astype(F32)
    x2 = r1.astype(BF16)
    x3 = (r1 - x2.astype(F32)).astype(BF16)
    return x1, x2, x3


def _hgrn_chunks(chains, tri_ref, sel_ref, sgn_ref, msk_ref, heads, bd, nlev):
    t = chains[0][0].shape[0]

    def head_scores(qx, kx, half=None, upper=True):
        if half is not None:
            lo = half if upper else 0
            qx = jnp.concatenate([qx[b + lo:b + lo + half] for b in range(0, t, 2 * half)], axis=0)
        qx = qx.astype(BF16)
        rows = qx.shape[0]
        qh = jnp.concatenate([jnp.where(heads[hd], qx, jnp.zeros_like(qx)) for hd in range(C_HEADS)], axis=0)
        s = _dot_nt(qh, kx)
        out = [s[hd * rows:(hd + 1) * rows, :] for hd in range(C_HEADS)]
        if half is None:
            return out
        zero = jnp.zeros((half, s.shape[1]), F32)
        full = []
        for sh in out:
            blocks = []
            for i in range(rows // half):
                piece = sh[i * half:(i + 1) * half]
                blocks += [zero, piece] if upper else [piece, zero]
            full.append(jnp.concatenate(blocks, axis=0))
        return full

    cums, tots = [], []
    for q, k, vb, lf, st, dr in chains:
        l1, l2, l3 = _split3(lf)
        tri = tri_ref[dr]
        cums.append(_dot(tri, l1) + _dot(tri, l2) + _dot(tri, l3))
        tots.append(jnp.sum(lf, axis=0, keepdims=True))
    mids = [_dot(sel_ref[c[5]], cum.astype(BF16)) for c, cum in zip(chains, cums)]

    atts = []
    for q, k, vb, lf, st, dr in chains:
        diag = msk_ref[dr, nlev] > 0.5
        atts.append([jnp.where(diag, s, 0.0) for s in head_scores(q, k.astype(BF16))])
    for lev in range(nlev):
        for ci, (q, k, vb, lf, st, dr) in enumerate(chains):
            e = jnp.exp(sgn_ref[dr, lev] * (cums[ci] - mids[ci][lev * t:(lev + 1) * t, :]))
            m = msk_ref[dr, lev] > 0.5
            half = (1 << lev) if (1 << lev) >= HG_TRIM_MIN else None
            for hd, s in enumerate(head_scores(q * e, (k * e).astype(BF16), half, upper=(dr == 0))):
                atts[ci][hd] = atts[ci][hd] + jnp.where(m, s, 0.0)

    outs = []
    for ci, (q, k, vb, lf, st, dr) in enumerate(chains):
        o = _dot_nt((q * jnp.exp(cums[ci])).astype(BF16), st.astype(BF16))
        att_wide = jnp.concatenate([a.astype(BF16) for a in atts[ci]], axis=1)
        v_heads = jnp.concatenate([jnp.where(heads[hd], vb, jnp.zeros_like(vb)) for hd in range(C_HEADS)], axis=0)
        o = o + _dot(att_wide, v_heads)
        kend = (k * jnp.exp(tots[ci] - cums[ci])).astype(BF16)
        vt = vb.astype(F32).T.astype(BF16)
        new = st * jnp.exp(tots[ci]) + jnp.where(bd, _dot(vt, kend), 0.0)
        outs.append((o, new))
    return outs


def _hgrn_kernel(qf_ref, qr_ref, kf_ref, kr_ref, vf_ref, vr_ref, lff_ref, lfr_ref, s0_ref,
                 tri_ref, sel_ref, sgn_ref, msk_ref, hm_ref, bd_ref,
                 of_ref, or_ref, sf_ref, st_scr, *, nlev, nb):
    c = pl.program_id(0)

    @pl.when(c == 0)
    def _():
        st_scr[...] = s0_ref[...]

    heads = [hm_ref[pl.ds(hd, 1), :] > 0.5 for hd in range(C_HEADS)]
    bd = bd_ref[...] > 0.5
    dirs = ((qf_ref, kf_ref, vf_ref, lff_ref, of_ref), (qr_ref, kr_ref, vr_ref, lfr_ref, or_ref))
    chains, sinks = [], []
    for dr, (q_ref, k_ref, v_ref, lf_ref, o_ref) in enumerate(dirs):
        for b in range(nb):
            chains.append((q_ref[b].astype(F32), k_ref[0, b].astype(F32), v_ref[b], lf_ref[0, b], st_scr[dr, b], dr))
            sinks.append((o_ref, dr, b))
    outs = _hgrn_chunks(chains, tri_ref, sel_ref, sgn_ref, msk_ref, heads, bd, nlev)
    for (o_ref, dr, b), (o, new) in zip(sinks, outs):
        o_ref[b] = o
        st_scr[dr, b] = new
        sf_ref[dr, b] = new


def _hgrn_consts():
    t = HG_T
    ti = jnp.arange(t)[:, None]
    si = jnp.arange(t)[None, :]
    tri, sel, sgn, msk = [], [], [], []
    for reverse in (False, True):
        tri.append((si >= ti) if reverse else (si <= ti))
        sels, sgns, msks = [], [], []
        for lev in range(HG_LEVELS):
            h = 1 << lev
            blk_t, blk_s = ti // (2 * h), si // (2 * h)
            hi_t, hi_s = (ti % (2 * h)) >= h, (si % (2 * h)) >= h
            if reverse:
                mid = blk_t * 2 * h + h
                q_role_t, k_role_s = ~hi_t, hi_s
            else:
                mid = blk_t * 2 * h + h - 1
                q_role_t, k_role_s = hi_t, ~hi_s
            sels.append(si == mid)
            sgns.append(jnp.where(q_role_t, 1.0, -1.0))
            msks.append((blk_t == blk_s) & q_role_t & k_role_s)
        msks.append(ti == si)
        sel.append(jnp.concatenate(sels, axis=0))
        sgn.append(jnp.stack(sgns))
        msk.append(jnp.stack(msks))
    lane_head = jnp.arange(HG_W) // C_DK
    hm = (lane_head[None, :] == jnp.arange(C_HEADS)[:, None]).astype(F32)
    bd = (lane_head[:, None] == lane_head[None, :]).astype(F32)
    return (jnp.stack(tri).astype(BF16), jnp.stack(sel).astype(BF16), jnp.stack(sgn).astype(F32),
            jnp.stack(msk).astype(F32), hm, bd)


def _hgrn_scan(q, k, v, lf, s0, consts):
    tri, sel, sgn, msk, hm, bd = consts
    nb, s, w = q.shape
    t = HG_T
    nc = s // t
    fwd3 = pl.BlockSpec((nb, t, w), lambda c: (0, c, 0))
    rev3 = pl.BlockSpec((nb, t, w), lambda c: (0, nc - 1 - c, 0))
    fwd4 = pl.BlockSpec((1, nb, t, w), lambda c: (0, 0, c, 0))
    rev4 = pl.BlockSpec((1, nb, t, w), lambda c: (1, 0, nc - 1 - c, 0))
    whole = lambda a: pl.BlockSpec(a.shape, lambda c: (0,) * a.ndim)
    sgn = sgn.reshape(2, HG_LEVELS, t, 1)
    o_f, o_r, sf = pl.pallas_call(
        functools.partial(_hgrn_kernel, nlev=HG_LEVELS, nb=nb),
        grid=(nc,),
        in_specs=[fwd3, rev3, fwd4, rev4, fwd3, rev3, fwd4, rev4, whole(s0),
                  whole(tri), whole(sel), whole(sgn), whole(msk), whole(hm), whole(bd)],
        out_specs=[fwd3, rev3, whole(s0)],
        out_shape=[
            jax.ShapeDtypeStruct((nb, s, w), F32),
            jax.ShapeDtypeStruct((nb, s, w), F32),
            jax.ShapeDtypeStruct((2, nb, w, w), F32),
        ],
        scratch_shapes=[pltpu.VMEM((2, nb, w, w), F32)],
        compiler_params=_params(("arbitrary",)),
        name="hgrn_scan",
    )(q, q, k, k, v, v, lf, lf, s0, tri, sel, sgn, msk, hm, bd)
    return (o_f, o_r), sf


def _attn_kernel(*refs, nseg, tks, g, tq):
    q_ref = refs[0]
    kv_refs = refs[1:1 + 2 * nseg]
    o_ref = refs[1 + 2 * nseg]
    s_scr, p_scr, acc_scr = refs[2 + 2 * nseg:]
    dv = o_ref.shape[-1]
    n = g * tq
    qt = jnp.concatenate([q_ref[0, hd] for hd in range(g)], axis=-1)

    chunks = [(kv_refs[2 * seg], kv_refs[2 * seg + 1], off, tks[seg])
              for seg in range(nseg) for off in range(0, kv_refs[2 * seg].shape[2], tks[seg])]

    def issue(ci):
        k_ref, _, off, tk = chunks[ci]
        s = _dot(k_ref[0, 0, off:off + tk, :], qt)
        s_scr[ci % 2, 0:tk, :] = s
        return jnp.max(s, axis=0, keepdims=True)

    m = jnp.full((1, n), -jnp.inf, F32)
    acc_scr[...] = jnp.zeros_like(acc_scr)
    smax = issue(0)
    for ci, (_, vt_ref, off, tk) in enumerate(chunks):
        smax_next = issue(ci + 1) if ci + 1 < len(chunks) else None
        slot = ci % 2
        m_new = jnp.maximum(m, smax)
        alpha = jnp.exp2(m - m_new)
        p_scr[slot, 0:tk, :] = jnp.exp2(s_scr[slot, 0:tk, :] - m_new).astype(BF16)
        vt1 = jnp.concatenate([vt_ref[0, 0, :, off:off + tk], jnp.ones((ATTN_ONES, tk), BF16)], axis=0)
        acc_scr[...] = alpha * acc_scr[...] + _dot(vt1, p_scr[slot, 0:tk, :])
        m, smax = m_new, smax_next
    acc = acc_scr[...]
    out = acc[:dv] / acc[dv:dv + 1, :]
    out = jnp.concatenate([out, jnp.zeros((LANES - dv, n), F32)], axis=0).T
    o_ref[0] = out[:, :dv].reshape(g, tq, dv).astype(o_ref.dtype)


def _attention(q, kvs):
    nb, hq, dk, sq = q.shape
    hkv = kvs[0][0].shape[1]
    dv = kvs[0][1].shape[2]
    g = hq // hkv
    tq = _pick(sq, tuple(n // g for n in ATTN_QUERIES))
    tks = tuple(_pick(k.shape[2], (ATTN_SUB, 128)) for k, _ in kvs)
    in_specs = [pl.BlockSpec((1, g, dk, tq), lambda b, h, i: (b, h, 0, i))]
    args = [q]
    for k, v in kvs:
        sk = k.shape[2]
        in_specs.append(pl.BlockSpec((1, 1, sk, dk), lambda b, h, i: (b, h, 0, 0)))
        in_specs.append(pl.BlockSpec((1, 1, dv, sk), lambda b, h, i: (b, h, 0, 0)))
        args += [k, v]
    return pl.pallas_call(
        functools.partial(_attn_kernel, nseg=len(kvs), tks=tks, g=g, tq=tq),
        grid=(nb, hkv, sq // tq),
        in_specs=in_specs,
        out_specs=pl.BlockSpec((1, g, tq, dv), lambda b, h, i: (b, h, i, 0)),
        out_shape=jax.ShapeDtypeStruct((nb, hq, sq, dv), BF16),
        scratch_shapes=[pltpu.VMEM((2, max(tks), g * tq), F32), pltpu.VMEM((2, max(tks), g * tq), BF16),
                        pltpu.VMEM((dv + ATTN_ONES, g * tq), F32)],
        compiler_params=_params(("parallel", "parallel", "arbitrary")),
        name="attention",
    )(*args)


def _merge_kernel(x_ref, mod_ref, gpre_ref, gpost_ref, wg_ref, wb_ref, wo_ref,
                  u_ref, yf_ref, yr_ref, sd_ref, wglu_ref, bo_ref, hof_ref, hor_ref, hg_ref, hn_ref, avg_ref, do_ref,
                  o_ref):
    x = x_ref[0]
    shift = mod_ref[0, pl.ds(3, 1), :]
    scale = mod_ref[0, pl.ds(4, 1), :]
    gate = mod_ref[0, pl.ds(5, 1), :]
    h = (_rms(x, gpre_ref[...]) * (1.0 + scale) + shift).astype(BF16)
    d = x.shape[-1]

    def branch_gate(i):
        return jax.nn.sigmoid(_dot(h, wg_ref[:, i * d:(i + 1) * d]))

    ysum = yf_ref[0] + yr_ref[0]
    y = sd_ref[...] * u_ref[0].astype(F32) + jnp.concatenate([ysum[hf] for hf in range(A_WIDTH // LANES)], axis=-1)
    ge = jax.nn.gelu(y)
    ya = ge * jax.nn.sigmoid(_dot(ge.astype(BF16), wglu_ref[...]))
    merged = branch_gate(0) * _dot(ya.astype(BF16), wb_ref[0])

    yb = jnp.concatenate([bo_ref[0, hd] for hd in range(B_HEADS)], axis=-1)
    merged = merged + branch_gate(1) * _dot(yb, wb_ref[1])

    o2 = hof_ref[0] + hor_ref[0]
    ms = _dot((o2 * o2).astype(BF16), avg_ref[...])
    gz = hg_ref[0].astype(F32)
    yc = o2 * lax.rsqrt(ms + EPS) * hn_ref[...] * (gz * jax.nn.sigmoid(gz))
    merged = merged + branch_gate(2) * _dot(yc.astype(BF16), wb_ref[2])

    yd = jnp.concatenate([do_ref[0, hd] for hd in range(D_HEADS)], axis=-1)
    merged = merged + branch_gate(3) * _dot(yd, wb_ref[3])

    yo = _dot(merged.astype(BF16), wo_ref[...])
    o_ref[0] = x + gate * _rms(yo, gpost_ref[...])


def _merge(x, mod, g_pre, g_post, wg, wb, wo, u, yf, yr, s5d, wglu, bo, ho, hg, hn, avg, do):
    nb, s, d = x.shape
    tm = _pick(s, (512, 256))
    const2 = lambda b, i: (0, 0)
    const3 = lambda b, i: (0, 0, 0)
    tok = lambda w: pl.BlockSpec((1, tm, w), lambda b, i: (b, i, 0))
    halves = pl.BlockSpec((1, A_WIDTH // LANES, tm, LANES), lambda b, i: (b, 0, i, 0))
    return pl.pallas_call(
        _merge_kernel,
        grid=(nb, s // tm),
        in_specs=[
            tok(d),
            pl.BlockSpec((1, N_MOD, d), lambda b, i: (b, 0, 0)),
            pl.BlockSpec((1, d), const2),
            pl.BlockSpec((1, d), const2),
            pl.BlockSpec((d, N_BRANCH * d), const2),
            pl.BlockSpec((N_BRANCH, BRANCH_W, d), const3),
            pl.BlockSpec((d, d), const2),
            tok(A_WIDTH), halves, halves,
            pl.BlockSpec((1, A_WIDTH), const2),
            pl.BlockSpec((A_WIDTH, A_WIDTH), const2),
            pl.BlockSpec((1, B_HEADS, tm, B_V), lambda b, i: (b, 0, i, 0)),
            tok(HG_W), tok(HG_W),
            tok(HG_W),
            pl.BlockSpec((1, HG_W), const2),
            pl.BlockSpec((HG_W, HG_W), const2),
            pl.BlockSpec((1, D_HEADS, tm, D_HEAD), lambda b, i: (b, 0, i, 0)),
        ],
        out_specs=tok(d),
        out_shape=jax.ShapeDtypeStruct(x.shape, F32),
        compiler_params=_params(("parallel", "parallel")),
        name="merge_out",
    )(x, mod, g_pre.reshape(1, d), g_post.reshape(1, d), wg, wb, wo, u, yf, yr, s5d, wglu, bo, ho[0], ho[1], hg, hn,
      avg, do)


def _pad_cols(w, width):
    return jnp.pad(w, ((0, 0), (0, width - w.shape[1])))


def _proj_weight(w_in_mix):
    offs = [0]
    for wdt in (A_WIDTH, B_Q_LORA, B_KV_LORA, B_ROPE, HG_W, HG_W, HG_W, HG_W, HG_W,
                D_HEADS * D_HEAD, D_KV_HEADS * D_HEAD, D_KV_HEADS * D_HEAD):
        offs.append(offs[-1] + wdt)
    p = [w_in_mix[:, offs[i]:offs[i + 1]] for i in range(12)]
    d = w_in_mix.shape[0]
    z = lambda n: jnp.zeros((d, n), w_in_mix.dtype)
    cols = [p[0], _pad_cols(p[1], 256), p[2],
            jnp.concatenate([z(B_NOPE), p[3], z(LANES - B_NOPE - B_ROPE)], axis=1),
            p[4], p[5], p[6], p[7], p[8]]
    for i, nh in ((9, D_HEADS), (10, D_KV_HEADS), (11, D_KV_HEADS)):
        for hd in range(nh):
            cols.append(_pad_cols(p[i][:, hd * D_HEAD:(hd + 1) * D_HEAD], LANES))
    return jnp.concatenate(cols, axis=1).astype(BF16)


def _rope_tables(n_tok, rot_dim, lane_off, identity):
    cos = jnp.ones((n_tok, LANES), F32)
    sin_a = jnp.zeros((n_tok, LANES), F32)
    sin_b = jnp.zeros((n_tok, LANES), F32)
    if not identity:
        n_rows = n_tok // GRID_W
        rows = jnp.repeat(jnp.arange(n_rows, dtype=F32), GRID_W)
        cols = jnp.tile(jnp.arange(GRID_W, dtype=F32), n_rows)
        half = rot_dim // 2
        inv = ROPE_THETA ** (-jnp.arange(0, half, 2, dtype=F32) / half)
        ang_r = rows[:, None] * inv
        ang_c = cols[:, None] * inv
        ang = jnp.concatenate([ang_r, ang_r, ang_c, ang_c], axis=-1)
        c, s = jnp.cos(ang), jnp.sin(ang)
        quarter = rot_dim // 4
        first = (np.arange(rot_dim) % (2 * quarter)) < quarter

        def place(a, fill):
            left = jnp.full((n_tok, lane_off), fill, F32)
            right = jnp.full((n_tok, LANES - lane_off - rot_dim), fill, F32)
            return jnp.concatenate([left, a, right], axis=1)

        cos = place(c, 1.0)
        sin_a = place(jnp.where(first, -s, 0.0), 0.0)
        sin_b = place(jnp.where(first, 0.0, s), 0.0)
    return jnp.stack([cos, sin_a, sin_b])


def kernel(x, c, ctx, c_ctx, w_ada, b_ada, norm_pre, norm_post, ffn_w1, ffn_w3, ffn_w2, w_in,
           s5_lambda_re, s5_lambda_im, s5_log_dt, s5_b_re, s5_b_im, s5_c_re, s5_c_im, s5_d, s5_w_glu,
           mla_q_norm, mla_w_uq, mla_kv_norm, mla_w_ukv, hgrn_lb_raw, hgrn_o_norm,
           gqa_q_norm, gqa_k_norm, w_branch, w_out):
    nb, seq, d = x.shape
    n_ctx = ctx.shape[1]
    depth = w_ada.shape[0]

    rows = max(8, -(-(nb + 1) // 8) * 8)
    cvec = jnp.zeros((rows, d), F32).at[:nb].set(c).at[nb].set(c_ctx)
    mod_all = _modulation(cvec, w_ada, b_ada).reshape(depth, rows, N_MOD, d)

    lb_step = jax.nn.softmax(hgrn_lb_raw.astype(F32), axis=1)
    lb_all = jnp.clip(jnp.cumsum(lb_step, axis=1) - lb_step[:, :1], 0.0, 1.0)

    rope_b_lat = _rope_tables(seq, B_ROPE, B_NOPE, False)
    rope_d_lat = _rope_tables(seq, D_HEAD, 0, False)
    rope_b_ctx = _rope_tables(n_ctx, B_ROPE, B_NOPE, True)
    rope_d_ctx = _rope_tables(n_ctx, D_HEAD, 0, True)
    hg_consts = _hgrn_consts()
    s5_w = _s5_weights(s5_lambda_re, s5_lambda_im, s5_log_dt, s5_b_re, s5_b_im, s5_c_re, s5_c_im, S5_MAX_LEVELS)
    lane_head = jnp.arange(HG_W) // C_DV
    avg = (lane_head[:, None] == lane_head[None, :]).astype(BF16) * (1.0 / C_DV)
    n_mixcols = w_in.shape[-1] - N_BRANCH * d
    ffn_w = (ffn_w1.astype(BF16), ffn_w3.astype(BF16), ffn_w2.astype(BF16))

    x_lat, x_ctx = x, ctx
    for layer in range(depth):
        last = layer == depth - 1
        mod_lat = mod_all[layer, :nb]
        mod_ctx = jnp.broadcast_to(mod_all[layer, nb:nb + 1], (nb, N_MOD, d))
        bf = lambda w: w.astype(BF16)
        ffn_a = (norm_pre[layer, 0], norm_post[layer, 0], *ffn_w, layer, 0)
        ffn_b = (norm_pre[layer, 2], norm_post[layer, 2], *ffn_w, layer, 1)

        x_lat = _ffn(x_lat, mod_lat, 0, *ffn_a)
        x_ctx = _ffn(x_ctx, mod_ctx, 0, *ffn_a)

        wp = _proj_weight(w_in[layer, :, :n_mixcols])
        wg = bf(w_in[layer, :, n_mixcols:])
        qn = _pad_cols(mla_q_norm[layer].reshape(1, B_Q_LORA), 256)
        wuq = mla_w_uq[layer].reshape(B_Q_LORA, B_HEADS, B_NOPE + B_ROPE)
        wuq = jnp.pad(wuq, ((0, 256 - B_Q_LORA), (0, 0), (0, LANES - B_NOPE - B_ROPE))).reshape(256, B_HEADS * LANES)
        wukv = mla_w_ukv[layer].reshape(B_KV_LORA, B_HEADS, B_NOPE + B_V)
        wuk = jnp.pad(wukv[:, :, :B_NOPE], ((0, 0), (0, 0), (0, LANES - B_NOPE))).reshape(B_KV_LORA, B_HEADS * LANES)
        wuv = jnp.pad(wukv[:, :, B_NOPE:], ((0, 0), (0, 0), (0, LANES - B_V))).reshape(B_KV_LORA, B_HEADS * LANES)
        kvn = mla_kv_norm[layer].reshape(1, B_KV_LORA)
        gqn = _pad_cols(gqa_q_norm[layer].reshape(1, D_HEAD), LANES)
        gkn = _pad_cols(gqa_k_norm[layer].reshape(1, D_HEAD), LANES)
        lb = lb_all[:, layer]
        proj_args = (norm_pre[layer, 1], wp)
        mla_args = (qn, bf(wuq), kvn, bf(wuk), bf(wuv), lb, gqn, gkn)

        pl_ = _inproj(x_lat, mod_lat, *proj_args, rope_b_lat, rope_d_lat, *mla_args)
        pc_ = _inproj(x_ctx, mod_ctx, *proj_args, rope_b_ctx, rope_d_ctx, *mla_args)
        (u_l, u2_l, bq_l, bk_l, bv_l, hq_l, hv_l, hk_l, hl_l, hg_l, dq_l, dk_l, dv_l) = pl_
        (u_c, u2_c, bq_c, bk_c, bv_c, hq_c, hv_c, hk_c, hl_c, hg_c, dq_c, dk_c, dv_c) = pc_

        ys_l, ys_c = [], []
        for dr, reverse in enumerate((False, True)):
            wts = tuple(w[layer, dr] for w in s5_w)
            x0 = jnp.zeros((nb, 1, 2 * S5_NSTATE), F32)
            y_c, x_end = _s5_scan(u2_c, x0, wts, reverse)
            y_l, _ = _s5_scan(u2_l, x_end, wts, reverse)
            ys_l.append(y_l)
            ys_c.append(y_c)

        s0 = jnp.zeros((2, nb, HG_W, HG_W), F32)
        ho_c, s_ctx = _hgrn_scan(hq_c, hk_c, hv_c, hl_c, s0, hg_consts)
        ho_l, _ = _hgrn_scan(hq_l, hk_l, hv_l, hl_l, s_ctx, hg_consts)

        bo_l = _attention(bq_l, [(bk_c, bv_c), (bk_l, bv_l)])
        do_l = _attention(dq_l, [(dk_c, dv_c), (dk_l, dv_l)])

        merge_w = (norm_pre[layer, 1], norm_post[layer, 1], wg, bf(w_branch[layer]), bf(w_out[layer]))
        s5_ro = (s5_d[layer].reshape(1, A_WIDTH), bf(s5_w_glu[layer]))
        hn = jnp.tile(hgrn_o_norm[layer], C_HEADS).reshape(1, HG_W)
        x_lat_new = _merge(x_lat, mod_lat, *merge_w, u_l, ys_l[0], ys_l[1], *s5_ro, bo_l, ho_l, hg_l, hn, avg, do_l)
        if not last:
            bo_c = _attention(bq_c, [(bk_c, bv_c)])
            do_c = _attention(dq_c, [(dk_c, dv_c)])
            x_ctx = _merge(x_ctx, mod_ctx, *merge_w, u_c, ys_c[0], ys_c[1], *s5_ro, bo_c, ho_c, hg_c, hn, avg, do_c)
            x_ctx = _ffn(x_ctx, mod_ctx, 2, *ffn_b)
        x_lat = _ffn(x_lat_new, mod_lat, 2, *ffn_b)
    return x_lat
```
